```python
import math
import jax, jax.numpy as jnp
from jax import lax
import numpy as np

D_MODEL = 1024
BATCH = 8
SEQ = 16384
DEPTH = 2

HEAD_DIM = 64
GRID_W = 64
NA_HEADS = 4
SW_HEADS = 6
SW_KV_HEADS = 2
AX_HEADS = 6
AX_KV_HEADS = 2
MIX_WIDTH = (NA_HEADS + SW_HEADS + AX_HEADS) * HEAD_DIM
NA_WIN_ROWS = 8
NA_WIN_COLS = 16
SW_RADIUS = 128
BLOCK = 128
T5_BUCKETS = 32
T5_MAX_DIST = 128
ROPE_THETA = 10000.0
FFN_HIDDEN = ((8 * D_MODEL + 3 * 256 - 1) // (3 * 256)) * 256
IN_SPLITS = (NA_HEADS * HEAD_DIM, NA_HEADS * HEAD_DIM, NA_HEADS * HEAD_DIM,
             SW_HEADS * HEAD_DIM, SW_KV_HEADS * HEAD_DIM, SW_KV_HEADS * HEAD_DIM,
             AX_HEADS * HEAD_DIM, AX_KV_HEADS * HEAD_DIM, AX_KV_HEADS * HEAD_DIM)
IN_WIDTH = sum(IN_SPLITS)
GROUP_WIDTHS = (NA_HEADS * HEAD_DIM, SW_HEADS * HEAD_DIM, AX_HEADS * HEAD_DIM)
EPS = 1e-6
NEG_INF = -1e30

kernel_name = "hybrid_parallel_heads_encoder"


def rms_norm(x, g):
    xf = x.astype(jnp.float32)
    y = xf * lax.rsqrt(jnp.mean(xf * xf, axis=-1, keepdims=True) + EPS)
    return (y * g.astype(jnp.float32)).astype(x.dtype)


def split_cols(t, sizes):
    out = []
    start = 0
    for s in sizes:
        out.append(t[..., start:start + s])
        start += s
    return out


def neighborhood_attention(q, k, v, rpb):
    B, S, H, d = q.shape
    rows = S // GRID_W
    kh = min(NA_WIN_ROWS, rows)
    kw = NA_WIN_COLS
    qg = q.reshape(B, rows, GRID_W, H, d)
    kg = k.reshape(B, rows, GRID_W, H, d)
    vg = v.reshape(B, rows, GRID_W, H, d)
    cols = jnp.arange(GRID_W)
    col_start = jnp.clip(cols - kw // 2, 0, GRID_W - kw)
    col_idx = col_start[:, None] + jnp.arange(kw)[None, :]
    col_off = col_idx - cols[:, None] + (NA_WIN_COLS - 1)
    scale = d ** -0.5

    def one_row(r):
        rs = jnp.clip(r - kh // 2, 0, rows - kh)
        k_band = lax.dynamic_slice_in_dim(kg, rs, kh, axis=1)
        v_band = lax.dynamic_slice_in_dim(vg, rs, kh, axis=1)
        k_nb = k_band[:, :, col_idx]
        v_nb = v_band[:, :, col_idx]
        q_row = lax.dynamic_index_in_dim(qg, r, axis=1, keepdims=False)
        s = jnp.einsum('bqhd,brqwhd->bhqrw', q_row, k_nb).astype(jnp.float32) * scale
        row_off = rs + jnp.arange(kh) - r + (NA_WIN_ROWS - 1)
        bias = rpb[:, row_off[None, :, None], col_off[:, None, :]]
        s = s + bias[None].astype(jnp.float32)
        p = jax.nn.softmax(s.reshape(B, H, GRID_W, kh * kw), axis=-1)
        p = p.reshape(B, H, GRID_W, kh, kw).astype(v.dtype)
        return jnp.einsum('bhqrw,brqwhd->bqhd', p, v_nb)

    out = lax.map(one_row, jnp.arange(rows))
    return out.transpose(1, 0, 2, 3, 4).reshape(B, S, H * d)


def t5_bucket(rel):
    nb = T5_BUCKETS // 2
    ret = (rel > 0).astype(jnp.int32) * nb
    n = jnp.abs(rel)
    max_exact = nb // 2
    nf = jnp.maximum(n, max_exact).astype(jnp.float32)
    large = max_exact + (jnp.log(nf / max_exact) / math.log(T5_MAX_DIST / max_exact)
                         * (nb - max_exact)).astype(jnp.int32)
    large = jnp.minimum(large, nb - 1)
    return ret + jnp.where(n < max_exact, n, large)


def sliding_window_attention(q, k, v, sink, t5_table):
    B, S, H, d = q.shape
    G = k.shape[2]
    R = H // G
    nb = S // BLOCK
    scale = d ** -0.5
    qb = q.reshape(B, nb, BLOCK, G, R, d)

    def band(t):
        tb = t.reshape(B, nb, BLOCK, G, d)
        pad = jnp.zeros_like(tb[:, :1])
        tp = jnp.concatenate([pad, tb, pad], axis=1)
        return jnp.concatenate([tp[:, :-2], tp[:, 1:-1], tp[:, 2:]], axis=2)

    kb = band(k)
    vb = band(v)
    qpos = jnp.arange(BLOCK)
    kpos = jnp.arange(3 * BLOCK) - BLOCK
    rel = kpos[None, :] - qpos[:, None]
    bias = t5_table[t5_bucket(rel)].astype(jnp.float32)
    bias = jnp.transpose(bias, (2, 0, 1)).reshape(G, R, BLOCK, 3 * BLOCK)
    kabs = jnp.arange(nb)[:, None] * BLOCK + kpos[None, :]
    valid = (jnp.abs(rel) <= SW_RADIUS)[None] & ((kabs >= 0) & (kabs < S))[:, None, :]
    s = jnp.einsum('bnqgrd,bnkgd->bngrqk', qb, kb).astype(jnp.float32) * scale + bias
    s = jnp.where(valid[None, :, None, None], s, NEG_INF)
    sink_logits = jnp.broadcast_to(sink.reshape(G, R, 1, 1).astype(jnp.float32), s.shape[:-1] + (1,))
    p = jax.nn.softmax(jnp.concatenate([s, sink_logits], axis=-1), axis=-1)[..., :-1]
    o = jnp.einsum('bngrqk,bnkgd->bnqgrd', p.astype(v.dtype), vb)
    return o.reshape(B, S, H * d)


def axial_rope_tables(S):
    t = jnp.arange(S)
    row = (t // GRID_W).astype(jnp.float32)
    col = (t % GRID_W).astype(jnp.float32)
    axis_dim = HEAD_DIM // 2
    freqs = ROPE_THETA ** (-jnp.arange(0, axis_dim, 2, dtype=jnp.float32) / axis_dim)
    ang = jnp.stack([row[:, None] * freqs, col[:, None] * freqs], axis=1)
    return jnp.cos(ang), jnp.sin(ang)


def apply_axial_rope(x, cos, sin):
    B, S, H, d = x.shape
    xr = x.reshape(B, S, H, 2, 2, d // 4).astype(jnp.float32)
    x1 = xr[..., 0, :]
    x2 = xr[..., 1, :]
    c = cos[None, :, None]
    s = sin[None, :, None]
    out = jnp.stack([x1 * c - x2 * s, x2 * c + x1 * s], axis=-2)
    return out.reshape(B, S, H, d).astype(x.dtype)


def axial_attention(q, k, v, gq, gk):
    B, S, H, d = q.shape
    G = k.shape[2]
    R = H // G
    nb = S // BLOCK
    scale = d ** -0.5
    cos, sin = axial_rope_tables(S)
    q = apply_axial_rope(rms_norm(q, gq), cos, sin)
    k = apply_axial_rope(rms_norm(k, gk), cos, sin)
    qb = q.reshape(B, nb, BLOCK, G, R, d).transpose(1, 0, 2, 3, 4, 5)

    def one_block(qblk):
        s = jnp.einsum('bqgrd,bkgd->bgrqk', qblk, k).astype(jnp.float32) * scale
        p = jax.nn.softmax(s, axis=-1).astype(v.dtype)
        return jnp.einsum('bgrqk,bkgd->bqgrd', p, v)

    o = lax.map(one_block, qb)
    return o.transpose(1, 0, 2, 3, 4, 5).reshape(B, S, H * d)


def _fwd_setup_inputs(seed: int = 0) -> dict:
    key = jax.random.key(seed)
    ks = jax.random.split(key, 18)
    D = D_MODEL
    L = DEPTH
    F = FFN_HIDDEN

    def nrm(k, shape, scale):
        return jax.random.normal(k, shape, jnp.float32) * scale

    return {
        "x": nrm(ks[0], (BATCH, SEQ, D), 1.0),
        "c": nrm(ks[1], (BATCH, D), 1.0),
        "w_mod": nrm(ks[2], (L, D, 6 * D), 0.5 * D ** -0.5),
        "b_mod": nrm(ks[3], (L, 6 * D), 0.01),
        "g_attn": 1.0 + nrm(ks[4], (L, D), 0.01),
        "w_in": nrm(ks[5], (L, D, IN_WIDTH), D ** -0.5),
        "rpb_na": nrm(ks[6], (L, NA_HEADS, 2 * NA_WIN_ROWS - 1, 2 * NA_WIN_COLS - 1), 0.1),
        "sink_sw": nrm(ks[7], (L, SW_HEADS), 0.5),
        "t5_table": nrm(ks[8], (T5_BUCKETS, SW_HEADS), 0.1),
        "gq_ax": 1.0 + nrm(ks[9], (L, HEAD_DIM), 0.01),
        "gk_ax": 1.0 + nrm(ks[10], (L, HEAD_DIM), 0.01),
        "g_group": 1.0 + nrm(ks[11], (L, MIX_WIDTH), 0.01),
        "w_o": nrm(ks[12], (L, MIX_WIDTH, D), MIX_WIDTH ** -0.5),
        "g_ffn": 1.0 + nrm(ks[13], (L, D), 0.01),
        "w_gu": nrm(ks[14], (L, D, 2 * F), D ** -0.5),
        "w_down": nrm(ks[15], (L, F, D), F ** -0.5),
        "g_final": 1.0 + nrm(ks[16], (D,), 0.01),
    }


def _fwd_reference(x, c, w_mod, b_mod, g_attn, w_in, rpb_na, sink_sw, t5_table, gq_ax, gk_ax,
              g_group, w_o, g_ffn, w_gu, w_down, g_final):
    B, S, D = x.shape
    cond = jax.nn.silu(c)
    for l in range(DEPTH):
        mod = cond @ w_mod[l] + b_mod[l]
        sh_a, sc_a, gt_a, sh_f, sc_f, gt_f = [m[:, None, :] for m in split_cols(mod, (D,) * 6)]

        h = rms_norm(x, g_attn[l]) * (1 + sc_a) + sh_a
        proj = h @ w_in[l]
        qa, ka, va, qb, kb, vb, qc, kc, vc = split_cols(proj, IN_SPLITS)
        ya = neighborhood_attention(qa.reshape(B, S, NA_HEADS, HEAD_DIM),
                                    ka.reshape(B, S, NA_HEADS, HEAD_DIM),
                                    va.reshape(B, S, NA_HEADS, HEAD_DIM), rpb_na[l])
        yb = sliding_window_attention(qb.reshape(B, S, SW_HEADS, HEAD_DIM),
                                      kb.reshape(B, S, SW_KV_HEADS, HEAD_DIM),
                                      vb.reshape(B, S, SW_KV_HEADS, HEAD_DIM),
                                      sink_sw[l], t5_table)
        yc = axial_attention(qc.reshape(B, S, AX_HEADS, HEAD_DIM),
                             kc.reshape(B, S, AX_KV_HEADS, HEAD_DIM),
                             vc.reshape(B, S, AX_KV_HEADS, HEAD_DIM), gq_ax[l], gk_ax[l])
        ga, gb, gc = split_cols(g_group[l], GROUP_WIDTHS)
        y = jnp.concatenate([rms_norm(ya, ga), rms_norm(yb, gb), rms_norm(yc, gc)], axis=-1)
        x = x + gt_a * (y @ w_o[l])

        h = rms_norm(x, g_ffn[l]) * (1 + sc_f) + sh_f
        gate, up = split_cols(h @ w_gu[l], (FFN_HIDDEN, FFN_HIDDEN))
        x = x + gt_f * ((jax.nn.silu(gate) * up) @ w_down[l])
    return rms_norm(x, g_final)


import jax as _jax
import jax.numpy as _jnp

TWIN_FORMAT = 'train_step'
FWD_PARAMS = ['x', 'c', 'w_mod', 'b_mod', 'g_attn', 'w_in', 'rpb_na', 'sink_sw', 't5_table', 'gq_ax', 'gk_ax', 'g_group', 'w_o', 'g_ffn', 'w_gu', 'w_down', 'g_final']
TWIN_WEIGHTS = ['w_mod', 'b_mod', 'g_attn', 'w_in', 'rpb_na', 'sink_sw', 't5_table', 'gq_ax', 'gk_ax', 'g_group', 'w_o', 'g_ffn', 'w_gu', 'w_down', 'g_final']
TWIN_DIFF_INPUT = 'x'
TWIN_INPUTS = ['x', 'c', 'w_mod', 'b_mod', 'g_attn', 'w_in', 'rpb_na', 'sink_sw', 't5_table', 'gq_ax', 'gk_ax', 'g_group', 'w_o', 'g_ffn', 'w_gu', 'w_down', 'g_final', 'loss_target', 'm_w_mod', 'm_b_mod', 'm_g_attn', 'm_w_in', 'm_rpb_na', 'm_sink_sw', 'm_t5_table', 'm_gq_ax', 'm_gk_ax', 'm_g_group', 'm_w_o', 'm_g_ffn', 'm_w_gu', 'm_w_down', 'm_g_final', 'v_w_mod', 'v_b_mod', 'v_g_attn', 'v_w_in', 'v_rpb_na', 'v_sink_sw', 'v_t5_table', 'v_gq_ax', 'v_gk_ax', 'v_g_group', 'v_w_o', 'v_g_ffn', 'v_w_gu', 'v_w_down', 'v_g_final']
TWIN_OUTPUTS = ['loss', 'grad_x', 'grad_w_mod', 'grad_b_mod', 'grad_g_attn', 'grad_w_in', 'grad_rpb_na', 'grad_sink_sw', 'grad_t5_table', 'grad_gq_ax', 'grad_gk_ax', 'grad_g_group', 'grad_w_o', 'grad_g_ffn', 'grad_w_gu', 'grad_w_down', 'grad_g_final', 'delta_w_mod', 'delta_b_mod', 'delta_g_attn', 'delta_w_in', 'delta_rpb_na', 'delta_sink_sw', 'delta_t5_table', 'delta_gq_ax', 'delta_gk_ax', 'delta_g_group', 'delta_w_o', 'delta_g_ffn', 'delta_w_gu', 'delta_w_down', 'delta_g_final', 'new_m_w_mod', 'new_m_b_mod', 'new_m_g_attn', 'new_m_w_in', 'new_m_rpb_na', 'new_m_sink_sw', 'new_m_t5_table', 'new_m_gq_ax', 'new_m_gk_ax', 'new_m_g_group', 'new_m_w_o', 'new_m_g_ffn', 'new_m_w_gu', 'new_m_w_down', 'new_m_g_final', 'new_v_w_mod', 'new_v_b_mod', 'new_v_g_attn', 'new_v_w_in', 'new_v_rpb_na', 'new_v_sink_sw', 'new_v_t5_table', 'new_v_gq_ax', 'new_v_gk_ax', 'new_v_g_group', 'new_v_w_o', 'new_v_g_ffn', 'new_v_w_gu', 'new_v_w_down', 'new_v_g_final']
TWIN_LEAF_KINDS = {'loss': 'loss', 'grad_x': 'grad_x', 'grad_w_mod': 'grad_w', 'grad_b_mod': 'grad_w', 'grad_g_attn': 'grad_w', 'grad_w_in': 'grad_w', 'grad_rpb_na': 'grad_w', 'grad_sink_sw': 'grad_w', 'grad_t5_table': 'grad_w', 'grad_gq_ax': 'grad_w', 'grad_gk_ax': 'grad_w', 'grad_g_group': 'grad_w', 'grad_w_o': 'grad_w', 'grad_g_ffn': 'grad_w', 'grad_w_gu': 'grad_w', 'grad_w_down': 'grad_w', 'grad_g_final': 'grad_w', 'delta_w_mod': 'delta_w', 'delta_b_mod': 'delta_w', 'delta_g_attn': 'delta_w', 'delta_w_in': 'delta_w', 'delta_rpb_na': 'delta_w', 'delta_sink_sw': 'delta_w', 'delta_t5_table': 'delta_w', 'delta_gq_ax': 'delta_w', 'delta_gk_ax': 'delta_w', 'delta_g_group': 'delta_w', 'delta_w_o': 'delta_w', 'delta_g_ffn': 'delta_w', 'delta_w_gu': 'delta_w', 'delta_w_down': 'delta_w', 'delta_g_final': 'delta_w', 'new_m_w_mod': 'new_m', 'new_m_b_mod': 'new_m', 'new_m_g_attn': 'new_m', 'new_m_w_in': 'new_m', 'new_m_rpb_na': 'new_m', 'new_m_sink_sw': 'new_m', 'new_m_t5_table': 'new_m', 'new_m_gq_ax': 'new_m', 'new_m_gk_ax': 'new_m', 'new_m_g_group': 'new_m', 'new_m_w_o': 'new_m', 'new_m_g_ffn': 'new_m', 'new_m_w_gu': 'new_m', 'new_m_w_down': 'new_m', 'new_m_g_final': 'new_m', 'new_v_w_mod': 'new_v', 'new_v_b_mod': 'new_v', 'new_v_g_attn': 'new_v', 'new_v_w_in': 'new_v', 'new_v_rpb_na': 'new_v', 'new_v_sink_sw': 'new_v', 'new_v_t5_table': 'new_v', 'new_v_gq_ax': 'new_v', 'new_v_gk_ax': 'new_v', 'new_v_g_group': 'new_v', 'new_v_w_o': 'new_v', 'new_v_g_ffn': 'new_v', 'new_v_w_gu': 'new_v', 'new_v_w_down': 'new_v', 'new_v_g_final': 'new_v'}


def _forward(args):
    return _fwd_reference(*[args[k] for k in FWD_PARAMS])


def _output_shape():
    def fwd():
        inp = _fwd_setup_inputs(0)
        return _fwd_reference(*[inp[k] for k in FWD_PARAMS])
    out = _jax.eval_shape(fwd)
    return out.shape, out.dtype

N_MICROBATCH = 1
ADAM_LR = 0.001
ADAM_B1 = 0.9
ADAM_B2 = 0.999
ADAM_EPS = 1e-08
ADAM_WD = 0.01
ADAM_STEP = 10
PER_EXAMPLE_BATCH_AXIS = {'x': 0, 'c': 0, 'loss_target': 0}
SHARED_INPUTS = []
_WEIGHT_DTYPES = {'w_mod': _jnp.float32, 'b_mod': _jnp.float32, 'g_attn': _jnp.float32, 'w_in': _jnp.float32, 'rpb_na': _jnp.float32, 'sink_sw': _jnp.float32, 't5_table': _jnp.float32, 'gq_ax': _jnp.float32, 'gk_ax': _jnp.float32, 'g_group': _jnp.float32, 'w_o': _jnp.float32, 'g_ffn': _jnp.float32, 'w_gu': _jnp.float32, 'w_down': _jnp.float32, 'g_final': _jnp.float32}
MOMENT_SCALE = {'w_mod': 1.490594e-01, 'b_mod': 2.679817e-01, 'g_attn': 8.504075e-02, 'w_in': 8.947306e-02, 'rpb_na': 1.821763e-02, 'sink_sw': 3.546732e-03, 't5_table': 7.906321e-02, 'gq_ax': 5.929104e-02, 'gk_ax': 5.849699e-02, 'g_group': 1.223226e-01, 'w_o': 1.194817e-01, 'g_ffn': 9.878347e-02, 'w_gu': 4.296524e-02, 'w_down': 7.037873e-02, 'g_final': 1.285938e+02}


def _to_microbatches(a, axis):
    t = _jnp.moveaxis(a, axis, 0)
    t = t.reshape((N_MICROBATCH, t.shape[0] // N_MICROBATCH) + t.shape[1:])
    return _jnp.moveaxis(t, 1, axis + 1)


def setup_inputs(seed: int = 0) -> dict:
    inp = _fwd_setup_inputs(seed)
    key = _jax.random.fold_in(_jax.random.key(seed), 7919)
    shape, _ = _output_shape()
    out = dict(inp)
    out["loss_target"] = _jax.random.normal(_jax.random.fold_in(key, 0), shape, _jnp.float32)
    for i, name in enumerate(TWIN_WEIGHTS):
        w = inp[name].astype(_jnp.float32)
        if MOMENT_SCALE is None:
            s = _jnp.sqrt(_jnp.mean(_jnp.square(w)) + 1e-30)
        else:
            s = MOMENT_SCALE[name]
        km, kv = _jax.random.split(_jax.random.fold_in(key, i + 1))
        out[name] = w
        out["m_" + name] = s * _jax.random.normal(km, w.shape, _jnp.float32)
        out["v_" + name] = (s * s) * _jax.random.uniform(kv, w.shape, _jnp.float32, 0.5, 1.5)
    if N_MICROBATCH > 1:
        for name, axis in PER_EXAMPLE_BATCH_AXIS.items():
            out[name] = _to_microbatches(out[name], axis)
    return {'x': out['x'], 'c': out['c'], 'w_mod': out['w_mod'], 'b_mod': out['b_mod'], 'g_attn': out['g_attn'], 'w_in': out['w_in'], 'rpb_na': out['rpb_na'], 'sink_sw': out['sink_sw'], 't5_table': out['t5_table'], 'gq_ax': out['gq_ax'], 'gk_ax': out['gk_ax'], 'g_group': out['g_group'], 'w_o': out['w_o'], 'g_ffn': out['g_ffn'], 'w_gu': out['w_gu'], 'w_down': out['w_down'], 'g_final': out['g_final'], 'loss_target': out['loss_target'], 'm_w_mod': out['m_w_mod'], 'm_b_mod': out['m_b_mod'], 'm_g_attn': out['m_g_attn'], 'm_w_in': out['m_w_in'], 'm_rpb_na': out['m_rpb_na'], 'm_sink_sw': out['m_sink_sw'], 'm_t5_table': out['m_t5_table'], 'm_gq_ax': out['m_gq_ax'], 'm_gk_ax': out['m_gk_ax'], 'm_g_group': out['m_g_group'], 'm_w_o': out['m_w_o'], 'm_g_ffn': out['m_g_ffn'], 'm_w_gu': out['m_w_gu'], 'm_w_down': out['m_w_down'], 'm_g_final': out['m_g_final'], 'v_w_mod': out['v_w_mod'], 'v_b_mod': out['v_b_mod'], 'v_g_attn': out['v_g_attn'], 'v_w_in': out['v_w_in'], 'v_rpb_na': out['v_rpb_na'], 'v_sink_sw': out['v_sink_sw'], 'v_t5_table': out['v_t5_table'], 'v_gq_ax': out['v_gq_ax'], 'v_gk_ax': out['v_gk_ax'], 'v_g_group': out['v_g_group'], 'v_w_o': out['v_w_o'], 'v_g_ffn': out['v_g_ffn'], 'v_w_gu': out['v_w_gu'], 'v_w_down': out['v_w_down'], 'v_g_final': out['v_g_final']}


def _loss(weights, diff, rest, loss_target):
    with _jax.named_scope("forward"):
        args = {**rest, TWIN_DIFF_INPUT: diff, **{k: w.astype(_WEIGHT_DTYPES[k]) for k, w in weights.items()}}
        y = _forward(args)
    with _jax.named_scope("loss_head"):
        err = _jnp.square(y.astype(_jnp.float32) - loss_target)
        return 0.5 * _jnp.sum(_jnp.mean(err, axis=-1)) if err.ndim else 0.5 * err


def _adamw(w, g, m, v):
    m = ADAM_B1 * m + (1.0 - ADAM_B1) * g
    v = ADAM_B2 * v + (1.0 - ADAM_B2) * _jnp.square(g)
    m_hat = m / (1.0 - ADAM_B1 ** ADAM_STEP)
    v_hat = v / (1.0 - ADAM_B2 ** ADAM_STEP)
    delta = -ADAM_LR * (m_hat / (_jnp.sqrt(v_hat) + ADAM_EPS) + ADAM_WD * w)
    return delta, m, v


def reference(x, c, w_mod, b_mod, g_attn, w_in, rpb_na, sink_sw, t5_table, gq_ax, gk_ax, g_group, w_o, g_ffn, w_gu, w_down, g_final, loss_target, m_w_mod, m_b_mod, m_g_attn, m_w_in, m_rpb_na, m_sink_sw, m_t5_table, m_gq_ax, m_gk_ax, m_g_group, m_w_o, m_g_ffn, m_w_gu, m_w_down, m_g_final, v_w_mod, v_b_mod, v_g_attn, v_w_in, v_rpb_na, v_sink_sw, v_t5_table, v_gq_ax, v_gk_ax, v_g_group, v_w_o, v_g_ffn, v_w_gu, v_w_down, v_g_final):
    given = dict(x=x, c=c, w_mod=w_mod, b_mod=b_mod, g_attn=g_attn, w_in=w_in, rpb_na=rpb_na, sink_sw=sink_sw, t5_table=t5_table, gq_ax=gq_ax, gk_ax=gk_ax, g_group=g_group, w_o=w_o, g_ffn=g_ffn, w_gu=w_gu, w_down=w_down, g_final=g_final, loss_target=loss_target, m_w_mod=m_w_mod, m_b_mod=m_b_mod, m_g_attn=m_g_attn, m_w_in=m_w_in, m_rpb_na=m_rpb_na, m_sink_sw=m_sink_sw, m_t5_table=m_t5_table, m_gq_ax=m_gq_ax, m_gk_ax=m_gk_ax, m_g_group=m_g_group, m_w_o=m_w_o, m_g_ffn=m_g_ffn, m_w_gu=m_w_gu, m_w_down=m_w_down, m_g_final=m_g_final, v_w_mod=v_w_mod, v_b_mod=v_b_mod, v_g_attn=v_g_attn, v_w_in=v_w_in, v_rpb_na=v_rpb_na, v_sink_sw=v_sink_sw, v_t5_table=v_t5_table, v_gq_ax=v_gq_ax, v_gk_ax=v_gk_ax, v_g_group=v_g_group, v_w_o=v_w_o, v_g_ffn=v_g_ffn, v_w_gu=v_w_gu, v_w_down=v_w_down, v_g_final=v_g_final)
    weights = {n: given[n] for n in TWIN_WEIGHTS}
    shared = {n: given[n] for n in SHARED_INPUTS}
    per_example = {n: given[n] for n in ['x', 'c']}
    grad_fn = _jax.value_and_grad(_loss, argnums=(0, 1))

    def one_microbatch(ex, loss_target):
        ex = dict(ex)
        diff = ex.pop(TWIN_DIFF_INPUT)
        return grad_fn(weights, diff, {**shared, **ex}, loss_target)

    if N_MICROBATCH == 1:
        loss, (grad_w, grad_x) = one_microbatch(per_example, given["loss_target"])
    else:
        def body(carry, xs):
            loss_sum, grad_sum = carry
            l_k, (gw_k, gx_k) = one_microbatch(xs[0], xs[1])
            with _jax.named_scope("update"):
                return (loss_sum + l_k, _jax.tree.map(_jnp.add, grad_sum, gw_k)), gx_k

        init = (_jnp.zeros((), _jnp.float32), _jax.tree.map(_jnp.zeros_like, weights))
        (loss, grad_w), grad_x = _jax.lax.scan(body, init, (per_example, given["loss_target"]))
    with _jax.named_scope("update"):
        delta_w, new_m, new_v = {}, {}, {}
        for n in TWIN_WEIGHTS:
            delta_w[n], new_m[n], new_v[n] = _adamw(weights[n], grad_w[n], given["m_" + n], given["v_" + n])
    return (loss, grad_x, *[grad_w[n] for n in TWIN_WEIGHTS], *[delta_w[n] for n in TWIN_WEIGHTS],
            *[new_m[n] for n in TWIN_WEIGHTS], *[new_v[n] for n in TWIN_WEIGHTS])
```

```python
import functools
import math

import jax
import jax.numpy as jnp
from jax import lax
from jax.experimental import pallas as pl
from jax.experimental.pallas import tpu as pltpu

F32 = jnp.float32
BF16 = jnp.bfloat16
I32 = jnp.int32

DEPTH = 2
HEAD_DIM = 64
GRID_W = 64
NA_HEADS = 4
SW_HEADS = 6
SW_KV_HEADS = 2
AX_HEADS = 6
AX_KV_HEADS = 2
NA_WIN_ROWS = 8
NA_WIN_COLS = 16
SW_RADIUS = 128
T5_BUCKETS = 32
T5_MAX_DIST = 128
ROPE_THETA = 10000.0
EPS = 1e-6
NEG_INF = -1e30
QK_SCALE = HEAD_DIM ** -0.5

NA_W = NA_HEADS * HEAD_DIM
SW_W = SW_HEADS * HEAD_DIM
SW_KV_W = SW_KV_HEADS * HEAD_DIM
AX_W = AX_HEADS * HEAD_DIM
AX_KV_W = AX_KV_HEADS * HEAD_DIM
OFF_QA, OFF_KA, OFF_VA = 0, NA_W, 2 * NA_W
OFF_QB = 3 * NA_W
OFF_KB = OFF_QB + SW_W
OFF_VB = OFF_KB + SW_KV_W
OFF_QC = OFF_VB + SW_KV_W
OFF_KC = OFF_QC + AX_W
OFF_VC = OFF_KC + AX_KV_W
IN_WIDTH = OFF_VC + AX_KV_W
MIX_WIDTH = NA_W + SW_W + AX_W

ADAM_LR = 0.001
ADAM_B1 = 0.9
ADAM_B2 = 0.999
ADAM_EPS = 1e-08
ADAM_WD = 0.01
ADAM_STEP = 10

N_CHIPS = 4
N_DEV = 8
LANES = 128
VMEM_LIMIT_V7X = 56 * 1024 * 1024
MESH = pl.DeviceIdType.MESH

NT = (((1,), (1,)), ((), ()))
TN = (((0,), (0,)), ((), ()))


def _params(*sem):
    return pltpu.CompilerParams(dimension_semantics=sem if sem else None,
                                vmem_limit_bytes=VMEM_LIMIT_V7X)


def _tile(n, pref, mult=8):
    t = (min(pref, n) // mult) * mult
    while t >= mult:
        if n % t == 0:
            return t
        t -= mult
    return n


def _row_spec(tm, width, col=0):
    return pl.BlockSpec((tm, width), lambda i, *_: (i, col))


def _const_spec(shape):
    nd = len(shape)
    return pl.BlockSpec(shape, lambda *_: (0,) * nd)


def _rsq(ms):
    return lax.rsqrt(ms + EPS)


def _rope_tables(S):
    t = jnp.arange(S)
    row = (t // GRID_W).astype(F32)
    col = (t % GRID_W).astype(F32)
    axis_dim = HEAD_DIM // 2
    freqs = ROPE_THETA ** (-jnp.arange(0, axis_dim, 2, dtype=F32) / axis_dim)
    ar = row[:, None] * freqs
    ac = col[:, None] * freqs
    z = jnp.zeros_like(ar)
    cos64 = jnp.concatenate([jnp.cos(ar), jnp.cos(ar), jnp.cos(ac), jnp.cos(ac)], axis=1)
    sa64 = jnp.concatenate([-jnp.sin(ar), z, -jnp.sin(ac), z], axis=1)
    sb64 = jnp.concatenate([z, jnp.sin(ar), z, jnp.sin(ac)], axis=1)
    two = lambda a: jnp.concatenate([a, a], axis=1)
    return two(cos64), two(sa64), two(sb64)


def _pair_sum(v):
    lane = lax.broadcasted_iota(I32, v.shape, 1)
    lo = lane < HEAD_DIM
    s_lo = jnp.sum(jnp.where(lo, v, 0.0), axis=-1, keepdims=True)
    s_hi = jnp.sum(jnp.where(lo, 0.0, v), axis=-1, keepdims=True)
    return jnp.where(lo, s_lo, s_hi)


def _rope(t, cos, sa, sb):
    return t * cos + pltpu.roll(t, LANES - 16, 1) * sa + pltpu.roll(t, 16, 1) * sb


def _rope_t(t, cos, sa, sb):
    return t * cos + pltpu.roll(t * sa, 16, 1) + pltpu.roll(t * sb, LANES - 16, 1)


def _qk_prep_chunk(x, g128, cos, sa, sb):
    r = _rsq(_pair_sum(x * x) * (1.0 / HEAD_DIM))
    return _rope(x * r * g128, cos, sa, sb)


def _qk_prep_bwd_chunk(x, dy, g128, cos, sa, sb):
    dn = _rope_t(dy, cos, sa, sb)
    r = _rsq(_pair_sum(x * x) * (1.0 / HEAD_DIM))
    xhat = x * r
    dg = jnp.sum(dn * xhat, axis=0, keepdims=True)
    dxh = dn * g128
    dx = r * (dxh - xhat * (_pair_sum(dxh * xhat) * (1.0 / HEAD_DIM)))
    return dx, dg


def _ln_mod(xv, g, sc, sh):
    r = _rsq(jnp.mean(xv * xv, axis=-1, keepdims=True))
    return xv * r * g * (1.0 + sc) + sh


def _inproj_fwd(x, g, sc, sh, w, gq128, gk128, rope):
    S, D = x.shape
    tm = _tile(S, 512)
    cos, sa, sb = rope

    def body(x_ref, g_ref, sc_ref, sh_ref, w_ref, gq_ref, gk_ref, cos_ref, sa_ref, sb_ref,
             h_ref, proj_ref, qc_ref, kc_ref):
        hb = _ln_mod(x_ref[...], g_ref[...], sc_ref[...], sh_ref[...]).astype(BF16)
        h_ref[...] = hb
        acc = jnp.dot(hb, w_ref[...], preferred_element_type=F32)
        proj_ref[...] = acc.astype(BF16)
        c, a, b = cos_ref[...], sa_ref[...], sb_ref[...]
        for j in range(AX_W // LANES):
            xq = acc[:, OFF_QC + j * LANES: OFF_QC + (j + 1) * LANES]
            qc_ref[:, j * LANES:(j + 1) * LANES] = (
                _qk_prep_chunk(xq, gq_ref[...], c, a, b) * QK_SCALE).astype(BF16)
        for j in range(AX_KV_W // LANES):
            xk = acc[:, OFF_KC + j * LANES: OFF_KC + (j + 1) * LANES]
            kc_ref[:, j * LANES:(j + 1) * LANES] = _qk_prep_chunk(xk, gk_ref[...], c, a, b).astype(BF16)

    vec = _const_spec((1, D))
    v128 = _const_spec((1, LANES))
    return pl.pallas_call(
        body, name="inproj_fwd", grid=(S // tm,),
        in_specs=[_row_spec(tm, D), vec, vec, vec, _const_spec(w.shape), v128, v128,
                  _row_spec(tm, LANES), _row_spec(tm, LANES), _row_spec(tm, LANES)],
        out_specs=[_row_spec(tm, D), _row_spec(tm, IN_WIDTH), _row_spec(tm, AX_W), _row_spec(tm, AX_KV_W)],
        out_shape=[jax.ShapeDtypeStruct((S, D), BF16), jax.ShapeDtypeStruct((S, IN_WIDTH), BF16),
                   jax.ShapeDtypeStruct((S, AX_W), BF16), jax.ShapeDtypeStruct((S, AX_KV_W), BF16)],
        compiler_params=_params("parallel"),
    )(x, g, sc, sh, w, gq128, gk128, cos, sa, sb)


class _Band:
    def __init__(self, kind, S):
        self.kind = kind
        self.S = S
        if kind == "na":
            self.hq, self.g, self.halo = NA_HEADS, NA_HEADS, (NA_WIN_ROWS // 2) * GRID_W
            self.q_off, self.k_off, self.v_off = OFF_QA, OFF_KA, OFF_VA
        else:
            self.hq, self.g, self.halo = SW_HEADS, SW_KV_HEADS, SW_RADIUS
            self.q_off, self.k_off, self.v_off = OFF_QB, OFF_KB, OFF_VB
        self.bq = 2 * self.halo
        self.bk = self.bq + 2 * self.halo
        self.nb = S // self.bq
        self.rep = self.hq // self.g
        self.qw = self.hq * HEAD_DIM
        self.kw = self.g * HEAD_DIM

    def mask(self, n):
        qi = lax.broadcasted_iota(I32, (self.bq, self.bk), 0) + n * self.bq
        kj = lax.broadcasted_iota(I32, (self.bq, self.bk), 1) + (n * self.bq - self.halo)
        if self.kind == "sw":
            return (jnp.abs(kj - qi) <= SW_RADIUS) & (kj >= 0) & (kj < self.S)
        rows = self.S // GRID_W
        r, col = qi >> 6, qi & (GRID_W - 1)
        kr, kc = kj >> 6, kj & (GRID_W - 1)
        rs = jnp.clip(r - NA_WIN_ROWS // 2, 0, rows - NA_WIN_ROWS)
        cs = jnp.clip(col - NA_WIN_COLS // 2, 0, GRID_W - NA_WIN_COLS)
        return (kr >= rs) & (kr < rs + NA_WIN_ROWS) & (kc >= cs) & (kc < cs + NA_WIN_COLS)

    def qkv_specs(self):
        ratio = self.bq // self.halo
        last = self.S // self.halo - 1
        q = pl.BlockSpec((self.bq, self.qw), lambda n, o=self.q_off // self.qw: (n, o))
        specs = [q]
        for off in (self.k_off, self.v_off):
            o = off // self.kw
            specs.append(pl.BlockSpec((self.halo, self.kw), lambda n, o=o: (jnp.maximum(n * ratio - 1, 0), o)))
            specs.append(pl.BlockSpec((self.bq, self.kw), lambda n, o=o: (n, o)))
            specs.append(pl.BlockSpec((self.halo, self.kw), lambda n, o=o: (jnp.minimum((n + 1) * ratio, last), o)))
        return specs


def _band_probs(bd, h, q_ref, kcat, bias_ref, sink_ref, mask):
    sl = slice(h * HEAD_DIM, (h + 1) * HEAD_DIM)
    qh = q_ref[:, sl] * QK_SCALE
    s = lax.dot_general(qh, kcat, NT, preferred_element_type=F32) + bias_ref[h]
    s = jnp.where(mask, s, NEG_INF)
    m = jnp.max(s, axis=-1, keepdims=True)
    if sink_ref is not None:
        m = jnp.maximum(m, sink_ref[0:1, h:h + 1])
    p = jnp.exp(s - m)
    l = jnp.sum(p, axis=-1, keepdims=True)
    if sink_ref is not None:
        l = l + jnp.exp(sink_ref[0:1, h:h + 1] - m)
    return qh, p / l, m, l


def _band_fwd(bd, proj, bias, sink, gg):
    S = bd.S
    has_sink = sink is not None

    def body(*refs):
        q_ref, kp, km, kn, vp, vm, vn, bias_ref = refs[:8]
        k = 8
        sink_ref = None
        if has_sink:
            sink_ref = refs[k]
            k += 1
        gg_ref, raw_ref, yn_ref, o_scr = refs[k:k + 4]
        mask = bd.mask(pl.program_id(0))
        for g in range(bd.g):
            gs = slice(g * HEAD_DIM, (g + 1) * HEAD_DIM)
            kcat = jnp.concatenate([kp[:, gs], km[:, gs], kn[:, gs]], axis=0)
            vcat = jnp.concatenate([vp[:, gs], vm[:, gs], vn[:, gs]], axis=0)
            for h in range(g * bd.rep, (g + 1) * bd.rep):
                _, pn, _, _ = _band_probs(bd, h, q_ref, kcat, bias_ref, sink_ref, mask)
                o_scr[:, h * HEAD_DIM:(h + 1) * HEAD_DIM] = jnp.dot(
                    pn.astype(BF16), vcat, preferred_element_type=F32)
        o = o_scr[...]
        raw_ref[...] = o.astype(BF16)
        r = _rsq(jnp.mean(o * o, axis=-1, keepdims=True))
        yn_ref[...] = (o * r * gg_ref[...]).astype(BF16)

    in_specs = bd.qkv_specs() + [_const_spec(bias.shape)]
    args = [proj] * 7 + [bias]
    if has_sink:
        in_specs.append(_const_spec(sink.shape))
        args.append(sink)
    in_specs.append(_const_spec(gg.shape))
    args.append(gg)
    out = jax.ShapeDtypeStruct((S, bd.qw), BF16)
    return pl.pallas_call(
        body, name=bd.kind + "_fwd", grid=(bd.nb,), in_specs=in_specs,
        out_specs=[_row_spec(bd.bq, bd.qw), _row_spec(bd.bq, bd.qw)], out_shape=[out, out],
        scratch_shapes=[pltpu.VMEM((bd.bq, bd.qw), F32)],
        compiler_params=_params("parallel"),
    )(*args)


def _band_bwd(bd, proj, bias, sink, dy):
    S = bd.S
    has_sink = sink is not None

    def body(*refs):
        q_ref, kp, km, kn, vp, vm, vn, bias_ref = refs[:8]
        k = 8
        sink_ref = None
        if has_sink:
            sink_ref = refs[k]
            k += 1
        do_ref = refs[k]
        dq_ref, dkm, dvm, dkp, dvp, dkn, dvn, dbias_ref = refs[k + 1:k + 9]
        k += 9
        dsink_ref = None
        if has_sink:
            dsink_ref = refs[k]
            k += 1
        dk_scr, dv_scr = refs[k:k + 2]
        n = pl.program_id(0)

        @pl.when(n == 0)
        def _():
            dbias_ref[...] = jnp.zeros_like(dbias_ref)
            if has_sink:
                dsink_ref[...] = jnp.zeros_like(dsink_ref)

        mask = bd.mask(n)
        lane = lax.broadcasted_iota(I32, (1, LANES), 1)
        for g in range(bd.g):
            gs = slice(g * HEAD_DIM, (g + 1) * HEAD_DIM)
            kcat = jnp.concatenate([kp[:, gs], km[:, gs], kn[:, gs]], axis=0)
            vcat = jnp.concatenate([vp[:, gs], vm[:, gs], vn[:, gs]], axis=0)
            dk_g = jnp.zeros((bd.bk, HEAD_DIM), F32)
            dv_g = jnp.zeros((bd.bk, HEAD_DIM), F32)
            for h in range(g * bd.rep, (g + 1) * bd.rep):
                sl = slice(h * HEAD_DIM, (h + 1) * HEAD_DIM)
                qh, pn, m, l = _band_probs(bd, h, q_ref, kcat, bias_ref, sink_ref, mask)
                doh = do_ref[:, sl]
                dp = lax.dot_general(doh, vcat, NT, preferred_element_type=F32)
                delta = jnp.sum(pn * dp, axis=-1, keepdims=True)
                ds = pn * (dp - delta)
                dbias_ref[h] += ds
                if has_sink:
                    p_sink = jnp.exp(sink_ref[0:1, h:h + 1] - m) / l
                    dsink_ref[...] += jnp.where(lane == h, -jnp.sum(p_sink * delta, axis=0, keepdims=True), 0.0)
                dsb = ds.astype(BF16)
                dq_ref[:, sl] = jnp.dot(dsb, kcat, preferred_element_type=F32) * QK_SCALE
                dk_g = dk_g + lax.dot_general(dsb, qh, TN, preferred_element_type=F32)
                dv_g = dv_g + lax.dot_general(pn.astype(BF16), doh, TN, preferred_element_type=F32)
            dk_scr[:, gs] = dk_g
            dv_scr[:, gs] = dv_g
        h0, h1 = bd.halo, bd.halo + bd.bq
        dkp[0] = dk_scr[0:h0, :]
        dkm[...] = dk_scr[h0:h1, :]
        dkn[0] = dk_scr[h1:bd.bk, :]
        dvp[0] = dv_scr[0:h0, :]
        dvm[...] = dv_scr[h0:h1, :]
        dvn[0] = dv_scr[h1:bd.bk, :]

    in_specs = bd.qkv_specs() + [_const_spec(bias.shape)]
    args = [proj] * 7 + [bias]
    if has_sink:
        in_specs.append(_const_spec(sink.shape))
        args.append(sink)
    in_specs.append(_row_spec(bd.bq, bd.qw))
    args.append(dy)
    halo_spec = pl.BlockSpec((1, bd.halo, bd.kw), lambda n: (n, 0, 0))
    halo_shape = jax.ShapeDtypeStruct((bd.nb, bd.halo, bd.kw), F32)
    main_shape = jax.ShapeDtypeStruct((S, bd.kw), F32)
    out_specs = [_row_spec(bd.bq, bd.qw), _row_spec(bd.bq, bd.kw), _row_spec(bd.bq, bd.kw),
                 halo_spec, halo_spec, halo_spec, halo_spec, _const_spec(bias.shape)]
    out_shape = [jax.ShapeDtypeStruct((S, bd.qw), F32), main_shape, main_shape,
                 halo_shape, halo_shape, halo_shape, halo_shape, jax.ShapeDtypeStruct(bias.shape, F32)]
    if has_sink:
        out_specs.append(_const_spec((1, LANES)))
        out_shape.append(jax.ShapeDtypeStruct((1, LANES), F32))
    return pl.pallas_call(
        body, name=bd.kind + "_bwd", grid=(bd.nb,), in_specs=in_specs, out_specs=out_specs, out_shape=out_shape,
        scratch_shapes=[pltpu.VMEM((bd.bk, bd.kw), F32), pltpu.VMEM((bd.bk, bd.kw), F32)],
        compiler_params=_params("arbitrary"),
    )(*args)


def _halo_to_rows(prev, nxt):
    nb, halo, w = prev.shape
    z = jnp.zeros((1, halo, w), prev.dtype)
    first = jnp.concatenate([z, nxt[:-1]], axis=0)
    second = jnp.concatenate([prev[1:], z], axis=0)
    return jnp.concatenate([first, second], axis=1).reshape(nb * 2 * halo, w)


def _ax_blocks(S):
    return _tile(S, 512), _tile(S, 512)


def _ax_fwd(qc, kc, proj, gg):
    S = qc.shape[0]
    bq, bk = _ax_blocks(S)
    nk = S // bk
    rep = AX_HEADS // AX_KV_HEADS

    def body(q_ref, k_ref, v_ref, gg_ref, raw_ref, yn_ref, lse_ref, m_scr, l_scr, acc_scr):
        kv = pl.program_id(1)

        @pl.when(kv == 0)
        def _():
            m_scr[...] = jnp.full(m_scr.shape, NEG_INF, F32)
            l_scr[...] = jnp.zeros_like(l_scr)
            acc_scr[...] = jnp.zeros_like(acc_scr)

        for h in range(AX_HEADS):
            g = h // rep
            sl = slice(h * HEAD_DIM, (h + 1) * HEAD_DIM)
            gs = slice(g * HEAD_DIM, (g + 1) * HEAD_DIM)
            s = lax.dot_general(q_ref[:, sl], k_ref[:, gs], NT, preferred_element_type=F32)
            m_prev = m_scr[h]
            m_new = jnp.maximum(m_prev, jnp.max(s, axis=-1, keepdims=True))
            alpha = jnp.exp(m_prev - m_new)
            p = jnp.exp(s - m_new)
            l_scr[h] = alpha * l_scr[h] + jnp.sum(p, axis=-1, keepdims=True)
            acc_scr[:, sl] = alpha * acc_scr[:, sl] + jnp.dot(p.astype(BF16), v_ref[:, gs],
                                                             preferred_element_type=F32)
            m_scr[h] = m_new

        @pl.when(kv == nk - 1)
        def _():
            for h in range(AX_HEADS):
                sl = slice(h * HEAD_DIM, (h + 1) * HEAD_DIM)
                acc_scr[:, sl] = acc_scr[:, sl] / l_scr[h]
                lse_ref[h] = jnp.broadcast_to(m_scr[h] + jnp.log(l_scr[h]), (bq, LANES))
            o = acc_scr[...]
            raw_ref[...] = o.astype(BF16)
            r = _rsq(jnp.mean(o * o, axis=-1, keepdims=True))
            yn_ref[...] = (o * r * gg_ref[...]).astype(BF16)

    out = jax.ShapeDtypeStruct((S, AX_W), BF16)
    return pl.pallas_call(
        body, name="ax_fwd", grid=(S // bq, nk),
        in_specs=[pl.BlockSpec((bq, AX_W), lambda i, j: (i, 0)),
                  pl.BlockSpec((bk, AX_KV_W), lambda i, j: (j, 0)),
                  pl.BlockSpec((bk, AX_KV_W), lambda i, j: (j, OFF_VC // AX_KV_W)),
                  _const_spec(gg.shape)],
        out_specs=[pl.BlockSpec((bq, AX_W), lambda i, j: (i, 0)),
                   pl.BlockSpec((bq, AX_W), lambda i, j: (i, 0)),
                   pl.BlockSpec((AX_HEADS, bq, LANES), lambda i, j: (0, i, 0))],
        out_shape=[out, out, jax.ShapeDtypeStruct((AX_HEADS, S, LANES), F32)],
        scratch_shapes=[pltpu.VMEM((AX_HEADS, bq, 1), F32), pltpu.VMEM((AX_HEADS, bq, 1), F32),
                        pltpu.VMEM((bq, AX_W), F32)],
        compiler_params=_params("parallel", "arbitrary"),
    )(qc, kc, proj, gg)


def _ax_ds(h, q_ref, k_ref, v_ref, do_ref, o_ref, lse_ref):
    rep = AX_HEADS // AX_KV_HEADS
    g = h // rep
    sl = slice(h * HEAD_DIM, (h + 1) * HEAD_DIM)
    gs = slice(g * HEAD_DIM, (g + 1) * HEAD_DIM)
    qh = q_ref[:, sl]
    doh = do_ref[:, sl]
    s = lax.dot_general(qh, k_ref[:, gs], NT, preferred_element_type=F32)
    p = jnp.exp(s - lse_ref[h][:, 0:1])
    dp = lax.dot_general(doh, v_ref[:, gs], NT, preferred_element_type=F32)
    delta = jnp.sum(doh.astype(F32) * o_ref[:, sl].astype(F32), axis=-1, keepdims=True)
    return qh, doh, p, p * (dp - delta), sl, gs


def _ax_bwd_dq(qc, kc, proj, dy, raw, lse):
    S = qc.shape[0]
    bq, bk = _ax_blocks(S)
    nk = S // bk

    def body(q_ref, k_ref, v_ref, do_ref, o_ref, lse_ref, dq_ref):
        kv = pl.program_id(1)

        @pl.when(kv == 0)
        def _():
            dq_ref[...] = jnp.zeros_like(dq_ref)

        for h in range(AX_HEADS):
            _, _, _, ds, sl, gs = _ax_ds(h, q_ref, k_ref, v_ref, do_ref, o_ref, lse_ref)
            dq_ref[:, sl] += jnp.dot(ds.astype(BF16), k_ref[:, gs], preferred_element_type=F32)

        @pl.when(kv == nk - 1)
        def _():
            dq_ref[...] = dq_ref[...] * QK_SCALE

    qspec = pl.BlockSpec((bq, AX_W), lambda i, j: (i, 0))
    return pl.pallas_call(
        body, name="ax_bwd_dq", grid=(S // bq, nk),
        in_specs=[qspec, pl.BlockSpec((bk, AX_KV_W), lambda i, j: (j, 0)),
                  pl.BlockSpec((bk, AX_KV_W), lambda i, j: (j, OFF_VC // AX_KV_W)),
                  qspec, qspec, pl.BlockSpec((AX_HEADS, bq, LANES), lambda i, j: (0, i, 0))],
        out_specs=qspec, out_shape=jax.ShapeDtypeStruct((S, AX_W), F32),
        compiler_params=_params("parallel", "arbitrary"),
    )(qc, kc, proj, dy, raw, lse)


def _ax_bwd_dkv(qc, kc, proj, dy, raw, lse):
    S = qc.shape[0]
    bq, bk = _ax_blocks(S)

    def body(q_ref, k_ref, v_ref, do_ref, o_ref, lse_ref, dk_ref, dv_ref):
        @pl.when(pl.program_id(1) == 0)
        def _():
            dk_ref[...] = jnp.zeros_like(dk_ref)
            dv_ref[...] = jnp.zeros_like(dv_ref)

        for h in range(AX_HEADS):
            qh, doh, p, ds, sl, gs = _ax_ds(h, q_ref, k_ref, v_ref, do_ref, o_ref, lse_ref)
            dv_ref[:, gs] += lax.dot_general(p.astype(BF16), doh, TN, preferred_element_type=F32)
            dk_ref[:, gs] += lax.dot_general(ds.astype(BF16), qh, TN, preferred_element_type=F32)

    qspec = pl.BlockSpec((bq, AX_W), lambda j, i: (i, 0))
    kspec = pl.BlockSpec((bk, AX_KV_W), lambda j, i: (j, 0))
    out = jax.ShapeDtypeStruct((S, AX_KV_W), F32)
    return pl.pallas_call(
        body, name="ax_bwd_dkv", grid=(S // bk, S // bq),
        in_specs=[qspec, kspec, pl.BlockSpec((bk, AX_KV_W), lambda j, i: (j, OFF_VC // AX_KV_W)),
                  qspec, qspec, pl.BlockSpec((AX_HEADS, bq, LANES), lambda j, i: (0, i, 0))],
        out_specs=[kspec, kspec], out_shape=[out, out],
        compiler_params=_params("parallel", "arbitrary"),
    )(qc, kc, proj, dy, raw, lse)


def _oproj_fwd(x, yna, ynb, ync, w, gt):
    S, D = x.shape
    tm = _tile(S, 512)

    def body(x_ref, a_ref, b_ref, c_ref, w_ref, gt_ref, x1_ref, ao_ref, yn_ref):
        yn_ref[:, 0:NA_W] = a_ref[...]
        yn_ref[:, NA_W:NA_W + SW_W] = b_ref[...]
        yn_ref[:, NA_W + SW_W:MIX_WIDTH] = c_ref[...]
        acc = jnp.dot(yn_ref[...], w_ref[...], preferred_element_type=F32)
        ao_ref[...] = acc.astype(BF16)
        x1_ref[...] = x_ref[...] + gt_ref[...] * acc

    return pl.pallas_call(
        body, name="oproj_fwd", grid=(S // tm,),
        in_specs=[_row_spec(tm, D), _row_spec(tm, NA_W), _row_spec(tm, SW_W), _row_spec(tm, AX_W),
                  _const_spec(w.shape), _const_spec((1, D))],
        out_specs=[_row_spec(tm, D), _row_spec(tm, D), _row_spec(tm, MIX_WIDTH)],
        out_shape=[jax.ShapeDtypeStruct((S, D), F32), jax.ShapeDtypeStruct((S, D), BF16),
                   jax.ShapeDtypeStruct((S, MIX_WIDTH), BF16)],
        compiler_params=_params("parallel"),
    )(x, yna, ynb, ync, w, gt)


def _gu_fwd(x, g, sc, sh, w):
    S, D = x.shape
    F2 = w.shape[1]
    tn = F2 // 4
    tm = _tile(S, 512)

    def body(x_ref, g_ref, sc_ref, sh_ref, w_ref, h_ref, gu_ref, act_ref):
        @pl.when(pl.program_id(1) == 0)
        def _():
            h_ref[...] = _ln_mod(x_ref[...], g_ref[...], sc_ref[...], sh_ref[...]).astype(BF16)

        acc = jnp.dot(h_ref[...], w_ref[...], preferred_element_type=F32)
        gu_ref[...] = acc.astype(BF16)
        gate, up = acc[:, :tn], acc[:, tn:]
        act_ref[...] = (gate * (1.0 / (1.0 + jnp.exp(-gate))) * up).astype(BF16)

    vec = pl.BlockSpec((1, D), lambda i, j: (0, 0))
    return pl.pallas_call(
        body, name="gu_fwd", grid=(S // tm, 2),
        in_specs=[pl.BlockSpec((tm, D), lambda i, j: (i, 0)), vec, vec, vec,
                  pl.BlockSpec((D, 2 * tn), lambda i, j: (0, j))],
        out_specs=[pl.BlockSpec((tm, D), lambda i, j: (i, 0)), pl.BlockSpec((tm, 2 * tn), lambda i, j: (i, j)),
                   pl.BlockSpec((tm, tn), lambda i, j: (i, j))],
        out_shape=[jax.ShapeDtypeStruct((S, D), BF16), jax.ShapeDtypeStruct((S, F2), BF16),
                   jax.ShapeDtypeStruct((S, F2 // 2), BF16)],
        compiler_params=_params("parallel", "arbitrary"),
    )(x, g, sc, sh, w)


def _down_fwd(x, act, w, gt):
    S, D = x.shape
    F = act.shape[1]
    tm = _tile(S, 512)

    def body(x_ref, a_ref, w_ref, gt_ref, x2_ref, fo_ref):
        acc = jnp.dot(a_ref[...], w_ref[...], preferred_element_type=F32)
        fo_ref[...] = acc.astype(BF16)
        x2_ref[...] = x_ref[...] + gt_ref[...] * acc

    return pl.pallas_call(
        body, name="down_fwd", grid=(S // tm,),
        in_specs=[_row_spec(tm, D), _row_spec(tm, F), _const_spec(w.shape), _const_spec((1, D))],
        out_specs=[_row_spec(tm, D), _row_spec(tm, D)],
        out_shape=[jax.ShapeDtypeStruct((S, D), F32), jax.ShapeDtypeStruct((S, D), BF16)],
        compiler_params=_params("parallel"),
    )(x, act, w, gt)


def _final_loss(x, g, target):
    S, D = x.shape
    tm = _tile(S, 512)

    def body(x_ref, g_ref, t_ref, dx_ref, loss_ref, dg_ref):
        @pl.when(pl.program_id(0) == 0)
        def _():
            loss_ref[...] = jnp.zeros_like(loss_ref)
            dg_ref[...] = jnp.zeros_like(dg_ref)

        xv = x_ref[...]
        r = _rsq(jnp.mean(xv * xv, axis=-1, keepdims=True))
        xhat = xv * r
        err = xhat * g_ref[...] - t_ref[...]
        loss_ref[...] += 0.5 * jnp.sum(jnp.mean(err * err, axis=-1, keepdims=True), axis=0, keepdims=True)
        dy = err * (1.0 / D)
        dg_ref[...] += jnp.sum(dy * xhat, axis=0, keepdims=True)
        dxh = dy * g_ref[...]
        dx_ref[...] = r * (dxh - xhat * jnp.mean(dxh * xhat, axis=-1, keepdims=True))

    return pl.pallas_call(
        body, name="final_loss", grid=(S // tm,),
        in_specs=[_row_spec(tm, D), _const_spec((1, D)), _row_spec(tm, D)],
        out_specs=[_row_spec(tm, D), _const_spec((1, LANES)), _const_spec((1, D))],
        out_shape=[jax.ShapeDtypeStruct((S, D), F32), jax.ShapeDtypeStruct((1, LANES), F32),
                   jax.ShapeDtypeStruct((1, D), F32)],
        compiler_params=_params("arbitrary"),
    )(x, g, target)


def _ffn_bwd1(dx2, fo, gt, w_down, gu):
    S, D = dx2.shape
    F2 = gu.shape[1]
    tn = F2 // 4
    tm = _tile(S, 512)

    def body(dx_ref, fo_ref, gt_ref, w_ref, gu_ref, dfo_ref, dgu_ref, dgt_ref):
        i, j = pl.program_id(0), pl.program_id(1)

        @pl.when((i == 0) & (j == 0))
        def _():
            dgt_ref[...] = jnp.zeros_like(dgt_ref)

        @pl.when(j == 0)
        def _():
            dxv = dx_ref[...]
            dfo_ref[...] = (dxv * gt_ref[...]).astype(BF16)
            dgt_ref[...] += jnp.sum(dxv * fo_ref[...].astype(F32), axis=0, keepdims=True)

        dact = lax.dot_general(dfo_ref[...], w_ref[...], NT, preferred_element_type=F32)
        gate = gu_ref[:, :tn].astype(F32)
        up = gu_ref[:, tn:].astype(F32)
        sig = 1.0 / (1.0 + jnp.exp(-gate))
        dgu_ref[:, :tn] = (dact * up * (sig * (1.0 + gate * (1.0 - sig)))).astype(BF16)
        dgu_ref[:, tn:] = (dact * (gate * sig)).astype(BF16)

    vec = pl.BlockSpec((1, D), lambda i, j: (0, 0))
    row = pl.BlockSpec((tm, D), lambda i, j: (i, 0))
    return pl.pallas_call(
        body, name="ffn_bwd1", grid=(S // tm, 2),
        in_specs=[row, row, vec, pl.BlockSpec((tn, D), lambda i, j: (j, 0)),
                  pl.BlockSpec((tm, 2 * tn), lambda i, j: (i, j))],
        out_specs=[row, pl.BlockSpec((tm, 2 * tn), lambda i, j: (i, j)), vec],
        out_shape=[jax.ShapeDtypeStruct((S, D), BF16), jax.ShapeDtypeStruct((S, F2), BF16),
                   jax.ShapeDtypeStruct((1, D), F32)],
        compiler_params=_params("arbitrary", "arbitrary"),
    )(dx2, fo, gt, w_down, gu)


def _nt_ln_bwd(a, w, x, g, sc, dres, name):
    S, D = x.shape
    K = a.shape[1]
    tm = _tile(S, 256)

    def body(a_ref, w_ref, x_ref, g_ref, sc_ref, dres_ref, dx_ref, dsh_ref, dsc_ref, dg_ref):
        @pl.when(pl.program_id(0) == 0)
        def _():
            dsh_ref[...] = jnp.zeros_like(dsh_ref)
            dsc_ref[...] = jnp.zeros_like(dsc_ref)
            dg_ref[...] = jnp.zeros_like(dg_ref)

        dh = lax.dot_general(a_ref[...], w_ref[...], NT, preferred_element_type=F32)
        xv = x_ref[...]
        r = _rsq(jnp.mean(xv * xv, axis=-1, keepdims=True))
        xhat = xv * r
        gv = g_ref[...]
        dsh_ref[...] += jnp.sum(dh, axis=0, keepdims=True)
        dsc_ref[...] += jnp.sum(dh * (xhat * gv), axis=0, keepdims=True)
        dn = dh * (1.0 + sc_ref[...])
        dg_ref[...] += jnp.sum(dn * xhat, axis=0, keepdims=True)
        dxh = dn * gv
        dx_ref[...] = dres_ref[...] + r * (dxh - xhat * jnp.mean(dxh * xhat, axis=-1, keepdims=True))

    vec = _const_spec((1, D))
    vshape = jax.ShapeDtypeStruct((1, D), F32)
    return pl.pallas_call(
        body, name=name, grid=(S // tm,),
        in_specs=[_row_spec(tm, K), _const_spec(w.shape), _row_spec(tm, D), vec, vec, _row_spec(tm, D)],
        out_specs=[_row_spec(tm, D), vec, vec, vec],
        out_shape=[jax.ShapeDtypeStruct((S, D), F32), vshape, vshape, vshape],
        compiler_params=_params("arbitrary"),
    )(a, w, x, g, sc, dres)


def _oproj_bwd(dx1, ao, gt, w, ya, yb, yc, gg):
    S, D = dx1.shape
    tm = _tile(S, 512)
    groups = ((0, NA_W), (NA_W, SW_W), (NA_W + SW_W, AX_W))

    def body(dx_ref, ao_ref, gt_ref, w_ref, ya_ref, yb_ref, yc_ref, gg_ref,
             dao_ref, dya_ref, dyb_ref, dyc_ref, dgt_ref, dgg_ref):
        @pl.when(pl.program_id(0) == 0)
        def _():
            dgt_ref[...] = jnp.zeros_like(dgt_ref)
            dgg_ref[...] = jnp.zeros_like(dgg_ref)

        dxv = dx_ref[...]
        dao = (dxv * gt_ref[...]).astype(BF16)
        dao_ref[...] = dao
        dgt_ref[...] += jnp.sum(dxv * ao_ref[...].astype(F32), axis=0, keepdims=True)
        dyn = lax.dot_general(dao, w_ref[...], NT, preferred_element_type=F32)
        for (off, wd), y_ref, dy_ref in zip(groups, (ya_ref, yb_ref, yc_ref), (dya_ref, dyb_ref, dyc_ref)):
            y = y_ref[...].astype(F32)
            d = dyn[:, off:off + wd]
            r = _rsq(jnp.mean(y * y, axis=-1, keepdims=True))
            yhat = y * r
            dgg_ref[:, off:off + wd] += jnp.sum(d * yhat, axis=0, keepdims=True)
            dyh = d * gg_ref[:, off:off + wd]
            dy_ref[...] = (r * (dyh - yhat * jnp.mean(dyh * yhat, axis=-1, keepdims=True))).astype(BF16)

    vec = _const_spec((1, D))
    mvec = _const_spec((1, MIX_WIDTH))
    return pl.pallas_call(
        body, name="oproj_bwd", grid=(S // tm,),
        in_specs=[_row_spec(tm, D), _row_spec(tm, D), vec, _const_spec(w.shape),
                  _row_spec(tm, NA_W), _row_spec(tm, SW_W), _row_spec(tm, AX_W), mvec],
        out_specs=[_row_spec(tm, D), _row_spec(tm, NA_W), _row_spec(tm, SW_W), _row_spec(tm, AX_W), vec, mvec],
        out_shape=[jax.ShapeDtypeStruct((S, D), BF16), jax.ShapeDtypeStruct((S, NA_W), BF16),
                   jax.ShapeDtypeStruct((S, SW_W), BF16), jax.ShapeDtypeStruct((S, AX_W), BF16),
                   jax.ShapeDtypeStruct((1, D), F32), jax.ShapeDtypeStruct((1, MIX_WIDTH), F32)],
        compiler_params=_params("arbitrary"),
    )(dx1, ao, gt, w, ya, yb, yc, gg)


def _dproj_assemble(proj, na, sw, ax, gq128, gk128, rope):
    S = proj.shape[0]
    tm = _tile(S, 512)
    cos, sa, sb = rope

    def body(proj_ref, qa, ka, kah, va, vah, qb, kb, kbh, vb, vbh, qc, kc, vc,
             gq_ref, gk_ref, cos_ref, sa_ref, sb_ref, out_ref, dgq_ref, dgk_ref):
        @pl.when(pl.program_id(0) == 0)
        def _():
            dgq_ref[...] = jnp.zeros_like(dgq_ref)
            dgk_ref[...] = jnp.zeros_like(dgk_ref)

        out_ref[:, OFF_QA:OFF_KA] = qa[...].astype(BF16)
        out_ref[:, OFF_KA:OFF_VA] = (ka[...] + kah[...]).astype(BF16)
        out_ref[:, OFF_VA:OFF_QB] = (va[...] + vah[...]).astype(BF16)
        out_ref[:, OFF_QB:OFF_KB] = qb[...].astype(BF16)
        out_ref[:, OFF_KB:OFF_VB] = (kb[...] + kbh[...]).astype(BF16)
        out_ref[:, OFF_VB:OFF_QC] = (vb[...] + vbh[...]).astype(BF16)
        c, a, b = cos_ref[...], sa_ref[...], sb_ref[...]
        for j in range(AX_W // LANES):
            cols = slice(OFF_QC + j * LANES, OFF_QC + (j + 1) * LANES)
            dx, dg = _qk_prep_bwd_chunk(proj_ref[:, cols].astype(F32), qc[:, j * LANES:(j + 1) * LANES],
                                        gq_ref[...], c, a, b)
            out_ref[:, cols] = dx.astype(BF16)
            dgq_ref[...] += dg
        for j in range(AX_KV_W // LANES):
            cols = slice(OFF_KC + j * LANES, OFF_KC + (j + 1) * LANES)
            dx, dg = _qk_prep_bwd_chunk(proj_ref[:, cols].astype(F32), kc[:, j * LANES:(j + 1) * LANES],
                                        gk_ref[...], c, a, b)
            out_ref[:, cols] = dx.astype(BF16)
            dgk_ref[...] += dg
        out_ref[:, OFF_VC:IN_WIDTH] = vc[...].astype(BF16)

    v128 = _const_spec((1, LANES))
    r = lambda w: _row_spec(tm, w)
    return pl.pallas_call(
        body, name="dproj_assemble", grid=(S // tm,),
        in_specs=[r(IN_WIDTH), r(NA_W), r(NA_W), r(NA_W), r(NA_W), r(NA_W),
                  r(SW_W), r(SW_KV_W), r(SW_KV_W), r(SW_KV_W), r(SW_KV_W),
                  r(AX_W), r(AX_KV_W), r(AX_KV_W), v128, v128, r(LANES), r(LANES), r(LANES)],
        out_specs=[r(IN_WIDTH), v128, v128],
        out_shape=[jax.ShapeDtypeStruct((S, IN_WIDTH), BF16), jax.ShapeDtypeStruct((1, LANES), F32),
                   jax.ShapeDtypeStruct((1, LANES), F32)],
        compiler_params=_params("arbitrary"),
    )(proj, *na, *sw, *ax, gq128, gk128, cos, sa, sb)


def _tn_matmul(a, b, name):
    S, Ka = a.shape
    Nb = b.shape[1]
    tm = _tile(Ka, 1408, LANES)
    tn = _tile(Nb, 1408, LANES)
    tk = _tile(S, 512)
    nk = S // tk

    def body(a_ref, b_ref, o_ref):
        @pl.when(pl.program_id(2) == 0)
        def _():
            o_ref[...] = jnp.zeros_like(o_ref)

        o_ref[...] += lax.dot_general(a_ref[...], b_ref[...], TN, preferred_element_type=F32)

    return pl.pallas_call(
        body, name=name, grid=(Ka // tm, Nb // tn, nk),
        in_specs=[pl.BlockSpec((tk, tm), lambda i, j, k: (k, i)), pl.BlockSpec((tk, tn), lambda i, j, k: (k, j))],
        out_specs=pl.BlockSpec((tm, tn), lambda i, j, k: (i, j)),
        out_shape=jax.ShapeDtypeStruct((Ka, Nb), F32),
        compiler_params=_params("parallel", "parallel", "arbitrary"),
    )(a, b)


def _na_index(bd):
    rq = jnp.arange(bd.bq // GRID_W)
    rk = jnp.arange(bd.bk // GRID_W)
    col = jnp.arange(GRID_W)
    ri = jnp.clip(rk[None, :] - rq[:, None] - bd.halo // GRID_W + NA_WIN_ROWS - 1, 0, 2 * NA_WIN_ROWS - 2)
    ci = jnp.clip(col[None, :] - col[:, None] + NA_WIN_COLS - 1, 0, 2 * NA_WIN_COLS - 2)
    return ri, ci


def _na_one_hots(bd):
    ri, ci = _na_index(bd)
    oh_r = jax.nn.one_hot(ri, 2 * NA_WIN_ROWS - 1, dtype=F32)
    oh_c = jax.nn.one_hot(ci, 2 * NA_WIN_COLS - 1, dtype=F32)
    return oh_r, oh_c


def _na_bias(bd, rpb):
    oh_r, oh_c = _na_one_hots(bd)
    t = jnp.einsum("hab,qra->hqrb", rpb, oh_r, precision=lax.Precision.HIGHEST)
    b = jnp.einsum("hqrb,ckb->hqcrk", t, oh_c, precision=lax.Precision.HIGHEST)
    return b.reshape(NA_HEADS, bd.bq, bd.bk)


def _na_bias_t(bd, dbias):
    oh_r, oh_c = _na_one_hots(bd)
    d5 = dbias.reshape(NA_HEADS, bd.bq // GRID_W, GRID_W, bd.bk // GRID_W, GRID_W)
    t = jnp.einsum("hqcrk,ckb->hqrb", d5, oh_c, precision=lax.Precision.HIGHEST)
    return jnp.einsum("hqrb,qra->hab", t, oh_r, precision=lax.Precision.HIGHEST)


def _t5_bucket(rel):
    nb = T5_BUCKETS // 2
    ret = (rel > 0).astype(I32) * nb
    n = jnp.abs(rel)
    max_exact = nb // 2
    nf = jnp.maximum(n, max_exact).astype(F32)
    large = max_exact + (jnp.log(nf / max_exact) / math.log(T5_MAX_DIST / max_exact)
                         * (nb - max_exact)).astype(I32)
    large = jnp.minimum(large, nb - 1)
    return ret + jnp.where(n < max_exact, n, large)


def _sw_bucket(bd):
    rel = (jnp.arange(bd.bk) - bd.halo)[None, :] - jnp.arange(bd.bq)[:, None]
    return _t5_bucket(rel)


def _sw_bias(bd, t5):
    oh = jax.nn.one_hot(_sw_bucket(bd), T5_BUCKETS, dtype=F32)
    return jnp.einsum("bh,qkb->hqk", t5, oh, precision=lax.Precision.HIGHEST)


def _sw_bias_t(bd, dbias):
    oh = jax.nn.one_hot(_sw_bucket(bd), T5_BUCKETS, dtype=F32)
    return jnp.einsum("hqk,qkb->bh", dbias, oh, precision=lax.Precision.HIGHEST)


def _local_step(x, target, mod, w_in, w_o, w_gu, w_down, g_attn, rpb_na, sink_sw, t5_table, gq_ax, gk_ax,
                g_group, g_ffn, g_final):
    S, D = x.shape
    rope = _rope_tables(S)
    na, sw = _Band("na", S), _Band("sw", S)
    two = lambda v: jnp.concatenate([v, v])[None, :]
    sw_bias = _sw_bias(sw, t5_table)
    saved = []
    for l in range(DEPTH):
        sh_a, sc_a, gt_a, sh_f, sc_f, gt_f = [mod[l, k * D:(k + 1) * D][None, :] for k in range(6)]
        gq128, gk128 = two(gq_ax[l]), two(gk_ax[l])
        gg = g_group[l][None, :]
        sink = jnp.pad(sink_sw[l], (0, LANES - SW_HEADS))[None, :]
        na_bias = _na_bias(na, rpb_na[l])
        h, proj, qc, kc = _inproj_fwd(x, g_attn[l][None, :], sc_a, sh_a, w_in[l], gq128, gk128, rope)
        ya, yna = _band_fwd(na, proj, na_bias, None, gg[:, :NA_W])
        yb, ynb = _band_fwd(sw, proj, sw_bias, sink, gg[:, NA_W:NA_W + SW_W])
        yc, ync, lse = _ax_fwd(qc, kc, proj, gg[:, NA_W + SW_W:])
        x1, ao, yn = _oproj_fwd(x, yna, ynb, ync, w_o[l], gt_a)
        hf, gu, act = _gu_fwd(x1, g_ffn[l][None, :], sc_f, sh_f, w_gu[l])
        x2, fo = _down_fwd(x1, act, w_down[l], gt_f)
        saved.append(dict(x=x, x1=x1, h=h, proj=proj, qc=qc, kc=kc, ya=ya, yb=yb, yc=yc, lse=lse, ao=ao, yn=yn,
                          hf=hf, gu=gu, act=act, fo=fo, na_bias=na_bias, sink=sink, gq128=gq128, gk128=gk128,
                          gg=gg, mods=(sh_a, sc_a, gt_a, sh_f, sc_f, gt_f)))
        x = x2

    dx, loss_row, dg_final = _final_loss(x, g_final[None, :], target)
    gw = {k: [None] * DEPTH for k in ("w_in", "w_o", "w_gu", "w_down")}
    gs = {k: [None] * DEPTH for k in ("b_mod", "g_attn", "rpb_na", "sink_sw", "gq_ax", "gk_ax", "g_group", "g_ffn")}
    d_t5 = jnp.zeros((T5_BUCKETS, SW_HEADS), F32)
    for l in reversed(range(DEPTH)):
        s = saved[l]
        sh_a, sc_a, gt_a, sh_f, sc_f, gt_f = s["mods"]
        dfo, dgu, dgt_f = _ffn_bwd1(dx, s["fo"], gt_f, w_down[l], s["gu"])
        gw["w_down"][l] = _tn_matmul(s["act"], dfo, "dw_down")
        gw["w_gu"][l] = _tn_matmul(s["hf"], dgu, "dw_gu")
        dx1, dsh_f, dsc_f, gs["g_ffn"][l] = _nt_ln_bwd(dgu, w_gu[l], s["x1"], g_ffn[l][None, :], sc_f, dx, "ffn_bwd2")
        dao, dya, dyb, dyc, dgt_a, gs["g_group"][l] = _oproj_bwd(dx1, s["ao"], gt_a, w_o[l], s["ya"], s["yb"],
                                                                 s["yc"], s["gg"])
        gw["w_o"][l] = _tn_matmul(s["yn"], dao, "dw_o")
        dqa, dka, dva, dkap, dvap, dkan, dvan, dbias_na = _band_bwd(na, s["proj"], s["na_bias"], None, dya)
        dqb, dkb, dvb, dkbp, dvbp, dkbn, dvbn, dbias_sw, dsink = _band_bwd(sw, s["proj"], sw_bias, s["sink"], dyb)
        dqc = _ax_bwd_dq(s["qc"], s["kc"], s["proj"], dyc, s["yc"], s["lse"])
        dkc, dvc = _ax_bwd_dkv(s["qc"], s["kc"], s["proj"], dyc, s["yc"], s["lse"])
        dproj, dgq, dgk = _dproj_assemble(
            s["proj"], (dqa, dka, _halo_to_rows(dkap, dkan), dva, _halo_to_rows(dvap, dvan)),
            (dqb, dkb, _halo_to_rows(dkbp, dkbn), dvb, _halo_to_rows(dvbp, dvbn)), (dqc, dkc, dvc),
            s["gq128"], s["gk128"], rope)
        gw["w_in"][l] = _tn_matmul(s["h"], dproj, "dw_in")
        dx, dsh_a, dsc_a, gs["g_attn"][l] = _nt_ln_bwd(dproj, w_in[l], s["x"], g_attn[l][None, :], sc_a, dx1,
                                                       "inproj_bwd")
        gs["b_mod"][l] = jnp.concatenate([dsh_a, dsc_a, dgt_a, dsh_f, dsc_f, dgt_f], axis=1)[0]
        gs["rpb_na"][l] = _na_bias_t(na, dbias_na)
        gs["sink_sw"][l] = dsink[0, :SW_HEADS]
        d_t5 = d_t5 + _sw_bias_t(sw, dbias_sw)
        gs["gq_ax"][l] = dgq[0, :HEAD_DIM] + dgq[0, HEAD_DIM:]
        gs["gk_ax"][l] = dgk[0, :HEAD_DIM] + dgk[0, HEAD_DIM:]
        gs["g_attn"][l] = gs["g_attn"][l][0]
        gs["g_ffn"][l] = gs["g_ffn"][l][0]
        gs["g_group"][l] = gs["g_group"][l][0]

    gw = {k: jnp.stack(v) for k, v in gw.items()}
    small = {k: jnp.stack(v) for k, v in gs.items()}
    small["t5_table"] = d_t5
    small["g_final"] = dg_final[0]
    return loss_row[0, 0], dx, gw, small


MOD_ROWS = 16


def _mod_fwd(cond16, w):
    L, D, C = w.shape
    tn = _tile(C, 512, LANES)

    def body(c_ref, w_ref, o_ref):
        o_ref[0] = jnp.dot(c_ref[...], w_ref[0].astype(BF16), preferred_element_type=F32)

    return pl.pallas_call(
        body, name="mod_fwd", grid=(L, C // tn),
        in_specs=[pl.BlockSpec((MOD_ROWS, D), lambda l, j: (0, 0)), pl.BlockSpec((1, D, tn), lambda l, j: (l, 0, j))],
        out_specs=pl.BlockSpec((1, MOD_ROWS, tn), lambda l, j: (l, 0, j)),
        out_shape=jax.ShapeDtypeStruct((L, MOD_ROWS, C), F32),
        compiler_params=_params("parallel", "parallel"),
    )(cond16, w)


def _adamw_math(w, g, m, v):
    m = ADAM_B1 * m + (1.0 - ADAM_B1) * g
    v = ADAM_B2 * v + (1.0 - ADAM_B2) * (g * g)
    m_hat = m / (1.0 - ADAM_B1 ** ADAM_STEP)
    v_hat = v / (1.0 - ADAM_B2 ** ADAM_STEP)
    delta = -ADAM_LR * (m_hat / (jnp.sqrt(v_hat) + ADAM_EPS) + ADAM_WD * w)
    return delta, m, v


def _adamw(w, m, v, parts, name):
    R, C = w.shape
    tr = _tile(R, 256)
    n = len(parts)

    def body(*refs):
        w_ref, m_ref, v_ref = refs[:3]
        g = refs[3][...]
        for p in refs[4:3 + n]:
            g = g + p[...]
        g_ref, d_ref, m2_ref, v2_ref = refs[3 + n:]
        g_ref[...] = g
        d_ref[...], m2_ref[...], v2_ref[...] = _adamw_math(w_ref[...], g, m_ref[...], v_ref[...])

    spec = _row_spec(tr, C)
    shape = jax.ShapeDtypeStruct((R, C), F32)
    return pl.pallas_call(
        body, name=name, grid=(R // tr,), in_specs=[spec] * (3 + n), out_specs=[spec] * 4, out_shape=[shape] * 4,
        compiler_params=_params("parallel"),
    )(w, m, v, *parts)


def _wmod_adamw(cond_t, dmod16, w, m, v):
    L, D, C = w.shape
    tr = _tile(D, 256)

    def body(c_ref, d_ref, w_ref, m_ref, v_ref, g_ref, dl_ref, m2_ref, v2_ref):
        g = jnp.dot(c_ref[...], d_ref[0], preferred_element_type=F32)
        g_ref[0] = g
        dl_ref[0], m2_ref[0], v2_ref[0] = _adamw_math(w_ref[0], g, m_ref[0], v_ref[0])

    spec = pl.BlockSpec((1, tr, C), lambda l, i: (l, i, 0))
    shape = jax.ShapeDtypeStruct((L, D, C), F32)
    return pl.pallas_call(
        body, name="wmod_adamw", grid=(L, D // tr),
        in_specs=[pl.BlockSpec((tr, MOD_ROWS), lambda l, i: (i, 0)),
                  pl.BlockSpec((1, MOD_ROWS, C), lambda l, i: (l, 0, 0)), spec, spec, spec],
        out_specs=[spec] * 4, out_shape=[shape] * 4,
        compiler_params=_params("parallel", "parallel"),
    )(cond_t, dmod16, w, m, v)


def _sum_slots(a):
    P, R, C = a.shape
    tr = _tile(R, 256)

    def body(a_ref, o_ref):
        s = a_ref[0]
        for k in range(1, P):
            s = s + a_ref[k]
        o_ref[...] = s

    return pl.pallas_call(
        body, name="sum_slots", grid=(R // tr,),
        in_specs=[pl.BlockSpec((P, tr, C), lambda i: (0, i, 0))], out_specs=_row_spec(tr, C),
        out_shape=jax.ShapeDtypeStruct((R, C), F32), compiler_params=_params("parallel"),
    )(a)


def _axes():
    return lax.axis_index("x"), lax.axis_index("y"), lax.axis_index("c")


def _allgather_devices(v):
    N = v.shape[1]

    def body(v_ref, out_ref, send_sems, recv_sems, local_sem):
        x, y, c = _axes()

        def row(px, py, pc):
            return out_ref.at[pl.ds(4 * px + 2 * py + pc, 1), :]

        mine = pltpu.make_async_copy(v_ref, row(x, y, c), local_sem)
        mine.start()
        sends, recvs = [], []
        for k in range(1, N_DEV):
            peer = (x ^ (k >> 2), y ^ ((k >> 1) & 1), c ^ (k & 1))
            sems = dict(send_sem=send_sems.at[k - 1], recv_sem=recv_sems.at[k - 1], device_id=peer, device_id_type=MESH)
            sends.append(pltpu.make_async_remote_copy(src_ref=v_ref, dst_ref=row(x, y, c), **sems))
            recvs.append(pltpu.make_async_remote_copy(src_ref=v_ref, dst_ref=row(*peer), **sems))
        for cp in sends:
            cp.start()
        for cp in recvs:
            cp.wait_recv()
        for cp in sends:
            cp.wait_send()
        mine.wait()

    vmem = pl.BlockSpec(memory_space=pltpu.VMEM)
    return pl.pallas_call(
        body, name="allgather_devices", in_specs=[vmem], out_specs=vmem,
        out_shape=jax.ShapeDtypeStruct((N_DEV, N), v.dtype),
        scratch_shapes=[pltpu.SemaphoreType.DMA((N_DEV - 1,)), pltpu.SemaphoreType.DMA((N_DEV - 1,)),
                        pltpu.SemaphoreType.DMA],
        compiler_params=pltpu.CompilerParams(vmem_limit_bytes=VMEM_LIMIT_V7X),
    )(v)


def _chip_pos(order, px, py):
    return 2 * px + py if order == "natural" else 2 * py + px


def _block(ref, axis, pos, width):
    idx = [slice(None)] * len(ref.shape)
    idx[axis] = pl.ds(pl.multiple_of(pos * width, width), width)
    return ref.at[tuple(idx)]


def _chip_allgather(shards, axes, orders, name):
    n = len(shards)
    out_shapes = []
    for s, ax in zip(shards, axes):
        shp = list(s.shape)
        shp[ax] *= N_CHIPS
        out_shapes.append(jax.ShapeDtypeStruct(tuple(shp), s.dtype))

    def body(*refs):
        ins, outs = refs[:n], refs[n:2 * n]
        send_sems, recv_sems, local_sems = refs[2 * n:]
        x, y, c = _axes()
        place = lambda i, px, py: _block(outs[i], axes[i], _chip_pos(orders[i], px, py), shards[i].shape[axes[i]])
        local, sends, recvs = [], [], []
        for i in range(n):
            local.append(pltpu.make_async_copy(ins[i], place(i, x, y), local_sems.at[i]))
            for k in range(1, N_CHIPS):
                px, py = x ^ (k >> 1), y ^ (k & 1)
                j = i * (N_CHIPS - 1) + k - 1
                sems = dict(send_sem=send_sems.at[j], recv_sem=recv_sems.at[j], device_id=(px, py, c),
                            device_id_type=MESH)
                sends.append(pltpu.make_async_remote_copy(src_ref=ins[i], dst_ref=place(i, x, y), **sems))
                recvs.append(pltpu.make_async_remote_copy(src_ref=ins[i], dst_ref=place(i, px, py), **sems))
        for cp in local + sends:
            cp.start()
        for cp in recvs:
            cp.wait_recv()
        for cp in sends:
            cp.wait_send()
        for cp in local:
            cp.wait()

    hbm = pl.BlockSpec(memory_space=pl.ANY)
    nsem = n * (N_CHIPS - 1)
    return pl.pallas_call(
        body, name=name, in_specs=[hbm] * n, out_specs=[hbm] * n, out_shape=out_shapes,
        scratch_shapes=[pltpu.SemaphoreType.DMA((nsem,)), pltpu.SemaphoreType.DMA((nsem,)),
                        pltpu.SemaphoreType.DMA((n,))],
    )(*shards)


def _chip_scatter(grads, axes, orders, name):
    n = len(grads)
    widths, out_shapes = [], []
    for g, ax in zip(grads, axes):
        shp = list(g.shape)
        shp[ax] //= N_CHIPS
        widths.append(shp[ax])
        out_shapes.append(jax.ShapeDtypeStruct((N_CHIPS,) + tuple(shp), g.dtype))

    def body(*refs):
        ins, outs = refs[:n], refs[n:2 * n]
        send_sems, recv_sems, local_sems = refs[2 * n:]
        x, y, c = _axes()
        piece = lambda i, px, py: _block(ins[i], axes[i], _chip_pos(orders[i], px, py), widths[i])
        slot = lambda i, px, py: outs[i].at[2 * px + py]
        local, sends, recvs = [], [], []
        for i in range(n):
            local.append(pltpu.make_async_copy(piece(i, x, y), slot(i, x, y), local_sems.at[i]))
            for k in range(1, N_CHIPS):
                px, py = x ^ (k >> 1), y ^ (k & 1)
                j = i * (N_CHIPS - 1) + k - 1
                sems = dict(send_sem=send_sems.at[j], recv_sem=recv_sems.at[j], device_id=(px, py, c),
                            device_id_type=MESH)
                sends.append(pltpu.make_async_remote_copy(src_ref=piece(i, px, py), dst_ref=slot(i, x, y), **sems))
                recvs.append(pltpu.make_async_remote_copy(src_ref=piece(i, px, py), dst_ref=slot(i, px, py), **sems))
        for cp in local + sends:
            cp.start()
        for cp in recvs:
            cp.wait_recv()
        for cp in sends:
            cp.wait_send()
        for cp in local:
            cp.wait()

    hbm = pl.BlockSpec(memory_space=pl.ANY)
    nsem = n * (N_CHIPS - 1)
    return pl.pallas_call(
        body, name=name, in_specs=[hbm] * n, out_specs=[hbm] * n, out_shape=out_shapes,
        scratch_shapes=[pltpu.SemaphoreType.DMA((nsem,)), pltpu.SemaphoreType.DMA((nsem,)),
                        pltpu.SemaphoreType.DMA((n,))],
    )(*grads)


def _core_swap(arrays, name):
    n = len(arrays)

    def body(*refs):
        ins, outs = refs[:n], refs[n:2 * n]
        send_sems, recv_sems = refs[2 * n:]
        x, y, c = _axes()
        copies = [pltpu.make_async_remote_copy(src_ref=ins[i], dst_ref=outs[i], send_sem=send_sems.at[i],
                                               recv_sem=recv_sems.at[i], device_id=(x, y, 1 - c), device_id_type=MESH)
                  for i in range(n)]
        for cp in copies:
            cp.start()
        for cp in copies:
            cp.wait_recv()
        for cp in copies:
            cp.wait_send()

    hbm = pl.BlockSpec(memory_space=pl.ANY)
    return pl.pallas_call(
        body, name=name, in_specs=[hbm] * n, out_specs=[hbm] * n,
        out_shape=[jax.ShapeDtypeStruct(a.shape, a.dtype) for a in arrays],
        scratch_shapes=[pltpu.SemaphoreType.DMA((n,)), pltpu.SemaphoreType.DMA((n,))],
    )(*arrays)


SMALL = ("b_mod", "g_attn", "rpb_na", "sink_sw", "t5_table", "gq_ax", "gk_ax", "g_group", "g_ffn", "g_final")
BIG = ("w_in", "w_o", "w_gu", "w_down")
BIG_AXIS = {"w_in": 2, "w_o": 1, "w_gu": 2, "w_down": 1}
BIG_ORDER = {"w_in": "natural", "w_o": "natural", "w_gu": "gate_up_tiles", "w_down": "natural"}
WEIGHTS = ("w_mod", "b_mod", "g_attn", "w_in", "rpb_na", "sink_sw", "t5_table", "gq_ax", "gk_ax", "g_group",
           "w_o", "g_ffn", "w_gu", "w_down", "g_final")


def _pack(arrs):
    flat = jnp.concatenate([a.reshape(-1) for a in arrs])
    n = flat.shape[0]
    padded = -(-n // (8 * LANES)) * (8 * LANES)
    return jnp.pad(flat, (0, padded - n))


def _unpack(flat, like):
    out, off = [], 0
    for a in like:
        out.append(flat[off:off + a.size].reshape(a.shape))
        off += a.size
    return out


def kernel(x, c, w_mod, b_mod, g_attn, w_in, rpb_na, sink_sw, t5_table, gq_ax, gk_ax, g_group, w_o, g_ffn, w_gu, w_down, g_final, loss_target, m_w_mod, m_b_mod, m_g_attn, m_w_in, m_rpb_na, m_sink_sw, m_t5_table, m_gq_ax, m_gk_ax, m_g_group, m_w_o, m_g_ffn, m_w_gu, m_w_down, m_g_final, v_w_mod, v_b_mod, v_g_attn, v_w_in, v_rpb_na, v_sink_sw, v_t5_table, v_gq_ax, v_gk_ax, v_g_group, v_w_o, v_g_ffn, v_w_gu, v_w_down, v_g_final):
    W = dict(w_mod=w_mod, b_mod=b_mod, g_attn=g_attn, w_in=w_in, rpb_na=rpb_na, sink_sw=sink_sw, t5_table=t5_table,
             gq_ax=gq_ax, gk_ax=gk_ax, g_group=g_group, w_o=w_o, g_ffn=g_ffn, w_gu=w_gu, w_down=w_down,
             g_final=g_final)
    M = dict(w_mod=m_w_mod, b_mod=m_b_mod, g_attn=m_g_attn, w_in=m_w_in, rpb_na=m_rpb_na, sink_sw=m_sink_sw,
             t5_table=m_t5_table, gq_ax=m_gq_ax, gk_ax=m_gk_ax, g_group=m_g_group, w_o=m_w_o, g_ffn=m_g_ffn,
             w_gu=m_w_gu, w_down=m_w_down, g_final=m_g_final)
    V = dict(w_mod=v_w_mod, b_mod=v_b_mod, g_attn=v_g_attn, w_in=v_w_in, rpb_na=v_rpb_na, sink_sw=v_sink_sw,
             t5_table=v_t5_table, gq_ax=v_gq_ax, gk_ax=v_gk_ax, g_group=v_g_group, w_o=v_w_o, g_ffn=v_g_ffn,
             w_gu=v_w_gu, w_down=v_w_down, g_final=v_g_final)
    xi, yi, ci = _axes()
    me = 4 * xi + 2 * yi + ci
    chip = 2 * xi + yi
    D = x.shape[-1]
    mod_w = w_mod.shape[2]

    c_all = _allgather_devices(c)
    cond = c_all * (1.0 / (1.0 + jnp.exp(-c_all)))
    cond16 = jnp.pad(cond, ((0, MOD_ROWS - N_DEV), (0, 0))).astype(BF16)
    mod_part = _mod_fwd(cond16, w_mod)
    (mod_all,) = _chip_allgather([mod_part], [2], ["natural"], "allgather_mod")
    mod = lax.dynamic_slice_in_dim(mod_all, me, 1, axis=1)[:, 0, :] + b_mod

    w_full = _chip_allgather([W[k].astype(BF16) for k in BIG], [BIG_AXIS[k] for k in BIG],
                             [BIG_ORDER[k] for k in BIG], "allgather_weights")
    loss_part, grad_x, gw, small = _local_step(x[0], loss_target[0], mod, *w_full, g_attn, rpb_na, sink_sw, t5_table,
                                               gq_ax, gk_ax, g_group, g_ffn, g_final)

    small_all = _allgather_devices(_pack([small[k] for k in SMALL])[None, :])
    rows = small_all.shape[1] // LANES
    parts = [small_all[k].reshape(rows, LANES) for k in range(N_DEV)]
    pk = lambda d: _pack([d[k] for k in SMALL]).reshape(rows, LANES)
    small_out = [_unpack(o.reshape(-1), [W[k] for k in SMALL]) for o in _adamw(pk(W), pk(M), pk(V), parts, "adamw_small")]

    L = w_mod.shape[0]
    dmod_all = small_all[:, :L * 6 * D].reshape(N_DEV, L, 6 * D)
    dmod_mine = lax.dynamic_slice_in_dim(dmod_all, chip * mod_w, mod_w, axis=2)
    dmod16 = jnp.pad(jnp.transpose(dmod_mine, (1, 0, 2)), ((0, 0), (0, MOD_ROWS - N_DEV), (0, 0))).astype(BF16)
    wmod_out = _wmod_adamw(jnp.transpose(cond16), dmod16, w_mod, m_w_mod, v_w_mod)

    names = list(BIG)
    slots = _chip_scatter([gw[k] for k in names], [BIG_AXIS[k] for k in names], [BIG_ORDER[k] for k in names],
                          "scatter_grads")
    two_d = lambda a: a.reshape(-1, a.shape[-1])
    mine = [_sum_slots(s.reshape(N_CHIPS, -1, s.shape[-1])) for s in slots]
    theirs = _core_swap(mine, "swap_grads")
    big_out = {}
    for k, a, b in zip(names, mine, theirs):
        outs = _adamw(two_d(W[k]), two_d(M[k]), two_d(V[k]), [a, b], "adamw_" + k)
        big_out[k] = [o.reshape(W[k].shape) for o in outs]

    loss = lax.psum(loss_part, ("x", "y", "c"))
    per_kind = []
    for kind in range(4):
        for k in WEIGHTS:
            if k == "w_mod":
                per_kind.append(wmod_out[kind])
            elif k in big_out:
                per_kind.append(big_out[k][kind])
            else:
                per_kind.append(small_out[kind][SMALL.index(k)])
    return (loss, grad_x[None], *per_kind)
```

```python
import functools
import math

import jax
import jax.numpy as jnp
from jax import lax
from jax.experimental import pallas as pl
from jax.experimental.pallas import tpu as pltpu

F32 = jnp.float32
BF16 = jnp.bfloat16
I32 = jnp.int32

DEPTH = 2
HEAD_DIM = 64
GRID_W = 64
NA_HEADS = 4
SW_HEADS = 6
SW_KV_HEADS = 2
AX_HEADS = 6
AX_KV_HEADS = 2
NA_WIN_ROWS = 8
NA_WIN_COLS = 16
SW_RADIUS = 128
T5_BUCKETS = 32
T5_MAX_DIST = 128
ROPE_THETA = 10000.0
EPS = 1e-6
NEG_INF = -1e30
QK_SCALE = HEAD_DIM ** -0.5

NA_W = NA_HEADS * HEAD_DIM
SW_W = SW_HEADS * HEAD_DIM
SW_KV_W = SW_KV_HEADS * HEAD_DIM
AX_W = AX_HEADS * HEAD_DIM
AX_KV_W = AX_KV_HEADS * HEAD_DIM
OFF_QA, OFF_KA, OFF_VA = 0, NA_W, 2 * NA_W
OFF_QB = 3 * NA_W
OFF_KB = OFF_QB + SW_W
OFF_VB = OFF_KB + SW_KV_W
OFF_QC = OFF_VB + SW_KV_W
OFF_KC = OFF_QC + AX_W
OFF_VC = OFF_KC + AX_KV_W
IN_WIDTH = OFF_VC + AX_KV_W
MIX_WIDTH = NA_W + SW_W + AX_W

ADAM_LR = 0.001
ADAM_B1 = 0.9
ADAM_B2 = 0.999
ADAM_EPS = 1e-08
ADAM_WD = 0.01
ADAM_STEP = 10

N_CHIPS = 4
N_DEV = 8
LANES = 128
VMEM_LIMIT_V7X = 56 * 1024 * 1024
MESH = pl.DeviceIdType.MESH

NT = (((1,), (1,)), ((), ()))
TN = (((0,), (0,)), ((), ()))


def _params(*sem):
    return pltpu.CompilerParams(dimension_semantics=sem if sem else None,
                                vmem_limit_bytes=VMEM_LIMIT_V7X)


def _tile(n, pref, mult=8):
    t = (min(pref, n) // mult) * mult
    while t >= mult:
        if n % t == 0:
            return t
        t -= mult
    return n


def _row_spec(tm, width, col=0):
    return pl.BlockSpec((tm, width), lambda i, *_: (i, col))


def _const_spec(shape):
    nd = len(shape)
    return pl.BlockSpec(shape, lambda *_: (0,) * nd)


def _rsq(ms):
    return lax.rsqrt(ms + EPS)


def _rope_tables(S):
    t = jnp.arange(S)
    row = (t // GRID_W).astype(F32)
    col = (t % GRID_W).astype(F32)
    axis_dim = HEAD_DIM // 2
    freqs = ROPE_THETA ** (-jnp.arange(0, axis_dim, 2, dtype=F32) / axis_dim)
    ar = row[:, None] * freqs
    ac = col[:, None] * freqs
    z = jnp.zeros_like(ar)
    cos64 = jnp.concatenate([jnp.cos(ar), jnp.cos(ar), jnp.cos(ac), jnp.cos(ac)], axis=1)
    sa64 = jnp.concatenate([-jnp.sin(ar), z, -jnp.sin(ac), z], axis=1)
    sb64 = jnp.concatenate([z, jnp.sin(ar), z, jnp.sin(ac)], axis=1)
    two = lambda a: jnp.concatenate([a, a], axis=1)
    return two(cos64), two(sa64), two(sb64)


def _pair_sum(v):
    lane = lax.broadcasted_iota(I32, v.shape, 1)
    lo = lane < HEAD_DIM
    s_lo = jnp.sum(jnp.where(lo, v, 0.0), axis=-1, keepdims=True)
    s_hi = jnp.sum(jnp.where(lo, 0.0, v), axis=-1, keepdims=True)
    return jnp.where(lo, s_lo, s_hi)


def _rope(t, cos, sa, sb):
    return t * cos + pltpu.roll(t, LANES - 16, 1) * sa + pltpu.roll(t, 16, 1) * sb


def _rope_t(t, cos, sa, sb):
    return t * cos + pltpu.roll(t * sa, 16, 1) + pltpu.roll(t * sb, LANES - 16, 1)


def _qk_prep_chunk(x, g128, cos, sa, sb):
    r = _rsq(_pair_sum(x * x) * (1.0 / HEAD_DIM))
    return _rope(x * r * g128, cos, sa, sb)


def _qk_prep_bwd_chunk(x, dy, g128, cos, sa, sb):
    dn = _rope_t(dy, cos, sa, sb)
    r = _rsq(_pair_sum(x * x) * (1.0 / HEAD_DIM))
    xhat = x * r
    dg = jnp.sum(dn * xhat, axis=0, keepdims=True)
    dxh = dn * g128
    dx = r * (dxh - xhat * (_pair_sum(dxh * xhat) * (1.0 / HEAD_DIM)))
    return dx, dg


def _ln_mod(xv, g, sc, sh):
    r = _rsq(jnp.mean(xv * xv, axis=-1, keepdims=True))
    return xv * r * g * (1.0 + sc) + sh


def _inproj_fwd(x, g, sc, sh, w, gq128, gk128, rope):
    S, D = x.shape
    tm = _tile(S, 512)
    cos, sa, sb = rope

    def body(x_ref, g_ref, sc_ref, sh_ref, w_ref, gq_ref, gk_ref, cos_ref, sa_ref, sb_ref,
             h_ref, proj_ref, qc_ref, kc_ref):
        hb = _ln_mod(x_ref[...], g_ref[...], sc_ref[...], sh_ref[...]).astype(BF16)
        h_ref[...] = hb
        acc = jnp.dot(hb, w_ref[...], preferred_element_type=F32)
        proj_ref[...] = acc.astype(BF16)
        c, a, b = cos_ref[...], sa_ref[...], sb_ref[...]
        for j in range(AX_W // LANES):
            xq = acc[:, OFF_QC + j * LANES: OFF_QC + (j + 1) * LANES]
            qc_ref[:, j * LANES:(j + 1) * LANES] = (
                _qk_prep_chunk(xq, gq_ref[...], c, a, b) * QK_SCALE).astype(BF16)
        for j in range(AX_KV_W // LANES):
            xk = acc[:, OFF_KC + j * LANES: OFF_KC + (j + 1) * LANES]
            kc_ref[:, j * LANES:(j + 1) * LANES] = _qk_prep_chunk(xk, gk_ref[...], c, a, b).astype(BF16)

    vec = _const_spec((1, D))
    v128 = _const_spec((1, LANES))
    return pl.pallas_call(
        body, name="inproj_fwd", grid=(S // tm,),
        in_specs=[_row_spec(tm, D), vec, vec, vec, _const_spec(w.shape), v128, v128,
                  _row_spec(tm, LANES), _row_spec(tm, LANES), _row_spec(tm, LANES)],
        out_specs=[_row_spec(tm, D), _row_spec(tm, IN_WIDTH), _row_spec(tm, AX_W), _row_spec(tm, AX_KV_W)],
        out_shape=[jax.ShapeDtypeStruct((S, D), BF16), jax.ShapeDtypeStruct((S, IN_WIDTH), BF16),
                   jax.ShapeDtypeStruct((S, AX_W), BF16), jax.ShapeDtypeStruct((S, AX_KV_W), BF16)],
        compiler_params=_params("parallel"),
    )(x, g, sc, sh, w, gq128, gk128, cos, sa, sb)


class _Band:
    def __init__(self, kind, S):
        self.kind = kind
        self.S = S
        if kind == "na":
            self.hq, self.g, self.halo = NA_HEADS, NA_HEADS, (NA_WIN_ROWS // 2) * GRID_W
            self.q_off, self.k_off, self.v_off = OFF_QA, OFF_KA, OFF_VA
        else:
            self.hq, self.g, self.halo = SW_HEADS, SW_KV_HEADS, SW_RADIUS
            self.q_off, self.k_off, self.v_off = OFF_QB, OFF_KB, OFF_VB
        self.bq = 2 * self.halo
        self.bk = self.bq + 2 * self.halo
        self.nb = S // self.bq
        self.rep = self.hq // self.g
        self.qw = self.hq * HEAD_DIM
        self.kw = self.g * HEAD_DIM

    def mask(self, n):
        qi = lax.broadcasted_iota(I32, (self.bq, self.bk), 0) + n * self.bq
        kj = lax.broadcasted_iota(I32, (self.bq, self.bk), 1) + (n * self.bq - self.halo)
        if self.kind == "sw":
            return (jnp.abs(kj - qi) <= SW_RADIUS) & (kj >= 0) & (kj < self.S)
        rows = self.S // GRID_W
        r, col = qi >> 6, qi & (GRID_W - 1)
        kr, kc = kj >> 6, kj & (GRID_W - 1)
        rs = jnp.clip(r - NA_WIN_ROWS // 2, 0, rows - NA_WIN_ROWS)
        cs = jnp.clip(col - NA_WIN_COLS // 2, 0, GRID_W - NA_WIN_COLS)
        return (kr >= rs) & (kr < rs + NA_WIN_ROWS) & (kc >= cs) & (kc < cs + NA_WIN_COLS)

    def qkv_specs(self):
        ratio = self.bq // self.halo
        last = self.S // self.halo - 1
        q = pl.BlockSpec((self.bq, self.qw), lambda n, o=self.q_off // self.qw: (n, o))
        specs = [q]
        for off in (self.k_off, self.v_off):
            o = off // self.kw
            specs.append(pl.BlockSpec((self.halo, self.kw), lambda n, o=o: (jnp.maximum(n * ratio - 1, 0), o)))
            specs.append(pl.BlockSpec((self.bq, self.kw), lambda n, o=o: (n, o)))
            specs.append(pl.BlockSpec((self.halo, self.kw), lambda n, o=o: (jnp.minimum((n + 1) * ratio, last), o)))
        return specs


def _band_probs(bd, h, q_ref, kcat, bias_ref, sink_ref, mask):
    sl = slice(h * HEAD_DIM, (h + 1) * HEAD_DIM)
    qh = q_ref[:, sl] * QK_SCALE
    s = lax.dot_general(qh, kcat, NT, preferred_element_type=F32) + bias_ref[h]
    s = jnp.where(mask, s, NEG_INF)
    m = jnp.max(s, axis=-1, keepdims=True)
    if sink_ref is not None:
        m = jnp.maximum(m, sink_ref[0:1, h:h + 1])
    p = jnp.exp(s - m)
    l = jnp.sum(p, axis=-1, keepdims=True)
    if sink_ref is not None:
        l = l + jnp.exp(sink_ref[0:1, h:h + 1] - m)
    return qh, p / l, m, l


def _band_fwd(bd, proj, bias, sink, gg):
    S = bd.S
    has_sink = sink is not None

    def body(*refs):
        q_ref, kp, km, kn, vp, vm, vn, bias_ref = refs[:8]
        k = 8
        sink_ref = None
        if has_sink:
            sink_ref = refs[k]
            k += 1
        gg_ref, raw_ref, yn_ref, o_scr = refs[k:k + 4]
        mask = bd.mask(pl.program_id(0))
        for g in range(bd.g):
            gs = slice(g * HEAD_DIM, (g + 1) * HEAD_DIM)
            kcat = jnp.concatenate([kp[:, gs], km[:, gs], kn[:, gs]], axis=0)
            vcat = jnp.concatenate([vp[:, gs], vm[:, gs], vn[:, gs]], axis=0)
            for h in range(g * bd.rep, (g + 1) * bd.rep):
                _, pn, _, _ = _band_probs(bd, h, q_ref, kcat, bias_ref, sink_ref, mask)
                o_scr[:, h * HEAD_DIM:(h + 1) * HEAD_DIM] = jnp.dot(
                    pn.astype(BF16), vcat, preferred_element_type=F32)
        o = o_scr[...]
        raw_ref[...] = o.astype(BF16)
        r = _rsq(jnp.mean(o * o, axis=-1, keepdims=True))
        yn_ref[...] = (o * r * gg_ref[...]).astype(BF16)

    in_specs = bd.qkv_specs() + [_const_spec(bias.shape)]
    args = [proj] * 7 + [bias]
    if has_sink:
        in_specs.append(_const_spec(sink.shape))
        args.append(sink)
    in_specs.append(_const_spec(gg.shape))
    args.append(gg)
    out = jax.ShapeDtypeStruct((S, bd.qw), BF16)
    return pl.pallas_call(
        body, name=bd.kind + "_fwd", grid=(bd.nb,), in_specs=in_specs,
        out_specs=[_row_spec(bd.bq, bd.qw), _row_spec(bd.bq, bd.qw)], out_shape=[out, out],
        scratch_shapes=[pltpu.VMEM((bd.bq, bd.qw), F32)],
        compiler_params=_params("parallel"),
    )(*args)


def _band_bwd(bd, proj, bias, sink, dy):
    S = bd.S
    has_sink = sink is not None

    def body(*refs):
        q_ref, kp, km, kn, vp, vm, vn, bias_ref = refs[:8]
        k = 8
        sink_ref = None
        if has_sink:
            sink_ref = refs[k]
            k += 1
        do_ref = refs[k]
        dq_ref, dkm, dvm, dkp, dvp, dkn, dvn, dbias_ref = refs[k + 1:k + 9]
        k += 9
        dsink_ref = None
        if has_sink:
            dsink_ref = refs[k]
            k += 1
        dk_scr, dv_scr = refs[k:k + 2]
        n = pl.program_id(0)

        @pl.when(n == 0)
        def _():
            dbias_ref[...] = jnp.zeros_like(dbias_ref)
            if has_sink:
                dsink_ref[...] = jnp.zeros_like(dsink_ref)

        mask = bd.mask(n)
        lane = lax.broadcasted_iota(I32, (1, LANES), 1)
        for g in range(bd.g):
            gs = slice(g * HEAD_DIM, (g + 1) * HEAD_DIM)
            kcat = jnp.concatenate([kp[:, gs], km[:, gs], kn[:, gs]], axis=0)
            vcat = jnp.concatenate([vp[:, gs], vm[:, gs], vn[:, gs]], axis=0)
            dk_g = jnp.zeros((bd.bk, HEAD_DIM), F32)
            dv_g = jnp.zeros((bd.bk, HEAD_DIM), F32)
            for h in range(g * bd.rep, (g + 1) * bd.rep):
                sl = slice(h * HEAD_DIM, (h + 1) * HEAD_DIM)
                qh, pn, m, l = _band_probs(bd, h, q_ref, kcat, bias_ref, sink_ref, mask)
                doh = do_ref[:, sl]
                dp = lax.dot_general(doh, vcat, NT, preferred_element_type=F32)
                delta = jnp.sum(pn * dp, axis=-1, keepdims=True)
                ds = pn * (dp - delta)
                dbias_ref[h] += ds
                if has_sink:
                    p_sink = jnp.exp(sink_ref[0:1, h:h + 1] - m) / l
                    dsink_ref[...] += jnp.where(lane == h, -jnp.sum(p_sink * delta, axis=0, keepdims=True), 0.0)
                dsb = ds.astype(BF16)
                dq_ref[:, sl] = jnp.dot(dsb, kcat, preferred_element_type=F32) * QK_SCALE
                dk_g = dk_g + lax.dot_general(dsb, qh, TN, preferred_element_type=F32)
                dv_g = dv_g + lax.dot_general(pn.astype(BF16), doh, TN, preferred_element_type=F32)
            dk_scr[:, gs] = dk_g
            dv_scr[:, gs] = dv_g
        h0, h1 = bd.halo, bd.halo + bd.bq
        dkp[0] = dk_scr[0:h0, :]
        dkm[...] = dk_scr[h0:h1, :]
        dkn[0] = dk_scr[h1:bd.bk, :]
        dvp[0] = dv_scr[0:h0, :]
        dvm[...] = dv_scr[h0:h1, :]
        dvn[0] = dv_scr[h1:bd.bk, :]

    in_specs = bd.qkv_specs() + [_const_spec(bias.shape)]
    args = [proj] * 7 + [bias]
    if has_sink:
        in_specs.append(_const_spec(sink.shape))
        args.append(sink)
    in_specs.append(_row_spec(bd.bq, bd.qw))
    args.append(dy)
    halo_spec = pl.BlockSpec((1, bd.halo, bd.kw), lambda n: (n, 0, 0))
    halo_shape = jax.ShapeDtypeStruct((bd.nb, bd.halo, bd.kw), F32)
    main_shape = jax.ShapeDtypeStruct((S, bd.kw), F32)
    out_specs = [_row_spec(bd.bq, bd.qw), _row_spec(bd.bq, bd.kw), _row_spec(bd.bq, bd.kw),
                 halo_spec, halo_spec, halo_spec, halo_spec, _const_spec(bias.shape)]
    out_shape = [jax.ShapeDtypeStruct((S, bd.qw), F32), main_shape, main_shape,
                 halo_shape, halo_shape, halo_shape, halo_shape, jax.ShapeDtypeStruct(bias.shape, F32)]
    if has_sink:
        out_specs.append(_const_spec((1, LANES)))
        out_shape.append(jax.ShapeDtypeStruct((1, LANES), F32))
    return pl.pallas_call(
        body, name=bd.kind + "_bwd", grid=(bd.nb,), in_specs=in_specs, out_specs=out_specs, out_shape=out_shape,
        scratch_shapes=[pltpu.VMEM((bd.bk, bd.kw), F32), pltpu.VMEM((bd.bk, bd.kw), F32)],
        compiler_params=_params("arbitrary"),
    )(*args)


def _halo_to_rows(prev, nxt):
    nb, halo, w = prev.shape
    z = jnp.zeros((1, halo, w), prev.dtype)
    first = jnp.concatenate([z, nxt[:-1]], axis=0)
    second = jnp.concatenate([prev[1:], z], axis=0)
    return jnp.concatenate([first, second], axis=1).reshape(nb * 2 * halo, w)


AX_PAIRS = AX_W // LANES


def _ax_blocks(S):
    return _tile(S, 512), _tile(S, 512)


def _left_half(shape):
    return lax.broadcasted_iota(I32, shape, len(shape) - 1) < HEAD_DIM


def _half_variants(a):
    lo = _left_half(a.shape)
    zero = jnp.zeros_like(a)
    swapped = pltpu.roll(a, HEAD_DIM, 1)
    return ((jnp.where(lo, a, zero), jnp.where(lo, zero, swapped)),
            (jnp.where(lo, swapped, zero), jnp.where(lo, zero, a)))


def _ax_fwd(qc, kc, proj, gg):
    S = qc.shape[0]
    bq, bk = _ax_blocks(S)
    nk = S // bk
    rep = AX_HEADS // AX_KV_HEADS

    def body(q_ref, k_ref, v_ref, gg_ref, raw_ref, yn_ref, lse_ref, m_scr, l_scr, acc_scr):
        kv = pl.program_id(1)

        @pl.when(kv == 0)
        def _():
            m_scr[...] = jnp.full(m_scr.shape, NEG_INF, F32)
            l_scr[...] = jnp.zeros_like(l_scr)
            acc_scr[...] = jnp.zeros_like(acc_scr)

        kz, vz = _half_variants(k_ref[...]), _half_variants(v_ref[...])
        lo = _left_half((bq, LANES))
        for pr in range(AX_PAIRS):
            qp = q_ref[:, pr * LANES:(pr + 1) * LANES]
            pv, alphas = None, []
            for half in range(2):
                h = 2 * pr + half
                g = h // rep
                s = lax.dot_general(qp, kz[g][half], NT, preferred_element_type=F32)
                m_prev = m_scr[h]
                m_new = jnp.maximum(m_prev, jnp.max(s, axis=-1, keepdims=True))
                alpha = jnp.exp(m_prev - m_new)
                p = jnp.exp(s - jnp.tile(m_new, (1, bk // LANES)))
                l_scr[h] = alpha * l_scr[h] + jnp.sum(p, axis=-1, keepdims=True)
                m_scr[h] = m_new
                d = jnp.dot(p.astype(BF16), vz[g][half], preferred_element_type=F32)
                pv = d if pv is None else pv + d
                alphas.append(alpha)
            acc_scr[pr] = acc_scr[pr] * jnp.where(lo, alphas[0], alphas[1]) + pv

        @pl.when(kv == nk - 1)
        def _():
            ssq = jnp.zeros((bq, 1), F32)
            for pr in range(AX_PAIRS):
                o = acc_scr[pr] / jnp.where(lo, l_scr[2 * pr], l_scr[2 * pr + 1])
                acc_scr[pr] = o
                ssq = ssq + jnp.sum(o * o, axis=-1, keepdims=True)
            for h in range(AX_HEADS):
                lse_ref[h] = m_scr[h] + jnp.log(l_scr[h])
            r = _rsq(ssq * (1.0 / AX_W))
            for pr in range(AX_PAIRS):
                cols = slice(pr * LANES, (pr + 1) * LANES)
                o = acc_scr[pr]
                raw_ref[:, cols] = o.astype(BF16)
                yn_ref[:, cols] = (o * r * gg_ref[:, cols]).astype(BF16)

    out = jax.ShapeDtypeStruct((S, AX_W), BF16)
    return pl.pallas_call(
        body, name="ax_fwd", grid=(S // bq, nk),
        in_specs=[pl.BlockSpec((bq, AX_W), lambda i, j: (i, 0)),
                  pl.BlockSpec((bk, AX_KV_W), lambda i, j: (j, 0)),
                  pl.BlockSpec((bk, AX_KV_W), lambda i, j: (j, OFF_VC // AX_KV_W)),
                  _const_spec(gg.shape)],
        out_specs=[pl.BlockSpec((bq, AX_W), lambda i, j: (i, 0)),
                   pl.BlockSpec((bq, AX_W), lambda i, j: (i, 0)),
                   pl.BlockSpec((AX_HEADS, bq, LANES), lambda i, j: (0, i, 0))],
        out_shape=[out, out, jax.ShapeDtypeStruct((AX_HEADS, S, LANES), F32)],
        scratch_shapes=[pltpu.VMEM((AX_HEADS, bq, LANES), F32), pltpu.VMEM((AX_HEADS, bq, LANES), F32),
                        pltpu.VMEM((AX_PAIRS, bq, LANES), F32)],
        compiler_params=_params("parallel", "arbitrary"),
    )(qc, kc, proj, gg)


def _ax_bwd_dq(qc, kc, proj, dy, raw, lse):
    S = qc.shape[0]
    bq, bk = _ax_blocks(S)
    nk = S // bk
    rep = AX_HEADS // AX_KV_HEADS

    def body(q_ref, k_ref, v_ref, do_ref, o_ref, lse_ref, dq_ref, delta_ref):
        kv = pl.program_id(1)
        lo = _left_half((bq, LANES))

        @pl.when(kv == 0)
        def _():
            dq_ref[...] = jnp.zeros_like(dq_ref)
            for pr in range(AX_PAIRS):
                cols = slice(pr * LANES, (pr + 1) * LANES)
                prod = do_ref[:, cols].astype(F32) * o_ref[:, cols].astype(F32)
                left = jnp.sum(jnp.where(lo, prod, 0.0), axis=-1, keepdims=True)
                right = jnp.sum(jnp.where(lo, 0.0, prod), axis=-1, keepdims=True)
                delta_ref[2 * pr] = jnp.broadcast_to(left, (bq, LANES))
                delta_ref[2 * pr + 1] = jnp.broadcast_to(right, (bq, LANES))

        kz, vz = _half_variants(k_ref[...]), _half_variants(v_ref[...])
        reps = (1, bk // LANES)
        for pr in range(AX_PAIRS):
            cols = slice(pr * LANES, (pr + 1) * LANES)
            qp, dop = q_ref[:, cols], do_ref[:, cols]
            dq = None
            for half in range(2):
                h = 2 * pr + half
                g = h // rep
                s = lax.dot_general(qp, kz[g][half], NT, preferred_element_type=F32)
                p = jnp.exp(s - jnp.tile(lse_ref[h], reps))
                dp = lax.dot_general(dop, vz[g][half], NT, preferred_element_type=F32)
                ds = p * (dp - jnp.tile(delta_ref[h], reps))
                d = jnp.dot(ds.astype(BF16), kz[g][half], preferred_element_type=F32)
                dq = d if dq is None else dq + d
            dq_ref[:, cols] += dq

        @pl.when(kv == nk - 1)
        def _():
            dq_ref[...] = dq_ref[...] * QK_SCALE

    qspec = pl.BlockSpec((bq, AX_W), lambda i, j: (i, 0))
    stat = pl.BlockSpec((AX_HEADS, bq, LANES), lambda i, j: (0, i, 0))
    return pl.pallas_call(
        body, name="ax_bwd_dq", grid=(S // bq, nk),
        in_specs=[qspec, pl.BlockSpec((bk, AX_KV_W), lambda i, j: (j, 0)),
                  pl.BlockSpec((bk, AX_KV_W), lambda i, j: (j, OFF_VC // AX_KV_W)), qspec, qspec, stat],
        out_specs=[qspec, stat],
        out_shape=[jax.ShapeDtypeStruct((S, AX_W), F32), jax.ShapeDtypeStruct((AX_HEADS, S, LANES), F32)],
        compiler_params=_params("parallel", "arbitrary"),
    )(qc, kc, proj, dy, raw, lse)


def _ax_bwd_dkv(qc, kc, proj, dy, lse_row, delta_row):
    S = qc.shape[0]
    bq, bk = _ax_blocks(S)
    rep = AX_HEADS // AX_KV_HEADS

    def body(q_ref, k_ref, v_ref, do_ref, lse_ref, delta_ref, dk_ref, dv_ref):
        @pl.when(pl.program_id(1) == 0)
        def _():
            dk_ref[...] = jnp.zeros_like(dk_ref)
            dv_ref[...] = jnp.zeros_like(dv_ref)

        kz, vz = _half_variants(k_ref[...]), _half_variants(v_ref[...])
        dk, dv = None, None
        for pr in range(AX_PAIRS):
            cols = slice(pr * LANES, (pr + 1) * LANES)
            qp, dop = q_ref[:, cols], do_ref[:, cols]
            qz, doz = _half_variants(qp), _half_variants(dop)
            for half in range(2):
                h = 2 * pr + half
                g = h // rep
                s_t = lax.dot_general(kz[g][half], qp, NT, preferred_element_type=F32)
                p_t = jnp.exp(s_t - lse_ref[h])
                dp_t = lax.dot_general(vz[g][half], dop, NT, preferred_element_type=F32)
                ds_t = p_t * (dp_t - delta_ref[h])
                a = jnp.dot(p_t.astype(BF16), doz[half][g], preferred_element_type=F32)
                b = jnp.dot(ds_t.astype(BF16), qz[half][g], preferred_element_type=F32)
                dv = a if dv is None else dv + a
                dk = b if dk is None else dk + b
        dv_ref[...] += dv
        dk_ref[...] += dk

    qspec = pl.BlockSpec((bq, AX_W), lambda j, i: (i, 0))
    kspec = pl.BlockSpec((bk, AX_KV_W), lambda j, i: (j, 0))
    stat = pl.BlockSpec((AX_HEADS, 1, bq), lambda j, i: (0, 0, i))
    out = jax.ShapeDtypeStruct((S, AX_KV_W), F32)
    return pl.pallas_call(
        body, name="ax_bwd_dkv", grid=(S // bk, S // bq),
        in_specs=[qspec, kspec, pl.BlockSpec((bk, AX_KV_W), lambda j, i: (j, OFF_VC // AX_KV_W)),
                  qspec, stat, stat],
        out_specs=[kspec, kspec], out_shape=[out, out],
        compiler_params=_params("parallel", "arbitrary"),
    )(qc, kc, proj, dy, lse_row, delta_row)


def _oproj_fwd(x, yna, ynb, ync, w, gt):
    S, D = x.shape
    tm = _tile(S, 512)

    def body(x_ref, a_ref, b_ref, c_ref, w_ref, gt_ref, x1_ref, ao_ref, yn_ref):
        yn_ref[:, 0:NA_W] = a_ref[...]
        yn_ref[:, NA_W:NA_W + SW_W] = b_ref[...]
        yn_ref[:, NA_W + SW_W:MIX_WIDTH] = c_ref[...]
        acc = jnp.dot(yn_ref[...], w_ref[...], preferred_element_type=F32)
        ao_ref[...] = acc.astype(BF16)
        x1_ref[...] = x_ref[...] + gt_ref[...] * acc

    return pl.pallas_call(
        body, name="oproj_fwd", grid=(S // tm,),
        in_specs=[_row_spec(tm, D), _row_spec(tm, NA_W), _row_spec(tm, SW_W), _row_spec(tm, AX_W),
                  _const_spec(w.shape), _const_spec((1, D))],
        out_specs=[_row_spec(tm, D), _row_spec(tm, D), _row_spec(tm, MIX_WIDTH)],
        out_shape=[jax.ShapeDtypeStruct((S, D), F32), jax.ShapeDtypeStruct((S, D), BF16),
                   jax.ShapeDtypeStruct((S, MIX_WIDTH), BF16)],
        compiler_params=_params("parallel"),
    )(x, yna, ynb, ync, w, gt)


def _gu_fwd(x, g, sc, sh, w):
    S, D = x.shape
    F2 = w.shape[1]
    tn = F2 // 4
    tm = _tile(S, 512)

    def body(x_ref, g_ref, sc_ref, sh_ref, w_ref, h_ref, gu_ref, act_ref):
        @pl.when(pl.program_id(1) == 0)
        def _():
            h_ref[...] = _ln_mod(x_ref[...], g_ref[...], sc_ref[...], sh_ref[...]).astype(BF16)

        acc = jnp.dot(h_ref[...], w_ref[...], preferred_element_type=F32)
        gu_ref[...] = acc.astype(BF16)
        gate, up = acc[:, :tn], acc[:, tn:]
        act_ref[...] = (gate * (1.0 / (1.0 + jnp.exp(-gate))) * up).astype(BF16)

    vec = pl.BlockSpec((1, D), lambda i, j: (0, 0))
    return pl.pallas_call(
        body, name="gu_fwd", grid=(S // tm, 2),
        in_specs=[pl.BlockSpec((tm, D), lambda i, j: (i, 0)), vec, vec, vec,
                  pl.BlockSpec((D, 2 * tn), lambda i, j: (0, j))],
        out_specs=[pl.BlockSpec((tm, D), lambda i, j: (i, 0)), pl.BlockSpec((tm, 2 * tn), lambda i, j: (i, j)),
                   pl.BlockSpec((tm, tn), lambda i, j: (i, j))],
        out_shape=[jax.ShapeDtypeStruct((S, D), BF16), jax.ShapeDtypeStruct((S, F2), BF16),
                   jax.ShapeDtypeStruct((S, F2 // 2), BF16)],
        compiler_params=_params("parallel", "arbitrary"),
    )(x, g, sc, sh, w)


def _down_fwd(x, act, w, gt):
    S, D = x.shape
    F = act.shape[1]
    tm = _tile(S, 512)

    def body(x_ref, a_ref, w_ref, gt_ref, x2_ref, fo_ref):
        acc = jnp.dot(a_ref[...], w_ref[...], preferred_element_type=F32)
        fo_ref[...] = acc.astype(BF16)
        x2_ref[...] = x_ref[...] + gt_ref[...] * acc

    return pl.pallas_call(
        body, name="down_fwd", grid=(S // tm,),
        in_specs=[_row_spec(tm, D), _row_spec(tm, F), _const_spec(w.shape), _const_spec((1, D))],
        out_specs=[_row_spec(tm, D), _row_spec(tm, D)],
        out_shape=[jax.ShapeDtypeStruct((S, D), F32), jax.ShapeDtypeStruct((S, D), BF16)],
        compiler_params=_params("parallel"),
    )(x, act, w, gt)


def _final_loss(x, g, target):
    S, D = x.shape
    tm = _tile(S, 512)

    def body(x_ref, g_ref, t_ref, dx_ref, loss_ref, dg_ref):
        @pl.when(pl.program_id(0) == 0)
        def _():
            loss_ref[...] = jnp.zeros_like(loss_ref)
            dg_ref[...] = jnp.zeros_like(dg_ref)

        xv = x_ref[...]
        r = _rsq(jnp.mean(xv * xv, axis=-1, keepdims=True))
        xhat = xv * r
        err = xhat * g_ref[...] - t_ref[...]
        loss_ref[...] += 0.5 * jnp.sum(jnp.mean(err * err, axis=-1, keepdims=True), axis=0, keepdims=True)
        dy = err * (1.0 / D)
        dg_ref[...] += jnp.sum(dy * xhat, axis=0, keepdims=True)
        dxh = dy * g_ref[...]
        dx_ref[...] = r * (dxh - xhat * jnp.mean(dxh * xhat, axis=-1, keepdims=True))

    return pl.pallas_call(
        body, name="final_loss", grid=(S // tm,),
        in_specs=[_row_spec(tm, D), _const_spec((1, D)), _row_spec(tm, D)],
        out_specs=[_row_spec(tm, D), _const_spec((1, LANES)), _const_spec((1, D))],
        out_shape=[jax.ShapeDtypeStruct((S, D), F32), jax.ShapeDtypeStruct((1, LANES), F32),
                   jax.ShapeDtypeStruct((1, D), F32)],
        compiler_params=_params("arbitrary"),
    )(x, g, target)


def _ffn_bwd1(dx2, fo, gt, w_down, gu):
    S, D = dx2.shape
    F2 = gu.shape[1]
    tn = F2 // 4
    tm = _tile(S, 512)

    def body(dx_ref, fo_ref, gt_ref, w_ref, gu_ref, dfo_ref, dgu_ref, dgt_ref):
        i, j = pl.program_id(0), pl.program_id(1)

        @pl.when((i == 0) & (j == 0))
        def _():
            dgt_ref[...] = jnp.zeros_like(dgt_ref)

        @pl.when(j == 0)
        def _():
            dxv = dx_ref[...]
            dfo_ref[...] = (dxv * gt_ref[...]).astype(BF16)
            dgt_ref[...] += jnp.sum(dxv * fo_ref[...].astype(F32), axis=0, keepdims=True)

        dact = lax.dot_general(dfo_ref[...], w_ref[...], NT, preferred_element_type=F32)
        gate = gu_ref[:, :tn].astype(F32)
        up = gu_ref[:, tn:].astype(F32)
        sig = 1.0 / (1.0 + jnp.exp(-gate))
        dgu_ref[:, :tn] = (dact * up * (sig * (1.0 + gate * (1.0 - sig)))).astype(BF16)
        dgu_ref[:, tn:] = (dact * (gate * sig)).astype(BF16)

    vec = pl.BlockSpec((1, D), lambda i, j: (0, 0))
    row = pl.BlockSpec((tm, D), lambda i, j: (i, 0))
    return pl.pallas_call(
        body, name="ffn_bwd1", grid=(S // tm, 2),
        in_specs=[row, row, vec, pl.BlockSpec((tn, D), lambda i, j: (j, 0)),
                  pl.BlockSpec((tm, 2 * tn), lambda i, j: (i, j))],
        out_specs=[row, pl.BlockSpec((tm, 2 * tn), lambda i, j: (i, j)), vec],
        out_shape=[jax.ShapeDtypeStruct((S, D), BF16), jax.ShapeDtypeStruct((S, F2), BF16),
                   jax.ShapeDtypeStruct((1, D), F32)],
        compiler_params=_params("arbitrary", "arbitrary"),
    )(dx2, fo, gt, w_down, gu)


def _nt_ln_bwd(a, w, x, g, sc, dres, name):
    S, D = x.shape
    K = a.shape[1]
    tm = _tile(S, 256)

    def body(a_ref, w_ref, x_ref, g_ref, sc_ref, dres_ref, dx_ref, dsh_ref, dsc_ref, dg_ref):
        @pl.when(pl.program_id(0) == 0)
        def _():
            dsh_ref[...] = jnp.zeros_like(dsh_ref)
            dsc_ref[...] = jnp.zeros_like(dsc_ref)
            dg_ref[...] = jnp.zeros_like(dg_ref)

        dh = lax.dot_general(a_ref[...], w_ref[...], NT, preferred_element_type=F32)
        xv = x_ref[...]
        r = _rsq(jnp.mean(xv * xv, axis=-1, keepdims=True))
        xhat = xv * r
        gv = g_ref[...]
        dsh_ref[...] += jnp.sum(dh, axis=0, keepdims=True)
        dsc_ref[...] += jnp.sum(dh * (xhat * gv), axis=0, keepdims=True)
        dn = dh * (1.0 + sc_ref[...])
        dg_ref[...] += jnp.sum(dn * xhat, axis=0, keepdims=True)
        dxh = dn * gv
        dx_ref[...] = dres_ref[...] + r * (dxh - xhat * jnp.mean(dxh * xhat, axis=-1, keepdims=True))

    vec = _const_spec((1, D))
    vshape = jax.ShapeDtypeStruct((1, D), F32)
    return pl.pallas_call(
        body, name=name, grid=(S // tm,),
        in_specs=[_row_spec(tm, K), _const_spec(w.shape), _row_spec(tm, D), vec, vec, _row_spec(tm, D)],
        out_specs=[_row_spec(tm, D), vec, vec, vec],
        out_shape=[jax.ShapeDtypeStruct((S, D), F32), vshape, vshape, vshape],
        compiler_params=_params("arbitrary"),
    )(a, w, x, g, sc, dres)


def _oproj_bwd(dx1, ao, gt, w, ya, yb, yc, gg):
    S, D = dx1.shape
    tm = _tile(S, 512)
    groups = ((0, NA_W), (NA_W, SW_W), (NA_W + SW_W, AX_W))

    def body(dx_ref, ao_ref, gt_ref, w_ref, ya_ref, yb_ref, yc_ref, gg_ref,
             dao_ref, dya_ref, dyb_ref, dyc_ref, dgt_ref, dgg_ref):
        @pl.when(pl.program_id(0) == 0)
        def _():
            dgt_ref[...] = jnp.zeros_like(dgt_ref)
            dgg_ref[...] = jnp.zeros_like(dgg_ref)

        dxv = dx_ref[...]
        dao = (dxv * gt_ref[...]).astype(BF16)
        dao_ref[...] = dao
        dgt_ref[...] += jnp.sum(dxv * ao_ref[...].astype(F32), axis=0, keepdims=True)
        dyn = lax.dot_general(dao, w_ref[...], NT, preferred_element_type=F32)
        for (off, wd), y_ref, dy_ref in zip(groups, (ya_ref, yb_ref, yc_ref), (dya_ref, dyb_ref, dyc_ref)):
            y = y_ref[...].astype(F32)
            d = dyn[:, off:off + wd]
            r = _rsq(jnp.mean(y * y, axis=-1, keepdims=True))
            yhat = y * r
            dgg_ref[:, off:off + wd] += jnp.sum(d * yhat, axis=0, keepdims=True)
            dyh = d * gg_ref[:, off:off + wd]
            dy_ref[...] = (r * (dyh - yhat * jnp.mean(dyh * yhat, axis=-1, keepdims=True))).astype(BF16)

    vec = _const_spec((1, D))
    mvec = _const_spec((1, MIX_WIDTH))
    return pl.pallas_call(
        body, name="oproj_bwd", grid=(S // tm,),
        in_specs=[_row_spec(tm, D), _row_spec(tm, D), vec, _const_spec(w.shape),
                  _row_spec(tm, NA_W), _row_spec(tm, SW_W), _row_spec(tm, AX_W), mvec],
        out_specs=[_row_spec(tm, D), _row_spec(tm, NA_W), _row_spec(tm, SW_W), _row_spec(tm, AX_W), vec, mvec],
        out_shape=[jax.ShapeDtypeStruct((S, D), BF16), jax.ShapeDtypeStruct((S, NA_W), BF16),
                   jax.ShapeDtypeStruct((S, SW_W), BF16), jax.ShapeDtypeStruct((S, AX_W), BF16),
                   jax.ShapeDtypeStruct((1, D), F32), jax.ShapeDtypeStruct((1, MIX_WIDTH), F32)],
        compiler_params=_params("arbitrary"),
    )(dx1, ao, gt, w, ya, yb, yc, gg)


def _dproj_assemble(proj, na, sw, ax, gq128, gk128, rope):
    S = proj.shape[0]
    tm = _tile(S, 512)
    cos, sa, sb = rope

    def body(proj_ref, qa, ka, kah, va, vah, qb, kb, kbh, vb, vbh, qc, kc, vc,
             gq_ref, gk_ref, cos_ref, sa_ref, sb_ref, out_ref, dgq_ref, dgk_ref):
        @pl.when(pl.program_id(0) == 0)
        def _():
            dgq_ref[...] = jnp.zeros_like(dgq_ref)
            dgk_ref[...] = jnp.zeros_like(dgk_ref)

        out_ref[:, OFF_QA:OFF_KA] = qa[...].astype(BF16)
        out_ref[:, OFF_KA:OFF_VA] = (ka[...] + kah[...]).astype(BF16)
        out_ref[:, OFF_VA:OFF_QB] = (va[...] + vah[...]).astype(BF16)
        out_ref[:, OFF_QB:OFF_KB] = qb[...].astype(BF16)
        out_ref[:, OFF_KB:OFF_VB] = (kb[...] + kbh[...]).astype(BF16)
        out_ref[:, OFF_VB:OFF_QC] = (vb[...] + vbh[...]).astype(BF16)
        c, a, b = cos_ref[...], sa_ref[...], sb_ref[...]
        for j in range(AX_W // LANES):
            cols = slice(OFF_QC + j * LANES, OFF_QC + (j + 1) * LANES)
            dx, dg = _qk_prep_bwd_chunk(proj_ref[:, cols].astype(F32), qc[:, j * LANES:(j + 1) * LANES],
                                        gq_ref[...], c, a, b)
            out_ref[:, cols] = dx.astype(BF16)
            dgq_ref[...] += dg
        for j in range(AX_KV_W // LANES):
            cols = slice(OFF_KC + j * LANES, OFF_KC + (j + 1) * LANES)
            dx, dg = _qk_prep_bwd_chunk(proj_ref[:, cols].astype(F32), kc[:, j * LANES:(j + 1) * LANES],
                                        gk_ref[...], c, a, b)
            out_ref[:, cols] = dx.astype(BF16)
            dgk_ref[...] += dg
        out_ref[:, OFF_VC:IN_WIDTH] = vc[...].astype(BF16)

    v128 = _const_spec((1, LANES))
    r = lambda w: _row_spec(tm, w)
    return pl.pallas_call(
        body, name="dproj_assemble", grid=(S // tm,),
        in_specs=[r(IN_WIDTH), r(NA_W), r(NA_W), r(NA_W), r(NA_W), r(NA_W),
                  r(SW_W), r(SW_KV_W), r(SW_KV_W), r(SW_KV_W), r(SW_KV_W),
                  r(AX_W), r(AX_KV_W), r(AX_KV_W), v128, v128, r(LANES), r(LANES), r(LANES)],
        out_specs=[r(IN_WIDTH), v128, v128],
        out_shape=[jax.ShapeDtypeStruct((S, IN_WIDTH), BF16), jax.ShapeDtypeStruct((1, LANES), F32),
                   jax.ShapeDtypeStruct((1, LANES), F32)],
        compiler_params=_params("arbitrary"),
    )(proj, *na, *sw, *ax, gq128, gk128, cos, sa, sb)


def _tn_matmul(a, b, name):
    S, Ka = a.shape
    Nb = b.shape[1]
    tm = _tile(Ka, 1408, LANES)
    tn = _tile(Nb, 1408, LANES)
    tk = _tile(S, 512)
    nk = S // tk

    def body(a_ref, b_ref, o_ref):
        @pl.when(pl.program_id(2) == 0)
        def _():
            o_ref[...] = jnp.zeros_like(o_ref)

        o_ref[...] += lax.dot_general(a_ref[...], b_ref[...], TN, preferred_element_type=F32)

    return pl.pallas_call(
        body, name=name, grid=(Ka // tm, Nb // tn, nk),
        in_specs=[pl.BlockSpec((tk, tm), lambda i, j, k: (k, i)), pl.BlockSpec((tk, tn), lambda i, j, k: (k, j))],
        out_specs=pl.BlockSpec((tm, tn), lambda i, j, k: (i, j)),
        out_shape=jax.ShapeDtypeStruct((Ka, Nb), F32),
        compiler_params=_params("parallel", "parallel", "arbitrary"),
    )(a, b)


def _na_index(bd):
    rq = jnp.arange(bd.bq // GRID_W)
    rk = jnp.arange(bd.bk // GRID_W)
    col = jnp.arange(GRID_W)
    ri = jnp.clip(rk[None, :] - rq[:, None] - bd.halo // GRID_W + NA_WIN_ROWS - 1, 0, 2 * NA_WIN_ROWS - 2)
    ci = jnp.clip(col[None, :] - col[:, None] + NA_WIN_COLS - 1, 0, 2 * NA_WIN_COLS - 2)
    return ri, ci


def _na_one_hots(bd):
    ri, ci = _na_index(bd)
    oh_r = jax.nn.one_hot(ri, 2 * NA_WIN_ROWS - 1, dtype=F32)
    oh_c = jax.nn.one_hot(ci, 2 * NA_WIN_COLS - 1, dtype=F32)
    return oh_r, oh_c


def _na_bias(bd, rpb):
    oh_r, oh_c = _na_one_hots(bd)
    t = jnp.einsum("hab,qra->hqrb", rpb, oh_r, precision=lax.Precision.HIGHEST)
    b = jnp.einsum("hqrb,ckb->hqcrk", t, oh_c, precision=lax.Precision.HIGHEST)
    return b.reshape(NA_HEADS, bd.bq, bd.bk)


def _na_bias_t(bd, dbias):
    oh_r, oh_c = _na_one_hots(bd)
    d5 = dbias.reshape(NA_HEADS, bd.bq // GRID_W, GRID_W, bd.bk // GRID_W, GRID_W)
    t = jnp.einsum("hqcrk,ckb->hqrb", d5, oh_c, precision=lax.Precision.HIGHEST)
    return jnp.einsum("hqrb,qra->hab", t, oh_r, precision=lax.Precision.HIGHEST)


def _t5_bucket(rel):
    nb = T5_BUCKETS // 2
    ret = (rel > 0).astype(I32) * nb
    n = jnp.abs(rel)
    max_exact = nb // 2
    nf = jnp.maximum(n, max_exact).astype(F32)
    large = max_exact + (jnp.log(nf / max_exact) / math.log(T5_MAX_DIST / max_exact)
                         * (nb - max_exact)).astype(I32)
    large = jnp.minimum(large, nb - 1)
    return ret + jnp.where(n < max_exact, n, large)


def _sw_bucket(bd):
    rel = (jnp.arange(bd.bk) - bd.halo)[None, :] - jnp.arange(bd.bq)[:, None]
    return _t5_bucket(rel)


def _sw_bias(bd, t5):
    oh = jax.nn.one_hot(_sw_bucket(bd), T5_BUCKETS, dtype=F32)
    return jnp.einsum("bh,qkb->hqk", t5, oh, precision=lax.Precision.HIGHEST)


def _sw_bias_t(bd, dbias):
    oh = jax.nn.one_hot(_sw_bucket(bd), T5_BUCKETS, dtype=F32)
    return jnp.einsum("hqk,qkb->bh", dbias, oh, precision=lax.Precision.HIGHEST)


def _local_step(x, target, mod, w_in, w_o, w_gu, w_down, g_attn, rpb_na, sink_sw, t5_table, gq_ax, gk_ax,
                g_group, g_ffn, g_final):
    S, D = x.shape
    rope = _rope_tables(S)
    na, sw = _Band("na", S), _Band("sw", S)
    two = lambda v: jnp.concatenate([v, v])[None, :]
    sw_bias = _sw_bias(sw, t5_table)
    saved = []
    for l in range(DEPTH):
        sh_a, sc_a, gt_a, sh_f, sc_f, gt_f = [mod[l, k * D:(k + 1) * D][None, :] for k in range(6)]
        gq128, gk128 = two(gq_ax[l]), two(gk_ax[l])
        gg = g_group[l][None, :]
        sink = jnp.pad(sink_sw[l], (0, LANES - SW_HEADS))[None, :]
        na_bias = _na_bias(na, rpb_na[l])
        h, proj, qc, kc = _inproj_fwd(x, g_attn[l][None, :], sc_a, sh_a, w_in[l], gq128, gk128, rope)
        ya, yna = _band_fwd(na, proj, na_bias, None, gg[:, :NA_W])
        yb, ynb = _band_fwd(sw, proj, sw_bias, sink, gg[:, NA_W:NA_W + SW_W])
        yc, ync, lse = _ax_fwd(qc, kc, proj, gg[:, NA_W + SW_W:])
        x1, ao, yn = _oproj_fwd(x, yna, ynb, ync, w_o[l], gt_a)
        hf, gu, act = _gu_fwd(x1, g_ffn[l][None, :], sc_f, sh_f, w_gu[l])
        x2, fo = _down_fwd(x1, act, w_down[l], gt_f)
        saved.append(dict(x=x, x1=x1, h=h, proj=proj, qc=qc, kc=kc, ya=ya, yb=yb, yc=yc, lse=lse, ao=ao, yn=yn,
                          hf=hf, gu=gu, act=act, fo=fo, na_bias=na_bias, sink=sink, gq128=gq128, gk128=gk128,
                          gg=gg, mods=(sh_a, sc_a, gt_a, sh_f, sc_f, gt_f)))
        x = x2

    dx, loss_row, dg_final = _final_loss(x, g_final[None, :], target)
    gw = {k: [None] * DEPTH for k in ("w_in", "w_o", "w_gu", "w_down")}
    gs = {k: [None] * DEPTH for k in ("b_mod", "g_attn", "rpb_na", "sink_sw", "gq_ax", "gk_ax", "g_group", "g_ffn")}
    d_t5 = jnp.zeros((T5_BUCKETS, SW_HEADS), F32)
    for l in reversed(range(DEPTH)):
        s = saved[l]
        sh_a, sc_a, gt_a, sh_f, sc_f, gt_f = s["mods"]
        dfo, dgu, dgt_f = _ffn_bwd1(dx, s["fo"], gt_f, w_down[l], s["gu"])
        gw["w_down"][l] = _tn_matmul(s["act"], dfo, "dw_down")
        gw["w_gu"][l] = _tn_matmul(s["hf"], dgu, "dw_gu")
        dx1, dsh_f, dsc_f, gs["g_ffn"][l] = _nt_ln_bwd(dgu, w_gu[l], s["x1"], g_ffn[l][None, :], sc_f, dx, "ffn_bwd2")
        dao, dya, dyb, dyc, dgt_a, gs["g_group"][l] = _oproj_bwd(dx1, s["ao"], gt_a, w_o[l], s["ya"], s["yb"],
                                                                 s["yc"], s["gg"])
        gw["w_o"][l] = _tn_matmul(s["yn"], dao, "dw_o")
        dqa, dka, dva, dkap, dvap, dkan, dvan, dbias_na = _band_bwd(na, s["proj"], s["na_bias"], None, dya)
        dqb, dkb, dvb, dkbp, dvbp, dkbn, dvbn, dbias_sw, dsink = _band_bwd(sw, s["proj"], sw_bias, s["sink"], dyb)
        dqc, delta = _ax_bwd_dq(s["qc"], s["kc"], s["proj"], dyc, s["yc"], s["lse"])
        as_row = lambda a: a[:, :, 0][:, None, :]
        dkc, dvc = _ax_bwd_dkv(s["qc"], s["kc"], s["proj"], dyc, as_row(s["lse"]), as_row(delta))
        dproj, dgq, dgk = _dproj_assemble(
            s["proj"], (dqa, dka, _halo_to_rows(dkap, dkan), dva, _halo_to_rows(dvap, dvan)),
            (dqb, dkb, _halo_to_rows(dkbp, dkbn), dvb, _halo_to_rows(dvbp, dvbn)), (dqc, dkc, dvc),
            s["gq128"], s["gk128"], rope)
        gw["w_in"][l] = _tn_matmul(s["h"], dproj, "dw_in")
        dx, dsh_a, dsc_a, gs["g_attn"][l] = _nt_ln_bwd(dproj, w_in[l], s["x"], g_attn[l][None, :], sc_a, dx1,
                                                       "inproj_bwd")
        gs["b_mod"][l] = jnp.concatenate([dsh_a, dsc_a, dgt_a, dsh_f, dsc_f, dgt_f], axis=1)[0]
        gs["rpb_na"][l] = _na_bias_t(na, dbias_na)
        gs["sink_sw"][l] = dsink[0, :SW_HEADS]
        d_t5 = d_t5 + _sw_bias_t(sw, dbias_sw)
        gs["gq_ax"][l] = dgq[0, :HEAD_DIM] + dgq[0, HEAD_DIM:]
        gs["gk_ax"][l] = dgk[0, :HEAD_DIM] + dgk[0, HEAD_DIM:]
        gs["g_attn"][l] = gs["g_attn"][l][0]
        gs["g_ffn"][l] = gs["g_ffn"][l][0]
        gs["g_group"][l] = gs["g_group"][l][0]

    gw = {k: jnp.stack(v) for k, v in gw.items()}
    small = {k: jnp.stack(v) for k, v in gs.items()}
    small["t5_table"] = d_t5
    small["g_final"] = dg_final[0]
    return loss_row[0, 0], dx, gw, small


MOD_ROWS = 16


def _mod_fwd(cond16, w):
    L, D, C = w.shape
    tn = _tile(C, 512, LANES)

    def body(c_ref, w_ref, o_ref):
        o_ref[0] = jnp.dot(c_ref[...], w_ref[0].astype(BF16), preferred_element_type=F32)

    return pl.pallas_call(
        body, name="mod_fwd", grid=(L, C // tn),
        in_specs=[pl.BlockSpec((MOD_ROWS, D), lambda l, j: (0, 0)), pl.BlockSpec((1, D, tn), lambda l, j: (l, 0, j))],
        out_specs=pl.BlockSpec((1, MOD_ROWS, tn), lambda l, j: (l, 0, j)),
        out_shape=jax.ShapeDtypeStruct((L, MOD_ROWS, C), F32),
        compiler_params=_params("parallel", "parallel"),
    )(cond16, w)


def _adamw_math(w, g, m, v):
    m = ADAM_B1 * m + (1.0 - ADAM_B1) * g
    v = ADAM_B2 * v + (1.0 - ADAM_B2) * (g * g)
    m_hat = m / (1.0 - ADAM_B1 ** ADAM_STEP)
    v_hat = v / (1.0 - ADAM_B2 ** ADAM_STEP)
    delta = -ADAM_LR * (m_hat / (jnp.sqrt(v_hat) + ADAM_EPS) + ADAM_WD * w)
    return delta, m, v


def _adamw(w, m, v, parts, name):
    R, C = w.shape
    tr = _tile(R, 256)
    n = len(parts)

    def body(*refs):
        w_ref, m_ref, v_ref = refs[:3]
        g = refs[3][...]
        for p in refs[4:3 + n]:
            g = g + p[...]
        g_ref, d_ref, m2_ref, v2_ref = refs[3 + n:]
        g_ref[...] = g
        d_ref[...], m2_ref[...], v2_ref[...] = _adamw_math(w_ref[...], g, m_ref[...], v_ref[...])

    spec = _row_spec(tr, C)
    shape = jax.ShapeDtypeStruct((R, C), F32)
    return pl.pallas_call(
        body, name=name, grid=(R // tr,), in_specs=[spec] * (3 + n), out_specs=[spec] * 4, out_shape=[shape] * 4,
        compiler_params=_params("parallel"),
    )(w, m, v, *parts)


def _wmod_adamw(cond_t, dmod16, w, m, v):
    L, D, C = w.shape
    tr = _tile(D, 256)

    def body(c_ref, d_ref, w_ref, m_ref, v_ref, g_ref, dl_ref, m2_ref, v2_ref):
        g = jnp.dot(c_ref[...], d_ref[0], preferred_element_type=F32)
        g_ref[0] = g
        dl_ref[0], m2_ref[0], v2_ref[0] = _adamw_math(w_ref[0], g, m_ref[0], v_ref[0])

    spec = pl.BlockSpec((1, tr, C), lambda l, i: (l, i, 0))
    shape = jax.ShapeDtypeStruct((L, D, C), F32)
    return pl.pallas_call(
        body, name="wmod_adamw", grid=(L, D // tr),
        in_specs=[pl.BlockSpec((tr, MOD_ROWS), lambda l, i: (i, 0)),
                  pl.BlockSpec((1, MOD_ROWS, C), lambda l, i: (l, 0, 0)), spec, spec, spec],
        out_specs=[spec] * 4, out_shape=[shape] * 4,
        compiler_params=_params("parallel", "parallel"),
    )(cond_t, dmod16, w, m, v)


def _sum_slots(a):
    P, R, C = a.shape
    tr = _tile(R, 256)

    def body(a_ref, o_ref):
        s = a_ref[0]
        for k in range(1, P):
            s = s + a_ref[k]
        o_ref[...] = s

    return pl.pallas_call(
        body, name="sum_slots", grid=(R // tr,),
        in_specs=[pl.BlockSpec((P, tr, C), lambda i: (0, i, 0))], out_specs=_row_spec(tr, C),
        out_shape=jax.ShapeDtypeStruct((R, C), F32), compiler_params=_params("parallel"),
    )(a)


def _axes():
    return lax.axis_index("x"), lax.axis_index("y"), lax.axis_index("c")


def _allgather_devices(v):
    N = v.shape[1]

    def body(v_ref, out_ref, send_sems, recv_sems, local_sem):
        x, y, c = _axes()

        def row(px, py, pc):
            return out_ref.at[pl.ds(4 * px + 2 * py + pc, 1), :]

        mine = pltpu.make_async_copy(v_ref, row(x, y, c), local_sem)
        mine.start()
        sends, recvs = [], []
        for k in range(1, N_DEV):
            peer = (x ^ (k >> 2), y ^ ((k >> 1) & 1), c ^ (k & 1))
            sems = dict(send_sem=send_sems.at[k - 1], recv_sem=recv_sems.at[k - 1], device_id=peer, device_id_type=MESH)
            sends.append(pltpu.make_async_remote_copy(src_ref=v_ref, dst_ref=row(x, y, c), **sems))
            recvs.append(pltpu.make_async_remote_copy(src_ref=v_ref, dst_ref=row(*peer), **sems))
        for cp in sends:
            cp.start()
        for cp in recvs:
            cp.wait_recv()
        for cp in sends:
            cp.wait_send()
        mine.wait()

    vmem = pl.BlockSpec(memory_space=pltpu.VMEM)
    return pl.pallas_call(
        body, name="allgather_devices", in_specs=[vmem], out_specs=vmem,
        out_shape=jax.ShapeDtypeStruct((N_DEV, N), v.dtype),
        scratch_shapes=[pltpu.SemaphoreType.DMA((N_DEV - 1,)), pltpu.SemaphoreType.DMA((N_DEV - 1,)),
                        pltpu.SemaphoreType.DMA],
        compiler_params=pltpu.CompilerParams(vmem_limit_bytes=VMEM_LIMIT_V7X),
    )(v)


def _chip_pos(order, px, py):
    return 2 * px + py if order == "natural" else 2 * py + px


def _block(ref, axis, pos, width):
    idx = [slice(None)] * len(ref.shape)
    idx[axis] = pl.ds(pl.multiple_of(pos * width, width), width)
    return ref.at[tuple(idx)]


def _chip_allgather(shards, axes, orders, name):
    n = len(shards)
    out_shapes = []
    for s, ax in zip(shards, axes):
        shp = list(s.shape)
        shp[ax] *= N_CHIPS
        out_shapes.append(jax.ShapeDtypeStruct(tuple(shp), s.dtype))

    def body(*refs):
        ins, outs = refs[:n], refs[n:2 * n]
        send_sems, recv_sems, local_sems = refs[2 * n:]
        x, y, c = _axes()
        place = lambda i, px, py: _block(outs[i], axes[i], _chip_pos(orders[i], px, py), shards[i].shape[axes[i]])
        local, sends, recvs = [], [], []
        for i in range(n):
            local.append(pltpu.make_async_copy(ins[i], place(i, x, y), local_sems.at[i]))
            for k in range(1, N_CHIPS):
                px, py = x ^ (k >> 1), y ^ (k & 1)
                j = i * (N_CHIPS - 1) + k - 1
                sems = dict(send_sem=send_sems.at[j], recv_sem=recv_sems.at[j], device_id=(px, py, c),
                            device_id_type=MESH)
                sends.append(pltpu.make_async_remote_copy(src_ref=ins[i], dst_ref=place(i, x, y), **sems))
                recvs.append(pltpu.make_async_remote_copy(src_ref=ins[i], dst_ref=place(i, px, py), **sems))
        for cp in local + sends:
            cp.start()
        for cp in recvs:
            cp.wait_recv()
        for cp in sends:
            cp.wait_send()
        for cp in local:
            cp.wait()

    hbm = pl.BlockSpec(memory_space=pl.ANY)
    nsem = n * (N_CHIPS - 1)
    return pl.pallas_call(
        body, name=name, in_specs=[hbm] * n, out_specs=[hbm] * n, out_shape=out_shapes,
        scratch_shapes=[pltpu.SemaphoreType.DMA((nsem,)), pltpu.SemaphoreType.DMA((nsem,)),
                        pltpu.SemaphoreType.DMA((n,))],
    )(*shards)


def _chip_scatter(grads, axes, orders, name):
    n = len(grads)
    widths, out_shapes = [], []
    for g, ax in zip(grads, axes):
        shp = list(g.shape)
        shp[ax] //= N_CHIPS
        widths.append(shp[ax])
        out_shapes.append(jax.ShapeDtypeStruct((N_CHIPS,) + tuple(shp), g.dtype))

    def body(*refs):
        ins, outs = refs[:n], refs[n:2 * n]
        send_sems, recv_sems, local_sems = refs[2 * n:]
        x, y, c = _axes()
        piece = lambda i, px, py: _block(ins[i], axes[i], _chip_pos(orders[i], px, py), widths[i])
        slot = lambda i, px, py: outs[i].at[2 * px + py]
        local, sends, recvs = [], [], []
        for i in range(n):
            local.append(pltpu.make_async_copy(piece(i, x, y), slot(i, x, y), local_sems.at[i]))
            for k in range(1, N_CHIPS):
                px, py = x ^ (k >> 1), y ^ (k & 1)
                j = i * (N_CHIPS - 1) + k - 1
                sems = dict(send_sem=send_sems.at[j], recv_sem=recv_sems.at[j], device_id=(px, py, c),
                            device_id_type=MESH)
                sends.append(pltpu.make_async_remote_copy(src_ref=piece(i, px, py), dst_ref=slot(i, x, y), **sems))
                recvs.append(pltpu.make_async_remote_copy(src_ref=piece(i, px, py), dst_ref=slot(i, px, py), **sems))
        for cp in local + sends:
            cp.start()
        for cp in recvs:
            cp.wait_recv()
        for cp in sends:
            cp.wait_send()
        for cp in local:
            cp.wait()

    hbm = pl.BlockSpec(memory_space=pl.ANY)
    nsem = n * (N_CHIPS - 1)
    return pl.pallas_call(
        body, name=name, in_specs=[hbm] * n, out_specs=[hbm] * n, out_shape=out_shapes,
        scratch_shapes=[pltpu.SemaphoreType.DMA((nsem,)), pltpu.SemaphoreType.DMA((nsem,)),
                        pltpu.SemaphoreType.DMA((n,))],
    )(*grads)


def _core_swap(arrays, name):
    n = len(arrays)

    def body(*refs):
        ins, outs = refs[:n], refs[n:2 * n]
        send_sems, recv_sems = refs[2 * n:]
        x, y, c = _axes()
        copies = [pltpu.make_async_remote_copy(src_ref=ins[i], dst_ref=outs[i], send_sem=send_sems.at[i],
                                               recv_sem=recv_sems.at[i], device_id=(x, y, 1 - c), device_id_type=MESH)
                  for i in range(n)]
        for cp in copies:
            cp.start()
        for cp in copies:
            cp.wait_recv()
        for cp in copies:
            cp.wait_send()

    hbm = pl.BlockSpec(memory_space=pl.ANY)
    return pl.pallas_call(
        body, name=name, in_specs=[hbm] * n, out_specs=[hbm] * n,
        out_shape=[jax.ShapeDtypeStruct(a.shape, a.dtype) for a in arrays],
        scratch_shapes=[pltpu.SemaphoreType.DMA((n,)), pltpu.SemaphoreType.DMA((n,))],
    )(*arrays)


SMALL = ("b_mod", "g_attn", "rpb_na", "sink_sw", "t5_table", "gq_ax", "gk_ax", "g_group", "g_ffn", "g_final")
BIG = ("w_in", "w_o", "w_gu", "w_down")
BIG_AXIS = {"w_in": 2, "w_o": 1, "w_gu": 2, "w_down": 1}
BIG_ORDER = {"w_in": "natural", "w_o": "natural", "w_gu": "gate_up_tiles", "w_down": "natural"}
WEIGHTS = ("w_mod", "b_mod", "g_attn", "w_in", "rpb_na", "sink_sw", "t5_table", "gq_ax", "gk_ax", "g_group",
           "w_o", "g_ffn", "w_gu", "w_down", "g_final")


def _pack(arrs):
    flat = jnp.concatenate([a.reshape(-1) for a in arrs])
    n = flat.shape[0]
    padded = -(-n // (8 * LANES)) * (8 * LANES)
    return jnp.pad(flat, (0, padded - n))


def _unpack(flat, like):
    out, off = [], 0
    for a in like:
        out.append(flat[off:off + a.size].reshape(a.shape))
        off += a.size
    return out


def kernel(x, c, w_mod, b_mod, g_attn, w_in, rpb_na, sink_sw, t5_table, gq_ax, gk_ax, g_group, w_o, g_ffn, w_gu, w_down, g_final, loss_target, m_w_mod, m_b_mod, m_g_attn, m_w_in, m_rpb_na, m_sink_sw, m_t5_table, m_gq_ax, m_gk_ax, m_g_group, m_w_o, m_g_ffn, m_w_gu, m_w_down, m_g_final, v_w_mod, v_b_mod, v_g_attn, v_w_in, v_rpb_na, v_sink_sw, v_t5_table, v_gq_ax, v_gk_ax, v_g_group, v_w_o, v_g_ffn, v_w_gu, v_w_down, v_g_final):
    W = dict(w_mod=w_mod, b_mod=b_mod, g_attn=g_attn, w_in=w_in, rpb_na=rpb_na, sink_sw=sink_sw, t5_table=t5_table,
             gq_ax=gq_ax, gk_ax=gk_ax, g_group=g_group, w_o=w_o, g_ffn=g_ffn, w_gu=w_gu, w_down=w_down,
             g_final=g_final)
    M = dict(w_mod=m_w_mod, b_mod=m_b_mod, g_attn=m_g_attn, w_in=m_w_in, rpb_na=m_rpb_na, sink_sw=m_sink_sw,
             t5_table=m_t5_table, gq_ax=m_gq_ax, gk_ax=m_gk_ax, g_group=m_g_group, w_o=m_w_o, g_ffn=m_g_ffn,
             w_gu=m_w_gu, w_down=m_w_down, g_final=m_g_final)
    V = dict(w_mod=v_w_mod, b_mod=v_b_mod, g_attn=v_g_attn, w_in=v_w_in, rpb_na=v_rpb_na, sink_sw=v_sink_sw,
             t5_table=v_t5_table, gq_ax=v_gq_ax, gk_ax=v_gk_ax, g_group=v_g_group, w_o=v_w_o, g_ffn=v_g_ffn,
             w_gu=v_w_gu, w_down=v_w_down, g_final=v_g_final)
    xi, yi, ci = _axes()
    me = 4 * xi + 2 * yi + ci
    chip = 2 * xi + yi
    D = x.shape[-1]
    mod_w = w_mod.shape[2]

    c_all = _allgather_devices(c)
    cond = c_all * (1.0 / (1.0 + jnp.exp(-c_all)))
    cond16 = jnp.pad(cond, ((0, MOD_ROWS - N_DEV), (0, 0))).astype(BF16)
    mod_part = _mod_fwd(cond16, w_mod)
    (mod_all,) = _chip_allgather([mod_part], [2], ["natural"], "allgather_mod")
    mod = lax.dynamic_slice_in_dim(mod_all, me, 1, axis=1)[:, 0, :] + b_mod

    w_full = _chip_allgather([W[k].astype(BF16) for k in BIG], [BIG_AXIS[k] for k in BIG],
                             [BIG_ORDER[k] for k in BIG], "allgather_weights")
    loss_part, grad_x, gw, small = _local_step(x[0], loss_target[0], mod, *w_full, g_attn, rpb_na, sink_sw, t5_table,
                                               gq_ax, gk_ax, g_group, g_ffn, g_final)

    small_all = _allgather_devices(_pack([small[k] for k in SMALL])[None, :])
    rows = small_all.shape[1] // LANES
    parts = [small_all[k].reshape(rows, LANES) for k in range(N_DEV)]
    pk = lambda d: _pack([d[k] for k in SMALL]).reshape(rows, LANES)
    small_out = [_unpack(o.reshape(-1), [W[k] for k in SMALL]) for o in _adamw(pk(W), pk(M), pk(V), parts, "adamw_small")]

    L = w_mod.shape[0]
    dmod_all = small_all[:, :L * 6 * D].reshape(N_DEV, L, 6 * D)
    dmod_mine = lax.dynamic_slice_in_dim(dmod_all, chip * mod_w, mod_w, axis=2)
    dmod16 = jnp.pad(jnp.transpose(dmod_mine, (1, 0, 2)), ((0, 0), (0, MOD_ROWS - N_DEV), (0, 0))).astype(BF16)
    wmod_out = _wmod_adamw(jnp.transpose(cond16), dmod16, w_mod, m_w_mod, v_w_mod)

    names = list(BIG)
    slots = _chip_scatter([gw[k] for k in names], [BIG_AXIS[k] for k in names], [BIG_ORDER[k] for k in names],
                          "scatter_grads")
    two_d = lambda a: a.reshape(-1, a.shape[-1])
    mine = [_sum_slots(s.reshape(N_CHIPS, -1, s.shape[-1])) for s in slots]
    theirs = _core_swap(mine, "swap_grads")
    big_out = {}
    for k, a, b in zip(names, mine, theirs):
        outs = _adamw(two_d(W[k]), two_d(M[k]), two_d(V[k]), [a, b], "adamw_" + k)
        big_out[k] = [o.reshape(W[k].shape) for o in outs]

    loss = lax.psum(loss_part, ("x", "y", "c"))
    per_kind = []
    for kind in range(4):
        for k in WEIGHTS:
            if k == "w_mod":
                per_kind.append(wmod_out[kind])
            elif k in big_out:
                per_kind.append(big_out[k][kind])
            else:
                per_kind.append(small_out[kind][SMALL.index(k)])
    return (loss, grad_x[None], *per_kind)
```

```python
import functools
import math

import jax
import jax.numpy as jnp
from jax import lax
from jax.experimental import pallas as pl
from jax.experimental.pallas import tpu as pltpu

F32 = jnp.float32
BF16 = jnp.bfloat16
I32 = jnp.int32

DEPTH = 2
HEAD_DIM = 64
GRID_W = 64
NA_HEADS = 4
SW_HEADS = 6
SW_KV_HEADS = 2
AX_HEADS = 6
AX_KV_HEADS = 2
NA_WIN_ROWS = 8
NA_WIN_COLS = 16
SW_RADIUS = 128
T5_BUCKETS = 32
T5_MAX_DIST = 128
ROPE_THETA = 10000.0
EPS = 1e-6
NEG_INF = -1e30
QK_SCALE = HEAD_DIM ** -0.5

NA_W = NA_HEADS * HEAD_DIM
SW_W = SW_HEADS * HEAD_DIM
SW_KV_W = SW_KV_HEADS * HEAD_DIM
AX_W = AX_HEADS * HEAD_DIM
AX_KV_W = AX_KV_HEADS * HEAD_DIM
OFF_QA, OFF_KA, OFF_VA = 0, NA_W, 2 * NA_W
OFF_QB = 3 * NA_W
OFF_KB = OFF_QB + SW_W
OFF_VB = OFF_KB + SW_KV_W
OFF_QC = OFF_VB + SW_KV_W
OFF_KC = OFF_QC + AX_W
OFF_VC = OFF_KC + AX_KV_W
IN_WIDTH = OFF_VC + AX_KV_W
MIX_WIDTH = NA_W + SW_W + AX_W

ADAM_LR = 0.001
ADAM_B1 = 0.9
ADAM_B2 = 0.999
ADAM_EPS = 1e-08
ADAM_WD = 0.01
ADAM_STEP = 10

N_CHIPS = 4
N_DEV = 8
LANES = 128
VMEM_LIMIT_V7X = 56 * 1024 * 1024
MESH = pl.DeviceIdType.MESH

NT = (((1,), (1,)), ((), ()))
TN = (((0,), (0,)), ((), ()))


def _params(*sem):
    return pltpu.CompilerParams(dimension_semantics=sem if sem else None,
                                vmem_limit_bytes=VMEM_LIMIT_V7X)


def _tile(n, pref, mult=8):
    t = (min(pref, n) // mult) * mult
    while t >= mult:
        if n % t == 0:
            return t
        t -= mult
    return n


def _row_spec(tm, width, col=0):
    return pl.BlockSpec((tm, width), lambda i, *_: (i, col))


def _const_spec(shape):
    nd = len(shape)
    return pl.BlockSpec(shape, lambda *_: (0,) * nd)


def _rsq(ms):
    return lax.rsqrt(ms + EPS)


def _rope_tables(S):
    rows = S // GRID_W
    axis_dim = HEAD_DIM // 2
    quarter = axis_dim // 2
    lane = jnp.arange(LANES)
    freq = (ROPE_THETA ** (-(2 * (lane % quarter)).astype(F32) / axis_dim))[None, :]
    by_row = ((lane % HEAD_DIM) < axis_dim)[None, None, :]
    first = ((lane % axis_dim) < quarter)[None, :]
    ang_r = jnp.arange(rows, dtype=F32)[:, None] * freq
    ang_c = jnp.arange(GRID_W, dtype=F32)[:, None] * freq

    def table(fr, fc):
        t = jnp.where(by_row, fr[:, None, :], fc[None, :, :])
        return t.reshape(S, LANES)

    sin_r, sin_c = jnp.sin(ang_r), jnp.sin(ang_c)
    return (table(jnp.cos(ang_r), jnp.cos(ang_c)),
            table(jnp.where(first, -sin_r, 0.0), jnp.where(first, -sin_c, 0.0)),
            table(jnp.where(first, 0.0, sin_r), jnp.where(first, 0.0, sin_c)))


def _pair_sum(v):
    lane = lax.broadcasted_iota(I32, v.shape, 1)
    lo = lane < HEAD_DIM
    s_lo = jnp.sum(jnp.where(lo, v, 0.0), axis=-1, keepdims=True)
    s_hi = jnp.sum(jnp.where(lo, 0.0, v), axis=-1, keepdims=True)
    return jnp.where(lo, s_lo, s_hi)


def _rope(t, cos, sa, sb):
    return t * cos + pltpu.roll(t, LANES - 16, 1) * sa + pltpu.roll(t, 16, 1) * sb


def _rope_t(t, cos, sa, sb):
    return t * cos + pltpu.roll(t * sa, 16, 1) + pltpu.roll(t * sb, LANES - 16, 1)


def _qk_prep_chunk(x, g128, cos, sa, sb):
    r = _rsq(_pair_sum(x * x) * (1.0 / HEAD_DIM))
    return _rope(x * r * g128, cos, sa, sb)


def _qk_prep_bwd_chunk(x, dy, g128, cos, sa, sb):
    dn = _rope_t(dy, cos, sa, sb)
    r = _rsq(_pair_sum(x * x) * (1.0 / HEAD_DIM))
    xhat = x * r
    dg = jnp.sum(dn * xhat, axis=0, keepdims=True)
    dxh = dn * g128
    dx = r * (dxh - xhat * (_pair_sum(dxh * xhat) * (1.0 / HEAD_DIM)))
    return dx, dg


def _ln_mod(xv, g, sc, sh):
    r = _rsq(jnp.mean(xv * xv, axis=-1, keepdims=True))
    return xv * r * g * (1.0 + sc) + sh


def _inproj_fwd(x, g, sc, sh, w, gq128, gk128, rope):
    S, D = x.shape
    tm = _tile(S, 512)
    cos, sa, sb = rope

    def body(x_ref, g_ref, sc_ref, sh_ref, w_ref, gq_ref, gk_ref, cos_ref, sa_ref, sb_ref,
             h_ref, proj_ref, qc_ref, kc_ref):
        hb = _ln_mod(x_ref[...], g_ref[...], sc_ref[...], sh_ref[...]).astype(BF16)
        h_ref[...] = hb
        acc = jnp.dot(hb, w_ref[...], preferred_element_type=F32)
        proj_ref[...] = acc.astype(BF16)
        c, a, b = cos_ref[...], sa_ref[...], sb_ref[...]
        for j in range(AX_W // LANES):
            xq = acc[:, OFF_QC + j * LANES: OFF_QC + (j + 1) * LANES]
            qc_ref[:, j * LANES:(j + 1) * LANES] = (
                _qk_prep_chunk(xq, gq_ref[...], c, a, b) * QK_SCALE).astype(BF16)
        for j in range(AX_KV_W // LANES):
            xk = acc[:, OFF_KC + j * LANES: OFF_KC + (j + 1) * LANES]
            kc_ref[:, j * LANES:(j + 1) * LANES] = _qk_prep_chunk(xk, gk_ref[...], c, a, b).astype(BF16)

    vec = _const_spec((1, D))
    v128 = _const_spec((1, LANES))
    return pl.pallas_call(
        body, name="inproj_fwd", grid=(S // tm,),
        in_specs=[_row_spec(tm, D), vec, vec, vec, _const_spec(w.shape), v128, v128,
                  _row_spec(tm, LANES), _row_spec(tm, LANES), _row_spec(tm, LANES)],
        out_specs=[_row_spec(tm, D), _row_spec(tm, IN_WIDTH), _row_spec(tm, AX_W), _row_spec(tm, AX_KV_W)],
        out_shape=[jax.ShapeDtypeStruct((S, D), BF16), jax.ShapeDtypeStruct((S, IN_WIDTH), BF16),
                   jax.ShapeDtypeStruct((S, AX_W), BF16), jax.ShapeDtypeStruct((S, AX_KV_W), BF16)],
        compiler_params=_params("parallel"),
    )(x, g, sc, sh, w, gq128, gk128, cos, sa, sb)


class _Band:
    def __init__(self, kind, S):
        self.kind = kind
        self.S = S
        if kind == "na":
            self.hq, self.g, self.halo = NA_HEADS, NA_HEADS, (NA_WIN_ROWS // 2) * GRID_W
            self.q_off, self.k_off, self.v_off = OFF_QA, OFF_KA, OFF_VA
        else:
            self.hq, self.g, self.halo = SW_HEADS, SW_KV_HEADS, SW_RADIUS
            self.q_off, self.k_off, self.v_off = OFF_QB, OFF_KB, OFF_VB
        self.bq = 2 * self.halo
        self.bk = self.bq + 2 * self.halo
        self.nb = S // self.bq
        self.rep = self.hq // self.g
        self.qw = self.hq * HEAD_DIM
        self.kw = self.g * HEAD_DIM

    def mask(self, n):
        qi = lax.broadcasted_iota(I32, (self.bq, self.bk), 0) + n * self.bq
        kj = lax.broadcasted_iota(I32, (self.bq, self.bk), 1) + (n * self.bq - self.halo)
        if self.kind == "sw":
            return (jnp.abs(kj - qi) <= SW_RADIUS) & (kj >= 0) & (kj < self.S)
        rows = self.S // GRID_W
        r, col = qi >> 6, qi & (GRID_W - 1)
        kr, kc = kj >> 6, kj & (GRID_W - 1)
        rs = jnp.clip(r - NA_WIN_ROWS // 2, 0, rows - NA_WIN_ROWS)
        cs = jnp.clip(col - NA_WIN_COLS // 2, 0, GRID_W - NA_WIN_COLS)
        return (kr >= rs) & (kr < rs + NA_WIN_ROWS) & (kc >= cs) & (kc < cs + NA_WIN_COLS)

    def qkv_specs(self):
        ratio = self.bq // self.halo
        last = self.S // self.halo - 1
        q = pl.BlockSpec((self.bq, self.qw), lambda n, o=self.q_off // self.qw: (n, o))
        specs = [q]
        for off in (self.k_off, self.v_off):
            o = off // self.kw
            specs.append(pl.BlockSpec((self.halo, self.kw), lambda n, o=o: (jnp.maximum(n * ratio - 1, 0), o)))
            specs.append(pl.BlockSpec((self.bq, self.kw), lambda n, o=o: (n, o)))
            specs.append(pl.BlockSpec((self.halo, self.kw), lambda n, o=o: (jnp.minimum((n + 1) * ratio, last), o)))
        return specs


def _band_probs(bd, h, q_ref, kcat, bias_ref, sink_ref, mask):
    sl = slice(h * HEAD_DIM, (h + 1) * HEAD_DIM)
    qh = q_ref[:, sl] * QK_SCALE
    s = lax.dot_general(qh, kcat, NT, preferred_element_type=F32) + bias_ref[h]
    s = jnp.where(mask, s, NEG_INF)
    m = jnp.max(s, axis=-1, keepdims=True)
    if sink_ref is not None:
        m = jnp.maximum(m, sink_ref[0:1, h:h + 1])
    p = jnp.exp(s - m)
    l = jnp.sum(p, axis=-1, keepdims=True)
    if sink_ref is not None:
        l = l + jnp.exp(sink_ref[0:1, h:h + 1] - m)
    return qh, p / l, m, l


def _band_fwd(bd, proj, bias, sink, gg):
    S = bd.S
    has_sink = sink is not None

    def body(*refs):
        q_ref, kp, km, kn, vp, vm, vn, bias_ref = refs[:8]
        k = 8
        sink_ref = None
        if has_sink:
            sink_ref = refs[k]
            k += 1
        gg_ref, raw_ref, yn_ref, o_scr = refs[k:k + 4]
        mask = bd.mask(pl.program_id(0))
        for g in range(bd.g):
            gs = slice(g * HEAD_DIM, (g + 1) * HEAD_DIM)
            kcat = jnp.concatenate([kp[:, gs], km[:, gs], kn[:, gs]], axis=0)
            vcat = jnp.concatenate([vp[:, gs], vm[:, gs], vn[:, gs]], axis=0)
            for h in range(g * bd.rep, (g + 1) * bd.rep):
                _, pn, _, _ = _band_probs(bd, h, q_ref, kcat, bias_ref, sink_ref, mask)
                o_scr[:, h * HEAD_DIM:(h + 1) * HEAD_DIM] = jnp.dot(
                    pn.astype(BF16), vcat, preferred_element_type=F32)
        o = o_scr[...]
        raw_ref[...] = o.astype(BF16)
        r = _rsq(jnp.mean(o * o, axis=-1, keepdims=True))
        yn_ref[...] = (o * r * gg_ref[...]).astype(BF16)

    in_specs = bd.qkv_specs() + [_const_spec(bias.shape)]
    args = [proj] * 7 + [bias]
    if has_sink:
        in_specs.append(_const_spec(sink.shape))
        args.append(sink)
    in_specs.append(_const_spec(gg.shape))
    args.append(gg)
    out = jax.ShapeDtypeStruct((S, bd.qw), BF16)
    return pl.pallas_call(
        body, name=bd.kind + "_fwd", grid=(bd.nb,), in_specs=in_specs,
        out_specs=[_row_spec(bd.bq, bd.qw), _row_spec(bd.bq, bd.qw)], out_shape=[out, out],
        scratch_shapes=[pltpu.VMEM((bd.bq, bd.qw), F32)],
        compiler_params=_params("parallel"),
    )(*args)


def _band_bwd(bd, proj, bias, sink, dy):
    S = bd.S
    has_sink = sink is not None

    def body(*refs):
        q_ref, kp, km, kn, vp, vm, vn, bias_ref = refs[:8]
        k = 8
        sink_ref = None
        if has_sink:
            sink_ref = refs[k]
            k += 1
        do_ref = refs[k]
        dq_ref, dkm, dvm, dkp, dvp, dkn, dvn, dbias_ref = refs[k + 1:k + 9]
        k += 9
        dsink_ref = None
        if has_sink:
            dsink_ref = refs[k]
            k += 1
        dk_scr, dv_scr = refs[k:k + 2]
        n = pl.program_id(0)

        @pl.when(n == 0)
        def _():
            dbias_ref[...] = jnp.zeros_like(dbias_ref)
            if has_sink:
                dsink_ref[...] = jnp.zeros_like(dsink_ref)

        mask = bd.mask(n)
        lane = lax.broadcasted_iota(I32, (1, LANES), 1)
        for g in range(bd.g):
            gs = slice(g * HEAD_DIM, (g + 1) * HEAD_DIM)
            kcat = jnp.concatenate([kp[:, gs], km[:, gs], kn[:, gs]], axis=0)
            vcat = jnp.concatenate([vp[:, gs], vm[:, gs], vn[:, gs]], axis=0)
            dk_g = jnp.zeros((bd.bk, HEAD_DIM), F32)
            dv_g = jnp.zeros((bd.bk, HEAD_DIM), F32)
            for h in range(g * bd.rep, (g + 1) * bd.rep):
                sl = slice(h * HEAD_DIM, (h + 1) * HEAD_DIM)
                qh, pn, m, l = _band_probs(bd, h, q_ref, kcat, bias_ref, sink_ref, mask)
                doh = do_ref[:, sl]
                dp = lax.dot_general(doh, vcat, NT, preferred_element_type=F32)
                delta = jnp.sum(pn * dp, axis=-1, keepdims=True)
                ds = pn * (dp - delta)
                dbias_ref[h] += ds
                if has_sink:
                    p_sink = jnp.exp(sink_ref[0:1, h:h + 1] - m) / l
                    dsink_ref[...] += jnp.where(lane == h, -jnp.sum(p_sink * delta, axis=0, keepdims=True), 0.0)
                dsb = ds.astype(BF16)
                dq_ref[:, sl] = jnp.dot(dsb, kcat, preferred_element_type=F32) * QK_SCALE
                dk_g = dk_g + lax.dot_general(dsb, qh, TN, preferred_element_type=F32)
                dv_g = dv_g + lax.dot_general(pn.astype(BF16), doh, TN, preferred_element_type=F32)
            dk_scr[:, gs] = dk_g
            dv_scr[:, gs] = dv_g
        h0, h1 = bd.halo, bd.halo + bd.bq
        dkp[0] = dk_scr[0:h0, :]
        dkm[...] = dk_scr[h0:h1, :]
        dkn[0] = dk_scr[h1:bd.bk, :]
        dvp[0] = dv_scr[0:h0, :]
        dvm[...] = dv_scr[h0:h1, :]
        dvn[0] = dv_scr[h1:bd.bk, :]

    in_specs = bd.qkv_specs() + [_const_spec(bias.shape)]
    args = [proj] * 7 + [bias]
    if has_sink:
        in_specs.append(_const_spec(sink.shape))
        args.append(sink)
    in_specs.append(_row_spec(bd.bq, bd.qw))
    args.append(dy)
    halo_spec = pl.BlockSpec((1, bd.halo, bd.kw), lambda n: (n, 0, 0))
    halo_shape = jax.ShapeDtypeStruct((bd.nb, bd.halo, bd.kw), F32)
    main_shape = jax.ShapeDtypeStruct((S, bd.kw), F32)
    out_specs = [_row_spec(bd.bq, bd.qw), _row_spec(bd.bq, bd.kw), _row_spec(bd.bq, bd.kw),
                 halo_spec, halo_spec, halo_spec, halo_spec, _const_spec(bias.shape)]
    out_shape = [jax.ShapeDtypeStruct((S, bd.qw), F32), main_shape, main_shape,
                 halo_shape, halo_shape, halo_shape, halo_shape, jax.ShapeDtypeStruct(bias.shape, F32)]
    if has_sink:
        out_specs.append(_const_spec((1, LANES)))
        out_shape.append(jax.ShapeDtypeStruct((1, LANES), F32))
    return pl.pallas_call(
        body, name=bd.kind + "_bwd", grid=(bd.nb,), in_specs=in_specs, out_specs=out_specs, out_shape=out_shape,
        scratch_shapes=[pltpu.VMEM((bd.bk, bd.kw), F32), pltpu.VMEM((bd.bk, bd.kw), F32)],
        compiler_params=_params("arbitrary"),
    )(*args)


def _halo_to_rows(prev, nxt):
    nb, halo, w = prev.shape
    z = jnp.zeros((1, halo, w), prev.dtype)
    first = jnp.concatenate([z, nxt[:-1]], axis=0)
    second = jnp.concatenate([prev[1:], z], axis=0)
    return jnp.concatenate([first, second], axis=1).reshape(nb * 2 * halo, w)


AX_PAIRS = AX_W // LANES


def _ax_blocks(S):
    return _tile(S, 512), _tile(S, 512)


def _left_half(shape):
    return lax.broadcasted_iota(I32, shape, len(shape) - 1) < HEAD_DIM


def _half_variants(a, fill=0.0):
    lo = _left_half(a.shape)
    other = jnp.full_like(a, fill)
    swapped = pltpu.roll(a, HEAD_DIM, 1)
    return ((jnp.where(lo, a, other), jnp.where(lo, other, swapped)),
            (jnp.where(lo, swapped, other), jnp.where(lo, other, a)))


def _ax_fwd(qc, kc, proj, gg):
    S = qc.shape[0]
    bq, bk = _ax_blocks(S)
    nk = S // bk
    rep = AX_HEADS // AX_KV_HEADS

    def body(q_ref, k_ref, v_ref, gg_ref, raw_ref, yn_ref, lse_ref, m_scr, acc_scr):
        kv = pl.program_id(1)

        @pl.when(kv == 0)
        def _():
            m_scr[...] = jnp.full(m_scr.shape, NEG_INF, F32)
            acc_scr[...] = jnp.zeros_like(acc_scr)

        kz, vz = _half_variants(k_ref[...]), _half_variants(v_ref[...], 1.0)
        for pr in range(AX_PAIRS):
            qp = q_ref[:, pr * LANES:(pr + 1) * LANES]
            for half in range(2):
                h = 2 * pr + half
                g = h // rep
                s = lax.dot_general(qp, kz[g][half], NT, preferred_element_type=F32)
                m_prev = m_scr[h]
                m_new = jnp.maximum(m_prev, jnp.max(s, axis=-1, keepdims=True))
                p = jnp.exp(s - jnp.tile(m_new, (1, bk // LANES)))
                acc_scr[h] = jnp.exp(m_prev - m_new) * acc_scr[h] + jnp.dot(
                    p.astype(BF16), vz[g][half], preferred_element_type=F32)
                m_scr[h] = m_new

        @pl.when(kv == nk - 1)
        def _():
            lo = _left_half((bq, LANES))
            ssq = jnp.zeros((bq, 1), F32)
            for pr in range(AX_PAIRS):
                a0, a1 = acc_scr[2 * pr], acc_scr[2 * pr + 1]
                r0, r1 = pltpu.roll(a0, HEAD_DIM, 1), pltpu.roll(a1, HEAD_DIM, 1)
                lse_ref[2 * pr] = m_scr[2 * pr] + jnp.log(jnp.where(lo, r0, a0))
                lse_ref[2 * pr + 1] = m_scr[2 * pr + 1] + jnp.log(jnp.where(lo, a1, r1))
                o = jnp.where(lo, a0 / r0, a1 / r1)
                acc_scr[pr] = o
                ssq = ssq + jnp.sum(o * o, axis=-1, keepdims=True)
            r = _rsq(ssq * (1.0 / AX_W))
            for pr in range(AX_PAIRS):
                cols = slice(pr * LANES, (pr + 1) * LANES)
                o = acc_scr[pr]
                raw_ref[:, cols] = o.astype(BF16)
                yn_ref[:, cols] = (o * r * gg_ref[:, cols]).astype(BF16)

    out = jax.ShapeDtypeStruct((S, AX_W), BF16)
    return pl.pallas_call(
        body, name="ax_fwd", grid=(S // bq, nk),
        in_specs=[pl.BlockSpec((bq, AX_W), lambda i, j: (i, 0)),
                  pl.BlockSpec((bk, AX_KV_W), lambda i, j: (j, 0)),
                  pl.BlockSpec((bk, AX_KV_W), lambda i, j: (j, OFF_VC // AX_KV_W)),
                  _const_spec(gg.shape)],
        out_specs=[pl.BlockSpec((bq, AX_W), lambda i, j: (i, 0)),
                   pl.BlockSpec((bq, AX_W), lambda i, j: (i, 0)),
                   pl.BlockSpec((AX_HEADS, bq, LANES), lambda i, j: (0, i, 0))],
        out_shape=[out, out, jax.ShapeDtypeStruct((AX_HEADS, S, LANES), F32)],
        scratch_shapes=[pltpu.VMEM((AX_HEADS, bq, LANES), F32), pltpu.VMEM((AX_HEADS, bq, LANES), F32)],
        compiler_params=_params("parallel", "arbitrary"),
    )(qc, kc, proj, gg)


def _ax_delta(dy, raw):
    S = dy.shape[0]
    tm = _tile(S, 512)

    def body(do_ref, o_ref, delta_ref):
        lo = _left_half((tm, LANES))
        for pr in range(AX_PAIRS):
            cols = slice(pr * LANES, (pr + 1) * LANES)
            prod = do_ref[:, cols].astype(F32) * o_ref[:, cols].astype(F32)
            left = jnp.sum(jnp.where(lo, prod, 0.0), axis=-1, keepdims=True)
            right = jnp.sum(jnp.where(lo, 0.0, prod), axis=-1, keepdims=True)
            delta_ref[2 * pr] = jnp.broadcast_to(left, (tm, LANES))
            delta_ref[2 * pr + 1] = jnp.broadcast_to(right, (tm, LANES))

    return pl.pallas_call(
        body, name="ax_delta", grid=(S // tm,), in_specs=[_row_spec(tm, AX_W), _row_spec(tm, AX_W)],
        out_specs=pl.BlockSpec((AX_HEADS, tm, LANES), lambda i: (0, i, 0)),
        out_shape=jax.ShapeDtypeStruct((AX_HEADS, S, LANES), F32), compiler_params=_params("parallel"),
    )(dy, raw)


def _ax_bwd(qc, kc, proj, dy, lse_row, delta_row):
    S = qc.shape[0]
    bq, bk = _ax_blocks(S)
    nq, nk = S // bq, S // bk
    rep = AX_HEADS // AX_KV_HEADS

    def body(q_ref, k_ref, v_ref, do_ref, lse_ref, delta_ref, dk_ref, dv_ref, dq_hbm, dq_scr, sem):
        j, i = pl.program_id(0), pl.program_id(1)

        @pl.when(i == 0)
        def _():
            dk_ref[...] = jnp.zeros_like(dk_ref)
            dv_ref[...] = jnp.zeros_like(dv_ref)

        @pl.when(j == 0)
        def _():
            dq_scr[i] = jnp.zeros((bq, AX_W), F32)

        kz, vz = _half_variants(k_ref[...]), _half_variants(v_ref[...])
        dk, dv = None, None
        for pr in range(AX_PAIRS):
            cols = slice(pr * LANES, (pr + 1) * LANES)
            qp, dop = q_ref[:, cols], do_ref[:, cols]
            qz, doz = _half_variants(qp), _half_variants(dop)
            dq = None
            for half in range(2):
                h = 2 * pr + half
                g = h // rep
                s_t = lax.dot_general(kz[g][half], qp, NT, preferred_element_type=F32)
                p_t = jnp.exp(s_t - lse_ref[h])
                dp_t = lax.dot_general(vz[g][half], dop, NT, preferred_element_type=F32)
                ds_t = (p_t * (dp_t - delta_ref[h])).astype(BF16)
                a = jnp.dot(p_t.astype(BF16), doz[half][g], preferred_element_type=F32)
                b = jnp.dot(ds_t, qz[half][g], preferred_element_type=F32)
                d = lax.dot_general(ds_t, kz[g][half], TN, preferred_element_type=F32)
                dv = a if dv is None else dv + a
                dk = b if dk is None else dk + b
                dq = d if dq is None else dq + d
            dq_scr[i, :, cols] += dq
        dv_ref[...] += dv
        dk_ref[...] += dk

        @pl.when(j == nk - 1)
        def _():
            dq_scr[i] = dq_scr[i] * QK_SCALE
            out = pltpu.make_async_copy(dq_scr.at[i], dq_hbm.at[pl.ds(pl.multiple_of(i * bq, bq), bq), :], sem)
            out.start()
            out.wait()

    qspec = pl.BlockSpec((bq, AX_W), lambda j, i: (i, 0))
    kspec = pl.BlockSpec((bk, AX_KV_W), lambda j, i: (j, 0))
    stat = pl.BlockSpec((AX_HEADS, 1, bq), lambda j, i: (0, 0, i))
    out = jax.ShapeDtypeStruct((S, AX_KV_W), F32)
    dk, dv, dq = pl.pallas_call(
        body, name="ax_bwd", grid=(nk, nq),
        in_specs=[qspec, kspec, pl.BlockSpec((bk, AX_KV_W), lambda j, i: (j, OFF_VC // AX_KV_W)),
                  qspec, stat, stat],
        out_specs=[kspec, kspec, pl.BlockSpec(memory_space=pl.ANY)],
        out_shape=[out, out, jax.ShapeDtypeStruct((S, AX_W), F32)],
        scratch_shapes=[pltpu.VMEM((nq, bq, AX_W), F32), pltpu.SemaphoreType.DMA],
        compiler_params=_params("arbitrary", "arbitrary"),
    )(qc, kc, proj, dy, lse_row, delta_row)
    return dq, dk, dv


def _oproj_fwd(x, yna, ynb, ync, w, gt):
    S, D = x.shape
    tm = _tile(S, 512)

    def body(x_ref, a_ref, b_ref, c_ref, w_ref, gt_ref, x1_ref, ao_ref, yn_ref):
        yn_ref[:, 0:NA_W] = a_ref[...]
        yn_ref[:, NA_W:NA_W + SW_W] = b_ref[...]
        yn_ref[:, NA_W + SW_W:MIX_WIDTH] = c_ref[...]
        acc = jnp.dot(yn_ref[...], w_ref[...], preferred_element_type=F32)
        ao_ref[...] = acc.astype(BF16)
        x1_ref[...] = x_ref[...] + gt_ref[...] * acc

    return pl.pallas_call(
        body, name="oproj_fwd", grid=(S // tm,),
        in_specs=[_row_spec(tm, D), _row_spec(tm, NA_W), _row_spec(tm, SW_W), _row_spec(tm, AX_W),
                  _const_spec(w.shape), _const_spec((1, D))],
        out_specs=[_row_spec(tm, D), _row_spec(tm, D), _row_spec(tm, MIX_WIDTH)],
        out_shape=[jax.ShapeDtypeStruct((S, D), F32), jax.ShapeDtypeStruct((S, D), BF16),
                   jax.ShapeDtypeStruct((S, MIX_WIDTH), BF16)],
        compiler_params=_params("parallel"),
    )(x, yna, ynb, ync, w, gt)


def _gu_fwd(x, g, sc, sh, w):
    S, D = x.shape
    F2 = w.shape[1]
    tn = F2 // 4
    tm = _tile(S, 512)

    def body(x_ref, g_ref, sc_ref, sh_ref, w_ref, h_ref, gu_ref, act_ref):
        @pl.when(pl.program_id(1) == 0)
        def _():
            h_ref[...] = _ln_mod(x_ref[...], g_ref[...], sc_ref[...], sh_ref[...]).astype(BF16)

        acc = jnp.dot(h_ref[...], w_ref[...], preferred_element_type=F32)
        gu_ref[...] = acc.astype(BF16)
        gate, up = acc[:, :tn], acc[:, tn:]
        act_ref[...] = (gate * (1.0 / (1.0 + jnp.exp(-gate))) * up).astype(BF16)

    vec = pl.BlockSpec((1, D), lambda i, j: (0, 0))
    return pl.pallas_call(
        body, name="gu_fwd", grid=(S // tm, 2),
        in_specs=[pl.BlockSpec((tm, D), lambda i, j: (i, 0)), vec, vec, vec,
                  pl.BlockSpec((D, 2 * tn), lambda i, j: (0, j))],
        out_specs=[pl.BlockSpec((tm, D), lambda i, j: (i, 0)), pl.BlockSpec((tm, 2 * tn), lambda i, j: (i, j)),
                   pl.BlockSpec((tm, tn), lambda i, j: (i, j))],
        out_shape=[jax.ShapeDtypeStruct((S, D), BF16), jax.ShapeDtypeStruct((S, F2), BF16),
                   jax.ShapeDtypeStruct((S, F2 // 2), BF16)],
        compiler_params=_params("parallel", "arbitrary"),
    )(x, g, sc, sh, w)


def _down_fwd(x, act, w, gt):
    S, D = x.shape
    F = act.shape[1]
    tm = _tile(S, 512)

    def body(x_ref, a_ref, w_ref, gt_ref, x2_ref, fo_ref):
        acc = jnp.dot(a_ref[...], w_ref[...], preferred_element_type=F32)
        fo_ref[...] = acc.astype(BF16)
        x2_ref[...] = x_ref[...] + gt_ref[...] * acc

    return pl.pallas_call(
        body, name="down_fwd", grid=(S // tm,),
        in_specs=[_row_spec(tm, D), _row_spec(tm, F), _const_spec(w.shape), _const_spec((1, D))],
        out_specs=[_row_spec(tm, D), _row_spec(tm, D)],
        out_shape=[jax.ShapeDtypeStruct((S, D), F32), jax.ShapeDtypeStruct((S, D), BF16)],
        compiler_params=_params("parallel"),
    )(x, act, w, gt)


def _final_loss(x, g, target):
    S, D = x.shape
    tm = _tile(S, 512)

    def body(x_ref, g_ref, t_ref, dx_ref, loss_ref, dg_ref):
        @pl.when(pl.program_id(0) == 0)
        def _():
            loss_ref[...] = jnp.zeros_like(loss_ref)
            dg_ref[...] = jnp.zeros_like(dg_ref)

        xv = x_ref[...]
        r = _rsq(jnp.mean(xv * xv, axis=-1, keepdims=True))
        xhat = xv * r
        err = xhat * g_ref[...] - t_ref[...]
        loss_ref[...] += 0.5 * jnp.sum(jnp.mean(err * err, axis=-1, keepdims=True), axis=0, keepdims=True)
        dy = err * (1.0 / D)
        dg_ref[...] += jnp.sum(dy * xhat, axis=0, keepdims=True)
        dxh = dy * g_ref[...]
        dx_ref[...] = r * (dxh - xhat * jnp.mean(dxh * xhat, axis=-1, keepdims=True))

    return pl.pallas_call(
        body, name="final_loss", grid=(S // tm,),
        in_specs=[_row_spec(tm, D), _const_spec((1, D)), _row_spec(tm, D)],
        out_specs=[_row_spec(tm, D), _const_spec((1, LANES)), _const_spec((1, D))],
        out_shape=[jax.ShapeDtypeStruct((S, D), F32), jax.ShapeDtypeStruct((1, LANES), F32),
                   jax.ShapeDtypeStruct((1, D), F32)],
        compiler_params=_params("arbitrary"),
    )(x, g, target)


def _ffn_bwd1(dx2, fo, gt, w_down, gu):
    S, D = dx2.shape
    F2 = gu.shape[1]
    tn = F2 // 4
    tm = _tile(S, 512)

    def body(dx_ref, fo_ref, gt_ref, w_ref, gu_ref, dfo_ref, dgu_ref, dgt_ref):
        i, j = pl.program_id(0), pl.program_id(1)

        @pl.when((i == 0) & (j == 0))
        def _():
            dgt_ref[...] = jnp.zeros_like(dgt_ref)

        @pl.when(j == 0)
        def _():
            dxv = dx_ref[...]
            dfo_ref[...] = (dxv * gt_ref[...]).astype(BF16)
            dgt_ref[...] += jnp.sum(dxv * fo_ref[...].astype(F32), axis=0, keepdims=True)

        dact = lax.dot_general(dfo_ref[...], w_ref[...], NT, preferred_element_type=F32)
        gate = gu_ref[:, :tn].astype(F32)
        up = gu_ref[:, tn:].astype(F32)
        sig = 1.0 / (1.0 + jnp.exp(-gate))
        dgu_ref[:, :tn] = (dact * up * (sig * (1.0 + gate * (1.0 - sig)))).astype(BF16)
        dgu_ref[:, tn:] = (dact * (gate * sig)).astype(BF16)

    vec = pl.BlockSpec((1, D), lambda i, j: (0, 0))
    row = pl.BlockSpec((tm, D), lambda i, j: (i, 0))
    return pl.pallas_call(
        body, name="ffn_bwd1", grid=(S // tm, 2),
        in_specs=[row, row, vec, pl.BlockSpec((tn, D), lambda i, j: (j, 0)),
                  pl.BlockSpec((tm, 2 * tn), lambda i, j: (i, j))],
        out_specs=[row, pl.BlockSpec((tm, 2 * tn), lambda i, j: (i, j)), vec],
        out_shape=[jax.ShapeDtypeStruct((S, D), BF16), jax.ShapeDtypeStruct((S, F2), BF16),
                   jax.ShapeDtypeStruct((1, D), F32)],
        compiler_params=_params("arbitrary", "arbitrary"),
    )(dx2, fo, gt, w_down, gu)


def _nt_ln_bwd(a, w, x, g, sc, dres, name):
    S, D = x.shape
    K = a.shape[1]
    tm = _tile(S, 256)

    def body(a_ref, w_ref, x_ref, g_ref, sc_ref, dres_ref, dx_ref, dsh_ref, dsc_ref, dg_ref):
        @pl.when(pl.program_id(0) == 0)
        def _():
            dsh_ref[...] = jnp.zeros_like(dsh_ref)
            dsc_ref[...] = jnp.zeros_like(dsc_ref)
            dg_ref[...] = jnp.zeros_like(dg_ref)

        dh = lax.dot_general(a_ref[...], w_ref[...], NT, preferred_element_type=F32)
        xv = x_ref[...]
        r = _rsq(jnp.mean(xv * xv, axis=-1, keepdims=True))
        xhat = xv * r
        gv = g_ref[...]
        dsh_ref[...] += jnp.sum(dh, axis=0, keepdims=True)
        dsc_ref[...] += jnp.sum(dh * (xhat * gv), axis=0, keepdims=True)
        dn = dh * (1.0 + sc_ref[...])
        dg_ref[...] += jnp.sum(dn * xhat, axis=0, keepdims=True)
        dxh = dn * gv
        dx_ref[...] = dres_ref[...] + r * (dxh - xhat * jnp.mean(dxh * xhat, axis=-1, keepdims=True))

    vec = _const_spec((1, D))
    vshape = jax.ShapeDtypeStruct((1, D), F32)
    return pl.pallas_call(
        body, name=name, grid=(S // tm,),
        in_specs=[_row_spec(tm, K), _const_spec(w.shape), _row_spec(tm, D), vec, vec, _row_spec(tm, D)],
        out_specs=[_row_spec(tm, D), vec, vec, vec],
        out_shape=[jax.ShapeDtypeStruct((S, D), F32), vshape, vshape, vshape],
        compiler_params=_params("arbitrary"),
    )(a, w, x, g, sc, dres)


def _oproj_bwd(dx1, ao, gt, w, ya, yb, yc, gg):
    S, D = dx1.shape
    tm = _tile(S, 512)
    groups = ((0, NA_W), (NA_W, SW_W), (NA_W + SW_W, AX_W))

    def body(dx_ref, ao_ref, gt_ref, w_ref, ya_ref, yb_ref, yc_ref, gg_ref,
             dao_ref, dya_ref, dyb_ref, dyc_ref, dgt_ref, dgg_ref):
        @pl.when(pl.program_id(0) == 0)
        def _():
            dgt_ref[...] = jnp.zeros_like(dgt_ref)
            dgg_ref[...] = jnp.zeros_like(dgg_ref)

        dxv = dx_ref[...]
        dao = (dxv * gt_ref[...]).astype(BF16)
        dao_ref[...] = dao
        dgt_ref[...] += jnp.sum(dxv * ao_ref[...].astype(F32), axis=0, keepdims=True)
        dyn = lax.dot_general(dao, w_ref[...], NT, preferred_element_type=F32)
        for (off, wd), y_ref, dy_ref in zip(groups, (ya_ref, yb_ref, yc_ref), (dya_ref, dyb_ref, dyc_ref)):
            y = y_ref[...].astype(F32)
            d = dyn[:, off:off + wd]
            r = _rsq(jnp.mean(y * y, axis=-1, keepdims=True))
            yhat = y * r
            dgg_ref[:, off:off + wd] += jnp.sum(d * yhat, axis=0, keepdims=True)
            dyh = d * gg_ref[:, off:off + wd]
            dy_ref[...] = (r * (dyh - yhat * jnp.mean(dyh * yhat, axis=-1, keepdims=True))).astype(BF16)

    vec = _const_spec((1, D))
    mvec = _const_spec((1, MIX_WIDTH))
    return pl.pallas_call(
        body, name="oproj_bwd", grid=(S // tm,),
        in_specs=[_row_spec(tm, D), _row_spec(tm, D), vec, _const_spec(w.shape),
                  _row_spec(tm, NA_W), _row_spec(tm, SW_W), _row_spec(tm, AX_W), mvec],
        out_specs=[_row_spec(tm, D), _row_spec(tm, NA_W), _row_spec(tm, SW_W), _row_spec(tm, AX_W), vec, mvec],
        out_shape=[jax.ShapeDtypeStruct((S, D), BF16), jax.ShapeDtypeStruct((S, NA_W), BF16),
                   jax.ShapeDtypeStruct((S, SW_W), BF16), jax.ShapeDtypeStruct((S, AX_W), BF16),
                   jax.ShapeDtypeStruct((1, D), F32), jax.ShapeDtypeStruct((1, MIX_WIDTH), F32)],
        compiler_params=_params("arbitrary"),
    )(dx1, ao, gt, w, ya, yb, yc, gg)


def _dproj_assemble(proj, na, sw, ax, gq128, gk128, rope):
    S = proj.shape[0]
    tm = _tile(S, 512)
    cos, sa, sb = rope

    def body(proj_ref, qa, ka, kah, va, vah, qb, kb, kbh, vb, vbh, qc, kc, vc,
             gq_ref, gk_ref, cos_ref, sa_ref, sb_ref, out_ref, dgq_ref, dgk_ref):
        @pl.when(pl.program_id(0) == 0)
        def _():
            dgq_ref[...] = jnp.zeros_like(dgq_ref)
            dgk_ref[...] = jnp.zeros_like(dgk_ref)

        out_ref[:, OFF_QA:OFF_KA] = qa[...].astype(BF16)
        out_ref[:, OFF_KA:OFF_VA] = (ka[...] + kah[...]).astype(BF16)
        out_ref[:, OFF_VA:OFF_QB] = (va[...] + vah[...]).astype(BF16)
        out_ref[:, OFF_QB:OFF_KB] = qb[...].astype(BF16)
        out_ref[:, OFF_KB:OFF_VB] = (kb[...] + kbh[...]).astype(BF16)
        out_ref[:, OFF_VB:OFF_QC] = (vb[...] + vbh[...]).astype(BF16)
        c, a, b = cos_ref[...], sa_ref[...], sb_ref[...]
        for j in range(AX_W // LANES):
            cols = slice(OFF_QC + j * LANES, OFF_QC + (j + 1) * LANES)
            dx, dg = _qk_prep_bwd_chunk(proj_ref[:, cols].astype(F32), qc[:, j * LANES:(j + 1) * LANES],
                                        gq_ref[...], c, a, b)
            out_ref[:, cols] = dx.astype(BF16)
            dgq_ref[...] += dg
        for j in range(AX_KV_W // LANES):
            cols = slice(OFF_KC + j * LANES, OFF_KC + (j + 1) * LANES)
            dx, dg = _qk_prep_bwd_chunk(proj_ref[:, cols].astype(F32), kc[:, j * LANES:(j + 1) * LANES],
                                        gk_ref[...], c, a, b)
            out_ref[:, cols] = dx.astype(BF16)
            dgk_ref[...] += dg
        out_ref[:, OFF_VC:IN_WIDTH] = vc[...].astype(BF16)

    v128 = _const_spec((1, LANES))
    r = lambda w: _row_spec(tm, w)
    return pl.pallas_call(
        body, name="dproj_assemble", grid=(S // tm,),
        in_specs=[r(IN_WIDTH), r(NA_W), r(NA_W), r(NA_W), r(NA_W), r(NA_W),
                  r(SW_W), r(SW_KV_W), r(SW_KV_W), r(SW_KV_W), r(SW_KV_W),
                  r(AX_W), r(AX_KV_W), r(AX_KV_W), v128, v128, r(LANES), r(LANES), r(LANES)],
        out_specs=[r(IN_WIDTH), v128, v128],
        out_shape=[jax.ShapeDtypeStruct((S, IN_WIDTH), BF16), jax.ShapeDtypeStruct((1, LANES), F32),
                   jax.ShapeDtypeStruct((1, LANES), F32)],
        compiler_params=_params("arbitrary"),
    )(proj, *na, *sw, *ax, gq128, gk128, cos, sa, sb)


def _tn_matmul(a, b, name):
    S, Ka = a.shape
    Nb = b.shape[1]
    tm = _tile(Ka, 1408, LANES)
    tn = _tile(Nb, 1408, LANES)
    tk = _tile(S, 512)
    nk = S // tk

    def body(a_ref, b_ref, o_ref, acc_ref):
        k = pl.program_id(2)

        @pl.when(k == 0)
        def _():
            acc_ref[...] = jnp.zeros_like(acc_ref)

        acc_ref[...] += lax.dot_general(a_ref[...], b_ref[...], TN, preferred_element_type=F32)

        @pl.when(k == nk - 1)
        def _():
            o_ref[...] = acc_ref[...].astype(BF16)

    return pl.pallas_call(
        body, name=name, grid=(Ka // tm, Nb // tn, nk),
        in_specs=[pl.BlockSpec((tk, tm), lambda i, j, k: (k, i)), pl.BlockSpec((tk, tn), lambda i, j, k: (k, j))],
        out_specs=pl.BlockSpec((tm, tn), lambda i, j, k: (i, j)),
        out_shape=jax.ShapeDtypeStruct((Ka, Nb), BF16),
        scratch_shapes=[pltpu.VMEM((tm, tn), F32)],
        compiler_params=_params("parallel", "parallel", "arbitrary"),
    )(a, b)


def _na_index(bd):
    rq = jnp.arange(bd.bq // GRID_W)
    rk = jnp.arange(bd.bk // GRID_W)
    col = jnp.arange(GRID_W)
    ri = jnp.clip(rk[None, :] - rq[:, None] - bd.halo // GRID_W + NA_WIN_ROWS - 1, 0, 2 * NA_WIN_ROWS - 2)
    ci = jnp.clip(col[None, :] - col[:, None] + NA_WIN_COLS - 1, 0, 2 * NA_WIN_COLS - 2)
    return ri, ci


def _na_one_hots(bd):
    ri, ci = _na_index(bd)
    oh_r = jax.nn.one_hot(ri, 2 * NA_WIN_ROWS - 1, dtype=F32)
    oh_c = jax.nn.one_hot(ci, 2 * NA_WIN_COLS - 1, dtype=F32)
    return oh_r, oh_c


def _na_bias(bd, rpb):
    oh_r, oh_c = _na_one_hots(bd)
    t = jnp.einsum("hab,qra->hqrb", rpb, oh_r, precision=lax.Precision.HIGHEST)
    b = jnp.einsum("hqrb,ckb->hqcrk", t, oh_c, precision=lax.Precision.HIGHEST)
    return b.reshape(NA_HEADS, bd.bq, bd.bk)


def _na_bias_t(bd, dbias):
    oh_r, oh_c = _na_one_hots(bd)
    d5 = dbias.reshape(NA_HEADS, bd.bq // GRID_W, GRID_W, bd.bk // GRID_W, GRID_W)
    t = jnp.einsum("hqcrk,ckb->hqrb", d5, oh_c, precision=lax.Precision.HIGHEST)
    return jnp.einsum("hqrb,qra->hab", t, oh_r, precision=lax.Precision.HIGHEST)


def _t5_bucket(rel):
    nb = T5_BUCKETS // 2
    ret = (rel > 0).astype(I32) * nb
    n = jnp.abs(rel)
    max_exact = nb // 2
    nf = jnp.maximum(n, max_exact).astype(F32)
    large = max_exact + (jnp.log(nf / max_exact) / math.log(T5_MAX_DIST / max_exact)
                         * (nb - max_exact)).astype(I32)
    large = jnp.minimum(large, nb - 1)
    return ret + jnp.where(n < max_exact, n, large)


def _sw_bucket(bd):
    rel = (jnp.arange(bd.bk) - bd.halo)[None, :] - jnp.arange(bd.bq)[:, None]
    return _t5_bucket(rel)


def _sw_bias(bd, t5):
    oh = jax.nn.one_hot(_sw_bucket(bd), T5_BUCKETS, dtype=F32)
    return jnp.einsum("bh,qkb->hqk", t5, oh, precision=lax.Precision.HIGHEST)


def _sw_bias_t(bd, dbias):
    oh = jax.nn.one_hot(_sw_bucket(bd), T5_BUCKETS, dtype=F32)
    return jnp.einsum("hqk,qkb->bh", dbias, oh, precision=lax.Precision.HIGHEST)


def _local_step(x, target, mod, w_in, w_o, w_gu, w_down, g_attn, rpb_na, sink_sw, t5_table, gq_ax, gk_ax,
                g_group, g_ffn, g_final):
    S, D = x.shape
    rope = _rope_tables(S)
    na, sw = _Band("na", S), _Band("sw", S)
    two = lambda v: jnp.concatenate([v, v])[None, :]
    sw_bias = _sw_bias(sw, t5_table)
    saved = []
    for l in range(DEPTH):
        sh_a, sc_a, gt_a, sh_f, sc_f, gt_f = [mod[l, k * D:(k + 1) * D][None, :] for k in range(6)]
        gq128, gk128 = two(gq_ax[l]), two(gk_ax[l])
        gg = g_group[l][None, :]
        sink = jnp.pad(sink_sw[l], (0, LANES - SW_HEADS))[None, :]
        na_bias = _na_bias(na, rpb_na[l])
        h, proj, qc, kc = _inproj_fwd(x, g_attn[l][None, :], sc_a, sh_a, w_in[l], gq128, gk128, rope)
        ya, yna = _band_fwd(na, proj, na_bias, None, gg[:, :NA_W])
        yb, ynb = _band_fwd(sw, proj, sw_bias, sink, gg[:, NA_W:NA_W + SW_W])
        yc, ync, lse = _ax_fwd(qc, kc, proj, gg[:, NA_W + SW_W:])
        x1, ao, yn = _oproj_fwd(x, yna, ynb, ync, w_o[l], gt_a)
        hf, gu, act = _gu_fwd(x1, g_ffn[l][None, :], sc_f, sh_f, w_gu[l])
        x2, fo = _down_fwd(x1, act, w_down[l], gt_f)
        saved.append(dict(x=x, x1=x1, h=h, proj=proj, qc=qc, kc=kc, ya=ya, yb=yb, yc=yc, lse=lse, ao=ao, yn=yn,
                          hf=hf, gu=gu, act=act, fo=fo, na_bias=na_bias, sink=sink, gq128=gq128, gk128=gk128,
                          gg=gg, mods=(sh_a, sc_a, gt_a, sh_f, sc_f, gt_f)))
        x = x2

    dx, loss_row, dg_final = _final_loss(x, g_final[None, :], target)
    gw = {k: [None] * DEPTH for k in ("w_in", "w_o", "w_gu", "w_down")}
    gs = {k: [None] * DEPTH for k in ("b_mod", "g_attn", "rpb_na", "sink_sw", "gq_ax", "gk_ax", "g_group", "g_ffn")}
    d_t5 = jnp.zeros((T5_BUCKETS, SW_HEADS), F32)
    for l in reversed(range(DEPTH)):
        s = saved[l]
        sh_a, sc_a, gt_a, sh_f, sc_f, gt_f = s["mods"]
        dfo, dgu, dgt_f = _ffn_bwd1(dx, s["fo"], gt_f, w_down[l], s["gu"])
        gw["w_down"][l] = _tn_matmul(s["act"], dfo, "dw_down")
        gw["w_gu"][l] = _tn_matmul(s["hf"], dgu, "dw_gu")
        dx1, dsh_f, dsc_f, gs["g_ffn"][l] = _nt_ln_bwd(dgu, w_gu[l], s["x1"], g_ffn[l][None, :], sc_f, dx, "ffn_bwd2")
        dao, dya, dyb, dyc, dgt_a, gs["g_group"][l] = _oproj_bwd(dx1, s["ao"], gt_a, w_o[l], s["ya"], s["yb"],
                                                                 s["yc"], s["gg"])
        gw["w_o"][l] = _tn_matmul(s["yn"], dao, "dw_o")
        dqa, dka, dva, dkap, dvap, dkan, dvan, dbias_na = _band_bwd(na, s["proj"], s["na_bias"], None, dya)
        dqb, dkb, dvb, dkbp, dvbp, dkbn, dvbn, dbias_sw, dsink = _band_bwd(sw, s["proj"], sw_bias, s["sink"], dyb)
        as_row = lambda a: a[:, :, 0][:, None, :]
        dqc, dkc, dvc = _ax_bwd(s["qc"], s["kc"], s["proj"], dyc, as_row(s["lse"]), as_row(_ax_delta(dyc, s["yc"])))
        dproj, dgq, dgk = _dproj_assemble(
            s["proj"], (dqa, dka, _halo_to_rows(dkap, dkan), dva, _halo_to_rows(dvap, dvan)),
            (dqb, dkb, _halo_to_rows(dkbp, dkbn), dvb, _halo_to_rows(dvbp, dvbn)), (dqc, dkc, dvc),
            s["gq128"], s["gk128"], rope)
        gw["w_in"][l] = _tn_matmul(s["h"], dproj, "dw_in")
        dx, dsh_a, dsc_a, gs["g_attn"][l] = _nt_ln_bwd(dproj, w_in[l], s["x"], g_attn[l][None, :], sc_a, dx1,
                                                       "inproj_bwd")
        gs["b_mod"][l] = jnp.concatenate([dsh_a, dsc_a, dgt_a, dsh_f, dsc_f, dgt_f], axis=1)[0]
        gs["rpb_na"][l] = _na_bias_t(na, dbias_na)
        gs["sink_sw"][l] = dsink[0, :SW_HEADS]
        d_t5 = d_t5 + _sw_bias_t(sw, dbias_sw)
        gs["gq_ax"][l] = dgq[0, :HEAD_DIM] + dgq[0, HEAD_DIM:]
        gs["gk_ax"][l] = dgk[0, :HEAD_DIM] + dgk[0, HEAD_DIM:]
        gs["g_attn"][l] = gs["g_attn"][l][0]
        gs["g_ffn"][l] = gs["g_ffn"][l][0]
        gs["g_group"][l] = gs["g_group"][l][0]

    gw = {k: jnp.stack(v) for k, v in gw.items()}
    small = {k: jnp.stack(v) for k, v in gs.items()}
    small["t5_table"] = d_t5
    small["g_final"] = dg_final[0]
    return loss_row[0, 0], dx, gw, small


MOD_ROWS = 16


def _mod_fwd(cond16, w):
    L, D, C = w.shape
    tn = _tile(C, 512, LANES)

    def body(c_ref, w_ref, o_ref):
        o_ref[0] = jnp.dot(c_ref[...], w_ref[0].astype(BF16), preferred_element_type=F32)

    return pl.pallas_call(
        body, name="mod_fwd", grid=(L, C // tn),
        in_specs=[pl.BlockSpec((MOD_ROWS, D), lambda l, j: (0, 0)), pl.BlockSpec((1, D, tn), lambda l, j: (l, 0, j))],
        out_specs=pl.BlockSpec((1, MOD_ROWS, tn), lambda l, j: (l, 0, j)),
        out_shape=jax.ShapeDtypeStruct((L, MOD_ROWS, C), F32),
        compiler_params=_params("parallel", "parallel"),
    )(cond16, w)


def _adamw_math(w, g, m, v):
    m = ADAM_B1 * m + (1.0 - ADAM_B1) * g
    v = ADAM_B2 * v + (1.0 - ADAM_B2) * (g * g)
    m_hat = m / (1.0 - ADAM_B1 ** ADAM_STEP)
    v_hat = v / (1.0 - ADAM_B2 ** ADAM_STEP)
    delta = -ADAM_LR * (m_hat / (jnp.sqrt(v_hat) + ADAM_EPS) + ADAM_WD * w)
    return delta, m, v


def _adamw(w, m, v, parts, name):
    R, C = w.shape
    tr = _tile(R, 256)
    n = len(parts)

    def body(*refs):
        w_ref, m_ref, v_ref = refs[:3]
        g = refs[3][...]
        for p in refs[4:3 + n]:
            g = g + p[...]
        g_ref, d_ref, m2_ref, v2_ref = refs[3 + n:]
        g_ref[...] = g
        d_ref[...], m2_ref[...], v2_ref[...] = _adamw_math(w_ref[...], g, m_ref[...], v_ref[...])

    spec = _row_spec(tr, C)
    shape = jax.ShapeDtypeStruct((R, C), F32)
    return pl.pallas_call(
        body, name=name, grid=(R // tr,), in_specs=[spec] * (3 + n), out_specs=[spec] * 4, out_shape=[shape] * 4,
        compiler_params=_params("parallel"),
    )(w, m, v, *parts)


def _wmod_adamw(cond_t, dmod16, w, m, v):
    L, D, C = w.shape
    tr = _tile(D, 256)

    def body(c_ref, d_ref, w_ref, m_ref, v_ref, g_ref, dl_ref, m2_ref, v2_ref):
        g = jnp.dot(c_ref[...], d_ref[0], preferred_element_type=F32)
        g_ref[0] = g
        dl_ref[0], m2_ref[0], v2_ref[0] = _adamw_math(w_ref[0], g, m_ref[0], v_ref[0])

    spec = pl.BlockSpec((1, tr, C), lambda l, i: (l, i, 0))
    shape = jax.ShapeDtypeStruct((L, D, C), F32)
    return pl.pallas_call(
        body, name="wmod_adamw", grid=(L, D // tr),
        in_specs=[pl.BlockSpec((tr, MOD_ROWS), lambda l, i: (i, 0)),
                  pl.BlockSpec((1, MOD_ROWS, C), lambda l, i: (l, 0, 0)), spec, spec, spec],
        out_specs=[spec] * 4, out_shape=[shape] * 4,
        compiler_params=_params("parallel", "parallel"),
    )(cond_t, dmod16, w, m, v)


def _sum_slots(a):
    P, R, C = a.shape
    tr = _tile(R, 256, 16)

    def body(a_ref, o_ref):
        s = a_ref[0].astype(F32)
        for k in range(1, P):
            s = s + a_ref[k].astype(F32)
        o_ref[...] = s

    return pl.pallas_call(
        body, name="sum_slots", grid=(R // tr,),
        in_specs=[pl.BlockSpec((P, tr, C), lambda i: (0, i, 0))], out_specs=_row_spec(tr, C),
        out_shape=jax.ShapeDtypeStruct((R, C), F32), compiler_params=_params("parallel"),
    )(a)


def _axes():
    return lax.axis_index("x"), lax.axis_index("y"), lax.axis_index("c")


def _allgather_devices(v):
    N = v.shape[1]

    def body(v_ref, out_ref, send_sems, recv_sems, local_sem):
        x, y, c = _axes()

        def row(px, py, pc):
            return out_ref.at[pl.ds(4 * px + 2 * py + pc, 1), :]

        mine = pltpu.make_async_copy(v_ref, row(x, y, c), local_sem)
        mine.start()
        sends, recvs = [], []
        for k in range(1, N_DEV):
            peer = (x ^ (k >> 2), y ^ ((k >> 1) & 1), c ^ (k & 1))
            sems = dict(send_sem=send_sems.at[k - 1], recv_sem=recv_sems.at[k - 1], device_id=peer, device_id_type=MESH)
            sends.append(pltpu.make_async_remote_copy(src_ref=v_ref, dst_ref=row(x, y, c), **sems))
            recvs.append(pltpu.make_async_remote_copy(src_ref=v_ref, dst_ref=row(*peer), **sems))
        for cp in sends:
            cp.start()
        for cp in recvs:
            cp.wait_recv()
        for cp in sends:
            cp.wait_send()
        mine.wait()

    vmem = pl.BlockSpec(memory_space=pltpu.VMEM)
    return pl.pallas_call(
        body, name="allgather_devices", in_specs=[vmem], out_specs=vmem,
        out_shape=jax.ShapeDtypeStruct((N_DEV, N), v.dtype),
        scratch_shapes=[pltpu.SemaphoreType.DMA((N_DEV - 1,)), pltpu.SemaphoreType.DMA((N_DEV - 1,)),
                        pltpu.SemaphoreType.DMA],
        compiler_params=pltpu.CompilerParams(vmem_limit_bytes=VMEM_LIMIT_V7X),
    )(v)


def _chip_pos(order, px, py):
    return 2 * px + py if order == "natural" else 2 * py + px


def _block(ref, axis, pos, width):
    idx = [slice(None)] * len(ref.shape)
    idx[axis] = pl.ds(pl.multiple_of(pos * width, width), width)
    return ref.at[tuple(idx)]


def _chip_allgather(shards, axes, orders, name):
    n = len(shards)
    out_shapes = []
    for s, ax in zip(shards, axes):
        shp = list(s.shape)
        shp[ax] *= N_CHIPS
        out_shapes.append(jax.ShapeDtypeStruct(tuple(shp), s.dtype))

    def body(*refs):
        ins, outs = refs[:n], refs[n:2 * n]
        send_sems, recv_sems, local_sems = refs[2 * n:]
        x, y, c = _axes()
        place = lambda i, px, py: _block(outs[i], axes[i], _chip_pos(orders[i], px, py), shards[i].shape[axes[i]])
        local, sends, recvs = [], [], []
        for i in range(n):
            local.append(pltpu.make_async_copy(ins[i], place(i, x, y), local_sems.at[i]))
            for k in range(1, N_CHIPS):
                px, py = x ^ (k >> 1), y ^ (k & 1)
                j = i * (N_CHIPS - 1) + k - 1
                sems = dict(send_sem=send_sems.at[j], recv_sem=recv_sems.at[j], device_id=(px, py, c),
                            device_id_type=MESH)
                sends.append(pltpu.make_async_remote_copy(src_ref=ins[i], dst_ref=place(i, x, y), **sems))
                recvs.append(pltpu.make_async_remote_copy(src_ref=ins[i], dst_ref=place(i, px, py), **sems))
        for cp in local + sends:
            cp.start()
        for cp in recvs:
            cp.wait_recv()
        for cp in sends:
            cp.wait_send()
        for cp in local:
            cp.wait()

    hbm = pl.BlockSpec(memory_space=pl.ANY)
    nsem = n * (N_CHIPS - 1)
    return pl.pallas_call(
        body, name=name, in_specs=[hbm] * n, out_specs=[hbm] * n, out_shape=out_shapes,
        scratch_shapes=[pltpu.SemaphoreType.DMA((nsem,)), pltpu.SemaphoreType.DMA((nsem,)),
                        pltpu.SemaphoreType.DMA((n,))],
    )(*shards)


def _chip_scatter(grads, axes, orders, name):
    n = len(grads)
    widths, out_shapes = [], []
    for g, ax in zip(grads, axes):
        shp = list(g.shape)
        shp[ax] //= N_CHIPS
        widths.append(shp[ax])
        out_shapes.append(jax.ShapeDtypeStruct((N_CHIPS,) + tuple(shp), g.dtype))

    def body(*refs):
        ins, outs = refs[:n], refs[n:2 * n]
        send_sems, recv_sems, local_sems = refs[2 * n:]
        x, y, c = _axes()
        piece = lambda i, px, py: _block(ins[i], axes[i], _chip_pos(orders[i], px, py), widths[i])
        slot = lambda i, px, py: outs[i].at[2 * px + py]
        local, sends, recvs = [], [], []
        for i in range(n):
            local.append(pltpu.make_async_copy(piece(i, x, y), slot(i, x, y), local_sems.at[i]))
            for k in range(1, N_CHIPS):
                px, py = x ^ (k >> 1), y ^ (k & 1)
                j = i * (N_CHIPS - 1) + k - 1
                sems = dict(send_sem=send_sems.at[j], recv_sem=recv_sems.at[j], device_id=(px, py, c),
                            device_id_type=MESH)
                sends.append(pltpu.make_async_remote_copy(src_ref=piece(i, px, py), dst_ref=slot(i, x, y), **sems))
                recvs.append(pltpu.make_async_remote_copy(src_ref=piece(i, px, py), dst_ref=slot(i, px, py), **sems))
        for cp in local + sends:
            cp.start()
        for cp in recvs:
            cp.wait_recv()
        for cp in sends:
            cp.wait_send()
        for cp in local:
            cp.wait()

    hbm = pl.BlockSpec(memory_space=pl.ANY)
    nsem = n * (N_CHIPS - 1)
    return pl.pallas_call(
        body, name=name, in_specs=[hbm] * n, out_specs=[hbm] * n, out_shape=out_shapes,
        scratch_shapes=[pltpu.SemaphoreType.DMA((nsem,)), pltpu.SemaphoreType.DMA((nsem,)),
                        pltpu.SemaphoreType.DMA((n,))],
    )(*grads)


def _core_swap(arrays, name):
    n = len(arrays)

    def body(*refs):
        ins, outs = refs[:n], refs[n:2 * n]
        send_sems, recv_sems = refs[2 * n:]
        x, y, c = _axes()
        copies = [pltpu.make_async_remote_copy(src_ref=ins[i], dst_ref=outs[i], send_sem=send_sems.at[i],
                                               recv_sem=recv_sems.at[i], device_id=(x, y, 1 - c), device_id_type=MESH)
                  for i in range(n)]
        for cp in copies:
            cp.start()
        for cp in copies:
            cp.wait_recv()
        for cp in copies:
            cp.wait_send()

    hbm = pl.BlockSpec(memory_space=pl.ANY)
    return pl.pallas_call(
        body, name=name, in_specs=[hbm] * n, out_specs=[hbm] * n,
        out_shape=[jax.ShapeDtypeStruct(a.shape, a.dtype) for a in arrays],
        scratch_shapes=[pltpu.SemaphoreType.DMA((n,)), pltpu.SemaphoreType.DMA((n,))],
    )(*arrays)


SMALL = ("b_mod", "g_attn", "rpb_na", "sink_sw", "t5_table", "gq_ax", "gk_ax", "g_group", "g_ffn", "g_final")
BIG = ("w_in", "w_o", "w_gu", "w_down")
BIG_AXIS = {"w_in": 2, "w_o": 1, "w_gu": 2, "w_down": 1}
BIG_ORDER = {"w_in": "natural", "w_o": "natural", "w_gu": "gate_up_tiles", "w_down": "natural"}
WEIGHTS = ("w_mod", "b_mod", "g_attn", "w_in", "rpb_na", "sink_sw", "t5_table", "gq_ax", "gk_ax", "g_group",
           "w_o", "g_ffn", "w_gu", "w_down", "g_final")


def _pack(arrs):
    flat = jnp.concatenate([a.reshape(-1) for a in arrs])
    n = flat.shape[0]
    padded = -(-n // (8 * LANES)) * (8 * LANES)
    return jnp.pad(flat, (0, padded - n))


def _unpack(flat, like):
    out, off = [], 0
    for a in like:
        out.append(flat[off:off + a.size].reshape(a.shape))
        off += a.size
    return out


def kernel(x, c, w_mod, b_mod, g_attn, w_in, rpb_na, sink_sw, t5_table, gq_ax, gk_ax, g_group, w_o, g_ffn, w_gu, w_down, g_final, loss_target, m_w_mod, m_b_mod, m_g_attn, m_w_in, m_rpb_na, m_sink_sw, m_t5_table, m_gq_ax, m_gk_ax, m_g_group, m_w_o, m_g_ffn, m_w_gu, m_w_down, m_g_final, v_w_mod, v_b_mod, v_g_attn, v_w_in, v_rpb_na, v_sink_sw, v_t5_table, v_gq_ax, v_gk_ax, v_g_group, v_w_o, v_g_ffn, v_w_gu, v_w_down, v_g_final):
    W = dict(w_mod=w_mod, b_mod=b_mod, g_attn=g_attn, w_in=w_in, rpb_na=rpb_na, sink_sw=sink_sw, t5_table=t5_table,
             gq_ax=gq_ax, gk_ax=gk_ax, g_group=g_group, w_o=w_o, g_ffn=g_ffn, w_gu=w_gu, w_down=w_down,
             g_final=g_final)
    M = dict(w_mod=m_w_mod, b_mod=m_b_mod, g_attn=m_g_attn, w_in=m_w_in, rpb_na=m_rpb_na, sink_sw=m_sink_sw,
             t5_table=m_t5_table, gq_ax=m_gq_ax, gk_ax=m_gk_ax, g_group=m_g_group, w_o=m_w_o, g_ffn=m_g_ffn,
             w_gu=m_w_gu, w_down=m_w_down, g_final=m_g_final)
    V = dict(w_mod=v_w_mod, b_mod=v_b_mod, g_attn=v_g_attn, w_in=v_w_in, rpb_na=v_rpb_na, sink_sw=v_sink_sw,
             t5_table=v_t5_table, gq_ax=v_gq_ax, gk_ax=v_gk_ax, g_group=v_g_group, w_o=v_w_o, g_ffn=v_g_ffn,
             w_gu=v_w_gu, w_down=v_w_down, g_final=v_g_final)
    xi, yi, ci = _axes()
    me = 4 * xi + 2 * yi + ci
    chip = 2 * xi + yi
    D = x.shape[-1]
    mod_w = w_mod.shape[2]

    c_all = _allgather_devices(c)
    cond = c_all * (1.0 / (1.0 + jnp.exp(-c_all)))
    cond16 = jnp.pad(cond, ((0, MOD_ROWS - N_DEV), (0, 0))).astype(BF16)
    mod_part = _mod_fwd(cond16, w_mod)
    (mod_all,) = _chip_allgather([mod_part], [2], ["natural"], "allgather_mod")
    mod = lax.dynamic_slice_in_dim(mod_all, me, 1, axis=1)[:, 0, :] + b_mod

    w_full = _chip_allgather([W[k].astype(BF16) for k in BIG], [BIG_AXIS[k] for k in BIG],
                             [BIG_ORDER[k] for k in BIG], "allgather_weights")
    loss_part, grad_x, gw, small = _local_step(x[0], loss_target[0], mod, *w_full, g_attn, rpb_na, sink_sw, t5_table,
                                               gq_ax, gk_ax, g_group, g_ffn, g_final)

    small_all = _allgather_devices(_pack([small[k] for k in SMALL])[None, :])
    rows = small_all.shape[1] // LANES
    parts = [small_all[k].reshape(rows, LANES) for k in range(N_DEV)]
    pk = lambda d: _pack([d[k] for k in SMALL]).reshape(rows, LANES)
    small_out = [_unpack(o.reshape(-1), [W[k] for k in SMALL]) for o in _adamw(pk(W), pk(M), pk(V), parts, "adamw_small")]

    L = w_mod.shape[0]
    dmod_all = small_all[:, :L * 6 * D].reshape(N_DEV, L, 6 * D)
    dmod_mine = lax.dynamic_slice_in_dim(dmod_all, chip * mod_w, mod_w, axis=2)
    dmod16 = jnp.pad(jnp.transpose(dmod_mine, (1, 0, 2)), ((0, 0), (0, MOD_ROWS - N_DEV), (0, 0))).astype(BF16)
    wmod_out = _wmod_adamw(jnp.transpose(cond16), dmod16, w_mod, m_w_mod, v_w_mod)

    names = list(BIG)
    slots = _chip_scatter([gw[k] for k in names], [BIG_AXIS[k] for k in names], [BIG_ORDER[k] for k in names],
                          "scatter_grads")
    two_d = lambda a: a.reshape(-1, a.shape[-1])
    mine = [_sum_slots(s.reshape(N_CHIPS, -1, s.shape[-1])) for s in slots]
    theirs = _core_swap(mine, "swap_grads")
    big_out = {}
    for k, a, b in zip(names, mine, theirs):
        outs = _adamw(two_d(W[k]), two_d(M[k]), two_d(V[k]), [a, b], "adamw_" + k)
        big_out[k] = [o.reshape(W[k].shape) for o in outs]

    loss = lax.psum(loss_part, ("x", "y", "c"))
    per_kind = []
    for kind in range(4):
        for k in WEIGHTS:
            if k == "w_mod":
                per_kind.append(wmod_out[kind])
            elif k in big_out:
                per_kind.append(big_out[k][kind])
            else:
                per_kind.append(small_out[kind][SMALL.index(k)])
    return (loss, grad_x[None], *per_kind)
```

```python
import functools
import math

import jax
import jax.numpy as jnp
from jax import lax
from jax.experimental import pallas as pl
from jax.experimental.pallas import tpu as pltpu

F32 = jnp.float32
BF16 = jnp.bfloat16
I32 = jnp.int32

DEPTH = 2
HEAD_DIM = 64
GRID_W = 64
NA_HEADS = 4
SW_HEADS = 6
SW_KV_HEADS = 2
AX_HEADS = 6
AX_KV_HEADS = 2
NA_WIN_ROWS = 8
NA_WIN_COLS = 16
SW_RADIUS = 128
T5_BUCKETS = 32
T5_MAX_DIST = 128
ROPE_THETA = 10000.0
EPS = 1e-6
NEG_INF = -1e30
QK_SCALE = HEAD_DIM ** -0.5

NA_W = NA_HEADS * HEAD_DIM
SW_W = SW_HEADS * HEAD_DIM
SW_KV_W = SW_KV_HEADS * HEAD_DIM
AX_W = AX_HEADS * HEAD_DIM
AX_KV_W = AX_KV_HEADS * HEAD_DIM
OFF_QA, OFF_KA, OFF_VA = 0, NA_W, 2 * NA_W
OFF_QB = 3 * NA_W
OFF_KB = OFF_QB + SW_W
OFF_VB = OFF_KB + SW_KV_W
OFF_QC = OFF_VB + SW_KV_W
OFF_KC = OFF_QC + AX_W
OFF_VC = OFF_KC + AX_KV_W
IN_WIDTH = OFF_VC + AX_KV_W
MIX_WIDTH = NA_W + SW_W + AX_W

ADAM_LR = 0.001
ADAM_B1 = 0.9
ADAM_B2 = 0.999
ADAM_EPS = 1e-08
ADAM_WD = 0.01
ADAM_STEP = 10

N_CHIPS = 4
N_DEV = 8
LANES = 128
VMEM_LIMIT_V7X = 56 * 1024 * 1024
MESH = pl.DeviceIdType.MESH

NT = (((1,), (1,)), ((), ()))
TN = (((0,), (0,)), ((), ()))


def _params(*sem):
    return pltpu.CompilerParams(dimension_semantics=sem if sem else None,
                                vmem_limit_bytes=VMEM_LIMIT_V7X)


def _tile(n, pref, mult=8):
    t = (min(pref, n) // mult) * mult
    while t >= mult:
        if n % t == 0:
            return t
        t -= mult
    return n


def _row_spec(tm, width, col=0):
    return pl.BlockSpec((tm, width), lambda i, *_: (i, col))


def _const_spec(shape):
    nd = len(shape)
    return pl.BlockSpec(shape, lambda *_: (0,) * nd)


def _rsq(ms):
    return lax.rsqrt(ms + EPS)


def _rope_tables(S):
    rows = S // GRID_W
    axis_dim = HEAD_DIM // 2
    quarter = axis_dim // 2
    lane = jnp.arange(LANES)
    freq = (ROPE_THETA ** (-(2 * (lane % quarter)).astype(F32) / axis_dim))[None, :]
    by_row = ((lane % HEAD_DIM) < axis_dim)[None, None, :]
    first = ((lane % axis_dim) < quarter)[None, :]
    ang_r = jnp.arange(rows, dtype=F32)[:, None] * freq
    ang_c = jnp.arange(GRID_W, dtype=F32)[:, None] * freq

    def table(fr, fc):
        t = jnp.where(by_row, fr[:, None, :], fc[None, :, :])
        return t.reshape(S, LANES)

    sin_r, sin_c = jnp.sin(ang_r), jnp.sin(ang_c)
    return (table(jnp.cos(ang_r), jnp.cos(ang_c)),
            table(jnp.where(first, -sin_r, 0.0), jnp.where(first, -sin_c, 0.0)),
            table(jnp.where(first, 0.0, sin_r), jnp.where(first, 0.0, sin_c)))


def _pair_sum(v):
    lane = lax.broadcasted_iota(I32, v.shape, 1)
    lo = lane < HEAD_DIM
    s_lo = jnp.sum(jnp.where(lo, v, 0.0), axis=-1, keepdims=True)
    s_hi = jnp.sum(jnp.where(lo, 0.0, v), axis=-1, keepdims=True)
    return jnp.where(lo, s_lo, s_hi)


def _rope(t, cos, sa, sb):
    return t * cos + pltpu.roll(t, LANES - 16, 1) * sa + pltpu.roll(t, 16, 1) * sb


def _rope_t(t, cos, sa, sb):
    return t * cos + pltpu.roll(t * sa, 16, 1) + pltpu.roll(t * sb, LANES - 16, 1)


def _qk_prep_chunk(x, g128, cos, sa, sb):
    r = _rsq(_pair_sum(x * x) * (1.0 / HEAD_DIM))
    return _rope(x * r * g128, cos, sa, sb)


def _qk_prep_bwd_chunk(x, dy, g128, cos, sa, sb):
    dn = _rope_t(dy, cos, sa, sb)
    r = _rsq(_pair_sum(x * x) * (1.0 / HEAD_DIM))
    xhat = x * r
    dg = jnp.sum(dn * xhat, axis=0, keepdims=True)
    dxh = dn * g128
    dx = r * (dxh - xhat * (_pair_sum(dxh * xhat) * (1.0 / HEAD_DIM)))
    return dx, dg


def _ln_mod(xv, g, sc, sh):
    r = _rsq(jnp.mean(xv * xv, axis=-1, keepdims=True))
    return xv * r * g * (1.0 + sc) + sh


def _inproj_fwd(x, g, sc, sh, w, gq128, gk128, rope):
    S, D = x.shape
    tm = _tile(S, 512)
    cos, sa, sb = rope

    def body(x_ref, g_ref, sc_ref, sh_ref, w_ref, gq_ref, gk_ref, cos_ref, sa_ref, sb_ref,
             h_ref, proj_ref, qc_ref, kc_ref):
        hb = _ln_mod(x_ref[...], g_ref[...], sc_ref[...], sh_ref[...]).astype(BF16)
        h_ref[...] = hb
        acc = jnp.dot(hb, w_ref[...], preferred_element_type=F32)
        proj_ref[...] = acc.astype(BF16)
        c, a, b = cos_ref[...], sa_ref[...], sb_ref[...]
        for j in range(AX_W // LANES):
            xq = acc[:, OFF_QC + j * LANES: OFF_QC + (j + 1) * LANES]
            qc_ref[:, j * LANES:(j + 1) * LANES] = (
                _qk_prep_chunk(xq, gq_ref[...], c, a, b) * QK_SCALE).astype(BF16)
        for j in range(AX_KV_W // LANES):
            xk = acc[:, OFF_KC + j * LANES: OFF_KC + (j + 1) * LANES]
            kc_ref[:, j * LANES:(j + 1) * LANES] = _qk_prep_chunk(xk, gk_ref[...], c, a, b).astype(BF16)

    vec = _const_spec((1, D))
    v128 = _const_spec((1, LANES))
    return pl.pallas_call(
        body, name="inproj_fwd", grid=(S // tm,),
        in_specs=[_row_spec(tm, D), vec, vec, vec, _const_spec(w.shape), v128, v128,
                  _row_spec(tm, LANES), _row_spec(tm, LANES), _row_spec(tm, LANES)],
        out_specs=[_row_spec(tm, D), _row_spec(tm, IN_WIDTH), _row_spec(tm, AX_W), _row_spec(tm, AX_KV_W)],
        out_shape=[jax.ShapeDtypeStruct((S, D), BF16), jax.ShapeDtypeStruct((S, IN_WIDTH), BF16),
                   jax.ShapeDtypeStruct((S, AX_W), BF16), jax.ShapeDtypeStruct((S, AX_KV_W), BF16)],
        compiler_params=_params("parallel"),
    )(x, g, sc, sh, w, gq128, gk128, cos, sa, sb)


class _Band:
    def __init__(self, kind, S):
        self.kind = kind
        self.S = S
        if kind == "na":
            self.hq, self.g, self.halo = NA_HEADS, NA_HEADS, (NA_WIN_ROWS // 2) * GRID_W
            self.q_off, self.k_off, self.v_off = OFF_QA, OFF_KA, OFF_VA
        else:
            self.hq, self.g, self.halo = SW_HEADS, SW_KV_HEADS, SW_RADIUS
            self.q_off, self.k_off, self.v_off = OFF_QB, OFF_KB, OFF_VB
        self.bq = 2 * self.halo
        self.bk = self.bq + 2 * self.halo
        self.nb = S // self.bq
        self.rep = self.hq // self.g
        self.qw = self.hq * HEAD_DIM
        self.kw = self.g * HEAD_DIM

    def mask(self, n):
        qi = lax.broadcasted_iota(I32, (self.bq, self.bk), 0) + n * self.bq
        kj = lax.broadcasted_iota(I32, (self.bq, self.bk), 1) + (n * self.bq - self.halo)
        if self.kind == "sw":
            return (jnp.abs(kj - qi) <= SW_RADIUS) & (kj >= 0) & (kj < self.S)
        rows = self.S // GRID_W
        r, col = qi >> 6, qi & (GRID_W - 1)
        kr, kc = kj >> 6, kj & (GRID_W - 1)
        rs = jnp.clip(r - NA_WIN_ROWS // 2, 0, rows - NA_WIN_ROWS)
        cs = jnp.clip(col - NA_WIN_COLS // 2, 0, GRID_W - NA_WIN_COLS)
        return (kr >= rs) & (kr < rs + NA_WIN_ROWS) & (kc >= cs) & (kc < cs + NA_WIN_COLS)

    def qkv_specs(self):
        ratio = self.bq // self.halo
        last = self.S // self.halo - 1
        q = pl.BlockSpec((self.bq, self.qw), lambda n, o=self.q_off // self.qw: (n, o))
        specs = [q]
        for off in (self.k_off, self.v_off):
            o = off // self.kw
            specs.append(pl.BlockSpec((self.halo, self.kw), lambda n, o=o: (jnp.maximum(n * ratio - 1, 0), o)))
            specs.append(pl.BlockSpec((self.bq, self.kw), lambda n, o=o: (n, o)))
            specs.append(pl.BlockSpec((self.halo, self.kw), lambda n, o=o: (jnp.minimum((n + 1) * ratio, last), o)))
        return specs


def _band_probs(bd, h, q_ref, kcat, bias_ref, sink_ref, mask):
    sl = slice(h * HEAD_DIM, (h + 1) * HEAD_DIM)
    qh = q_ref[:, sl] * QK_SCALE
    s = lax.dot_general(qh, kcat, NT, preferred_element_type=F32) + bias_ref[h]
    s = jnp.where(mask, s, NEG_INF)
    m = jnp.max(s, axis=-1, keepdims=True)
    if sink_ref is not None:
        m = jnp.maximum(m, sink_ref[0:1, h:h + 1])
    p = jnp.exp(s - m)
    l = jnp.sum(p, axis=-1, keepdims=True)
    if sink_ref is not None:
        l = l + jnp.exp(sink_ref[0:1, h:h + 1] - m)
    return qh, p / l, m, l


def _band_fwd(bd, proj, bias, sink, gg):
    S = bd.S
    has_sink = sink is not None

    def body(*refs):
        q_ref, kp, km, kn, vp, vm, vn, bias_ref = refs[:8]
        k = 8
        sink_ref = None
        if has_sink:
            sink_ref = refs[k]
            k += 1
        gg_ref, raw_ref, yn_ref, o_scr = refs[k:k + 4]
        mask = bd.mask(pl.program_id(0))
        for g in range(bd.g):
            gs = slice(g * HEAD_DIM, (g + 1) * HEAD_DIM)
            kcat = jnp.concatenate([kp[:, gs], km[:, gs], kn[:, gs]], axis=0)
            vcat = jnp.concatenate([vp[:, gs], vm[:, gs], vn[:, gs]], axis=0)
            for h in range(g * bd.rep, (g + 1) * bd.rep):
                _, pn, _, _ = _band_probs(bd, h, q_ref, kcat, bias_ref, sink_ref, mask)
                o_scr[:, h * HEAD_DIM:(h + 1) * HEAD_DIM] = jnp.dot(
                    pn.astype(BF16), vcat, preferred_element_type=F32)
        o = o_scr[...]
        raw_ref[...] = o.astype(BF16)
        r = _rsq(jnp.mean(o * o, axis=-1, keepdims=True))
        yn_ref[...] = (o * r * gg_ref[...]).astype(BF16)

    in_specs = bd.qkv_specs() + [_const_spec(bias.shape)]
    args = [proj] * 7 + [bias]
    if has_sink:
        in_specs.append(_const_spec(sink.shape))
        args.append(sink)
    in_specs.append(_const_spec(gg.shape))
    args.append(gg)
    out = jax.ShapeDtypeStruct((S, bd.qw), BF16)
    return pl.pallas_call(
        body, name=bd.kind + "_fwd", grid=(bd.nb,), in_specs=in_specs,
        out_specs=[_row_spec(bd.bq, bd.qw), _row_spec(bd.bq, bd.qw)], out_shape=[out, out],
        scratch_shapes=[pltpu.VMEM((bd.bq, bd.qw), F32)],
        compiler_params=_params("parallel"),
    )(*args)


def _band_bwd(bd, proj, bias, sink, dy):
    S = bd.S
    has_sink = sink is not None

    def body(*refs):
        q_ref, kp, km, kn, vp, vm, vn, bias_ref = refs[:8]
        k = 8
        sink_ref = None
        if has_sink:
            sink_ref = refs[k]
            k += 1
        do_ref = refs[k]
        dq_ref, dkm, dvm, dkp, dvp, dkn, dvn, dbias_ref = refs[k + 1:k + 9]
        k += 9
        dsink_ref = None
        if has_sink:
            dsink_ref = refs[k]
            k += 1
        dk_scr, dv_scr = refs[k:k + 2]
        n = pl.program_id(0)

        @pl.when(n == 0)
        def _():
            dbias_ref[...] = jnp.zeros_like(dbias_ref)
            if has_sink:
                dsink_ref[...] = jnp.zeros_like(dsink_ref)

        mask = bd.mask(n)
        lane = lax.broadcasted_iota(I32, (1, LANES), 1)
        for g in range(bd.g):
            gs = slice(g * HEAD_DIM, (g + 1) * HEAD_DIM)
            kcat = jnp.concatenate([kp[:, gs], km[:, gs], kn[:, gs]], axis=0)
            vcat = jnp.concatenate([vp[:, gs], vm[:, gs], vn[:, gs]], axis=0)
            dk_g = jnp.zeros((bd.bk, HEAD_DIM), F32)
            dv_g = jnp.zeros((bd.bk, HEAD_DIM), F32)
            for h in range(g * bd.rep, (g + 1) * bd.rep):
                sl = slice(h * HEAD_DIM, (h + 1) * HEAD_DIM)
                qh, pn, m, l = _band_probs(bd, h, q_ref, kcat, bias_ref, sink_ref, mask)
                doh = do_ref[:, sl]
                dp = lax.dot_general(doh, vcat, NT, preferred_element_type=F32)
                delta = jnp.sum(pn * dp, axis=-1, keepdims=True)
                ds = pn * (dp - delta)
                dbias_ref[h] += ds
                if has_sink:
                    p_sink = jnp.exp(sink_ref[0:1, h:h + 1] - m) / l
                    dsink_ref[...] += jnp.where(lane == h, -jnp.sum(p_sink * delta, axis=0, keepdims=True), 0.0)
                dsb = ds.astype(BF16)
                dq_ref[:, sl] = jnp.dot(dsb, kcat, preferred_element_type=F32) * QK_SCALE
                dk_g = dk_g + lax.dot_general(dsb, qh, TN, preferred_element_type=F32)
                dv_g = dv_g + lax.dot_general(pn.astype(BF16), doh, TN, preferred_element_type=F32)
            dk_scr[:, gs] = dk_g
            dv_scr[:, gs] = dv_g
        h0, h1 = bd.halo, bd.halo + bd.bq
        dkp[0] = dk_scr[0:h0, :]
        dkm[...] = dk_scr[h0:h1, :]
        dkn[0] = dk_scr[h1:bd.bk, :]
        dvp[0] = dv_scr[0:h0, :]
        dvm[...] = dv_scr[h0:h1, :]
        dvn[0] = dv_scr[h1:bd.bk, :]

    in_specs = bd.qkv_specs() + [_const_spec(bias.shape)]
    args = [proj] * 7 + [bias]
    if has_sink:
        in_specs.append(_const_spec(sink.shape))
        args.append(sink)
    in_specs.append(_row_spec(bd.bq, bd.qw))
    args.append(dy)
    halo_spec = pl.BlockSpec((1, bd.halo, bd.kw), lambda n: (n, 0, 0))
    halo_shape = jax.ShapeDtypeStruct((bd.nb, bd.halo, bd.kw), F32)
    main_shape = jax.ShapeDtypeStruct((S, bd.kw), F32)
    out_specs = [_row_spec(bd.bq, bd.qw), _row_spec(bd.bq, bd.kw), _row_spec(bd.bq, bd.kw),
                 halo_spec, halo_spec, halo_spec, halo_spec, _const_spec(bias.shape)]
    out_shape = [jax.ShapeDtypeStruct((S, bd.qw), F32), main_shape, main_shape,
                 halo_shape, halo_shape, halo_shape, halo_shape, jax.ShapeDtypeStruct(bias.shape, F32)]
    if has_sink:
        out_specs.append(_const_spec((1, LANES)))
        out_shape.append(jax.ShapeDtypeStruct((1, LANES), F32))
    return pl.pallas_call(
        body, name=bd.kind + "_bwd", grid=(bd.nb,), in_specs=in_specs, out_specs=out_specs, out_shape=out_shape,
        scratch_shapes=[pltpu.VMEM((bd.bk, bd.kw), F32), pltpu.VMEM((bd.bk, bd.kw), F32)],
        compiler_params=_params("arbitrary"),
    )(*args)


def _halo_to_rows(prev, nxt):
    nb, halo, w = prev.shape
    z = jnp.zeros((1, halo, w), prev.dtype)
    first = jnp.concatenate([z, nxt[:-1]], axis=0)
    second = jnp.concatenate([prev[1:], z], axis=0)
    return jnp.concatenate([first, second], axis=1).reshape(nb * 2 * halo, w)


AX_PAIRS = AX_W // LANES


AX_FWD_BLOCKS = (1024, 2048)
AX_BWD_BLOCKS = (1024, 1024)


def _ax_blocks(S, blocks):
    return _tile(S, blocks[0]), _tile(S, blocks[1])


def _left_half(shape):
    return lax.broadcasted_iota(I32, shape, len(shape) - 1) < HEAD_DIM


def _half_variants(a, fill=0.0):
    lo = _left_half(a.shape)
    other = jnp.full_like(a, fill)
    swapped = pltpu.roll(a, HEAD_DIM, 1)
    return ((jnp.where(lo, a, other), jnp.where(lo, other, swapped)),
            (jnp.where(lo, swapped, other), jnp.where(lo, other, a)))


def _ax_fwd(qc, kc, proj, gg):
    S = qc.shape[0]
    bq, bk = _ax_blocks(S, AX_FWD_BLOCKS)
    nk = S // bk
    rep = AX_HEADS // AX_KV_HEADS

    def body(q_ref, k_ref, v_ref, gg_ref, raw_ref, yn_ref, lse_ref, m_scr, acc_scr):
        kv = pl.program_id(1)

        @pl.when(kv == 0)
        def _():
            m_scr[...] = jnp.full(m_scr.shape, NEG_INF, F32)
            acc_scr[...] = jnp.zeros_like(acc_scr)

        kz, vz = _half_variants(k_ref[...]), _half_variants(v_ref[...], 1.0)
        for pr in range(AX_PAIRS):
            qp = q_ref[:, pr * LANES:(pr + 1) * LANES]
            for half in range(2):
                h = 2 * pr + half
                g = h // rep
                s = lax.dot_general(qp, kz[g][half], NT, preferred_element_type=F32)
                m_prev = m_scr[h]
                m_new = jnp.maximum(m_prev, jnp.max(s, axis=-1, keepdims=True))
                p = jnp.exp(s - jnp.tile(m_new, (1, bk // LANES)))
                acc_scr[h] = jnp.exp(m_prev - m_new) * acc_scr[h] + jnp.dot(
                    p.astype(BF16), vz[g][half], preferred_element_type=F32)
                m_scr[h] = m_new

        @pl.when(kv == nk - 1)
        def _():
            lo = _left_half((bq, LANES))
            ssq = jnp.zeros((bq, 1), F32)
            for pr in range(AX_PAIRS):
                a0, a1 = acc_scr[2 * pr], acc_scr[2 * pr + 1]
                r0, r1 = pltpu.roll(a0, HEAD_DIM, 1), pltpu.roll(a1, HEAD_DIM, 1)
                lse_ref[2 * pr] = m_scr[2 * pr] + jnp.log(jnp.where(lo, r0, a0))
                lse_ref[2 * pr + 1] = m_scr[2 * pr + 1] + jnp.log(jnp.where(lo, a1, r1))
                o = jnp.where(lo, a0 / r0, a1 / r1)
                acc_scr[pr] = o
                ssq = ssq + jnp.sum(o * o, axis=-1, keepdims=True)
            r = _rsq(ssq * (1.0 / AX_W))
            for pr in range(AX_PAIRS):
                cols = slice(pr * LANES, (pr + 1) * LANES)
                o = acc_scr[pr]
                raw_ref[:, cols] = o.astype(BF16)
                yn_ref[:, cols] = (o * r * gg_ref[:, cols]).astype(BF16)

    out = jax.ShapeDtypeStruct((S, AX_W), BF16)
    return pl.pallas_call(
        body, name="ax_fwd", grid=(S // bq, nk),
        in_specs=[pl.BlockSpec((bq, AX_W), lambda i, j: (i, 0)),
                  pl.BlockSpec((bk, AX_KV_W), lambda i, j: (j, 0)),
                  pl.BlockSpec((bk, AX_KV_W), lambda i, j: (j, OFF_VC // AX_KV_W)),
                  _const_spec(gg.shape)],
        out_specs=[pl.BlockSpec((bq, AX_W), lambda i, j: (i, 0)),
                   pl.BlockSpec((bq, AX_W), lambda i, j: (i, 0)),
                   pl.BlockSpec((AX_HEADS, bq, LANES), lambda i, j: (0, i, 0))],
        out_shape=[out, out, jax.ShapeDtypeStruct((AX_HEADS, S, LANES), F32)],
        scratch_shapes=[pltpu.VMEM((AX_HEADS, bq, LANES), F32), pltpu.VMEM((AX_HEADS, bq, LANES), F32)],
        compiler_params=_params("parallel", "arbitrary"),
    )(qc, kc, proj, gg)


def _ax_delta(dy, raw):
    S = dy.shape[0]
    tm = _tile(S, 512)

    def body(do_ref, o_ref, delta_ref):
        lo = _left_half((tm, LANES))
        for pr in range(AX_PAIRS):
            cols = slice(pr * LANES, (pr + 1) * LANES)
            prod = do_ref[:, cols].astype(F32) * o_ref[:, cols].astype(F32)
            left = jnp.sum(jnp.where(lo, prod, 0.0), axis=-1, keepdims=True)
            right = jnp.sum(jnp.where(lo, 0.0, prod), axis=-1, keepdims=True)
            delta_ref[2 * pr] = jnp.broadcast_to(left, (tm, LANES))
            delta_ref[2 * pr + 1] = jnp.broadcast_to(right, (tm, LANES))

    return pl.pallas_call(
        body, name="ax_delta", grid=(S // tm,), in_specs=[_row_spec(tm, AX_W), _row_spec(tm, AX_W)],
        out_specs=pl.BlockSpec((AX_HEADS, tm, LANES), lambda i: (0, i, 0)),
        out_shape=jax.ShapeDtypeStruct((AX_HEADS, S, LANES), F32), compiler_params=_params("parallel"),
    )(dy, raw)


def _ax_bwd(qc, kc, proj, dy, lse_row, delta_row):
    S = qc.shape[0]
    bq, bk = _ax_blocks(S, AX_BWD_BLOCKS)
    nq, nk = S // bq, S // bk
    rep = AX_HEADS // AX_KV_HEADS

    def body(q_ref, k_ref, v_ref, do_ref, lse_ref, delta_ref, dk_ref, dv_ref, dq_hbm, dq_scr, sem):
        j, i = pl.program_id(0), pl.program_id(1)

        @pl.when(i == 0)
        def _():
            dk_ref[...] = jnp.zeros_like(dk_ref)
            dv_ref[...] = jnp.zeros_like(dv_ref)

        @pl.when(j == 0)
        def _():
            dq_scr[i] = jnp.zeros((bq, AX_W), F32)

        kz, vz = _half_variants(k_ref[...]), _half_variants(v_ref[...])
        dk, dv = None, None
        for pr in range(AX_PAIRS):
            cols = slice(pr * LANES, (pr + 1) * LANES)
            qp, dop = q_ref[:, cols], do_ref[:, cols]
            qz, doz = _half_variants(qp), _half_variants(dop)
            dq = None
            for half in range(2):
                h = 2 * pr + half
                g = h // rep
                s_t = lax.dot_general(kz[g][half], qp, NT, preferred_element_type=F32)
                p_t = jnp.exp(s_t - lse_ref[h])
                dp_t = lax.dot_general(vz[g][half], dop, NT, preferred_element_type=F32)
                ds_t = (p_t * (dp_t - delta_ref[h])).astype(BF16)
                a = jnp.dot(p_t.astype(BF16), doz[half][g], preferred_element_type=F32)
                b = jnp.dot(ds_t, qz[half][g], preferred_element_type=F32)
                d = lax.dot_general(ds_t, kz[g][half], TN, preferred_element_type=F32)
                dv = a if dv is None else dv + a
                dk = b if dk is None else dk + b
                dq = d if dq is None else dq + d
            dq_scr[i, :, cols] += dq
        dv_ref[...] += dv
        dk_ref[...] += dk

        @pl.when(j == nk - 1)
        def _():
            dq_scr[i] = dq_scr[i] * QK_SCALE
            out = pltpu.make_async_copy(dq_scr.at[i], dq_hbm.at[pl.ds(pl.multiple_of(i * bq, bq), bq), :], sem)
            out.start()
            out.wait()

    qspec = pl.BlockSpec((bq, AX_W), lambda j, i: (i, 0))
    kspec = pl.BlockSpec((bk, AX_KV_W), lambda j, i: (j, 0))
    stat = pl.BlockSpec((AX_HEADS, 1, bq), lambda j, i: (0, 0, i))
    out = jax.ShapeDtypeStruct((S, AX_KV_W), F32)
    dk, dv, dq = pl.pallas_call(
        body, name="ax_bwd", grid=(nk, nq),
        in_specs=[qspec, kspec, pl.BlockSpec((bk, AX_KV_W), lambda j, i: (j, OFF_VC // AX_KV_W)),
                  qspec, stat, stat],
        out_specs=[kspec, kspec, pl.BlockSpec(memory_space=pl.ANY)],
        out_shape=[out, out, jax.ShapeDtypeStruct((S, AX_W), F32)],
        scratch_shapes=[pltpu.VMEM((nq, bq, AX_W), F32), pltpu.SemaphoreType.DMA],
        compiler_params=_params("arbitrary", "arbitrary"),
    )(qc, kc, proj, dy, lse_row, delta_row)
    return dq, dk, dv


def _oproj_fwd(x, yna, ynb, ync, w, gt):
    S, D = x.shape
    tm = _tile(S, 512)

    def body(x_ref, a_ref, b_ref, c_ref, w_ref, gt_ref, x1_ref, ao_ref, yn_ref):
        yn_ref[:, 0:NA_W] = a_ref[...]
        yn_ref[:, NA_W:NA_W + SW_W] = b_ref[...]
        yn_ref[:, NA_W + SW_W:MIX_WIDTH] = c_ref[...]
        acc = jnp.dot(yn_ref[...], w_ref[...], preferred_element_type=F32)
        ao_ref[...] = acc.astype(BF16)
        x1_ref[...] = x_ref[...] + gt_ref[...] * acc

    return pl.pallas_call(
        body, name="oproj_fwd", grid=(S // tm,),
        in_specs=[_row_spec(tm, D), _row_spec(tm, NA_W), _row_spec(tm, SW_W), _row_spec(tm, AX_W),
                  _const_spec(w.shape), _const_spec((1, D))],
        out_specs=[_row_spec(tm, D), _row_spec(tm, D), _row_spec(tm, MIX_WIDTH)],
        out_shape=[jax.ShapeDtypeStruct((S, D), F32), jax.ShapeDtypeStruct((S, D), BF16),
                   jax.ShapeDtypeStruct((S, MIX_WIDTH), BF16)],
        compiler_params=_params("parallel"),
    )(x, yna, ynb, ync, w, gt)


def _gu_fwd(x, g, sc, sh, w):
    S, D = x.shape
    F2 = w.shape[1]
    tn = F2 // 4
    tm = _tile(S, 512)

    def body(x_ref, g_ref, sc_ref, sh_ref, w_ref, h_ref, gu_ref, act_ref):
        @pl.when(pl.program_id(1) == 0)
        def _():
            h_ref[...] = _ln_mod(x_ref[...], g_ref[...], sc_ref[...], sh_ref[...]).astype(BF16)

        acc = jnp.dot(h_ref[...], w_ref[...], preferred_element_type=F32)
        gu_ref[...] = acc.astype(BF16)
        gate, up = acc[:, :tn], acc[:, tn:]
        act_ref[...] = (gate * (1.0 / (1.0 + jnp.exp(-gate))) * up).astype(BF16)

    vec = pl.BlockSpec((1, D), lambda i, j: (0, 0))
    return pl.pallas_call(
        body, name="gu_fwd", grid=(S // tm, 2),
        in_specs=[pl.BlockSpec((tm, D), lambda i, j: (i, 0)), vec, vec, vec,
                  pl.BlockSpec((D, 2 * tn), lambda i, j: (0, j))],
        out_specs=[pl.BlockSpec((tm, D), lambda i, j: (i, 0)), pl.BlockSpec((tm, 2 * tn), lambda i, j: (i, j)),
                   pl.BlockSpec((tm, tn), lambda i, j: (i, j))],
        out_shape=[jax.ShapeDtypeStruct((S, D), BF16), jax.ShapeDtypeStruct((S, F2), BF16),
                   jax.ShapeDtypeStruct((S, F2 // 2), BF16)],
        compiler_params=_params("parallel", "arbitrary"),
    )(x, g, sc, sh, w)


def _down_fwd(x, act, w, gt):
    S, D = x.shape
    F = act.shape[1]
    tm = _tile(S, 512)

    def body(x_ref, a_ref, w_ref, gt_ref, x2_ref, fo_ref):
        acc = jnp.dot(a_ref[...], w_ref[...], preferred_element_type=F32)
        fo_ref[...] = acc.astype(BF16)
        x2_ref[...] = x_ref[...] + gt_ref[...] * acc

    return pl.pallas_call(
        body, name="down_fwd", grid=(S // tm,),
        in_specs=[_row_spec(tm, D), _row_spec(tm, F), _const_spec(w.shape), _const_spec((1, D))],
        out_specs=[_row_spec(tm, D), _row_spec(tm, D)],
        out_shape=[jax.ShapeDtypeStruct((S, D), F32), jax.ShapeDtypeStruct((S, D), BF16)],
        compiler_params=_params("parallel"),
    )(x, act, w, gt)


def _final_loss(x, g, target):
    S, D = x.shape
    tm = _tile(S, 512)

    def body(x_ref, g_ref, t_ref, dx_ref, loss_ref, dg_ref):
        @pl.when(pl.program_id(0) == 0)
        def _():
            loss_ref[...] = jnp.zeros_like(loss_ref)
            dg_ref[...] = jnp.zeros_like(dg_ref)

        xv = x_ref[...]
        r = _rsq(jnp.mean(xv * xv, axis=-1, keepdims=True))
        xhat = xv * r
        err = xhat * g_ref[...] - t_ref[...]
        loss_ref[...] += 0.5 * jnp.sum(jnp.mean(err * err, axis=-1, keepdims=True), axis=0, keepdims=True)
        dy = err * (1.0 / D)
        dg_ref[...] += jnp.sum(dy * xhat, axis=0, keepdims=True)
        dxh = dy * g_ref[...]
        dx_ref[...] = r * (dxh - xhat * jnp.mean(dxh * xhat, axis=-1, keepdims=True))

    return pl.pallas_call(
        body, name="final_loss", grid=(S // tm,),
        in_specs=[_row_spec(tm, D), _const_spec((1, D)), _row_spec(tm, D)],
        out_specs=[_row_spec(tm, D), _const_spec((1, LANES)), _const_spec((1, D))],
        out_shape=[jax.ShapeDtypeStruct((S, D), F32), jax.ShapeDtypeStruct((1, LANES), F32),
                   jax.ShapeDtypeStruct((1, D), F32)],
        compiler_params=_params("arbitrary"),
    )(x, g, target)


def _ffn_bwd1(dx2, fo, gt, w_down, gu):
    S, D = dx2.shape
    F2 = gu.shape[1]
    tn = F2 // 4
    tm = _tile(S, 512)

    def body(dx_ref, fo_ref, gt_ref, w_ref, gu_ref, dfo_ref, dgu_ref, dgt_ref):
        i, j = pl.program_id(0), pl.program_id(1)

        @pl.when((i == 0) & (j == 0))
        def _():
            dgt_ref[...] = jnp.zeros_like(dgt_ref)

        @pl.when(j == 0)
        def _():
            dxv = dx_ref[...]
            dfo_ref[...] = (dxv * gt_ref[...]).astype(BF16)
            dgt_ref[...] += jnp.sum(dxv * fo_ref[...].astype(F32), axis=0, keepdims=True)

        dact = lax.dot_general(dfo_ref[...], w_ref[...], NT, preferred_element_type=F32)
        gate = gu_ref[:, :tn].astype(F32)
        up = gu_ref[:, tn:].astype(F32)
        sig = 1.0 / (1.0 + jnp.exp(-gate))
        dgu_ref[:, :tn] = (dact * up * (sig * (1.0 + gate * (1.0 - sig)))).astype(BF16)
        dgu_ref[:, tn:] = (dact * (gate * sig)).astype(BF16)

    vec = pl.BlockSpec((1, D), lambda i, j: (0, 0))
    row = pl.BlockSpec((tm, D), lambda i, j: (i, 0))
    return pl.pallas_call(
        body, name="ffn_bwd1", grid=(S // tm, 2),
        in_specs=[row, row, vec, pl.BlockSpec((tn, D), lambda i, j: (j, 0)),
                  pl.BlockSpec((tm, 2 * tn), lambda i, j: (i, j))],
        out_specs=[row, pl.BlockSpec((tm, 2 * tn), lambda i, j: (i, j)), vec],
        out_shape=[jax.ShapeDtypeStruct((S, D), BF16), jax.ShapeDtypeStruct((S, F2), BF16),
                   jax.ShapeDtypeStruct((1, D), F32)],
        compiler_params=_params("arbitrary", "arbitrary"),
    )(dx2, fo, gt, w_down, gu)


def _nt_ln_bwd(a, w, x, g, sc, dres, name):
    S, D = x.shape
    K = a.shape[1]
    tm = _tile(S, 256)

    def body(a_ref, w_ref, x_ref, g_ref, sc_ref, dres_ref, dx_ref, dsh_ref, dsc_ref, dg_ref):
        @pl.when(pl.program_id(0) == 0)
        def _():
            dsh_ref[...] = jnp.zeros_like(dsh_ref)
            dsc_ref[...] = jnp.zeros_like(dsc_ref)
            dg_ref[...] = jnp.zeros_like(dg_ref)

        dh = lax.dot_general(a_ref[...], w_ref[...], NT, preferred_element_type=F32)
        xv = x_ref[...]
        r = _rsq(jnp.mean(xv * xv, axis=-1, keepdims=True))
        xhat = xv * r
        gv = g_ref[...]
        dsh_ref[...] += jnp.sum(dh, axis=0, keepdims=True)
        dsc_ref[...] += jnp.sum(dh * (xhat * gv), axis=0, keepdims=True)
        dn = dh * (1.0 + sc_ref[...])
        dg_ref[...] += jnp.sum(dn * xhat, axis=0, keepdims=True)
        dxh = dn * gv
        dx_ref[...] = dres_ref[...] + r * (dxh - xhat * jnp.mean(dxh * xhat, axis=-1, keepdims=True))

    vec = _const_spec((1, D))
    vshape = jax.ShapeDtypeStruct((1, D), F32)
    return pl.pallas_call(
        body, name=name, grid=(S // tm,),
        in_specs=[_row_spec(tm, K), _const_spec(w.shape), _row_spec(tm, D), vec, vec, _row_spec(tm, D)],
        out_specs=[_row_spec(tm, D), vec, vec, vec],
        out_shape=[jax.ShapeDtypeStruct((S, D), F32), vshape, vshape, vshape],
        compiler_params=_params("arbitrary"),
    )(a, w, x, g, sc, dres)


def _oproj_bwd(dx1, ao, gt, w, ya, yb, yc, gg):
    S, D = dx1.shape
    tm = _tile(S, 512)
    groups = ((0, NA_W), (NA_W, SW_W), (NA_W + SW_W, AX_W))

    def body(dx_ref, ao_ref, gt_ref, w_ref, ya_ref, yb_ref, yc_ref, gg_ref,
             dao_ref, dya_ref, dyb_ref, dyc_ref, dgt_ref, dgg_ref):
        @pl.when(pl.program_id(0) == 0)
        def _():
            dgt_ref[...] = jnp.zeros_like(dgt_ref)
            dgg_ref[...] = jnp.zeros_like(dgg_ref)

        dxv = dx_ref[...]
        dao = (dxv * gt_ref[...]).astype(BF16)
        dao_ref[...] = dao
        dgt_ref[...] += jnp.sum(dxv * ao_ref[...].astype(F32), axis=0, keepdims=True)
        dyn = lax.dot_general(dao, w_ref[...], NT, preferred_element_type=F32)
        for (off, wd), y_ref, dy_ref in zip(groups, (ya_ref, yb_ref, yc_ref), (dya_ref, dyb_ref, dyc_ref)):
            y = y_ref[...].astype(F32)
            d = dyn[:, off:off + wd]
            r = _rsq(jnp.mean(y * y, axis=-1, keepdims=True))
            yhat = y * r
            dgg_ref[:, off:off + wd] += jnp.sum(d * yhat, axis=0, keepdims=True)
            dyh = d * gg_ref[:, off:off + wd]
            dy_ref[...] = (r * (dyh - yhat * jnp.mean(dyh * yhat, axis=-1, keepdims=True))).astype(BF16)

    vec = _const_spec((1, D))
    mvec = _const_spec((1, MIX_WIDTH))
    return pl.pallas_call(
        body, name="oproj_bwd", grid=(S // tm,),
        in_specs=[_row_spec(tm, D), _row_spec(tm, D), vec, _const_spec(w.shape),
                  _row_spec(tm, NA_W), _row_spec(tm, SW_W), _row_spec(tm, AX_W), mvec],
        out_specs=[_row_spec(tm, D), _row_spec(tm, NA_W), _row_spec(tm, SW_W), _row_spec(tm, AX_W), vec, mvec],
        out_shape=[jax.ShapeDtypeStruct((S, D), BF16), jax.ShapeDtypeStruct((S, NA_W), BF16),
                   jax.ShapeDtypeStruct((S, SW_W), BF16), jax.ShapeDtypeStruct((S, AX_W), BF16),
                   jax.ShapeDtypeStruct((1, D), F32), jax.ShapeDtypeStruct((1, MIX_WIDTH), F32)],
        compiler_params=_params("arbitrary"),
    )(dx1, ao, gt, w, ya, yb, yc, gg)


def _dproj_assemble(proj, na, sw, ax, gq128, gk128, rope):
    S = proj.shape[0]
    tm = _tile(S, 512)
    cos, sa, sb = rope

    def body(proj_ref, qa, ka, kah, va, vah, qb, kb, kbh, vb, vbh, qc, kc, vc,
             gq_ref, gk_ref, cos_ref, sa_ref, sb_ref, out_ref, dgq_ref, dgk_ref):
        @pl.when(pl.program_id(0) == 0)
        def _():
            dgq_ref[...] = jnp.zeros_like(dgq_ref)
            dgk_ref[...] = jnp.zeros_like(dgk_ref)

        out_ref[:, OFF_QA:OFF_KA] = qa[...].astype(BF16)
        out_ref[:, OFF_KA:OFF_VA] = (ka[...] + kah[...]).astype(BF16)
        out_ref[:, OFF_VA:OFF_QB] = (va[...] + vah[...]).astype(BF16)
        out_ref[:, OFF_QB:OFF_KB] = qb[...].astype(BF16)
        out_ref[:, OFF_KB:OFF_VB] = (kb[...] + kbh[...]).astype(BF16)
        out_ref[:, OFF_VB:OFF_QC] = (vb[...] + vbh[...]).astype(BF16)
        c, a, b = cos_ref[...], sa_ref[...], sb_ref[...]
        for j in range(AX_W // LANES):
            cols = slice(OFF_QC + j * LANES, OFF_QC + (j + 1) * LANES)
            dx, dg = _qk_prep_bwd_chunk(proj_ref[:, cols].astype(F32), qc[:, j * LANES:(j + 1) * LANES],
                                        gq_ref[...], c, a, b)
            out_ref[:, cols] = dx.astype(BF16)
            dgq_ref[...] += dg
        for j in range(AX_KV_W // LANES):
            cols = slice(OFF_KC + j * LANES, OFF_KC + (j + 1) * LANES)
            dx, dg = _qk_prep_bwd_chunk(proj_ref[:, cols].astype(F32), kc[:, j * LANES:(j + 1) * LANES],
                                        gk_ref[...], c, a, b)
            out_ref[:, cols] = dx.astype(BF16)
            dgk_ref[...] += dg
        out_ref[:, OFF_VC:IN_WIDTH] = vc[...].astype(BF16)

    v128 = _const_spec((1, LANES))
    r = lambda w: _row_spec(tm, w)
    return pl.pallas_call(
        body, name="dproj_assemble", grid=(S // tm,),
        in_specs=[r(IN_WIDTH), r(NA_W), r(NA_W), r(NA_W), r(NA_W), r(NA_W),
                  r(SW_W), r(SW_KV_W), r(SW_KV_W), r(SW_KV_W), r(SW_KV_W),
                  r(AX_W), r(AX_KV_W), r(AX_KV_W), v128, v128, r(LANES), r(LANES), r(LANES)],
        out_specs=[r(IN_WIDTH), v128, v128],
        out_shape=[jax.ShapeDtypeStruct((S, IN_WIDTH), BF16), jax.ShapeDtypeStruct((1, LANES), F32),
                   jax.ShapeDtypeStruct((1, LANES), F32)],
        compiler_params=_params("arbitrary"),
    )(proj, *na, *sw, *ax, gq128, gk128, cos, sa, sb)


def _tn_matmul(a, b, name):
    S, Ka = a.shape
    Nb = b.shape[1]
    tm = _tile(Ka, 1408, LANES)
    tn = _tile(Nb, 1408, LANES)
    tk = _tile(S, 512)
    nk = S // tk

    def body(a_ref, b_ref, o_ref, acc_ref):
        k = pl.program_id(2)

        @pl.when(k == 0)
        def _():
            acc_ref[...] = jnp.zeros_like(acc_ref)

        acc_ref[...] += lax.dot_general(a_ref[...], b_ref[...], TN, preferred_element_type=F32)

        @pl.when(k == nk - 1)
        def _():
            o_ref[...] = acc_ref[...].astype(BF16)

    return pl.pallas_call(
        body, name=name, grid=(Ka // tm, Nb // tn, nk),
        in_specs=[pl.BlockSpec((tk, tm), lambda i, j, k: (k, i)), pl.BlockSpec((tk, tn), lambda i, j, k: (k, j))],
        out_specs=pl.BlockSpec((tm, tn), lambda i, j, k: (i, j)),
        out_shape=jax.ShapeDtypeStruct((Ka, Nb), BF16),
        scratch_shapes=[pltpu.VMEM((tm, tn), F32)],
        compiler_params=_params("parallel", "parallel", "arbitrary"),
    )(a, b)


def _na_index(bd):
    rq = jnp.arange(bd.bq // GRID_W)
    rk = jnp.arange(bd.bk // GRID_W)
    col = jnp.arange(GRID_W)
    ri = jnp.clip(rk[None, :] - rq[:, None] - bd.halo // GRID_W + NA_WIN_ROWS - 1, 0, 2 * NA_WIN_ROWS - 2)
    ci = jnp.clip(col[None, :] - col[:, None] + NA_WIN_COLS - 1, 0, 2 * NA_WIN_COLS - 2)
    return ri, ci


def _na_one_hots(bd):
    ri, ci = _na_index(bd)
    oh_r = jax.nn.one_hot(ri, 2 * NA_WIN_ROWS - 1, dtype=F32)
    oh_c = jax.nn.one_hot(ci, 2 * NA_WIN_COLS - 1, dtype=F32)
    return oh_r, oh_c


def _na_bias(bd, rpb):
    oh_r, oh_c = _na_one_hots(bd)
    t = jnp.einsum("hab,qra->hqrb", rpb, oh_r, precision=lax.Precision.HIGHEST)
    b = jnp.einsum("hqrb,ckb->hqcrk", t, oh_c, precision=lax.Precision.HIGHEST)
    return b.reshape(NA_HEADS, bd.bq, bd.bk)


def _na_bias_t(bd, dbias):
    oh_r, oh_c = _na_one_hots(bd)
    d5 = dbias.reshape(NA_HEADS, bd.bq // GRID_W, GRID_W, bd.bk // GRID_W, GRID_W)
    t = jnp.einsum("hqcrk,ckb->hqrb", d5, oh_c, precision=lax.Precision.HIGHEST)
    return jnp.einsum("hqrb,qra->hab", t, oh_r, precision=lax.Precision.HIGHEST)


def _t5_bucket(rel):
    nb = T5_BUCKETS // 2
    ret = (rel > 0).astype(I32) * nb
    n = jnp.abs(rel)
    max_exact = nb // 2
    nf = jnp.maximum(n, max_exact).astype(F32)
    large = max_exact + (jnp.log(nf / max_exact) / math.log(T5_MAX_DIST / max_exact)
                         * (nb - max_exact)).astype(I32)
    large = jnp.minimum(large, nb - 1)
    return ret + jnp.where(n < max_exact, n, large)


def _sw_bucket(bd):
    rel = (jnp.arange(bd.bk) - bd.halo)[None, :] - jnp.arange(bd.bq)[:, None]
    return _t5_bucket(rel)


def _sw_bias(bd, t5):
    oh = jax.nn.one_hot(_sw_bucket(bd), T5_BUCKETS, dtype=F32)
    return jnp.einsum("bh,qkb->hqk", t5, oh, precision=lax.Precision.HIGHEST)


def _sw_bias_t(bd, dbias):
    oh = jax.nn.one_hot(_sw_bucket(bd), T5_BUCKETS, dtype=F32)
    return jnp.einsum("hqk,qkb->bh", dbias, oh, precision=lax.Precision.HIGHEST)


def _local_step(x, target, mod, w_in, w_o, w_gu, w_down, g_attn, rpb_na, sink_sw, t5_table, gq_ax, gk_ax,
                g_group, g_ffn, g_final):
    S, D = x.shape
    rope = _rope_tables(S)
    na, sw = _Band("na", S), _Band("sw", S)
    two = lambda v: jnp.concatenate([v, v])[None, :]
    sw_bias = _sw_bias(sw, t5_table)
    saved = []
    for l in range(DEPTH):
        sh_a, sc_a, gt_a, sh_f, sc_f, gt_f = [mod[l, k * D:(k + 1) * D][None, :] for k in range(6)]
        gq128, gk128 = two(gq_ax[l]), two(gk_ax[l])
        gg = g_group[l][None, :]
        sink = jnp.pad(sink_sw[l], (0, LANES - SW_HEADS))[None, :]
        na_bias = _na_bias(na, rpb_na[l])
        h, proj, qc, kc = _inproj_fwd(x, g_attn[l][None, :], sc_a, sh_a, w_in[l], gq128, gk128, rope)
        ya, yna = _band_fwd(na, proj, na_bias, None, gg[:, :NA_W])
        yb, ynb = _band_fwd(sw, proj, sw_bias, sink, gg[:, NA_W:NA_W + SW_W])
        yc, ync, lse = _ax_fwd(qc, kc, proj, gg[:, NA_W + SW_W:])
        x1, ao, yn = _oproj_fwd(x, yna, ynb, ync, w_o[l], gt_a)
        hf, gu, act = _gu_fwd(x1, g_ffn[l][None, :], sc_f, sh_f, w_gu[l])
        x2, fo = _down_fwd(x1, act, w_down[l], gt_f)
        saved.append(dict(x=x, x1=x1, h=h, proj=proj, qc=qc, kc=kc, ya=ya, yb=yb, yc=yc, lse=lse, ao=ao, yn=yn,
                          hf=hf, gu=gu, act=act, fo=fo, na_bias=na_bias, sink=sink, gq128=gq128, gk128=gk128,
                          gg=gg, mods=(sh_a, sc_a, gt_a, sh_f, sc_f, gt_f)))
        x = x2

    dx, loss_row, dg_final = _final_loss(x, g_final[None, :], target)
    gw = {k: [None] * DEPTH for k in ("w_in", "w_o", "w_gu", "w_down")}
    gs = {k: [None] * DEPTH for k in ("b_mod", "g_attn", "rpb_na", "sink_sw", "gq_ax", "gk_ax", "g_group", "g_ffn")}
    d_t5 = jnp.zeros((T5_BUCKETS, SW_HEADS), F32)
    for l in reversed(range(DEPTH)):
        s = saved[l]
        sh_a, sc_a, gt_a, sh_f, sc_f, gt_f = s["mods"]
        dfo, dgu, dgt_f = _ffn_bwd1(dx, s["fo"], gt_f, w_down[l], s["gu"])
        gw["w_down"][l] = _tn_matmul(s["act"], dfo, "dw_down")
        gw["w_gu"][l] = _tn_matmul(s["hf"], dgu, "dw_gu")
        dx1, dsh_f, dsc_f, gs["g_ffn"][l] = _nt_ln_bwd(dgu, w_gu[l], s["x1"], g_ffn[l][None, :], sc_f, dx, "ffn_bwd2")
        dao, dya, dyb, dyc, dgt_a, gs["g_group"][l] = _oproj_bwd(dx1, s["ao"], gt_a, w_o[l], s["ya"], s["yb"],
                                                                 s["yc"], s["gg"])
        gw["w_o"][l] = _tn_matmul(s["yn"], dao, "dw_o")
        dqa, dka, dva, dkap, dvap, dkan, dvan, dbias_na = _band_bwd(na, s["proj"], s["na_bias"], None, dya)
        dqb, dkb, dvb, dkbp, dvbp, dkbn, dvbn, dbias_sw, dsink = _band_bwd(sw, s["proj"], sw_bias, s["sink"], dyb)
        as_row = lambda a: a[:, :, 0][:, None, :]
        dqc, dkc, dvc = _ax_bwd(s["qc"], s["kc"], s["proj"], dyc, as_row(s["lse"]), as_row(_ax_delta(dyc, s["yc"])))
        dproj, dgq, dgk = _dproj_assemble(
            s["proj"], (dqa, dka, _halo_to_rows(dkap, dkan), dva, _halo_to_rows(dvap, dvan)),
            (dqb, dkb, _halo_to_rows(dkbp, dkbn), dvb, _halo_to_rows(dvbp, dvbn)), (dqc, dkc, dvc),
            s["gq128"], s["gk128"], rope)
        gw["w_in"][l] = _tn_matmul(s["h"], dproj, "dw_in")
        dx, dsh_a, dsc_a, gs["g_attn"][l] = _nt_ln_bwd(dproj, w_in[l], s["x"], g_attn[l][None, :], sc_a, dx1,
                                                       "inproj_bwd")
        gs["b_mod"][l] = jnp.concatenate([dsh_a, dsc_a, dgt_a, dsh_f, dsc_f, dgt_f], axis=1)[0]
        gs["rpb_na"][l] = _na_bias_t(na, dbias_na)
        gs["sink_sw"][l] = dsink[0, :SW_HEADS]
        d_t5 = d_t5 + _sw_bias_t(sw, dbias_sw)
        gs["gq_ax"][l] = dgq[0, :HEAD_DIM] + dgq[0, HEAD_DIM:]
        gs["gk_ax"][l] = dgk[0, :HEAD_DIM] + dgk[0, HEAD_DIM:]
        gs["g_attn"][l] = gs["g_attn"][l][0]
        gs["g_ffn"][l] = gs["g_ffn"][l][0]
        gs["g_group"][l] = gs["g_group"][l][0]

    gw = {k: jnp.stack(v) for k, v in gw.items()}
    small = {k: jnp.stack(v) for k, v in gs.items()}
    small["t5_table"] = d_t5
    small["g_final"] = dg_final[0]
    return loss_row[0, 0], dx, gw, small


MOD_ROWS = 16


def _mod_fwd(cond16, w):
    L, D, C = w.shape
    tn = _tile(C, 512, LANES)

    def body(c_ref, w_ref, o_ref):
        o_ref[0] = jnp.dot(c_ref[...], w_ref[0].astype(BF16), preferred_element_type=F32)

    return pl.pallas_call(
        body, name="mod_fwd", grid=(L, C // tn),
        in_specs=[pl.BlockSpec((MOD_ROWS, D), lambda l, j: (0, 0)), pl.BlockSpec((1, D, tn), lambda l, j: (l, 0, j))],
        out_specs=pl.BlockSpec((1, MOD_ROWS, tn), lambda l, j: (l, 0, j)),
        out_shape=jax.ShapeDtypeStruct((L, MOD_ROWS, C), F32),
        compiler_params=_params("parallel", "parallel"),
    )(cond16, w)


def _adamw_math(w, g, m, v):
    m = ADAM_B1 * m + (1.0 - ADAM_B1) * g
    v = ADAM_B2 * v + (1.0 - ADAM_B2) * (g * g)
    m_hat = m / (1.0 - ADAM_B1 ** ADAM_STEP)
    v_hat = v / (1.0 - ADAM_B2 ** ADAM_STEP)
    delta = -ADAM_LR * (m_hat / (jnp.sqrt(v_hat) + ADAM_EPS) + ADAM_WD * w)
    return delta, m, v


def _adamw(w, m, v, parts, name):
    R, C = w.shape
    tr = _tile(R, 256)
    n = len(parts)

    def body(*refs):
        w_ref, m_ref, v_ref = refs[:3]
        g = refs[3][...]
        for p in refs[4:3 + n]:
            g = g + p[...]
        g_ref, d_ref, m2_ref, v2_ref = refs[3 + n:]
        g_ref[...] = g
        d_ref[...], m2_ref[...], v2_ref[...] = _adamw_math(w_ref[...], g, m_ref[...], v_ref[...])

    spec = _row_spec(tr, C)
    shape = jax.ShapeDtypeStruct((R, C), F32)
    return pl.pallas_call(
        body, name=name, grid=(R // tr,), in_specs=[spec] * (3 + n), out_specs=[spec] * 4, out_shape=[shape] * 4,
        compiler_params=_params("parallel"),
    )(w, m, v, *parts)


def _wmod_adamw(cond_t, dmod16, w, m, v):
    L, D, C = w.shape
    tr = _tile(D, 256)

    def body(c_ref, d_ref, w_ref, m_ref, v_ref, g_ref, dl_ref, m2_ref, v2_ref):
        g = jnp.dot(c_ref[...], d_ref[0], preferred_element_type=F32)
        g_ref[0] = g
        dl_ref[0], m2_ref[0], v2_ref[0] = _adamw_math(w_ref[0], g, m_ref[0], v_ref[0])

    spec = pl.BlockSpec((1, tr, C), lambda l, i: (l, i, 0))
    shape = jax.ShapeDtypeStruct((L, D, C), F32)
    return pl.pallas_call(
        body, name="wmod_adamw", grid=(L, D // tr),
        in_specs=[pl.BlockSpec((tr, MOD_ROWS), lambda l, i: (i, 0)),
                  pl.BlockSpec((1, MOD_ROWS, C), lambda l, i: (l, 0, 0)), spec, spec, spec],
        out_specs=[spec] * 4, out_shape=[shape] * 4,
        compiler_params=_params("parallel", "parallel"),
    )(cond_t, dmod16, w, m, v)


def _sum_slots(a):
    P, R, C = a.shape
    tr = _tile(R, 256, 16)

    def body(a_ref, o_ref):
        s = a_ref[0].astype(F32)
        for k in range(1, P):
            s = s + a_ref[k].astype(F32)
        o_ref[...] = s

    return pl.pallas_call(
        body, name="sum_slots", grid=(R // tr,),
        in_specs=[pl.BlockSpec((P, tr, C), lambda i: (0, i, 0))], out_specs=_row_spec(tr, C),
        out_shape=jax.ShapeDtypeStruct((R, C), F32), compiler_params=_params("parallel"),
    )(a)


def _axes():
    return lax.axis_index("x"), lax.axis_index("y"), lax.axis_index("c")


def _allgather_devices(v):
    N = v.shape[1]

    def body(v_ref, out_ref, send_sems, recv_sems, local_sem):
        x, y, c = _axes()

        def row(px, py, pc):
            return out_ref.at[pl.ds(4 * px + 2 * py + pc, 1), :]

        mine = pltpu.make_async_copy(v_ref, row(x, y, c), local_sem)
        mine.start()
        sends, recvs = [], []
        for k in range(1, N_DEV):
            peer = (x ^ (k >> 2), y ^ ((k >> 1) & 1), c ^ (k & 1))
            sems = dict(send_sem=send_sems.at[k - 1], recv_sem=recv_sems.at[k - 1], device_id=peer, device_id_type=MESH)
            sends.append(pltpu.make_async_remote_copy(src_ref=v_ref, dst_ref=row(x, y, c), **sems))
            recvs.append(pltpu.make_async_remote_copy(src_ref=v_ref, dst_ref=row(*peer), **sems))
        for cp in sends:
            cp.start()
        for cp in recvs:
            cp.wait_recv()
        for cp in sends:
            cp.wait_send()
        mine.wait()

    vmem = pl.BlockSpec(memory_space=pltpu.VMEM)
    return pl.pallas_call(
        body, name="allgather_devices", in_specs=[vmem], out_specs=vmem,
        out_shape=jax.ShapeDtypeStruct((N_DEV, N), v.dtype),
        scratch_shapes=[pltpu.SemaphoreType.DMA((N_DEV - 1,)), pltpu.SemaphoreType.DMA((N_DEV - 1,)),
                        pltpu.SemaphoreType.DMA],
        compiler_params=pltpu.CompilerParams(vmem_limit_bytes=VMEM_LIMIT_V7X),
    )(v)


def _chip_pos(order, px, py):
    return 2 * px + py if order == "natural" else 2 * py + px


def _block(ref, axis, pos, width):
    idx = [slice(None)] * len(ref.shape)
    idx[axis] = pl.ds(pl.multiple_of(pos * width, width), width)
    return ref.at[tuple(idx)]


def _chip_allgather(shards, axes, orders, name):
    n = len(shards)
    out_shapes = []
    for s, ax in zip(shards, axes):
        shp = list(s.shape)
        shp[ax] *= N_CHIPS
        out_shapes.append(jax.ShapeDtypeStruct(tuple(shp), s.dtype))

    def body(*refs):
        ins, outs = refs[:n], refs[n:2 * n]
        send_sems, recv_sems, local_sems = refs[2 * n:]
        x, y, c = _axes()
        place = lambda i, px, py: _block(outs[i], axes[i], _chip_pos(orders[i], px, py), shards[i].shape[axes[i]])
        local, sends, recvs = [], [], []
        for i in range(n):
            local.append(pltpu.make_async_copy(ins[i], place(i, x, y), local_sems.at[i]))
            for k in range(1, N_CHIPS):
                px, py = x ^ (k >> 1), y ^ (k & 1)
                j = i * (N_CHIPS - 1) + k - 1
                sems = dict(send_sem=send_sems.at[j], recv_sem=recv_sems.at[j], device_id=(px, py, c),
                            device_id_type=MESH)
                sends.append(pltpu.make_async_remote_copy(src_ref=ins[i], dst_ref=place(i, x, y), **sems))
                recvs.append(pltpu.make_async_remote_copy(src_ref=ins[i], dst_ref=place(i, px, py), **sems))
        for cp in local + sends:
            cp.start()
        for cp in recvs:
            cp.wait_recv()
        for cp in sends:
            cp.wait_send()
        for cp in local:
            cp.wait()

    hbm = pl.BlockSpec(memory_space=pl.ANY)
    nsem = n * (N_CHIPS - 1)
    return pl.pallas_call(
        body, name=name, in_specs=[hbm] * n, out_specs=[hbm] * n, out_shape=out_shapes,
        scratch_shapes=[pltpu.SemaphoreType.DMA((nsem,)), pltpu.SemaphoreType.DMA((nsem,)),
                        pltpu.SemaphoreType.DMA((n,))],
    )(*shards)


def _chip_scatter(grads, axes, orders, name):
    n = len(grads)
    widths, out_shapes = [], []
    for g, ax in zip(grads, axes):
        shp = list(g.shape)
        shp[ax] //= N_CHIPS
        widths.append(shp[ax])
        out_shapes.append(jax.ShapeDtypeStruct((N_CHIPS,) + tuple(shp), g.dtype))

    def body(*refs):
        ins, outs = refs[:n], refs[n:2 * n]
        send_sems, recv_sems, local_sems = refs[2 * n:]
        x, y, c = _axes()
        piece = lambda i, px, py: _block(ins[i], axes[i], _chip_pos(orders[i], px, py), widths[i])
        slot = lambda i, px, py: outs[i].at[2 * px + py]
        local, sends, recvs = [], [], []
        for i in range(n):
            local.append(pltpu.make_async_copy(piece(i, x, y), slot(i, x, y), local_sems.at[i]))
            for k in range(1, N_CHIPS):
                px, py = x ^ (k >> 1), y ^ (k & 1)
                j = i * (N_CHIPS - 1) + k - 1
                sems = dict(send_sem=send_sems.at[j], recv_sem=recv_sems.at[j], device_id=(px, py, c),
                            device_id_type=MESH)
                sends.append(pltpu.make_async_remote_copy(src_ref=piece(i, px, py), dst_ref=slot(i, x, y), **sems))
                recvs.append(pltpu.make_async_remote_copy(src_ref=piece(i, px, py), dst_ref=slot(i, px, py), **sems))
        for cp in local + sends:
            cp.start()
        for cp in recvs:
            cp.wait_recv()
        for cp in sends:
            cp.wait_send()
        for cp in local:
            cp.wait()

    hbm = pl.BlockSpec(memory_space=pl.ANY)
    nsem = n * (N_CHIPS - 1)
    return pl.pallas_call(
        body, name=name, in_specs=[hbm] * n, out_specs=[hbm] * n, out_shape=out_shapes,
        scratch_shapes=[pltpu.SemaphoreType.DMA((nsem,)), pltpu.SemaphoreType.DMA((nsem,)),
                        pltpu.SemaphoreType.DMA((n,))],
    )(*grads)


def _core_swap(arrays, name):
    n = len(arrays)

    def body(*refs):
        ins, outs = refs[:n], refs[n:2 * n]
        send_sems, recv_sems = refs[2 * n:]
        x, y, c = _axes()
        copies = [pltpu.make_async_remote_copy(src_ref=ins[i], dst_ref=outs[i], send_sem=send_sems.at[i],
                                               recv_sem=recv_sems.at[i], device_id=(x, y, 1 - c), device_id_type=MESH)
                  for i in range(n)]
        for cp in copies:
            cp.start()
        for cp in copies:
            cp.wait_recv()
        for cp in copies:
            cp.wait_send()

    hbm = pl.BlockSpec(memory_space=pl.ANY)
    return pl.pallas_call(
        body, name=name, in_specs=[hbm] * n, out_specs=[hbm] * n,
        out_shape=[jax.ShapeDtypeStruct(a.shape, a.dtype) for a in arrays],
        scratch_shapes=[pltpu.SemaphoreType.DMA((n,)), pltpu.SemaphoreType.DMA((n,))],
    )(*arrays)


SMALL = ("b_mod", "g_attn", "rpb_na", "sink_sw", "t5_table", "gq_ax", "gk_ax", "g_group", "g_ffn", "g_final")
BIG = ("w_in", "w_o", "w_gu", "w_down")
BIG_AXIS = {"w_in": 2, "w_o": 1, "w_gu": 2, "w_down": 1}
BIG_ORDER = {"w_in": "natural", "w_o": "natural", "w_gu": "gate_up_tiles", "w_down": "natural"}
WEIGHTS = ("w_mod", "b_mod", "g_attn", "w_in", "rpb_na", "sink_sw", "t5_table", "gq_ax", "gk_ax", "g_group",
           "w_o", "g_ffn", "w_gu", "w_down", "g_final")


def _pack(arrs):
    flat = jnp.concatenate([a.reshape(-1) for a in arrs])
    n = flat.shape[0]
    padded = -(-n // (8 * LANES)) * (8 * LANES)
    return jnp.pad(flat, (0, padded - n))


def _unpack(flat, like):
    out, off = [], 0
    for a in like:
        out.append(flat[off:off + a.size].reshape(a.shape))
        off += a.size
    return out


def kernel(x, c, w_mod, b_mod, g_attn, w_in, rpb_na, sink_sw, t5_table, gq_ax, gk_ax, g_group, w_o, g_ffn, w_gu, w_down, g_final, loss_target, m_w_mod, m_b_mod, m_g_attn, m_w_in, m_rpb_na, m_sink_sw, m_t5_table, m_gq_ax, m_gk_ax, m_g_group, m_w_o, m_g_ffn, m_w_gu, m_w_down, m_g_final, v_w_mod, v_b_mod, v_g_attn, v_w_in, v_rpb_na, v_sink_sw, v_t5_table, v_gq_ax, v_gk_ax, v_g_group, v_w_o, v_g_ffn, v_w_gu, v_w_down, v_g_final):
    W = dict(w_mod=w_mod, b_mod=b_mod, g_attn=g_attn, w_in=w_in, rpb_na=rpb_na, sink_sw=sink_sw, t5_table=t5_table,
             gq_ax=gq_ax, gk_ax=gk_ax, g_group=g_group, w_o=w_o, g_ffn=g_ffn, w_gu=w_gu, w_down=w_down,
             g_final=g_final)
    M = dict(w_mod=m_w_mod, b_mod=m_b_mod, g_attn=m_g_attn, w_in=m_w_in, rpb_na=m_rpb_na, sink_sw=m_sink_sw,
             t5_table=m_t5_table, gq_ax=m_gq_ax, gk_ax=m_gk_ax, g_group=m_g_group, w_o=m_w_o, g_ffn=m_g_ffn,
             w_gu=m_w_gu, w_down=m_w_down, g_final=m_g_final)
    V = dict(w_mod=v_w_mod, b_mod=v_b_mod, g_attn=v_g_attn, w_in=v_w_in, rpb_na=v_rpb_na, sink_sw=v_sink_sw,
             t5_table=v_t5_table, gq_ax=v_gq_ax, gk_ax=v_gk_ax, g_group=v_g_group, w_o=v_w_o, g_ffn=v_g_ffn,
             w_gu=v_w_gu, w_down=v_w_down, g_final=v_g_final)
    xi, yi, ci = _axes()
    me = 4 * xi + 2 * yi + ci
    chip = 2 * xi + yi
    D = x.shape[-1]
    mod_w = w_mod.shape[2]

    c_all = _allgather_devices(c)
    cond = c_all * (1.0 / (1.0 + jnp.exp(-c_all)))
    cond16 = jnp.pad(cond, ((0, MOD_ROWS - N_DEV), (0, 0))).astype(BF16)
    mod_part = _mod_fwd(cond16, w_mod)
    (mod_all,) = _chip_allgather([mod_part], [2], ["natural"], "allgather_mod")
    mod = lax.dynamic_slice_in_dim(mod_all, me, 1, axis=1)[:, 0, :] + b_mod

    w_full = _chip_allgather([W[k].astype(BF16) for k in BIG], [BIG_AXIS[k] for k in BIG],
                             [BIG_ORDER[k] for k in BIG], "allgather_weights")
    loss_part, grad_x, gw, small = _local_step(x[0], loss_target[0], mod, *w_full, g_attn, rpb_na, sink_sw, t5_table,
                                               gq_ax, gk_ax, g_group, g_ffn, g_final)

    small_all = _allgather_devices(_pack([small[k] for k in SMALL])[None, :])
    rows = small_all.shape[1] // LANES
    parts = [small_all[k].reshape(rows, LANES) for k in range(N_DEV)]
    pk = lambda d: _pack([d[k] for k in SMALL]).reshape(rows, LANES)
    small_out = [_unpack(o.reshape(-1), [W[k] for k in SMALL]) for o in _adamw(pk(W), pk(M), pk(V), parts, "adamw_small")]

    L = w_mod.shape[0]
    dmod_all = small_all[:, :L * 6 * D].reshape(N_DEV, L, 6 * D)
    dmod_mine = lax.dynamic_slice_in_dim(dmod_all, chip * mod_w, mod_w, axis=2)
    dmod16 = jnp.pad(jnp.transpose(dmod_mine, (1, 0, 2)), ((0, 0), (0, MOD_ROWS - N_DEV), (0, 0))).astype(BF16)
    wmod_out = _wmod_adamw(jnp.transpose(cond16), dmod16, w_mod, m_w_mod, v_w_mod)

    names = list(BIG)
    slots = _chip_scatter([gw[k] for k in names], [BIG_AXIS[k] for k in names], [BIG_ORDER[k] for k in names],
                          "scatter_grads")
    two_d = lambda a: a.reshape(-1, a.shape[-1])
    mine = [_sum_slots(s.reshape(N_CHIPS, -1, s.shape[-1])) for s in slots]
    theirs = _core_swap(mine, "swap_grads")
    big_out = {}
    for k, a, b in zip(names, mine, theirs):
        outs = _adamw(two_d(W[k]), two_d(M[k]), two_d(V[k]), [a, b], "adamw_" + k)
        big_out[k] = [o.reshape(W[k].shape) for o in outs]

    loss = lax.psum(loss_part, ("x", "y", "c"))
    per_kind = []
    for kind in range(4):
        for k in WEIGHTS:
            if k == "w_mod":
                per_kind.append(wmod_out[kind])
            elif k in big_out:
                per_kind.append(big_out[k][kind])
            else:
                per_kind.append(small_out[kind][SMALL.index(k)])
    return (loss, grad_x[None], *per_kind)
```

```python
import functools
import math

import jax
import jax.numpy as jnp
from jax import lax
from jax.experimental import pallas as pl
from jax.experimental.pallas import tpu as pltpu

F32 = jnp.float32
BF16 = jnp.bfloat16
I32 = jnp.int32

DEPTH = 2
HEAD_DIM = 64
GRID_W = 64
NA_HEADS = 4
SW_HEADS = 6
SW_KV_HEADS = 2
AX_HEADS = 6
AX_KV_HEADS = 2
NA_WIN_ROWS = 8
NA_WIN_COLS = 16
SW_RADIUS = 128
T5_BUCKETS = 32
T5_MAX_DIST = 128
ROPE_THETA = 10000.0
EPS = 1e-6
NEG_INF = -1e30
QK_SCALE = HEAD_DIM ** -0.5

NA_W = NA_HEADS * HEAD_DIM
SW_W = SW_HEADS * HEAD_DIM
SW_KV_W = SW_KV_HEADS * HEAD_DIM
AX_W = AX_HEADS * HEAD_DIM
AX_KV_W = AX_KV_HEADS * HEAD_DIM
OFF_QA, OFF_KA, OFF_VA = 0, NA_W, 2 * NA_W
OFF_QB = 3 * NA_W
OFF_KB = OFF_QB + SW_W
OFF_VB = OFF_KB + SW_KV_W
OFF_QC = OFF_VB + SW_KV_W
OFF_KC = OFF_QC + AX_W
OFF_VC = OFF_KC + AX_KV_W
IN_WIDTH = OFF_VC + AX_KV_W
MIX_WIDTH = NA_W + SW_W + AX_W

ADAM_LR = 0.001
ADAM_B1 = 0.9
ADAM_B2 = 0.999
ADAM_EPS = 1e-08
ADAM_WD = 0.01
ADAM_STEP = 10

N_CHIPS = 4
N_DEV = 8
LANES = 128
VMEM_LIMIT_V7X = 56 * 1024 * 1024
ROW_TILE = 512
ROW_TILE_WIDE = 256
TOKEN_CHUNK = 1024
MESH = pl.DeviceIdType.MESH

NT = (((1,), (1,)), ((), ()))
TN = (((0,), (0,)), ((), ()))


def _params(*sem):
    return pltpu.CompilerParams(dimension_semantics=sem if sem else None,
                                vmem_limit_bytes=VMEM_LIMIT_V7X)


def _tile(n, pref, mult=8):
    t = (min(pref, n) // mult) * mult
    while t >= mult:
        if n % t == 0:
            return t
        t -= mult
    return n


def _row_spec(tm, width, col=0):
    return pl.BlockSpec((tm, width), lambda i, *_: (i, col))


def _const_spec(shape):
    nd = len(shape)
    return pl.BlockSpec(shape, lambda *_: (0,) * nd)


def _rsq(ms):
    return lax.rsqrt(ms + EPS)


def _rope_tables(S):
    rows = S // GRID_W
    axis_dim = HEAD_DIM // 2
    quarter = axis_dim // 2
    lane = jnp.arange(LANES)
    freq = (ROPE_THETA ** (-(2 * (lane % quarter)).astype(F32) / axis_dim))[None, :]
    by_row = ((lane % HEAD_DIM) < axis_dim)[None, None, :]
    first = ((lane % axis_dim) < quarter)[None, :]
    ang_r = jnp.arange(rows, dtype=F32)[:, None] * freq
    ang_c = jnp.arange(GRID_W, dtype=F32)[:, None] * freq

    def table(fr, fc):
        t = jnp.where(by_row, fr[:, None, :], fc[None, :, :])
        return t.reshape(S, LANES)

    sin_r, sin_c = jnp.sin(ang_r), jnp.sin(ang_c)
    return (table(jnp.cos(ang_r), jnp.cos(ang_c)),
            table(jnp.where(first, -sin_r, 0.0), jnp.where(first, -sin_c, 0.0)),
            table(jnp.where(first, 0.0, sin_r), jnp.where(first, 0.0, sin_c)))


def _pair_sum(v):
    lane = lax.broadcasted_iota(I32, v.shape, 1)
    lo = lane < HEAD_DIM
    s_lo = jnp.sum(jnp.where(lo, v, 0.0), axis=-1, keepdims=True)
    s_hi = jnp.sum(jnp.where(lo, 0.0, v), axis=-1, keepdims=True)
    return jnp.where(lo, s_lo, s_hi)


def _rope(t, cos, sa, sb):
    return t * cos + pltpu.roll(t, LANES - 16, 1) * sa + pltpu.roll(t, 16, 1) * sb


def _rope_t(t, cos, sa, sb):
    return t * cos + pltpu.roll(t * sa, 16, 1) + pltpu.roll(t * sb, LANES - 16, 1)


def _qk_prep_chunk(x, g128, cos, sa, sb):
    r = _rsq(_pair_sum(x * x) * (1.0 / HEAD_DIM))
    return _rope(x * r * g128, cos, sa, sb)


def _qk_prep_bwd_chunk(x, dy, g128, cos, sa, sb):
    dn = _rope_t(dy, cos, sa, sb)
    r = _rsq(_pair_sum(x * x) * (1.0 / HEAD_DIM))
    xhat = x * r
    dg = jnp.sum(dn * xhat, axis=0, keepdims=True)
    dxh = dn * g128
    dx = r * (dxh - xhat * (_pair_sum(dxh * xhat) * (1.0 / HEAD_DIM)))
    return dx, dg


def _ln_mod(xv, g, sc, sh):
    r = _rsq(jnp.mean(xv * xv, axis=-1, keepdims=True))
    return xv * r * g * (1.0 + sc) + sh


def _inproj_fwd(x, g, sc, sh, w, gq128, gk128, rope):
    S, D = x.shape
    tm = _tile(S, ROW_TILE)
    cos, sa, sb = rope

    def body(x_ref, g_ref, sc_ref, sh_ref, w_ref, gq_ref, gk_ref, cos_ref, sa_ref, sb_ref,
             h_ref, proj_ref, qc_ref, kc_ref):
        hb = _ln_mod(x_ref[...], g_ref[...], sc_ref[...], sh_ref[...]).astype(BF16)
        h_ref[...] = hb
        acc = jnp.dot(hb, w_ref[...], preferred_element_type=F32)
        proj_ref[...] = acc.astype(BF16)
        c, a, b = cos_ref[...], sa_ref[...], sb_ref[...]
        for j in range(AX_W // LANES):
            xq = acc[:, OFF_QC + j * LANES: OFF_QC + (j + 1) * LANES]
            qc_ref[:, j * LANES:(j + 1) * LANES] = (
                _qk_prep_chunk(xq, gq_ref[...], c, a, b) * QK_SCALE).astype(BF16)
        for j in range(AX_KV_W // LANES):
            xk = acc[:, OFF_KC + j * LANES: OFF_KC + (j + 1) * LANES]
            kc_ref[:, j * LANES:(j + 1) * LANES] = _qk_prep_chunk(xk, gk_ref[...], c, a, b).astype(BF16)

    vec = _const_spec((1, D))
    v128 = _const_spec((1, LANES))
    return pl.pallas_call(
        body, name="inproj_fwd", grid=(S // tm,),
        in_specs=[_row_spec(tm, D), vec, vec, vec, _const_spec(w.shape), v128, v128,
                  _row_spec(tm, LANES), _row_spec(tm, LANES), _row_spec(tm, LANES)],
        out_specs=[_row_spec(tm, D), _row_spec(tm, IN_WIDTH), _row_spec(tm, AX_W), _row_spec(tm, AX_KV_W)],
        out_shape=[jax.ShapeDtypeStruct((S, D), BF16), jax.ShapeDtypeStruct((S, IN_WIDTH), BF16),
                   jax.ShapeDtypeStruct((S, AX_W), BF16), jax.ShapeDtypeStruct((S, AX_KV_W), BF16)],
        compiler_params=_params("parallel"),
    )(x, g, sc, sh, w, gq128, gk128, cos, sa, sb)


class _Band:
    def __init__(self, kind, S):
        self.kind = kind
        self.S = S
        if kind == "na":
            self.hq, self.g, self.halo = NA_HEADS, NA_HEADS, (NA_WIN_ROWS // 2) * GRID_W
            self.q_off, self.k_off, self.v_off = OFF_QA, OFF_KA, OFF_VA
        else:
            self.hq, self.g, self.halo = SW_HEADS, SW_KV_HEADS, SW_RADIUS
            self.q_off, self.k_off, self.v_off = OFF_QB, OFF_KB, OFF_VB
        self.bq = 2 * self.halo
        self.bk = self.bq + 2 * self.halo
        self.nb = S // self.bq
        self.rep = self.hq // self.g
        self.qw = self.hq * HEAD_DIM
        self.kw = self.g * HEAD_DIM

    def kv_of(self, h):
        return (h // 2, h % 2) if self.kind == "na" else (0, h // self.rep)

    def mask(self, n, transposed=False):
        shape = (self.bk, self.bq) if transposed else (self.bq, self.bk)
        qi = lax.broadcasted_iota(I32, shape, 1 if transposed else 0) + n * self.bq
        kj = lax.broadcasted_iota(I32, shape, 0 if transposed else 1) + (n * self.bq - self.halo)
        if self.kind == "sw":
            return (jnp.abs(kj - qi) <= SW_RADIUS) & (kj >= 0) & (kj < self.S)
        rows = self.S // GRID_W
        r, col = qi >> 6, qi & (GRID_W - 1)
        kr, kc = kj >> 6, kj & (GRID_W - 1)
        rs = jnp.clip(r - NA_WIN_ROWS // 2, 0, rows - NA_WIN_ROWS)
        cs = jnp.clip(col - NA_WIN_COLS // 2, 0, GRID_W - NA_WIN_COLS)
        return (kr >= rs) & (kr < rs + NA_WIN_ROWS) & (kc >= cs) & (kc < cs + NA_WIN_COLS)

    def qkv_specs(self):
        ratio = self.bq // self.halo
        last = self.S // self.halo - 1
        q = pl.BlockSpec((self.bq, self.qw), lambda n, o=self.q_off // self.qw: (n, o))
        specs = [q]
        for off in (self.k_off, self.v_off):
            o = off // self.kw
            specs.append(pl.BlockSpec((self.halo, self.kw), lambda n, o=o: (jnp.maximum(n * ratio - 1, 0), o)))
            specs.append(pl.BlockSpec((self.bq, self.kw), lambda n, o=o: (n, o)))
            specs.append(pl.BlockSpec((self.halo, self.kw), lambda n, o=o: (jnp.minimum((n + 1) * ratio, last), o)))
        return specs


def _band_kv_variants(bd, refs, fill):
    out = []
    for blk in range(bd.kw // LANES):
        cols = slice(blk * LANES, (blk + 1) * LANES)
        out.append(_half_variants(jnp.concatenate([r[:, cols] for r in refs], axis=0), fill))
    return out


def _band_fwd(bd, proj, bias, sink, gg):
    S = bd.S
    has_sink = sink is not None

    def body(*refs):
        q_ref, kp, km, kn, vp, vm, vn, bias_ref = refs[:8]
        k = 8
        sink_ref = None
        if has_sink:
            sink_ref = refs[k]
            k += 1
        gg_ref, raw_ref, yn_ref, o_scr = refs[k:k + 4]
        mask = bd.mask(pl.program_id(0))
        lo = _left_half((bd.bq, LANES))
        kzs, vzs = _band_kv_variants(bd, (kp, km, kn), 0.0), _band_kv_variants(bd, (vp, vm, vn), 1.0)
        for pr in range(bd.hq // 2):
            cols = slice(pr * LANES, (pr + 1) * LANES)
            qp = q_ref[:, cols] * QK_SCALE
            acc = []
            for half in range(2):
                h = 2 * pr + half
                blk, src = bd.kv_of(h)
                s = lax.dot_general(qp, kzs[blk][src][half], NT, preferred_element_type=F32) + bias_ref[h]
                s = jnp.where(mask, s, NEG_INF)
                m = jnp.max(s, axis=-1, keepdims=True)
                if has_sink:
                    m = jnp.maximum(m, sink_ref[0:1, h:h + 1])
                a = jnp.dot(jnp.exp(s - m).astype(BF16), vzs[blk][src][half], preferred_element_type=F32)
                if has_sink:
                    e = jnp.exp(sink_ref[0:1, h:h + 1] - m)
                    a = a + (jnp.where(lo, 0.0, e) if half == 0 else jnp.where(lo, e, 0.0))
                acc.append(a)
            o_scr[:, cols] = jnp.where(lo, acc[0] / pltpu.roll(acc[0], HEAD_DIM, 1),
                                       acc[1] / pltpu.roll(acc[1], HEAD_DIM, 1))
        o = o_scr[...]
        raw_ref[...] = o.astype(BF16)
        r = _rsq(jnp.mean(o * o, axis=-1, keepdims=True))
        yn_ref[...] = (o * r * gg_ref[...]).astype(BF16)

    in_specs = bd.qkv_specs() + [_const_spec(bias.shape)]
    args = [proj] * 7 + [bias]
    if has_sink:
        in_specs.append(_const_spec(sink.shape))
        args.append(sink)
    in_specs.append(_const_spec(gg.shape))
    args.append(gg)
    out = jax.ShapeDtypeStruct((S, bd.qw), BF16)
    return pl.pallas_call(
        body, name=bd.kind + "_fwd", grid=(bd.nb,), in_specs=in_specs,
        out_specs=[_row_spec(bd.bq, bd.qw), _row_spec(bd.bq, bd.qw)], out_shape=[out, out],
        scratch_shapes=[pltpu.VMEM((bd.bq, bd.qw), F32)],
        compiler_params=_params("parallel"),
    )(*args)


def _band_bwd(bd, proj, bias, sink, dy):
    S = bd.S
    has_sink = sink is not None

    def body(*refs):
        q_ref, kp, km, kn, vp, vm, vn, bias_ref = refs[:8]
        k = 8
        sink_ref = None
        if has_sink:
            sink_ref = refs[k]
            k += 1
        do_ref = refs[k]
        dq_ref, dkm, dvm, dkp, dvp, dkn, dvn, dbias_ref = refs[k + 1:k + 9]
        k += 9
        dsink_ref = None
        if has_sink:
            dsink_ref = refs[k]
            k += 1
        dk_scr, dv_scr = refs[k:k + 2]
        n = pl.program_id(0)

        @pl.when(n == 0)
        def _():
            dbias_ref[...] = jnp.zeros_like(dbias_ref)
            if has_sink:
                dsink_ref[...] = jnp.zeros_like(dsink_ref)

        mask = bd.mask(n, transposed=True)
        lane = lax.broadcasted_iota(I32, (1, LANES), 1)
        kzs, vzs = _band_kv_variants(bd, (kp, km, kn), 0.0), _band_kv_variants(bd, (vp, vm, vn), 0.0)
        nblk = bd.kw // LANES
        dk, dv = [None] * nblk, [None] * nblk
        for pr in range(bd.hq // 2):
            cols = slice(pr * LANES, (pr + 1) * LANES)
            qp, dop = q_ref[:, cols] * QK_SCALE, do_ref[:, cols]
            qz, doz = _half_variants(qp), _half_variants(dop)
            dq = None
            for half in range(2):
                h = 2 * pr + half
                blk, dst = bd.kv_of(h)
                kz, vz = kzs[blk][dst][half], vzs[blk][dst][half]
                s = lax.dot_general(kz, qp, NT, preferred_element_type=F32) + bias_ref[h]
                s = jnp.where(mask, s, NEG_INF)
                m = jnp.max(s, axis=0, keepdims=True)
                if has_sink:
                    m = jnp.maximum(m, sink_ref[0:1, h:h + 1])
                p = jnp.exp(s - m)
                l = jnp.sum(p, axis=0, keepdims=True)
                if has_sink:
                    e = jnp.exp(sink_ref[0:1, h:h + 1] - m)
                    l = l + e
                inv = 1.0 / l
                pn = p * inv
                dp = lax.dot_general(vz, dop, NT, preferred_element_type=F32)
                delta = jnp.sum(pn * dp, axis=0, keepdims=True)
                ds = pn * (dp - delta)
                dbias_ref[h] += ds
                if has_sink:
                    dsink_ref[...] += jnp.where(lane == h, -jnp.sum(e * inv * delta, axis=1, keepdims=True), 0.0)
                dsb = ds.astype(BF16)
                a = jnp.dot(pn.astype(BF16), doz[half][dst], preferred_element_type=F32)
                b = jnp.dot(dsb, qz[half][dst], preferred_element_type=F32)
                d = lax.dot_general(dsb, kz, TN, preferred_element_type=F32)
                dv[blk] = a if dv[blk] is None else dv[blk] + a
                dk[blk] = b if dk[blk] is None else dk[blk] + b
                dq = d if dq is None else dq + d
            dq_ref[:, cols] = dq * QK_SCALE
        for blk in range(nblk):
            cols = slice(blk * LANES, (blk + 1) * LANES)
            dk_scr[:, cols] = dk[blk]
            dv_scr[:, cols] = dv[blk]
        h0, h1 = bd.halo, bd.halo + bd.bq
        dkp[0] = dk_scr[0:h0, :]
        dkm[...] = dk_scr[h0:h1, :]
        dkn[0] = dk_scr[h1:bd.bk, :]
        dvp[0] = dv_scr[0:h0, :]
        dvm[...] = dv_scr[h0:h1, :]
        dvn[0] = dv_scr[h1:bd.bk, :]

    in_specs = bd.qkv_specs() + [_const_spec(bias.shape)]
    args = [proj] * 7 + [bias]
    if has_sink:
        in_specs.append(_const_spec(sink.shape))
        args.append(sink)
    in_specs.append(_row_spec(bd.bq, bd.qw))
    args.append(dy)
    halo_spec = pl.BlockSpec((1, bd.halo, bd.kw), lambda n: (n, 0, 0))
    halo_shape = jax.ShapeDtypeStruct((bd.nb, bd.halo, bd.kw), F32)
    main_shape = jax.ShapeDtypeStruct((S, bd.kw), F32)
    out_specs = [_row_spec(bd.bq, bd.qw), _row_spec(bd.bq, bd.kw), _row_spec(bd.bq, bd.kw),
                 halo_spec, halo_spec, halo_spec, halo_spec, _const_spec(bias.shape)]
    out_shape = [jax.ShapeDtypeStruct((S, bd.qw), F32), main_shape, main_shape,
                 halo_shape, halo_shape, halo_shape, halo_shape, jax.ShapeDtypeStruct(bias.shape, F32)]
    if has_sink:
        out_specs.append(_const_spec((1, LANES)))
        out_shape.append(jax.ShapeDtypeStruct((1, LANES), F32))
    return pl.pallas_call(
        body, name=bd.kind + "_bwd", grid=(bd.nb,), in_specs=in_specs, out_specs=out_specs, out_shape=out_shape,
        scratch_shapes=[pltpu.VMEM((bd.bk, bd.kw), F32), pltpu.VMEM((bd.bk, bd.kw), F32)],
        compiler_params=_params("arbitrary"),
    )(*args)


def _halo_to_rows(prev, nxt):
    nb, halo, w = prev.shape
    z = jnp.zeros((1, halo, w), prev.dtype)
    first = jnp.concatenate([z, nxt[:-1]], axis=0)
    second = jnp.concatenate([prev[1:], z], axis=0)
    return jnp.concatenate([first, second], axis=1).reshape(nb * 2 * halo, w)


AX_PAIRS = AX_W // LANES


AX_FWD_BLOCKS = (1024, 2048)
AX_BWD_BLOCKS = (1024, 1024)


def _ax_blocks(S, blocks):
    return _tile(S, blocks[0]), _tile(S, blocks[1])


def _left_half(shape):
    return lax.broadcasted_iota(I32, shape, len(shape) - 1) < HEAD_DIM


def _half_variants(a, fill=0.0):
    lo = _left_half(a.shape)
    other = jnp.full_like(a, fill)
    swapped = pltpu.roll(a, HEAD_DIM, 1)
    return ((jnp.where(lo, a, other), jnp.where(lo, other, swapped)),
            (jnp.where(lo, swapped, other), jnp.where(lo, other, a)))


def _ax_fwd(qc, kc, proj, gg):
    S = qc.shape[0]
    bq, bk = _ax_blocks(S, AX_FWD_BLOCKS)
    nk = S // bk
    rep = AX_HEADS // AX_KV_HEADS

    def body(q_ref, k_ref, v_ref, gg_ref, raw_ref, yn_ref, lse_ref, m_scr, acc_scr):
        kv = pl.program_id(1)

        @pl.when(kv == 0)
        def _():
            m_scr[...] = jnp.full(m_scr.shape, NEG_INF, F32)
            acc_scr[...] = jnp.zeros_like(acc_scr)

        kz, vz = _half_variants(k_ref[...]), _half_variants(v_ref[...], 1.0)
        for pr in range(AX_PAIRS):
            qp = q_ref[:, pr * LANES:(pr + 1) * LANES]
            for half in range(2):
                h = 2 * pr + half
                g = h // rep
                s = lax.dot_general(qp, kz[g][half], NT, preferred_element_type=F32)
                m_prev = m_scr[h]
                m_new = jnp.maximum(m_prev, jnp.max(s, axis=-1, keepdims=True))
                p = jnp.exp(s - jnp.tile(m_new, (1, bk // LANES)))
                acc_scr[h] = jnp.exp(m_prev - m_new) * acc_scr[h] + jnp.dot(
                    p.astype(BF16), vz[g][half], preferred_element_type=F32)
                m_scr[h] = m_new

        @pl.when(kv == nk - 1)
        def _():
            lo = _left_half((bq, LANES))
            ssq = jnp.zeros((bq, 1), F32)
            for pr in range(AX_PAIRS):
                a0, a1 = acc_scr[2 * pr], acc_scr[2 * pr + 1]
                r0, r1 = pltpu.roll(a0, HEAD_DIM, 1), pltpu.roll(a1, HEAD_DIM, 1)
                lse_ref[2 * pr] = m_scr[2 * pr] + jnp.log(jnp.where(lo, r0, a0))
                lse_ref[2 * pr + 1] = m_scr[2 * pr + 1] + jnp.log(jnp.where(lo, a1, r1))
                o = jnp.where(lo, a0 / r0, a1 / r1)
                acc_scr[pr] = o
                ssq = ssq + jnp.sum(o * o, axis=-1, keepdims=True)
            r = _rsq(ssq * (1.0 / AX_W))
            for pr in range(AX_PAIRS):
                cols = slice(pr * LANES, (pr + 1) * LANES)
                o = acc_scr[pr]
                raw_ref[:, cols] = o.astype(BF16)
                yn_ref[:, cols] = (o * r * gg_ref[:, cols]).astype(BF16)

    out = jax.ShapeDtypeStruct((S, AX_W), BF16)
    return pl.pallas_call(
        body, name="ax_fwd", grid=(S // bq, nk),
        in_specs=[pl.BlockSpec((bq, AX_W), lambda i, j: (i, 0)),
                  pl.BlockSpec((bk, AX_KV_W), lambda i, j: (j, 0)),
                  pl.BlockSpec((bk, AX_KV_W), lambda i, j: (j, OFF_VC // AX_KV_W)),
                  _const_spec(gg.shape)],
        out_specs=[pl.BlockSpec((bq, AX_W), lambda i, j: (i, 0)),
                   pl.BlockSpec((bq, AX_W), lambda i, j: (i, 0)),
                   pl.BlockSpec((AX_HEADS, bq, LANES), lambda i, j: (0, i, 0))],
        out_shape=[out, out, jax.ShapeDtypeStruct((AX_HEADS, S, LANES), F32)],
        scratch_shapes=[pltpu.VMEM((AX_HEADS, bq, LANES), F32), pltpu.VMEM((AX_HEADS, bq, LANES), F32)],
        compiler_params=_params("parallel", "arbitrary"),
    )(qc, kc, proj, gg)


def _ax_delta(dy, raw):
    S = dy.shape[0]
    tm = _tile(S, ROW_TILE)

    def body(do_ref, o_ref, delta_ref):
        lo = _left_half((tm, LANES))
        for pr in range(AX_PAIRS):
            cols = slice(pr * LANES, (pr + 1) * LANES)
            prod = do_ref[:, cols].astype(F32) * o_ref[:, cols].astype(F32)
            left = jnp.sum(jnp.where(lo, prod, 0.0), axis=-1, keepdims=True)
            right = jnp.sum(jnp.where(lo, 0.0, prod), axis=-1, keepdims=True)
            delta_ref[2 * pr] = jnp.broadcast_to(left, (tm, LANES))
            delta_ref[2 * pr + 1] = jnp.broadcast_to(right, (tm, LANES))

    return pl.pallas_call(
        body, name="ax_delta", grid=(S // tm,), in_specs=[_row_spec(tm, AX_W), _row_spec(tm, AX_W)],
        out_specs=pl.BlockSpec((AX_HEADS, tm, LANES), lambda i: (0, i, 0)),
        out_shape=jax.ShapeDtypeStruct((AX_HEADS, S, LANES), F32), compiler_params=_params("parallel"),
    )(dy, raw)


def _ax_bwd(qc, kc, proj, dy, lse_row, delta_row):
    S = qc.shape[0]
    bq, bk = _ax_blocks(S, AX_BWD_BLOCKS)
    nq, nk = S // bq, S // bk
    rep = AX_HEADS // AX_KV_HEADS

    def body(q_ref, k_ref, v_ref, do_ref, lse_ref, delta_ref, dk_ref, dv_ref, dq_hbm, dq_scr, sem):
        j, i = pl.program_id(0), pl.program_id(1)

        @pl.when(i == 0)
        def _():
            dk_ref[...] = jnp.zeros_like(dk_ref)
            dv_ref[...] = jnp.zeros_like(dv_ref)

        @pl.when(j == 0)
        def _():
            dq_scr[i] = jnp.zeros((bq, AX_W), F32)

        kz, vz = _half_variants(k_ref[...]), _half_variants(v_ref[...])
        dk, dv = None, None
        for pr in range(AX_PAIRS):
            cols = slice(pr * LANES, (pr + 1) * LANES)
            qp, dop = q_ref[:, cols], do_ref[:, cols]
            qz, doz = _half_variants(qp), _half_variants(dop)
            dq = None
            for half in range(2):
                h = 2 * pr + half
                g = h // rep
                s_t = lax.dot_general(kz[g][half], qp, NT, preferred_element_type=F32)
                p_t = jnp.exp(s_t - lse_ref[h])
                dp_t = lax.dot_general(vz[g][half], dop, NT, preferred_element_type=F32)
                ds_t = (p_t * (dp_t - delta_ref[h])).astype(BF16)
                a = jnp.dot(p_t.astype(BF16), doz[half][g], preferred_element_type=F32)
                b = jnp.dot(ds_t, qz[half][g], preferred_element_type=F32)
                d = lax.dot_general(ds_t, kz[g][half], TN, preferred_element_type=F32)
                dv = a if dv is None else dv + a
                dk = b if dk is None else dk + b
                dq = d if dq is None else dq + d
            dq_scr[i, :, cols] += dq
        dv_ref[...] += dv
        dk_ref[...] += dk

        @pl.when(j == nk - 1)
        def _():
            dq_scr[i] = dq_scr[i] * QK_SCALE
            out = pltpu.make_async_copy(dq_scr.at[i], dq_hbm.at[pl.ds(pl.multiple_of(i * bq, bq), bq), :], sem)
            out.start()
            out.wait()

    qspec = pl.BlockSpec((bq, AX_W), lambda j, i: (i, 0))
    kspec = pl.BlockSpec((bk, AX_KV_W), lambda j, i: (j, 0))
    stat = pl.BlockSpec((AX_HEADS, 1, bq), lambda j, i: (0, 0, i))
    out = jax.ShapeDtypeStruct((S, AX_KV_W), F32)
    dk, dv, dq = pl.pallas_call(
        body, name="ax_bwd", grid=(nk, nq),
        in_specs=[qspec, kspec, pl.BlockSpec((bk, AX_KV_W), lambda j, i: (j, OFF_VC // AX_KV_W)),
                  qspec, stat, stat],
        out_specs=[kspec, kspec, pl.BlockSpec(memory_space=pl.ANY)],
        out_shape=[out, out, jax.ShapeDtypeStruct((S, AX_W), F32)],
        scratch_shapes=[pltpu.VMEM((nq, bq, AX_W), F32), pltpu.SemaphoreType.DMA],
        compiler_params=_params("arbitrary", "arbitrary"),
    )(qc, kc, proj, dy, lse_row, delta_row)
    return dq, dk, dv


def _oproj_fwd(x, yna, ynb, ync, w, gt):
    S, D = x.shape
    tm = _tile(S, ROW_TILE)

    def body(x_ref, a_ref, b_ref, c_ref, w_ref, gt_ref, x1_ref, ao_ref, yn_ref):
        yn_ref[:, 0:NA_W] = a_ref[...]
        yn_ref[:, NA_W:NA_W + SW_W] = b_ref[...]
        yn_ref[:, NA_W + SW_W:MIX_WIDTH] = c_ref[...]
        acc = jnp.dot(yn_ref[...], w_ref[...], preferred_element_type=F32)
        ao_ref[...] = acc.astype(BF16)
        x1_ref[...] = x_ref[...] + gt_ref[...] * acc

    return pl.pallas_call(
        body, name="oproj_fwd", grid=(S // tm,),
        in_specs=[_row_spec(tm, D), _row_spec(tm, NA_W), _row_spec(tm, SW_W), _row_spec(tm, AX_W),
                  _const_spec(w.shape), _const_spec((1, D))],
        out_specs=[_row_spec(tm, D), _row_spec(tm, D), _row_spec(tm, MIX_WIDTH)],
        out_shape=[jax.ShapeDtypeStruct((S, D), F32), jax.ShapeDtypeStruct((S, D), BF16),
                   jax.ShapeDtypeStruct((S, MIX_WIDTH), BF16)],
        compiler_params=_params("parallel"),
    )(x, yna, ynb, ync, w, gt)


def _gu_fwd(x, g, sc, sh, w):
    S, D = x.shape
    F2 = w.shape[1]
    tn = F2 // 4
    tm = _tile(S, ROW_TILE)

    def body(x_ref, g_ref, sc_ref, sh_ref, w_ref, h_ref, gu_ref, act_ref):
        @pl.when(pl.program_id(1) == 0)
        def _():
            h_ref[...] = _ln_mod(x_ref[...], g_ref[...], sc_ref[...], sh_ref[...]).astype(BF16)

        acc = jnp.dot(h_ref[...], w_ref[pl.program_id(1)], preferred_element_type=F32)
        gu_ref[...] = acc.astype(BF16)
        gate, up = acc[:, :tn], acc[:, tn:]
        act_ref[...] = (gate * (1.0 / (1.0 + jnp.exp(-gate))) * up).astype(BF16)

    vec = pl.BlockSpec((1, D), lambda i, j: (0, 0))
    w = jnp.transpose(w.reshape(D, 2, 2 * tn), (1, 0, 2))
    return pl.pallas_call(
        body, name="gu_fwd", grid=(S // tm, 2),
        in_specs=[pl.BlockSpec((tm, D), lambda i, j: (i, 0)), vec, vec, vec,
                  pl.BlockSpec((2, D, 2 * tn), lambda i, j: (0, 0, 0))],
        out_specs=[pl.BlockSpec((tm, D), lambda i, j: (i, 0)), pl.BlockSpec((tm, 2 * tn), lambda i, j: (i, j)),
                   pl.BlockSpec((tm, tn), lambda i, j: (i, j))],
        out_shape=[jax.ShapeDtypeStruct((S, D), BF16), jax.ShapeDtypeStruct((S, F2), BF16),
                   jax.ShapeDtypeStruct((S, F2 // 2), BF16)],
        compiler_params=_params("parallel", "arbitrary"),
    )(x, g, sc, sh, w)


def _down_fwd(x, act, w, gt):
    S, D = x.shape
    F = act.shape[1]
    tm = _tile(S, ROW_TILE)

    def body(x_ref, a_ref, w_ref, gt_ref, x2_ref, fo_ref):
        acc = jnp.dot(a_ref[...], w_ref[...], preferred_element_type=F32)
        fo_ref[...] = acc.astype(BF16)
        x2_ref[...] = x_ref[...] + gt_ref[...] * acc

    return pl.pallas_call(
        body, name="down_fwd", grid=(S // tm,),
        in_specs=[_row_spec(tm, D), _row_spec(tm, F), _const_spec(w.shape), _const_spec((1, D))],
        out_specs=[_row_spec(tm, D), _row_spec(tm, D)],
        out_shape=[jax.ShapeDtypeStruct((S, D), F32), jax.ShapeDtypeStruct((S, D), BF16)],
        compiler_params=_params("parallel"),
    )(x, act, w, gt)


def _final_loss(x, g, target):
    S, D = x.shape
    tm = _tile(S, ROW_TILE)

    def body(x_ref, g_ref, t_ref, dx_ref, loss_ref, dg_ref):
        @pl.when(pl.program_id(0) == 0)
        def _():
            loss_ref[...] = jnp.zeros_like(loss_ref)
            dg_ref[...] = jnp.zeros_like(dg_ref)

        xv = x_ref[...]
        r = _rsq(jnp.mean(xv * xv, axis=-1, keepdims=True))
        xhat = xv * r
        err = xhat * g_ref[...] - t_ref[...]
        loss_ref[...] += 0.5 * jnp.sum(jnp.mean(err * err, axis=-1, keepdims=True), axis=0, keepdims=True)
        dy = err * (1.0 / D)
        dg_ref[...] += jnp.sum(dy * xhat, axis=0, keepdims=True)
        dxh = dy * g_ref[...]
        dx_ref[...] = r * (dxh - xhat * jnp.mean(dxh * xhat, axis=-1, keepdims=True))

    return pl.pallas_call(
        body, name="final_loss", grid=(S // tm,),
        in_specs=[_row_spec(tm, D), _const_spec((1, D)), _row_spec(tm, D)],
        out_specs=[_row_spec(tm, D), _const_spec((1, LANES)), _const_spec((1, D))],
        out_shape=[jax.ShapeDtypeStruct((S, D), F32), jax.ShapeDtypeStruct((1, LANES), F32),
                   jax.ShapeDtypeStruct((1, D), F32)],
        compiler_params=_params("arbitrary"),
    )(x, g, target)


def _ffn_bwd1(dx2, fo, gt, w_down, gu):
    S, D = dx2.shape
    F2 = gu.shape[1]
    tn = F2 // 4
    tm = _tile(S, ROW_TILE)

    def body(dx_ref, fo_ref, gt_ref, w_ref, gu_ref, dfo_ref, dgu_ref, dgt_ref):
        i, j = pl.program_id(0), pl.program_id(1)

        @pl.when((i == 0) & (j == 0))
        def _():
            dgt_ref[...] = jnp.zeros_like(dgt_ref)

        @pl.when(j == 0)
        def _():
            dxv = dx_ref[...]
            dfo_ref[...] = (dxv * gt_ref[...]).astype(BF16)
            dgt_ref[...] += jnp.sum(dxv * fo_ref[...].astype(F32), axis=0, keepdims=True)

        dact = lax.dot_general(dfo_ref[...], w_ref[j], NT, preferred_element_type=F32)
        gate = gu_ref[:, :tn].astype(F32)
        up = gu_ref[:, tn:].astype(F32)
        sig = 1.0 / (1.0 + jnp.exp(-gate))
        dgu_ref[:, :tn] = (dact * up * (sig * (1.0 + gate * (1.0 - sig)))).astype(BF16)
        dgu_ref[:, tn:] = (dact * (gate * sig)).astype(BF16)

    vec = pl.BlockSpec((1, D), lambda i, j: (0, 0))
    row = pl.BlockSpec((tm, D), lambda i, j: (i, 0))
    return pl.pallas_call(
        body, name="ffn_bwd1", grid=(S // tm, 2),
        in_specs=[row, row, vec, pl.BlockSpec((2, tn, D), lambda i, j: (0, 0, 0)),
                  pl.BlockSpec((tm, 2 * tn), lambda i, j: (i, j))],
        out_specs=[row, pl.BlockSpec((tm, 2 * tn), lambda i, j: (i, j)), vec],
        out_shape=[jax.ShapeDtypeStruct((S, D), BF16), jax.ShapeDtypeStruct((S, F2), BF16),
                   jax.ShapeDtypeStruct((1, D), F32)],
        compiler_params=_params("arbitrary", "arbitrary"),
    )(dx2, fo, gt, w_down.reshape(2, tn, D), gu)


def _nt_ln_bwd(a, w, x, g, sc, dres, name):
    S, D = x.shape
    K = a.shape[1]
    tm = _tile(S, ROW_TILE_WIDE)

    def body(a_ref, w_ref, x_ref, g_ref, sc_ref, dres_ref, dx_ref, dsh_ref, dsc_ref, dg_ref):
        @pl.when(pl.program_id(0) == 0)
        def _():
            dsh_ref[...] = jnp.zeros_like(dsh_ref)
            dsc_ref[...] = jnp.zeros_like(dsc_ref)
            dg_ref[...] = jnp.zeros_like(dg_ref)

        dh = lax.dot_general(a_ref[...], w_ref[...], NT, preferred_element_type=F32)
        xv = x_ref[...]
        r = _rsq(jnp.mean(xv * xv, axis=-1, keepdims=True))
        xhat = xv * r
        gv = g_ref[...]
        dsh_ref[...] += jnp.sum(dh, axis=0, keepdims=True)
        dsc_ref[...] += jnp.sum(dh * (xhat * gv), axis=0, keepdims=True)
        dn = dh * (1.0 + sc_ref[...])
        dg_ref[...] += jnp.sum(dn * xhat, axis=0, keepdims=True)
        dxh = dn * gv
        dx_ref[...] = dres_ref[...] + r * (dxh - xhat * jnp.mean(dxh * xhat, axis=-1, keepdims=True))

    vec = _const_spec((1, D))
    vshape = jax.ShapeDtypeStruct((1, D), F32)
    return pl.pallas_call(
        body, name=name, grid=(S // tm,),
        in_specs=[_row_spec(tm, K), _const_spec(w.shape), _row_spec(tm, D), vec, vec, _row_spec(tm, D)],
        out_specs=[_row_spec(tm, D), vec, vec, vec],
        out_shape=[jax.ShapeDtypeStruct((S, D), F32), vshape, vshape, vshape],
        compiler_params=_params("arbitrary"),
    )(a, w, x, g, sc, dres)


def _oproj_bwd(dx1, ao, gt, w, ya, yb, yc, gg):
    S, D = dx1.shape
    tm = _tile(S, ROW_TILE)
    groups = ((0, NA_W), (NA_W, SW_W), (NA_W + SW_W, AX_W))

    def body(dx_ref, ao_ref, gt_ref, w_ref, ya_ref, yb_ref, yc_ref, gg_ref,
             dao_ref, dya_ref, dyb_ref, dyc_ref, dgt_ref, dgg_ref):
        @pl.when(pl.program_id(0) == 0)
        def _():
            dgt_ref[...] = jnp.zeros_like(dgt_ref)
            dgg_ref[...] = jnp.zeros_like(dgg_ref)

        dxv = dx_ref[...]
        dao = (dxv * gt_ref[...]).astype(BF16)
        dao_ref[...] = dao
        dgt_ref[...] += jnp.sum(dxv * ao_ref[...].astype(F32), axis=0, keepdims=True)
        dyn = lax.dot_general(dao, w_ref[...], NT, preferred_element_type=F32)
        for (off, wd), y_ref, dy_ref in zip(groups, (ya_ref, yb_ref, yc_ref), (dya_ref, dyb_ref, dyc_ref)):
            y = y_ref[...].astype(F32)
            d = dyn[:, off:off + wd]
            r = _rsq(jnp.mean(y * y, axis=-1, keepdims=True))
            yhat = y * r
            dgg_ref[:, off:off + wd] += jnp.sum(d * yhat, axis=0, keepdims=True)
            dyh = d * gg_ref[:, off:off + wd]
            dy_ref[...] = (r * (dyh - yhat * jnp.mean(dyh * yhat, axis=-1, keepdims=True))).astype(BF16)

    vec = _const_spec((1, D))
    mvec = _const_spec((1, MIX_WIDTH))
    return pl.pallas_call(
        body, name="oproj_bwd", grid=(S // tm,),
        in_specs=[_row_spec(tm, D), _row_spec(tm, D), vec, _const_spec(w.shape),
                  _row_spec(tm, NA_W), _row_spec(tm, SW_W), _row_spec(tm, AX_W), mvec],
        out_specs=[_row_spec(tm, D), _row_spec(tm, NA_W), _row_spec(tm, SW_W), _row_spec(tm, AX_W), vec, mvec],
        out_shape=[jax.ShapeDtypeStruct((S, D), BF16), jax.ShapeDtypeStruct((S, NA_W), BF16),
                   jax.ShapeDtypeStruct((S, SW_W), BF16), jax.ShapeDtypeStruct((S, AX_W), BF16),
                   jax.ShapeDtypeStruct((1, D), F32), jax.ShapeDtypeStruct((1, MIX_WIDTH), F32)],
        compiler_params=_params("arbitrary"),
    )(dx1, ao, gt, w, ya, yb, yc, gg)


def _dproj_assemble(proj, na, sw, ax, gq128, gk128, rope):
    S = proj.shape[0]
    tm = _tile(S, ROW_TILE)
    cos, sa, sb = rope

    def body(proj_ref, qa, ka, kah, va, vah, qb, kb, kbh, vb, vbh, qc, kc, vc,
             gq_ref, gk_ref, cos_ref, sa_ref, sb_ref, out_ref, dgq_ref, dgk_ref):
        @pl.when(pl.program_id(0) == 0)
        def _():
            dgq_ref[...] = jnp.zeros_like(dgq_ref)
            dgk_ref[...] = jnp.zeros_like(dgk_ref)

        out_ref[:, OFF_QA:OFF_KA] = qa[...].astype(BF16)
        out_ref[:, OFF_KA:OFF_VA] = (ka[...] + kah[...]).astype(BF16)
        out_ref[:, OFF_VA:OFF_QB] = (va[...] + vah[...]).astype(BF16)
        out_ref[:, OFF_QB:OFF_KB] = qb[...].astype(BF16)
        out_ref[:, OFF_KB:OFF_VB] = (kb[...] + kbh[...]).astype(BF16)
        out_ref[:, OFF_VB:OFF_QC] = (vb[...] + vbh[...]).astype(BF16)
        c, a, b = cos_ref[...], sa_ref[...], sb_ref[...]
        for j in range(AX_W // LANES):
            cols = slice(OFF_QC + j * LANES, OFF_QC + (j + 1) * LANES)
            dx, dg = _qk_prep_bwd_chunk(proj_ref[:, cols].astype(F32), qc[:, j * LANES:(j + 1) * LANES],
                                        gq_ref[...], c, a, b)
            out_ref[:, cols] = dx.astype(BF16)
            dgq_ref[...] += dg
        for j in range(AX_KV_W // LANES):
            cols = slice(OFF_KC + j * LANES, OFF_KC + (j + 1) * LANES)
            dx, dg = _qk_prep_bwd_chunk(proj_ref[:, cols].astype(F32), kc[:, j * LANES:(j + 1) * LANES],
                                        gk_ref[...], c, a, b)
            out_ref[:, cols] = dx.astype(BF16)
            dgk_ref[...] += dg
        out_ref[:, OFF_VC:IN_WIDTH] = vc[...].astype(BF16)

    v128 = _const_spec((1, LANES))
    r = lambda w: _row_spec(tm, w)
    return pl.pallas_call(
        body, name="dproj_assemble", grid=(S // tm,),
        in_specs=[r(IN_WIDTH), r(NA_W), r(NA_W), r(NA_W), r(NA_W), r(NA_W),
                  r(SW_W), r(SW_KV_W), r(SW_KV_W), r(SW_KV_W), r(SW_KV_W),
                  r(AX_W), r(AX_KV_W), r(AX_KV_W), v128, v128, r(LANES), r(LANES), r(LANES)],
        out_specs=[r(IN_WIDTH), v128, v128],
        out_shape=[jax.ShapeDtypeStruct((S, IN_WIDTH), BF16), jax.ShapeDtypeStruct((1, LANES), F32),
                   jax.ShapeDtypeStruct((1, LANES), F32)],
        compiler_params=_params("arbitrary"),
    )(proj, *na, *sw, *ax, gq128, gk128, cos, sa, sb)


def _tn_matmul(a, b, name):
    S, Ka = a.shape
    Nb = b.shape[1]
    tm = _tile(Ka, 1408, LANES)
    tn = _tile(Nb, 1408, LANES)
    tk = _tile(S, TOKEN_CHUNK)
    nk = S // tk

    def body(a_ref, b_ref, o_ref, acc_ref):
        k = pl.program_id(2)

        @pl.when(k == 0)
        def _():
            acc_ref[...] = jnp.zeros_like(acc_ref)

        acc_ref[...] += lax.dot_general(a_ref[...], b_ref[...], TN, preferred_element_type=F32)

        @pl.when(k == nk - 1)
        def _():
            o_ref[...] = acc_ref[...].astype(BF16)

    return pl.pallas_call(
        body, name=name, grid=(Ka // tm, Nb // tn, nk),
        in_specs=[pl.BlockSpec((tk, tm), lambda i, j, k: (k, i)), pl.BlockSpec((tk, tn), lambda i, j, k: (k, j))],
        out_specs=pl.BlockSpec((tm, tn), lambda i, j, k: (i, j)),
        out_shape=jax.ShapeDtypeStruct((Ka, Nb), BF16),
        scratch_shapes=[pltpu.VMEM((tm, tn), F32)],
        compiler_params=_params("parallel", "parallel", "arbitrary"),
    )(a, b)


def _na_index(bd):
    rq = jnp.arange(bd.bq // GRID_W)
    rk = jnp.arange(bd.bk // GRID_W)
    col = jnp.arange(GRID_W)
    ri = jnp.clip(rk[None, :] - rq[:, None] - bd.halo // GRID_W + NA_WIN_ROWS - 1, 0, 2 * NA_WIN_ROWS - 2)
    ci = jnp.clip(col[None, :] - col[:, None] + NA_WIN_COLS - 1, 0, 2 * NA_WIN_COLS - 2)
    return ri, ci


def _na_one_hots(bd):
    ri, ci = _na_index(bd)
    oh_r = jax.nn.one_hot(ri, 2 * NA_WIN_ROWS - 1, dtype=F32)
    oh_c = jax.nn.one_hot(ci, 2 * NA_WIN_COLS - 1, dtype=F32)
    return oh_r, oh_c


def _na_bias(bd, rpb):
    oh_r, oh_c = _na_one_hots(bd)
    t = jnp.einsum("hab,qra->hqrb", rpb, oh_r, precision=lax.Precision.HIGHEST)
    b = jnp.einsum("hqrb,ckb->hqcrk", t, oh_c, precision=lax.Precision.HIGHEST)
    return b.reshape(NA_HEADS, bd.bq, bd.bk)


def _na_bias_t(bd, dbias):
    oh_r, oh_c = _na_one_hots(bd)
    d5 = dbias.reshape(NA_HEADS, bd.bq // GRID_W, GRID_W, bd.bk // GRID_W, GRID_W)
    t = jnp.einsum("hqcrk,ckb->hqrb", d5, oh_c, precision=lax.Precision.HIGHEST)
    return jnp.einsum("hqrb,qra->hab", t, oh_r, precision=lax.Precision.HIGHEST)


def _t5_bucket(rel):
    nb = T5_BUCKETS // 2
    ret = (rel > 0).astype(I32) * nb
    n = jnp.abs(rel)
    max_exact = nb // 2
    nf = jnp.maximum(n, max_exact).astype(F32)
    large = max_exact + (jnp.log(nf / max_exact) / math.log(T5_MAX_DIST / max_exact)
                         * (nb - max_exact)).astype(I32)
    large = jnp.minimum(large, nb - 1)
    return ret + jnp.where(n < max_exact, n, large)


def _sw_bucket(bd):
    rel = (jnp.arange(bd.bk) - bd.halo)[None, :] - jnp.arange(bd.bq)[:, None]
    return _t5_bucket(rel)


def _sw_bias(bd, t5):
    oh = jax.nn.one_hot(_sw_bucket(bd), T5_BUCKETS, dtype=F32)
    return jnp.einsum("bh,qkb->hqk", t5, oh, precision=lax.Precision.HIGHEST)


def _sw_bias_t(bd, dbias):
    oh = jax.nn.one_hot(_sw_bucket(bd), T5_BUCKETS, dtype=F32)
    return jnp.einsum("hqk,qkb->bh", dbias, oh, precision=lax.Precision.HIGHEST)


def _local_step(x, target, mod, w_in, w_o, w_gu, w_down, g_attn, rpb_na, sink_sw, t5_table, gq_ax, gk_ax,
                g_group, g_ffn, g_final):
    S, D = x.shape
    rope = _rope_tables(S)
    na, sw = _Band("na", S), _Band("sw", S)
    two = lambda v: jnp.concatenate([v, v])[None, :]
    sw_bias = _sw_bias(sw, t5_table)
    saved = []
    for l in range(DEPTH):
        sh_a, sc_a, gt_a, sh_f, sc_f, gt_f = [mod[l, k * D:(k + 1) * D][None, :] for k in range(6)]
        gq128, gk128 = two(gq_ax[l]), two(gk_ax[l])
        gg = g_group[l][None, :]
        sink = jnp.pad(sink_sw[l], (0, LANES - SW_HEADS))[None, :]
        na_bias = _na_bias(na, rpb_na[l])
        h, proj, qc, kc = _inproj_fwd(x, g_attn[l][None, :], sc_a, sh_a, w_in[l], gq128, gk128, rope)
        ya, yna = _band_fwd(na, proj, na_bias, None, gg[:, :NA_W])
        yb, ynb = _band_fwd(sw, proj, sw_bias, sink, gg[:, NA_W:NA_W + SW_W])
        yc, ync, lse = _ax_fwd(qc, kc, proj, gg[:, NA_W + SW_W:])
        x1, ao, yn = _oproj_fwd(x, yna, ynb, ync, w_o[l], gt_a)
        hf, gu, act = _gu_fwd(x1, g_ffn[l][None, :], sc_f, sh_f, w_gu[l])
        x2, fo = _down_fwd(x1, act, w_down[l], gt_f)
        saved.append(dict(x=x, x1=x1, h=h, proj=proj, qc=qc, kc=kc, ya=ya, yb=yb, yc=yc, lse=lse, ao=ao, yn=yn,
                          hf=hf, gu=gu, act=act, fo=fo, na_bias=na_bias, sink=sink, gq128=gq128, gk128=gk128,
                          gg=gg, mods=(sh_a, sc_a, gt_a, sh_f, sc_f, gt_f)))
        x = x2

    dx, loss_row, dg_final = _final_loss(x, g_final[None, :], target)
    gw = {k: [None] * DEPTH for k in ("w_in", "w_o", "w_gu", "w_down")}
    gs = {k: [None] * DEPTH for k in ("b_mod", "g_attn", "rpb_na", "sink_sw", "gq_ax", "gk_ax", "g_group", "g_ffn")}
    d_t5 = jnp.zeros((T5_BUCKETS, SW_HEADS), F32)
    for l in reversed(range(DEPTH)):
        s = saved[l]
        sh_a, sc_a, gt_a, sh_f, sc_f, gt_f = s["mods"]
        dfo, dgu, dgt_f = _ffn_bwd1(dx, s["fo"], gt_f, w_down[l], s["gu"])
        gw["w_down"][l] = _tn_matmul(s["act"], dfo, "dw_down")
        gw["w_gu"][l] = _tn_matmul(s["hf"], dgu, "dw_gu")
        dx1, dsh_f, dsc_f, gs["g_ffn"][l] = _nt_ln_bwd(dgu, w_gu[l], s["x1"], g_ffn[l][None, :], sc_f, dx, "ffn_bwd2")
        dao, dya, dyb, dyc, dgt_a, gs["g_group"][l] = _oproj_bwd(dx1, s["ao"], gt_a, w_o[l], s["ya"], s["yb"],
                                                                 s["yc"], s["gg"])
        gw["w_o"][l] = _tn_matmul(s["yn"], dao, "dw_o")
        dqa, dka, dva, dkap, dvap, dkan, dvan, dbias_na = _band_bwd(na, s["proj"], jnp.swapaxes(s["na_bias"], 1, 2), None, dya)
        dqb, dkb, dvb, dkbp, dvbp, dkbn, dvbn, dbias_sw, dsink = _band_bwd(sw, s["proj"], jnp.swapaxes(sw_bias, 1, 2), s["sink"], dyb)
        as_row = lambda a: a[:, :, 0][:, None, :]
        dqc, dkc, dvc = _ax_bwd(s["qc"], s["kc"], s["proj"], dyc, as_row(s["lse"]), as_row(_ax_delta(dyc, s["yc"])))
        dproj, dgq, dgk = _dproj_assemble(
            s["proj"], (dqa, dka, _halo_to_rows(dkap, dkan), dva, _halo_to_rows(dvap, dvan)),
            (dqb, dkb, _halo_to_rows(dkbp, dkbn), dvb, _halo_to_rows(dvbp, dvbn)), (dqc, dkc, dvc),
            s["gq128"], s["gk128"], rope)
        gw["w_in"][l] = _tn_matmul(s["h"], dproj, "dw_in")
        dx, dsh_a, dsc_a, gs["g_attn"][l] = _nt_ln_bwd(dproj, w_in[l], s["x"], g_attn[l][None, :], sc_a, dx1,
                                                       "inproj_bwd")
        gs["b_mod"][l] = jnp.concatenate([dsh_a, dsc_a, dgt_a, dsh_f, dsc_f, dgt_f], axis=1)[0]
        gs["rpb_na"][l] = _na_bias_t(na, jnp.swapaxes(dbias_na, 1, 2))
        gs["sink_sw"][l] = dsink[0, :SW_HEADS]
        d_t5 = d_t5 + _sw_bias_t(sw, jnp.swapaxes(dbias_sw, 1, 2))
        gs["gq_ax"][l] = dgq[0, :HEAD_DIM] + dgq[0, HEAD_DIM:]
        gs["gk_ax"][l] = dgk[0, :HEAD_DIM] + dgk[0, HEAD_DIM:]
        gs["g_attn"][l] = gs["g_attn"][l][0]
        gs["g_ffn"][l] = gs["g_ffn"][l][0]
        gs["g_group"][l] = gs["g_group"][l][0]

    gw = {k: jnp.stack(v) for k, v in gw.items()}
    small = {k: jnp.stack(v) for k, v in gs.items()}
    small["t5_table"] = d_t5
    small["g_final"] = dg_final[0]
    return loss_row[0, 0], dx, gw, small


MOD_ROWS = 16


def _mod_fwd(cond16, w):
    L, D, C = w.shape
    tn = _tile(C, 512, LANES)

    def body(c_ref, w_ref, o_ref):
        o_ref[0] = jnp.dot(c_ref[...], w_ref[0].astype(BF16), preferred_element_type=F32)

    return pl.pallas_call(
        body, name="mod_fwd", grid=(L, C // tn),
        in_specs=[pl.BlockSpec((MOD_ROWS, D), lambda l, j: (0, 0)), pl.BlockSpec((1, D, tn), lambda l, j: (l, 0, j))],
        out_specs=pl.BlockSpec((1, MOD_ROWS, tn), lambda l, j: (l, 0, j)),
        out_shape=jax.ShapeDtypeStruct((L, MOD_ROWS, C), F32),
        compiler_params=_params("parallel", "parallel"),
    )(cond16, w)


def _adamw_math(w, g, m, v):
    m = ADAM_B1 * m + (1.0 - ADAM_B1) * g
    v = ADAM_B2 * v + (1.0 - ADAM_B2) * (g * g)
    m_hat = m / (1.0 - ADAM_B1 ** ADAM_STEP)
    v_hat = v / (1.0 - ADAM_B2 ** ADAM_STEP)
    delta = -ADAM_LR * (m_hat / (jnp.sqrt(v_hat) + ADAM_EPS) + ADAM_WD * w)
    return delta, m, v


def _adamw(w, m, v, parts, name):
    R, C = w.shape
    tr = _tile(R, 256)
    n = len(parts)

    def body(*refs):
        w_ref, m_ref, v_ref = refs[:3]
        g = refs[3][...]
        for p in refs[4:3 + n]:
            g = g + p[...]
        g_ref, d_ref, m2_ref, v2_ref = refs[3 + n:]
        g_ref[...] = g
        d_ref[...], m2_ref[...], v2_ref[...] = _adamw_math(w_ref[...], g, m_ref[...], v_ref[...])

    spec = _row_spec(tr, C)
    shape = jax.ShapeDtypeStruct((R, C), F32)
    return pl.pallas_call(
        body, name=name, grid=(R // tr,), in_specs=[spec] * (3 + n), out_specs=[spec] * 4, out_shape=[shape] * 4,
        compiler_params=_params("parallel"),
    )(w, m, v, *parts)


def _wmod_adamw(cond_t, dmod16, w, m, v):
    L, D, C = w.shape
    tr = _tile(D, 256)

    def body(c_ref, d_ref, w_ref, m_ref, v_ref, g_ref, dl_ref, m2_ref, v2_ref):
        g = jnp.dot(c_ref[...], d_ref[0], preferred_element_type=F32)
        g_ref[0] = g
        dl_ref[0], m2_ref[0], v2_ref[0] = _adamw_math(w_ref[0], g, m_ref[0], v_ref[0])

    spec = pl.BlockSpec((1, tr, C), lambda l, i: (l, i, 0))
    shape = jax.ShapeDtypeStruct((L, D, C), F32)
    return pl.pallas_call(
        body, name="wmod_adamw", grid=(L, D // tr),
        in_specs=[pl.BlockSpec((tr, MOD_ROWS), lambda l, i: (i, 0)),
                  pl.BlockSpec((1, MOD_ROWS, C), lambda l, i: (l, 0, 0)), spec, spec, spec],
        out_specs=[spec] * 4, out_shape=[shape] * 4,
        compiler_params=_params("parallel", "parallel"),
    )(cond_t, dmod16, w, m, v)


def _sum_slots(a):
    P, R, C = a.shape
    tr = _tile(R, 256, 16)

    def body(a_ref, o_ref):
        s = a_ref[0].astype(F32)
        for k in range(1, P):
            s = s + a_ref[k].astype(F32)
        o_ref[...] = s

    return pl.pallas_call(
        body, name="sum_slots", grid=(R // tr,),
        in_specs=[pl.BlockSpec((P, tr, C), lambda i: (0, i, 0))], out_specs=_row_spec(tr, C),
        out_shape=jax.ShapeDtypeStruct((R, C), F32), compiler_params=_params("parallel"),
    )(a)


def _axes():
    return lax.axis_index("x"), lax.axis_index("y"), lax.axis_index("c")


def _allgather_devices(v):
    N = v.shape[1]

    def body(v_ref, out_ref, send_sems, recv_sems, local_sem):
        x, y, c = _axes()

        def row(px, py, pc):
            return out_ref.at[pl.ds(4 * px + 2 * py + pc, 1), :]

        mine = pltpu.make_async_copy(v_ref, row(x, y, c), local_sem)
        mine.start()
        sends, recvs = [], []
        for k in range(1, N_DEV):
            peer = (x ^ (k >> 2), y ^ ((k >> 1) & 1), c ^ (k & 1))
            sems = dict(send_sem=send_sems.at[k - 1], recv_sem=recv_sems.at[k - 1], device_id=peer, device_id_type=MESH)
            sends.append(pltpu.make_async_remote_copy(src_ref=v_ref, dst_ref=row(x, y, c), **sems))
            recvs.append(pltpu.make_async_remote_copy(src_ref=v_ref, dst_ref=row(*peer), **sems))
        for cp in sends:
            cp.start()
        for cp in recvs:
            cp.wait_recv()
        for cp in sends:
            cp.wait_send()
        mine.wait()

    vmem = pl.BlockSpec(memory_space=pltpu.VMEM)
    return pl.pallas_call(
        body, name="allgather_devices", in_specs=[vmem], out_specs=vmem,
        out_shape=jax.ShapeDtypeStruct((N_DEV, N), v.dtype),
        scratch_shapes=[pltpu.SemaphoreType.DMA((N_DEV - 1,)), pltpu.SemaphoreType.DMA((N_DEV - 1,)),
                        pltpu.SemaphoreType.DMA],
        compiler_params=pltpu.CompilerParams(vmem_limit_bytes=VMEM_LIMIT_V7X),
    )(v)


def _chip_pos(order, px, py):
    return 2 * px + py if order == "natural" else 2 * py + px


def _block(ref, axis, pos, width):
    idx = [slice(None)] * len(ref.shape)
    idx[axis] = pl.ds(pl.multiple_of(pos * width, width), width)
    return ref.at[tuple(idx)]


def _chip_allgather(shards, axes, orders, name):
    n = len(shards)
    out_shapes = []
    for s, ax in zip(shards, axes):
        shp = list(s.shape)
        shp[ax] *= N_CHIPS
        out_shapes.append(jax.ShapeDtypeStruct(tuple(shp), s.dtype))

    def body(*refs):
        ins, outs = refs[:n], refs[n:2 * n]
        send_sems, recv_sems, local_sems = refs[2 * n:]
        x, y, c = _axes()
        place = lambda i, px, py: _block(outs[i], axes[i], _chip_pos(orders[i], px, py), shards[i].shape[axes[i]])
        local, sends, recvs = [], [], []
        for i in range(n):
            local.append(pltpu.make_async_copy(ins[i], place(i, x, y), local_sems.at[i]))
            for k in range(1, N_CHIPS):
                px, py = x ^ (k >> 1), y ^ (k & 1)
                j = i * (N_CHIPS - 1) + k - 1
                sems = dict(send_sem=send_sems.at[j], recv_sem=recv_sems.at[j], device_id=(px, py, c),
                            device_id_type=MESH)
                sends.append(pltpu.make_async_remote_copy(src_ref=ins[i], dst_ref=place(i, x, y), **sems))
                recvs.append(pltpu.make_async_remote_copy(src_ref=ins[i], dst_ref=place(i, px, py), **sems))
        for cp in local + sends:
            cp.start()
        for cp in recvs:
            cp.wait_recv()
        for cp in sends:
            cp.wait_send()
        for cp in local:
            cp.wait()

    hbm = pl.BlockSpec(memory_space=pl.ANY)
    nsem = n * (N_CHIPS - 1)
    return pl.pallas_call(
        body, name=name, in_specs=[hbm] * n, out_specs=[hbm] * n, out_shape=out_shapes,
        scratch_shapes=[pltpu.SemaphoreType.DMA((nsem,)), pltpu.SemaphoreType.DMA((nsem,)),
                        pltpu.SemaphoreType.DMA((n,))],
    )(*shards)


def _chip_scatter(grads, axes, orders, name):
    n = len(grads)
    widths, out_shapes = [], []
    for g, ax in zip(grads, axes):
        shp = list(g.shape)
        shp[ax] //= N_CHIPS
        widths.append(shp[ax])
        out_shapes.append(jax.ShapeDtypeStruct((N_CHIPS,) + tuple(shp), g.dtype))

    def body(*refs):
        ins, outs = refs[:n], refs[n:2 * n]
        send_sems, recv_sems, local_sems = refs[2 * n:]
        x, y, c = _axes()
        piece = lambda i, px, py: _block(ins[i], axes[i], _chip_pos(orders[i], px, py), widths[i])
        slot = lambda i, px, py: outs[i].at[2 * px + py]
        local, sends, recvs = [], [], []
        for i in range(n):
            local.append(pltpu.make_async_copy(piece(i, x, y), slot(i, x, y), local_sems.at[i]))
            for k in range(1, N_CHIPS):
                px, py = x ^ (k >> 1), y ^ (k & 1)
                j = i * (N_CHIPS - 1) + k - 1
                sems = dict(send_sem=send_sems.at[j], recv_sem=recv_sems.at[j], device_id=(px, py, c),
                            device_id_type=MESH)
                sends.append(pltpu.make_async_remote_copy(src_ref=piece(i, px, py), dst_ref=slot(i, x, y), **sems))
                recvs.append(pltpu.make_async_remote_copy(src_ref=piece(i, px, py), dst_ref=slot(i, px, py), **sems))
        for cp in local + sends:
            cp.start()
        for cp in recvs:
            cp.wait_recv()
        for cp in sends:
            cp.wait_send()
        for cp in local:
            cp.wait()

    hbm = pl.BlockSpec(memory_space=pl.ANY)
    nsem = n * (N_CHIPS - 1)
    return pl.pallas_call(
        body, name=name, in_specs=[hbm] * n, out_specs=[hbm] * n, out_shape=out_shapes,
        scratch_shapes=[pltpu.SemaphoreType.DMA((nsem,)), pltpu.SemaphoreType.DMA((nsem,)),
                        pltpu.SemaphoreType.DMA((n,))],
    )(*grads)


def _core_swap(arrays, name):
    n = len(arrays)

    def body(*refs):
        ins, outs = refs[:n], refs[n:2 * n]
        send_sems, recv_sems = refs[2 * n:]
        x, y, c = _axes()
        copies = [pltpu.make_async_remote_copy(src_ref=ins[i], dst_ref=outs[i], send_sem=send_sems.at[i],
                                               recv_sem=recv_sems.at[i], device_id=(x, y, 1 - c), device_id_type=MESH)
                  for i in range(n)]
        for cp in copies:
            cp.start()
        for cp in copies:
            cp.wait_recv()
        for cp in copies:
            cp.wait_send()

    hbm = pl.BlockSpec(memory_space=pl.ANY)
    return pl.pallas_call(
        body, name=name, in_specs=[hbm] * n, out_specs=[hbm] * n,
        out_shape=[jax.ShapeDtypeStruct(a.shape, a.dtype) for a in arrays],
        scratch_shapes=[pltpu.SemaphoreType.DMA((n,)), pltpu.SemaphoreType.DMA((n,))],
    )(*arrays)


SMALL = ("b_mod", "g_attn", "rpb_na", "sink_sw", "t5_table", "gq_ax", "gk_ax", "g_group", "g_ffn", "g_final")
BIG = ("w_in", "w_o", "w_gu", "w_down")
BIG_AXIS = {"w_in": 2, "w_o": 1, "w_gu": 2, "w_down": 1}
BIG_ORDER = {"w_in": "natural", "w_o": "natural", "w_gu": "gate_up_tiles", "w_down": "natural"}
WEIGHTS = ("w_mod", "b_mod", "g_attn", "w_in", "rpb_na", "sink_sw", "t5_table", "gq_ax", "gk_ax", "g_group",
           "w_o", "g_ffn", "w_gu", "w_down", "g_final")


def _pack(arrs):
    flat = jnp.concatenate([a.reshape(-1) for a in arrs])
    n = flat.shape[0]
    padded = -(-n // (8 * LANES)) * (8 * LANES)
    return jnp.pad(flat, (0, padded - n))


def _unpack(flat, like):
    out, off = [], 0
    for a in like:
        out.append(flat[off:off + a.size].reshape(a.shape))
        off += a.size
    return out


def kernel(x, c, w_mod, b_mod, g_attn, w_in, rpb_na, sink_sw, t5_table, gq_ax, gk_ax, g_group, w_o, g_ffn, w_gu, w_down, g_final, loss_target, m_w_mod, m_b_mod, m_g_attn, m_w_in, m_rpb_na, m_sink_sw, m_t5_table, m_gq_ax, m_gk_ax, m_g_group, m_w_o, m_g_ffn, m_w_gu, m_w_down, m_g_final, v_w_mod, v_b_mod, v_g_attn, v_w_in, v_rpb_na, v_sink_sw, v_t5_table, v_gq_ax, v_gk_ax, v_g_group, v_w_o, v_g_ffn, v_w_gu, v_w_down, v_g_final):
    W = dict(w_mod=w_mod, b_mod=b_mod, g_attn=g_attn, w_in=w_in, rpb_na=rpb_na, sink_sw=sink_sw, t5_table=t5_table,
             gq_ax=gq_ax, gk_ax=gk_ax, g_group=g_group, w_o=w_o, g_ffn=g_ffn, w_gu=w_gu, w_down=w_down,
             g_final=g_final)
    M = dict(w_mod=m_w_mod, b_mod=m_b_mod, g_attn=m_g_attn, w_in=m_w_in, rpb_na=m_rpb_na, sink_sw=m_sink_sw,
             t5_table=m_t5_table, gq_ax=m_gq_ax, gk_ax=m_gk_ax, g_group=m_g_group, w_o=m_w_o, g_ffn=m_g_ffn,
             w_gu=m_w_gu, w_down=m_w_down, g_final=m_g_final)
    V = dict(w_mod=v_w_mod, b_mod=v_b_mod, g_attn=v_g_attn, w_in=v_w_in, rpb_na=v_rpb_na, sink_sw=v_sink_sw,
             t5_table=v_t5_table, gq_ax=v_gq_ax, gk_ax=v_gk_ax, g_group=v_g_group, w_o=v_w_o, g_ffn=v_g_ffn,
             w_gu=v_w_gu, w_down=v_w_down, g_final=v_g_final)
    xi, yi, ci = _axes()
    me = 4 * xi + 2 * yi + ci
    chip = 2 * xi + yi
    D = x.shape[-1]
    mod_w = w_mod.shape[2]

    c_all = _allgather_devices(c)
    cond = c_all * (1.0 / (1.0 + jnp.exp(-c_all)))
    cond16 = jnp.pad(cond, ((0, MOD_ROWS - N_DEV), (0, 0))).astype(BF16)
    mod_part = _mod_fwd(cond16, w_mod)
    (mod_all,) = _chip_allgather([mod_part], [2], ["natural"], "allgather_mod")
    mod = lax.dynamic_slice_in_dim(mod_all, me, 1, axis=1)[:, 0, :] + b_mod

    w_full = _chip_allgather([W[k].astype(BF16) for k in BIG], [BIG_AXIS[k] for k in BIG],
                             [BIG_ORDER[k] for k in BIG], "allgather_weights")
    loss_part, grad_x, gw, small = _local_step(x[0], loss_target[0], mod, *w_full, g_attn, rpb_na, sink_sw, t5_table,
                                               gq_ax, gk_ax, g_group, g_ffn, g_final)

    small_all = _allgather_devices(_pack([small[k] for k in SMALL])[None, :])
    rows = small_all.shape[1] // LANES
    parts = [small_all[k].reshape(rows, LANES) for k in range(N_DEV)]
    pk = lambda d: _pack([d[k] for k in SMALL]).reshape(rows, LANES)
    small_out = [_unpack(o.reshape(-1), [W[k] for k in SMALL]) for o in _adamw(pk(W), pk(M), pk(V), parts, "adamw_small")]

    L = w_mod.shape[0]
    dmod_all = small_all[:, :L * 6 * D].reshape(N_DEV, L, 6 * D)
    dmod_mine = lax.dynamic_slice_in_dim(dmod_all, chip * mod_w, mod_w, axis=2)
    dmod16 = jnp.pad(jnp.transpose(dmod_mine, (1, 0, 2)), ((0, 0), (0, MOD_ROWS - N_DEV), (0, 0))).astype(BF16)
    wmod_out = _wmod_adamw(jnp.transpose(cond16), dmod16, w_mod, m_w_mod, v_w_mod)

    names = list(BIG)
    slots = _chip_scatter([gw[k] for k in names], [BIG_AXIS[k] for k in names], [BIG_ORDER[k] for k in names],
                          "scatter_grads")
    two_d = lambda a: a.reshape(-1, a.shape[-1])
    mine = [_sum_slots(s.reshape(N_CHIPS, -1, s.shape[-1])) for s in slots]
    theirs = _core_swap(mine, "swap_grads")
    big_out = {}
    for k, a, b in zip(names, mine, theirs):
        outs = _adamw(two_d(W[k]), two_d(M[k]), two_d(V[k]), [a, b], "adamw_" + k)
        big_out[k] = [o.reshape(W[k].shape) for o in outs]

    loss = lax.psum(loss_part, ("x", "y", "c"))
    per_kind = []
    for kind in range(4):
        for k in WEIGHTS:
            if k == "w_mod":
                per_kind.append(wmod_out[kind])
            elif k in big_out:
                per_kind.append(big_out[k][kind])
            else:
                per_kind.append(small_out[kind][SMALL.index(k)])
    return (loss, grad_x[None], *per_kind)
```

```python
import functools
import math

import jax
import jax.numpy as jnp
from jax import lax
from jax.experimental import pallas as pl
from jax.experimental.pallas import tpu as pltpu

F32 = jnp.float32
BF16 = jnp.bfloat16
I32 = jnp.int32

DEPTH = 2
HEAD_DIM = 64
GRID_W = 64
NA_HEADS = 4
SW_HEADS = 6
SW_KV_HEADS = 2
AX_HEADS = 6
AX_KV_HEADS = 2
NA_WIN_ROWS = 8
NA_WIN_COLS = 16
SW_RADIUS = 128
T5_BUCKETS = 32
T5_MAX_DIST = 128
ROPE_THETA = 10000.0
EPS = 1e-6
NEG_INF = -1e30
QK_SCALE = HEAD_DIM ** -0.5

NA_W = NA_HEADS * HEAD_DIM
SW_W = SW_HEADS * HEAD_DIM
SW_KV_W = SW_KV_HEADS * HEAD_DIM
AX_W = AX_HEADS * HEAD_DIM
AX_KV_W = AX_KV_HEADS * HEAD_DIM
OFF_QA, OFF_KA, OFF_VA = 0, NA_W, 2 * NA_W
OFF_QB = 3 * NA_W
OFF_KB = OFF_QB + SW_W
OFF_VB = OFF_KB + SW_KV_W
OFF_QC = OFF_VB + SW_KV_W
OFF_KC = OFF_QC + AX_W
OFF_VC = OFF_KC + AX_KV_W
IN_WIDTH = OFF_VC + AX_KV_W
MIX_WIDTH = NA_W + SW_W + AX_W

ADAM_LR = 0.001
ADAM_B1 = 0.9
ADAM_B2 = 0.999
ADAM_EPS = 1e-08
ADAM_WD = 0.01
ADAM_STEP = 10

N_CHIPS = 4
N_DEV = 8
LANES = 128
VMEM_LIMIT_V7X = 56 * 1024 * 1024
ROW_TILE = 512
ROW_TILE_WIDE = 256
TOKEN_CHUNK = 1024
MESH = pl.DeviceIdType.MESH

NT = (((1,), (1,)), ((), ()))
TN = (((0,), (0,)), ((), ()))


def _params(*sem):
    return pltpu.CompilerParams(dimension_semantics=sem if sem else None,
                                vmem_limit_bytes=VMEM_LIMIT_V7X)


def _tile(n, pref, mult=8):
    t = (min(pref, n) // mult) * mult
    while t >= mult:
        if n % t == 0:
            return t
        t -= mult
    return n


def _row_spec(tm, width, col=0):
    return pl.BlockSpec((tm, width), lambda i, *_: (i, col))


def _const_spec(shape):
    nd = len(shape)
    return pl.BlockSpec(shape, lambda *_: (0,) * nd)


def _rsq(ms):
    return lax.rsqrt(ms + EPS)


def _rope_tables(S):
    rows = S // GRID_W
    axis_dim = HEAD_DIM // 2
    quarter = axis_dim // 2
    lane = jnp.arange(LANES)
    freq = (ROPE_THETA ** (-(2 * (lane % quarter)).astype(F32) / axis_dim))[None, :]
    by_row = ((lane % HEAD_DIM) < axis_dim)[None, None, :]
    first = ((lane % axis_dim) < quarter)[None, :]
    ang_r = jnp.arange(rows, dtype=F32)[:, None] * freq
    ang_c = jnp.arange(GRID_W, dtype=F32)[:, None] * freq

    def table(fr, fc):
        t = jnp.where(by_row, fr[:, None, :], fc[None, :, :])
        return t.reshape(S, LANES)

    sin_r, sin_c = jnp.sin(ang_r), jnp.sin(ang_c)
    return (table(jnp.cos(ang_r), jnp.cos(ang_c)),
            table(jnp.where(first, -sin_r, 0.0), jnp.where(first, -sin_c, 0.0)),
            table(jnp.where(first, 0.0, sin_r), jnp.where(first, 0.0, sin_c)))


def _pair_sum(v):
    lane = lax.broadcasted_iota(I32, v.shape, 1)
    lo = lane < HEAD_DIM
    s_lo = jnp.sum(jnp.where(lo, v, 0.0), axis=-1, keepdims=True)
    s_hi = jnp.sum(jnp.where(lo, 0.0, v), axis=-1, keepdims=True)
    return jnp.where(lo, s_lo, s_hi)


def _rope(t, cos, sa, sb):
    return t * cos + pltpu.roll(t, LANES - 16, 1) * sa + pltpu.roll(t, 16, 1) * sb


def _rope_t(t, cos, sa, sb):
    return t * cos + pltpu.roll(t * sa, 16, 1) + pltpu.roll(t * sb, LANES - 16, 1)


def _qk_prep_chunk(x, g128, cos, sa, sb):
    r = _rsq(_pair_sum(x * x) * (1.0 / HEAD_DIM))
    return _rope(x * r * g128, cos, sa, sb)


def _qk_prep_bwd_chunk(x, dy, g128, cos, sa, sb):
    dn = _rope_t(dy, cos, sa, sb)
    r = _rsq(_pair_sum(x * x) * (1.0 / HEAD_DIM))
    xhat = x * r
    dg = jnp.sum(dn * xhat, axis=0, keepdims=True)
    dxh = dn * g128
    dx = r * (dxh - xhat * (_pair_sum(dxh * xhat) * (1.0 / HEAD_DIM)))
    return dx, dg


def _ln_mod(xv, g, sc, sh):
    r = _rsq(jnp.mean(xv * xv, axis=-1, keepdims=True))
    return xv * r * g * (1.0 + sc) + sh


def _inproj_fwd(x, g, sc, sh, w, gq128, gk128, rope):
    S, D = x.shape
    tm = _tile(S, ROW_TILE)
    cos, sa, sb = rope

    def body(x_ref, g_ref, sc_ref, sh_ref, w_ref, gq_ref, gk_ref, cos_ref, sa_ref, sb_ref,
             h_ref, proj_ref, qc_ref, kc_ref):
        hb = _ln_mod(x_ref[...], g_ref[...], sc_ref[...], sh_ref[...]).astype(BF16)
        h_ref[...] = hb
        acc = jnp.dot(hb, w_ref[...], preferred_element_type=F32)
        proj_ref[...] = acc.astype(BF16)
        c, a, b = cos_ref[...], sa_ref[...], sb_ref[...]
        for j in range(AX_W // LANES):
            xq = acc[:, OFF_QC + j * LANES: OFF_QC + (j + 1) * LANES]
            qc_ref[:, j * LANES:(j + 1) * LANES] = (
                _qk_prep_chunk(xq, gq_ref[...], c, a, b) * QK_SCALE).astype(BF16)
        for j in range(AX_KV_W // LANES):
            xk = acc[:, OFF_KC + j * LANES: OFF_KC + (j + 1) * LANES]
            kc_ref[:, j * LANES:(j + 1) * LANES] = _qk_prep_chunk(xk, gk_ref[...], c, a, b).astype(BF16)

    vec = _const_spec((1, D))
    v128 = _const_spec((1, LANES))
    return pl.pallas_call(
        body, name="inproj_fwd", grid=(S // tm,),
        in_specs=[_row_spec(tm, D), vec, vec, vec, _const_spec(w.shape), v128, v128,
                  _row_spec(tm, LANES), _row_spec(tm, LANES), _row_spec(tm, LANES)],
        out_specs=[_row_spec(tm, D), _row_spec(tm, IN_WIDTH), _row_spec(tm, AX_W), _row_spec(tm, AX_KV_W)],
        out_shape=[jax.ShapeDtypeStruct((S, D), BF16), jax.ShapeDtypeStruct((S, IN_WIDTH), BF16),
                   jax.ShapeDtypeStruct((S, AX_W), BF16), jax.ShapeDtypeStruct((S, AX_KV_W), BF16)],
        compiler_params=_params("parallel"),
    )(x, g, sc, sh, w, gq128, gk128, cos, sa, sb)


class _Band:
    def __init__(self, kind, S):
        self.kind = kind
        self.S = S
        if kind == "na":
            self.hq, self.g, self.halo = NA_HEADS, NA_HEADS, (NA_WIN_ROWS // 2) * GRID_W
            self.q_off, self.k_off, self.v_off = OFF_QA, OFF_KA, OFF_VA
        else:
            self.hq, self.g, self.halo = SW_HEADS, SW_KV_HEADS, SW_RADIUS
            self.q_off, self.k_off, self.v_off = OFF_QB, OFF_KB, OFF_VB
        self.bq = 2 * self.halo
        self.bk = self.bq + 2 * self.halo
        self.nb = S // self.bq
        self.rep = self.hq // self.g
        self.qw = self.hq * HEAD_DIM
        self.kw = self.g * HEAD_DIM

    def kv_of(self, h):
        return (h // 2, h % 2) if self.kind == "na" else (0, h // self.rep)

    def mask(self, n, transposed=False):
        shape = (self.bk, self.bq) if transposed else (self.bq, self.bk)
        qi = lax.broadcasted_iota(I32, shape, 1 if transposed else 0) + n * self.bq
        kj = lax.broadcasted_iota(I32, shape, 0 if transposed else 1) + (n * self.bq - self.halo)
        if self.kind == "sw":
            return (jnp.abs(kj - qi) <= SW_RADIUS) & (kj >= 0) & (kj < self.S)
        rows = self.S // GRID_W
        r, col = qi >> 6, qi & (GRID_W - 1)
        kr, kc = kj >> 6, kj & (GRID_W - 1)
        rs = jnp.clip(r - NA_WIN_ROWS // 2, 0, rows - NA_WIN_ROWS)
        cs = jnp.clip(col - NA_WIN_COLS // 2, 0, GRID_W - NA_WIN_COLS)
        return (kr >= rs) & (kr < rs + NA_WIN_ROWS) & (kc >= cs) & (kc < cs + NA_WIN_COLS)

    def qkv_specs(self):
        ratio = self.bq // self.halo
        last = self.S // self.halo - 1
        q = pl.BlockSpec((self.bq, self.qw), lambda n, o=self.q_off // self.qw: (n, o))
        specs = [q]
        for off in (self.k_off, self.v_off):
            o = off // self.kw
            specs.append(pl.BlockSpec((self.halo, self.kw), lambda n, o=o: (jnp.maximum(n * ratio - 1, 0), o)))
            specs.append(pl.BlockSpec((self.bq, self.kw), lambda n, o=o: (n, o)))
            specs.append(pl.BlockSpec((self.halo, self.kw), lambda n, o=o: (jnp.minimum((n + 1) * ratio, last), o)))
        return specs


def _band_kv_variants(bd, refs, fill):
    out = []
    for blk in range(bd.kw // LANES):
        cols = slice(blk * LANES, (blk + 1) * LANES)
        out.append(_half_variants(jnp.concatenate([r[:, cols] for r in refs], axis=0), fill))
    return out


def _band_fwd(bd, proj, bias, sink, gg):
    S = bd.S
    has_sink = sink is not None

    def body(*refs):
        q_ref, kp, km, kn, vp, vm, vn, bias_ref = refs[:8]
        k = 8
        sink_ref = None
        if has_sink:
            sink_ref = refs[k]
            k += 1
        gg_ref, raw_ref, yn_ref, o_scr = refs[k:k + 4]
        mask = bd.mask(pl.program_id(0))
        lo = _left_half((bd.bq, LANES))
        kzs, vzs = _band_kv_variants(bd, (kp, km, kn), 0.0), _band_kv_variants(bd, (vp, vm, vn), 1.0)
        for pr in range(bd.hq // 2):
            cols = slice(pr * LANES, (pr + 1) * LANES)
            qp = q_ref[:, cols] * QK_SCALE
            acc = []
            for half in range(2):
                h = 2 * pr + half
                blk, src = bd.kv_of(h)
                s = lax.dot_general(qp, kzs[blk][src][half], NT, preferred_element_type=F32) + bias_ref[h]
                s = jnp.where(mask, s, NEG_INF)
                m = jnp.max(s, axis=-1, keepdims=True)
                if has_sink:
                    m = jnp.maximum(m, sink_ref[0:1, h:h + 1])
                a = jnp.dot(jnp.exp(s - m).astype(BF16), vzs[blk][src][half], preferred_element_type=F32)
                if has_sink:
                    e = jnp.exp(sink_ref[0:1, h:h + 1] - m)
                    a = a + (jnp.where(lo, 0.0, e) if half == 0 else jnp.where(lo, e, 0.0))
                acc.append(a)
            o_scr[:, cols] = jnp.where(lo, acc[0] / pltpu.roll(acc[0], HEAD_DIM, 1),
                                       acc[1] / pltpu.roll(acc[1], HEAD_DIM, 1))
        o = o_scr[...]
        raw_ref[...] = o.astype(BF16)
        r = _rsq(jnp.mean(o * o, axis=-1, keepdims=True))
        yn_ref[...] = (o * r * gg_ref[...]).astype(BF16)

    in_specs = bd.qkv_specs() + [_const_spec(bias.shape)]
    args = [proj] * 7 + [bias]
    if has_sink:
        in_specs.append(_const_spec(sink.shape))
        args.append(sink)
    in_specs.append(_const_spec(gg.shape))
    args.append(gg)
    out = jax.ShapeDtypeStruct((S, bd.qw), BF16)
    return pl.pallas_call(
        body, name=bd.kind + "_fwd", grid=(bd.nb,), in_specs=in_specs,
        out_specs=[_row_spec(bd.bq, bd.qw), _row_spec(bd.bq, bd.qw)], out_shape=[out, out],
        scratch_shapes=[pltpu.VMEM((bd.bq, bd.qw), F32)],
        compiler_params=_params("parallel"),
    )(*args)


def _band_bwd(bd, proj, bias, sink, dy):
    S = bd.S
    has_sink = sink is not None

    def body(*refs):
        q_ref, kp, km, kn, vp, vm, vn, bias_ref = refs[:8]
        k = 8
        sink_ref = None
        if has_sink:
            sink_ref = refs[k]
            k += 1
        do_ref = refs[k]
        dq_ref, dkm, dvm, dkp, dvp, dkn, dvn, dbias_ref = refs[k + 1:k + 9]
        k += 9
        dsink_ref = None
        if has_sink:
            dsink_ref = refs[k]
            k += 1
        dk_scr, dv_scr = refs[k:k + 2]
        n = pl.program_id(0)

        @pl.when(n == 0)
        def _():
            dbias_ref[...] = jnp.zeros_like(dbias_ref)
            if has_sink:
                dsink_ref[...] = jnp.zeros_like(dsink_ref)

        mask = bd.mask(n, transposed=True)
        lane = lax.broadcasted_iota(I32, (1, LANES), 1)
        kzs, vzs = _band_kv_variants(bd, (kp, km, kn), 0.0), _band_kv_variants(bd, (vp, vm, vn), 0.0)
        nblk = bd.kw // LANES
        dk, dv = [None] * nblk, [None] * nblk
        for pr in range(bd.hq // 2):
            cols = slice(pr * LANES, (pr + 1) * LANES)
            qp, dop = q_ref[:, cols] * QK_SCALE, do_ref[:, cols]
            qz, doz = _half_variants(qp), _half_variants(dop)
            dq = None
            for half in range(2):
                h = 2 * pr + half
                blk, dst = bd.kv_of(h)
                kz, vz = kzs[blk][dst][half], vzs[blk][dst][half]
                s = lax.dot_general(kz, qp, NT, preferred_element_type=F32) + bias_ref[h]
                s = jnp.where(mask, s, NEG_INF)
                m = jnp.max(s, axis=0, keepdims=True)
                if has_sink:
                    m = jnp.maximum(m, sink_ref[0:1, h:h + 1])
                p = jnp.exp(s - m)
                l = jnp.sum(p, axis=0, keepdims=True)
                if has_sink:
                    e = jnp.exp(sink_ref[0:1, h:h + 1] - m)
                    l = l + e
                inv = 1.0 / l
                pn = p * inv
                dp = lax.dot_general(vz, dop, NT, preferred_element_type=F32)
                delta = jnp.sum(pn * dp, axis=0, keepdims=True)
                ds = pn * (dp - delta)
                dbias_ref[h] += ds
                if has_sink:
                    dsink_ref[...] += jnp.where(lane == h, -jnp.sum(e * inv * delta, axis=1, keepdims=True), 0.0)
                dsb = ds.astype(BF16)
                a = jnp.dot(pn.astype(BF16), doz[half][dst], preferred_element_type=F32)
                b = jnp.dot(dsb, qz[half][dst], preferred_element_type=F32)
                d = lax.dot_general(dsb, kz, TN, preferred_element_type=F32)
                dv[blk] = a if dv[blk] is None else dv[blk] + a
                dk[blk] = b if dk[blk] is None else dk[blk] + b
                dq = d if dq is None else dq + d
            dq_ref[:, cols] = dq * QK_SCALE
        for blk in range(nblk):
            cols = slice(blk * LANES, (blk + 1) * LANES)
            dk_scr[:, cols] = dk[blk]
            dv_scr[:, cols] = dv[blk]
        h0, h1 = bd.halo, bd.halo + bd.bq
        dkp[0] = dk_scr[0:h0, :]
        dkm[...] = dk_scr[h0:h1, :]
        dkn[0] = dk_scr[h1:bd.bk, :]
        dvp[0] = dv_scr[0:h0, :]
        dvm[...] = dv_scr[h0:h1, :]
        dvn[0] = dv_scr[h1:bd.bk, :]

    in_specs = bd.qkv_specs() + [_const_spec(bias.shape)]
    args = [proj] * 7 + [bias]
    if has_sink:
        in_specs.append(_const_spec(sink.shape))
        args.append(sink)
    in_specs.append(_row_spec(bd.bq, bd.qw))
    args.append(dy)
    halo_spec = pl.BlockSpec((1, bd.halo, bd.kw), lambda n: (n, 0, 0))
    halo_shape = jax.ShapeDtypeStruct((bd.nb, bd.halo, bd.kw), F32)
    main_shape = jax.ShapeDtypeStruct((S, bd.kw), F32)
    out_specs = [_row_spec(bd.bq, bd.qw), _row_spec(bd.bq, bd.kw), _row_spec(bd.bq, bd.kw),
                 halo_spec, halo_spec, halo_spec, halo_spec, _const_spec(bias.shape)]
    out_shape = [jax.ShapeDtypeStruct((S, bd.qw), F32), main_shape, main_shape,
                 halo_shape, halo_shape, halo_shape, halo_shape, jax.ShapeDtypeStruct(bias.shape, F32)]
    if has_sink:
        out_specs.append(_const_spec((1, LANES)))
        out_shape.append(jax.ShapeDtypeStruct((1, LANES), F32))
    return pl.pallas_call(
        body, name=bd.kind + "_bwd", grid=(bd.nb,), in_specs=in_specs, out_specs=out_specs, out_shape=out_shape,
        scratch_shapes=[pltpu.VMEM((bd.bk, bd.kw), F32), pltpu.VMEM((bd.bk, bd.kw), F32)],
        compiler_params=_params("arbitrary"),
    )(*args)


def _halo_to_rows(prev, nxt):
    nb, halo, w = prev.shape
    z = jnp.zeros((1, halo, w), prev.dtype)
    first = jnp.concatenate([z, nxt[:-1]], axis=0)
    second = jnp.concatenate([prev[1:], z], axis=0)
    return jnp.concatenate([first, second], axis=1).reshape(nb * 2 * halo, w)


AX_PAIRS = AX_W // LANES


AX_FWD_BLOCKS = (1024, 2048)
AX_BWD_BLOCKS = (1024, 1024)


def _ax_blocks(S, blocks):
    return _tile(S, blocks[0]), _tile(S, blocks[1])


def _left_half(shape):
    return lax.broadcasted_iota(I32, shape, len(shape) - 1) < HEAD_DIM


def _as_row(a):
    return jnp.transpose(a)[0:1, :]


def _half_variants(a, fill=0.0):
    lo = _left_half(a.shape)
    other = jnp.full_like(a, fill)
    swapped = pltpu.roll(a, HEAD_DIM, 1)
    return ((jnp.where(lo, a, other), jnp.where(lo, other, swapped)),
            (jnp.where(lo, swapped, other), jnp.where(lo, other, a)))


def _split_rider_refs(refs, n_in, n_out, rider):
    r_in, r_out = (len(rider.arrays), len(rider.out_shapes)) if rider else (0, 0)
    a, b, c = n_in + r_in, n_in + r_in + n_out, n_in + r_in + n_out + r_out
    n_sems = 3 if rider else 0
    return refs[:n_in], refs[n_in:a], refs[a:b], refs[b:c], refs[c:len(refs) - n_sems], refs[len(refs) - n_sems:]


def _ax_fwd(qc, kc, proj, gg, rider=None):
    S = qc.shape[0]
    bq, bk = _ax_blocks(S, AX_FWD_BLOCKS)
    nq, nk = S // bq, S // bk
    rep = AX_HEADS // AX_KV_HEADS

    def body(*refs):
        (q_ref, k_ref, v_ref, gg_ref), r_ins, (raw_ref, yn_ref, lse_ref), r_outs, (m_scr, acc_scr), r_sems = (
            _split_rider_refs(refs, 4, 3, rider))
        qi, kv = pl.program_id(0), pl.program_id(1)

        if rider:
            @pl.when((qi == 0) & (kv == 0))
            def _():
                rider.start(r_ins, r_outs, r_sems)

        @pl.when(kv == 0)
        def _():
            m_scr[...] = jnp.full(m_scr.shape, NEG_INF, F32)
            acc_scr[...] = jnp.zeros_like(acc_scr)

        kz, vz = _half_variants(k_ref[...]), _half_variants(v_ref[...], 1.0)
        for pr in range(AX_PAIRS):
            qp = q_ref[:, pr * LANES:(pr + 1) * LANES]
            for half in range(2):
                h = 2 * pr + half
                g = h // rep
                s = lax.dot_general(qp, kz[g][half], NT, preferred_element_type=F32)
                m_prev = m_scr[h]
                m_new = jnp.maximum(m_prev, jnp.max(s, axis=-1, keepdims=True))
                p = jnp.exp(s - jnp.tile(m_new, (1, bk // LANES)))
                acc_scr[h] = jnp.exp(m_prev - m_new) * acc_scr[h] + jnp.dot(
                    p.astype(BF16), vz[g][half], preferred_element_type=F32)
                m_scr[h] = m_new

        @pl.when(kv == nk - 1)
        def _():
            lo = _left_half((bq, LANES))
            ssq = jnp.zeros((bq, 1), F32)
            for pr in range(AX_PAIRS):
                a0, a1 = acc_scr[2 * pr], acc_scr[2 * pr + 1]
                r0, r1 = pltpu.roll(a0, HEAD_DIM, 1), pltpu.roll(a1, HEAD_DIM, 1)
                lse_ref[2 * pr] = _as_row(m_scr[2 * pr] + jnp.log(jnp.where(lo, r0, a0)))
                lse_ref[2 * pr + 1] = _as_row(m_scr[2 * pr + 1] + jnp.log(jnp.where(lo, a1, r1)))
                o = jnp.where(lo, a0 / r0, a1 / r1)
                acc_scr[pr] = o
                ssq = ssq + jnp.sum(o * o, axis=-1, keepdims=True)
            r = _rsq(ssq * (1.0 / AX_W))
            for pr in range(AX_PAIRS):
                cols = slice(pr * LANES, (pr + 1) * LANES)
                o = acc_scr[pr]
                raw_ref[:, cols] = o.astype(BF16)
                yn_ref[:, cols] = (o * r * gg_ref[:, cols]).astype(BF16)

        if rider:
            @pl.when((qi == nq - 1) & (kv == nk - 1))
            def _():
                rider.wait(r_ins, r_outs, r_sems)

    out = jax.ShapeDtypeStruct((S, AX_W), BF16)
    hbm = pl.BlockSpec(memory_space=pl.ANY)
    r_arrays, r_shapes, r_scratch = (rider.arrays, rider.out_shapes, rider.scratch) if rider else ([], [], [])
    res = pl.pallas_call(
        body, name="ax_fwd_gather" if rider else "ax_fwd", grid=(nq, nk),
        in_specs=[pl.BlockSpec((bq, AX_W), lambda i, j: (i, 0)),
                  pl.BlockSpec((bk, AX_KV_W), lambda i, j: (j, 0)),
                  pl.BlockSpec((bk, AX_KV_W), lambda i, j: (j, OFF_VC // AX_KV_W)),
                  _const_spec(gg.shape)] + [hbm] * len(r_arrays),
        out_specs=[pl.BlockSpec((bq, AX_W), lambda i, j: (i, 0)),
                   pl.BlockSpec((bq, AX_W), lambda i, j: (i, 0)),
                   pl.BlockSpec((AX_HEADS, 1, bq), lambda i, j: (0, 0, i))] + [hbm] * len(r_shapes),
        out_shape=[out, out, jax.ShapeDtypeStruct((AX_HEADS, 1, S), F32)] + r_shapes,
        scratch_shapes=[pltpu.VMEM((AX_HEADS, bq, LANES), F32), pltpu.VMEM((AX_HEADS, bq, LANES), F32)] + r_scratch,
        compiler_params=_params("arbitrary", "arbitrary"),
    )(qc, kc, proj, gg, *r_arrays)
    return res[0], res[1], res[2], list(res[3:])


def _ax_delta(dy, raw):
    S = dy.shape[0]
    tm = _tile(S, ROW_TILE)

    def body(do_ref, o_ref, delta_ref):
        lo = _left_half((tm, LANES))
        for pr in range(AX_PAIRS):
            cols = slice(pr * LANES, (pr + 1) * LANES)
            prod = do_ref[:, cols].astype(F32) * o_ref[:, cols].astype(F32)
            left = jnp.sum(jnp.where(lo, prod, 0.0), axis=-1, keepdims=True)
            right = jnp.sum(jnp.where(lo, 0.0, prod), axis=-1, keepdims=True)
            delta_ref[2 * pr] = _as_row(jnp.broadcast_to(left, (tm, LANES)))
            delta_ref[2 * pr + 1] = _as_row(jnp.broadcast_to(right, (tm, LANES)))

    return pl.pallas_call(
        body, name="ax_delta", grid=(S // tm,), in_specs=[_row_spec(tm, AX_W), _row_spec(tm, AX_W)],
        out_specs=pl.BlockSpec((AX_HEADS, 1, tm), lambda i: (0, 0, i)),
        out_shape=jax.ShapeDtypeStruct((AX_HEADS, 1, S), F32), compiler_params=_params("parallel"),
    )(dy, raw)


def _ax_bwd(qc, kc, proj, dy, lse_row, delta_row, rider=None):
    S = qc.shape[0]
    bq, bk = _ax_blocks(S, AX_BWD_BLOCKS)
    nq, nk = S // bq, S // bk
    rep = AX_HEADS // AX_KV_HEADS

    def body(*refs):
        ((q_ref, k_ref, v_ref, do_ref, lse_ref, delta_ref), r_ins, (dk_ref, dv_ref, dq_hbm), r_outs, (dq_scr, sem),
         r_sems) = _split_rider_refs(refs, 6, 3, rider)
        j, i = pl.program_id(0), pl.program_id(1)

        if rider:
            @pl.when((j == 0) & (i == 0))
            def _():
                rider.start(r_ins, r_outs, r_sems)

        @pl.when(i == 0)
        def _():
            dk_ref[...] = jnp.zeros_like(dk_ref)
            dv_ref[...] = jnp.zeros_like(dv_ref)

        @pl.when(j == 0)
        def _():
            dq_scr[i] = jnp.zeros((bq, AX_W), F32)

        kz, vz = _half_variants(k_ref[...]), _half_variants(v_ref[...])
        dk, dv = None, None
        for pr in range(AX_PAIRS):
            cols = slice(pr * LANES, (pr + 1) * LANES)
            qp, dop = q_ref[:, cols], do_ref[:, cols]
            qz, doz = _half_variants(qp), _half_variants(dop)
            dq = None
            for half in range(2):
                h = 2 * pr + half
                g = h // rep
                s_t = lax.dot_general(kz[g][half], qp, NT, preferred_element_type=F32)
                p_t = jnp.exp(s_t - lse_ref[h])
                dp_t = lax.dot_general(vz[g][half], dop, NT, preferred_element_type=F32)
                ds_t = (p_t * (dp_t - delta_ref[h])).astype(BF16)
                a = jnp.dot(p_t.astype(BF16), doz[half][g], preferred_element_type=F32)
                b = jnp.dot(ds_t, qz[half][g], preferred_element_type=F32)
                d = lax.dot_general(ds_t, kz[g][half], TN, preferred_element_type=F32)
                dv = a if dv is None else dv + a
                dk = b if dk is None else dk + b
                dq = d if dq is None else dq + d
            dq_scr[i, :, cols] += dq
        dv_ref[...] += dv
        dk_ref[...] += dk

        @pl.when(j == nk - 1)
        def _():
            dq_scr[i] = dq_scr[i] * QK_SCALE
            out = pltpu.make_async_copy(dq_scr.at[i], dq_hbm.at[pl.ds(pl.multiple_of(i * bq, bq), bq), :], sem)
            out.start()
            out.wait()

        if rider:
            @pl.when((j == nk - 1) & (i == nq - 1))
            def _():
                rider.wait(r_ins, r_outs, r_sems)

    qspec = pl.BlockSpec((bq, AX_W), lambda j, i: (i, 0))
    kspec = pl.BlockSpec((bk, AX_KV_W), lambda j, i: (j, 0))
    stat = pl.BlockSpec((AX_HEADS, 1, bq), lambda j, i: (0, 0, i))
    out = jax.ShapeDtypeStruct((S, AX_KV_W), F32)
    hbm = pl.BlockSpec(memory_space=pl.ANY)
    r_arrays, r_shapes, r_scratch = (rider.arrays, rider.out_shapes, rider.scratch) if rider else ([], [], [])
    res = pl.pallas_call(
        body, name="ax_bwd_scatter" if rider else "ax_bwd", grid=(nk, nq),
        in_specs=[qspec, kspec, pl.BlockSpec((bk, AX_KV_W), lambda j, i: (j, OFF_VC // AX_KV_W)),
                  qspec, stat, stat] + [hbm] * len(r_arrays),
        out_specs=[kspec, kspec, hbm] + [hbm] * len(r_shapes),
        out_shape=[out, out, jax.ShapeDtypeStruct((S, AX_W), F32)] + r_shapes,
        scratch_shapes=[pltpu.VMEM((nq, bq, AX_W), F32), pltpu.SemaphoreType.DMA] + r_scratch,
        compiler_params=_params("arbitrary", "arbitrary"),
    )(qc, kc, proj, dy, lse_row, delta_row, *r_arrays)
    return res[2], res[0], res[1], list(res[3:])


def _oproj_fwd(x, yna, ynb, ync, w, gt):
    S, D = x.shape
    tm = _tile(S, ROW_TILE)

    def body(x_ref, a_ref, b_ref, c_ref, w_ref, gt_ref, x1_ref, ao_ref, yn_ref):
        yn_ref[:, 0:NA_W] = a_ref[...]
        yn_ref[:, NA_W:NA_W + SW_W] = b_ref[...]
        yn_ref[:, NA_W + SW_W:MIX_WIDTH] = c_ref[...]
        acc = jnp.dot(yn_ref[...], w_ref[...], preferred_element_type=F32)
        ao_ref[...] = acc.astype(BF16)
        x1_ref[...] = x_ref[...] + gt_ref[...] * acc

    return pl.pallas_call(
        body, name="oproj_fwd", grid=(S // tm,),
        in_specs=[_row_spec(tm, D), _row_spec(tm, NA_W), _row_spec(tm, SW_W), _row_spec(tm, AX_W),
                  _const_spec(w.shape), _const_spec((1, D))],
        out_specs=[_row_spec(tm, D), _row_spec(tm, D), _row_spec(tm, MIX_WIDTH)],
        out_shape=[jax.ShapeDtypeStruct((S, D), F32), jax.ShapeDtypeStruct((S, D), BF16),
                   jax.ShapeDtypeStruct((S, MIX_WIDTH), BF16)],
        compiler_params=_params("parallel"),
    )(x, yna, ynb, ync, w, gt)


def _gu_fwd(x, g, sc, sh, w):
    S, D = x.shape
    tn = w.shape[2] // 2
    F2 = 4 * tn
    tm = _tile(S, ROW_TILE)

    def body(x_ref, g_ref, sc_ref, sh_ref, w_ref, h_ref, gu_ref, act_ref):
        @pl.when(pl.program_id(1) == 0)
        def _():
            h_ref[...] = _ln_mod(x_ref[...], g_ref[...], sc_ref[...], sh_ref[...]).astype(BF16)

        acc = jnp.dot(h_ref[...], w_ref[pl.program_id(1)], preferred_element_type=F32)
        gu_ref[...] = acc.astype(BF16)
        gate, up = acc[:, :tn], acc[:, tn:]
        act_ref[...] = (gate * (1.0 / (1.0 + jnp.exp(-gate))) * up).astype(BF16)

    vec = pl.BlockSpec((1, D), lambda i, j: (0, 0))
    return pl.pallas_call(
        body, name="gu_fwd", grid=(S // tm, 2),
        in_specs=[pl.BlockSpec((tm, D), lambda i, j: (i, 0)), vec, vec, vec,
                  pl.BlockSpec((2, D, 2 * tn), lambda i, j: (0, 0, 0))],
        out_specs=[pl.BlockSpec((tm, D), lambda i, j: (i, 0)), pl.BlockSpec((tm, 2 * tn), lambda i, j: (i, j)),
                   pl.BlockSpec((tm, tn), lambda i, j: (i, j))],
        out_shape=[jax.ShapeDtypeStruct((S, D), BF16), jax.ShapeDtypeStruct((S, F2), BF16),
                   jax.ShapeDtypeStruct((S, F2 // 2), BF16)],
        compiler_params=_params("parallel", "arbitrary"),
    )(x, g, sc, sh, w)


def _down_fwd(x, act, w, gt):
    S, D = x.shape
    F = act.shape[1]
    tm = _tile(S, ROW_TILE)

    def body(x_ref, a_ref, w_ref, gt_ref, x2_ref, fo_ref):
        acc = jnp.dot(a_ref[...], w_ref[...], preferred_element_type=F32)
        fo_ref[...] = acc.astype(BF16)
        x2_ref[...] = x_ref[...] + gt_ref[...] * acc

    return pl.pallas_call(
        body, name="down_fwd", grid=(S // tm,),
        in_specs=[_row_spec(tm, D), _row_spec(tm, F), _const_spec(w.shape), _const_spec((1, D))],
        out_specs=[_row_spec(tm, D), _row_spec(tm, D)],
        out_shape=[jax.ShapeDtypeStruct((S, D), F32), jax.ShapeDtypeStruct((S, D), BF16)],
        compiler_params=_params("parallel"),
    )(x, act, w, gt)


def _final_loss(x, g, target):
    S, D = x.shape
    tm = _tile(S, ROW_TILE)

    def body(x_ref, g_ref, t_ref, dx_ref, loss_ref, dg_ref):
        @pl.when(pl.program_id(0) == 0)
        def _():
            loss_ref[...] = jnp.zeros_like(loss_ref)
            dg_ref[...] = jnp.zeros_like(dg_ref)

        xv = x_ref[...]
        r = _rsq(jnp.mean(xv * xv, axis=-1, keepdims=True))
        xhat = xv * r
        err = xhat * g_ref[...] - t_ref[...]
        loss_ref[...] += 0.5 * jnp.sum(jnp.mean(err * err, axis=-1, keepdims=True), axis=0, keepdims=True)
        dy = err * (1.0 / D)
        dg_ref[...] += jnp.sum(dy * xhat, axis=0, keepdims=True)
        dxh = dy * g_ref[...]
        dx_ref[...] = r * (dxh - xhat * jnp.mean(dxh * xhat, axis=-1, keepdims=True))

    return pl.pallas_call(
        body, name="final_loss", grid=(S // tm,),
        in_specs=[_row_spec(tm, D), _const_spec((1, D)), _row_spec(tm, D)],
        out_specs=[_row_spec(tm, D), _const_spec((1, LANES)), _const_spec((1, D))],
        out_shape=[jax.ShapeDtypeStruct((S, D), F32), jax.ShapeDtypeStruct((1, LANES), F32),
                   jax.ShapeDtypeStruct((1, D), F32)],
        compiler_params=_params("arbitrary"),
    )(x, g, target)


def _ffn_bwd1(dx2, fo, gt, w_down, gu):
    S, D = dx2.shape
    F2 = gu.shape[1]
    tn = F2 // 4
    tm = _tile(S, ROW_TILE)

    def body(dx_ref, fo_ref, gt_ref, w_ref, gu_ref, dfo_ref, dgu_ref, dgt_ref):
        i, j = pl.program_id(0), pl.program_id(1)

        @pl.when((i == 0) & (j == 0))
        def _():
            dgt_ref[...] = jnp.zeros_like(dgt_ref)

        @pl.when(j == 0)
        def _():
            dxv = dx_ref[...]
            dfo_ref[...] = (dxv * gt_ref[...]).astype(BF16)
            dgt_ref[...] += jnp.sum(dxv * fo_ref[...].astype(F32), axis=0, keepdims=True)

        dact = lax.dot_general(dfo_ref[...], w_ref[j], NT, preferred_element_type=F32)
        gate = gu_ref[:, :tn].astype(F32)
        up = gu_ref[:, tn:].astype(F32)
        sig = 1.0 / (1.0 + jnp.exp(-gate))
        dgu_ref[:, :tn] = (dact * up * (sig * (1.0 + gate * (1.0 - sig)))).astype(BF16)
        dgu_ref[:, tn:] = (dact * (gate * sig)).astype(BF16)

    vec = pl.BlockSpec((1, D), lambda i, j: (0, 0))
    row = pl.BlockSpec((tm, D), lambda i, j: (i, 0))
    return pl.pallas_call(
        body, name="ffn_bwd1", grid=(S // tm, 2),
        in_specs=[row, row, vec, pl.BlockSpec((2, tn, D), lambda i, j: (0, 0, 0)),
                  pl.BlockSpec((tm, 2 * tn), lambda i, j: (i, j))],
        out_specs=[row, pl.BlockSpec((tm, 2 * tn), lambda i, j: (i, j)), vec],
        out_shape=[jax.ShapeDtypeStruct((S, D), BF16), jax.ShapeDtypeStruct((S, F2), BF16),
                   jax.ShapeDtypeStruct((1, D), F32)],
        compiler_params=_params("arbitrary", "arbitrary"),
    )(dx2, fo, gt, w_down.reshape(2, tn, D), gu)


def _nt_ln_bwd(a, w, x, g, sc, dres, name):
    S, D = x.shape
    K = a.shape[1]
    tm = _tile(S, ROW_TILE_WIDE)

    def body(a_ref, w_ref, x_ref, g_ref, sc_ref, dres_ref, dx_ref, dsh_ref, dsc_ref, dg_ref):
        @pl.when(pl.program_id(0) == 0)
        def _():
            dsh_ref[...] = jnp.zeros_like(dsh_ref)
            dsc_ref[...] = jnp.zeros_like(dsc_ref)
            dg_ref[...] = jnp.zeros_like(dg_ref)

        if len(w.shape) == 2:
            dh = lax.dot_general(a_ref[...], w_ref[...], NT, preferred_element_type=F32)
        else:
            kt = w.shape[2]
            dh = sum(lax.dot_general(a_ref[:, t * kt:(t + 1) * kt], w_ref[t], NT, preferred_element_type=F32)
                     for t in range(w.shape[0]))
        xv = x_ref[...]
        r = _rsq(jnp.mean(xv * xv, axis=-1, keepdims=True))
        xhat = xv * r
        gv = g_ref[...]
        dsh_ref[...] += jnp.sum(dh, axis=0, keepdims=True)
        dsc_ref[...] += jnp.sum(dh * (xhat * gv), axis=0, keepdims=True)
        dn = dh * (1.0 + sc_ref[...])
        dg_ref[...] += jnp.sum(dn * xhat, axis=0, keepdims=True)
        dxh = dn * gv
        dx_ref[...] = dres_ref[...] + r * (dxh - xhat * jnp.mean(dxh * xhat, axis=-1, keepdims=True))

    vec = _const_spec((1, D))
    vshape = jax.ShapeDtypeStruct((1, D), F32)
    return pl.pallas_call(
        body, name=name, grid=(S // tm,),
        in_specs=[_row_spec(tm, K), _const_spec(w.shape), _row_spec(tm, D), vec, vec, _row_spec(tm, D)],
        out_specs=[_row_spec(tm, D), vec, vec, vec],
        out_shape=[jax.ShapeDtypeStruct((S, D), F32), vshape, vshape, vshape],
        compiler_params=_params("arbitrary"),
    )(a, w, x, g, sc, dres)


def _oproj_bwd(dx1, ao, gt, w, ya, yb, yc, gg):
    S, D = dx1.shape
    tm = _tile(S, ROW_TILE)
    groups = ((0, NA_W), (NA_W, SW_W), (NA_W + SW_W, AX_W))

    def body(dx_ref, ao_ref, gt_ref, w_ref, ya_ref, yb_ref, yc_ref, gg_ref,
             dao_ref, dya_ref, dyb_ref, dyc_ref, dgt_ref, dgg_ref):
        @pl.when(pl.program_id(0) == 0)
        def _():
            dgt_ref[...] = jnp.zeros_like(dgt_ref)
            dgg_ref[...] = jnp.zeros_like(dgg_ref)

        dxv = dx_ref[...]
        dao = (dxv * gt_ref[...]).astype(BF16)
        dao_ref[...] = dao
        dgt_ref[...] += jnp.sum(dxv * ao_ref[...].astype(F32), axis=0, keepdims=True)
        dyn = lax.dot_general(dao, w_ref[...], NT, preferred_element_type=F32)
        for (off, wd), y_ref, dy_ref in zip(groups, (ya_ref, yb_ref, yc_ref), (dya_ref, dyb_ref, dyc_ref)):
            y = y_ref[...].astype(F32)
            d = dyn[:, off:off + wd]
            r = _rsq(jnp.mean(y * y, axis=-1, keepdims=True))
            yhat = y * r
            dgg_ref[:, off:off + wd] += jnp.sum(d * yhat, axis=0, keepdims=True)
            dyh = d * gg_ref[:, off:off + wd]
            dy_ref[...] = (r * (dyh - yhat * jnp.mean(dyh * yhat, axis=-1, keepdims=True))).astype(BF16)

    vec = _const_spec((1, D))
    mvec = _const_spec((1, MIX_WIDTH))
    return pl.pallas_call(
        body, name="oproj_bwd", grid=(S // tm,),
        in_specs=[_row_spec(tm, D), _row_spec(tm, D), vec, _const_spec(w.shape),
                  _row_spec(tm, NA_W), _row_spec(tm, SW_W), _row_spec(tm, AX_W), mvec],
        out_specs=[_row_spec(tm, D), _row_spec(tm, NA_W), _row_spec(tm, SW_W), _row_spec(tm, AX_W), vec, mvec],
        out_shape=[jax.ShapeDtypeStruct((S, D), BF16), jax.ShapeDtypeStruct((S, NA_W), BF16),
                   jax.ShapeDtypeStruct((S, SW_W), BF16), jax.ShapeDtypeStruct((S, AX_W), BF16),
                   jax.ShapeDtypeStruct((1, D), F32), jax.ShapeDtypeStruct((1, MIX_WIDTH), F32)],
        compiler_params=_params("arbitrary"),
    )(dx1, ao, gt, w, ya, yb, yc, gg)


def _dproj_assemble(proj, na, sw, ax, gq128, gk128, rope):
    S = proj.shape[0]
    tm = _tile(S, ROW_TILE)
    cos, sa, sb = rope

    def body(proj_ref, qa, ka, kah, va, vah, qb, kb, kbh, vb, vbh, qc, kc, vc,
             gq_ref, gk_ref, cos_ref, sa_ref, sb_ref, out_ref, dgq_ref, dgk_ref):
        @pl.when(pl.program_id(0) == 0)
        def _():
            dgq_ref[...] = jnp.zeros_like(dgq_ref)
            dgk_ref[...] = jnp.zeros_like(dgk_ref)

        out_ref[:, OFF_QA:OFF_KA] = qa[...].astype(BF16)
        out_ref[:, OFF_KA:OFF_VA] = (ka[...] + kah[...]).astype(BF16)
        out_ref[:, OFF_VA:OFF_QB] = (va[...] + vah[...]).astype(BF16)
        out_ref[:, OFF_QB:OFF_KB] = qb[...].astype(BF16)
        out_ref[:, OFF_KB:OFF_VB] = (kb[...] + kbh[...]).astype(BF16)
        out_ref[:, OFF_VB:OFF_QC] = (vb[...] + vbh[...]).astype(BF16)
        c, a, b = cos_ref[...], sa_ref[...], sb_ref[...]
        for j in range(AX_W // LANES):
            cols = slice(OFF_QC + j * LANES, OFF_QC + (j + 1) * LANES)
            dx, dg = _qk_prep_bwd_chunk(proj_ref[:, cols].astype(F32), qc[:, j * LANES:(j + 1) * LANES],
                                        gq_ref[...], c, a, b)
            out_ref[:, cols] = dx.astype(BF16)
            dgq_ref[...] += dg
        for j in range(AX_KV_W // LANES):
            cols = slice(OFF_KC + j * LANES, OFF_KC + (j + 1) * LANES)
            dx, dg = _qk_prep_bwd_chunk(proj_ref[:, cols].astype(F32), kc[:, j * LANES:(j + 1) * LANES],
                                        gk_ref[...], c, a, b)
            out_ref[:, cols] = dx.astype(BF16)
            dgk_ref[...] += dg
        out_ref[:, OFF_VC:IN_WIDTH] = vc[...].astype(BF16)

    v128 = _const_spec((1, LANES))
    r = lambda w: _row_spec(tm, w)
    return pl.pallas_call(
        body, name="dproj_assemble", grid=(S // tm,),
        in_specs=[r(IN_WIDTH), r(NA_W), r(NA_W), r(NA_W), r(NA_W), r(NA_W),
                  r(SW_W), r(SW_KV_W), r(SW_KV_W), r(SW_KV_W), r(SW_KV_W),
                  r(AX_W), r(AX_KV_W), r(AX_KV_W), v128, v128, r(LANES), r(LANES), r(LANES)],
        out_specs=[r(IN_WIDTH), v128, v128],
        out_shape=[jax.ShapeDtypeStruct((S, IN_WIDTH), BF16), jax.ShapeDtypeStruct((1, LANES), F32),
                   jax.ShapeDtypeStruct((1, LANES), F32)],
        compiler_params=_params("arbitrary"),
    )(proj, *na, *sw, *ax, gq128, gk128, cos, sa, sb)


def _tn_matmul(a, b, name):
    S, Ka = a.shape
    Nb = b.shape[1]
    tm = _tile(Ka, 1408, LANES)
    tn = _tile(Nb, 1408, LANES)
    tk = _tile(S, TOKEN_CHUNK)
    nk = S // tk

    def body(a_ref, b_ref, o_ref, acc_ref):
        k = pl.program_id(2)

        @pl.when(k == 0)
        def _():
            acc_ref[...] = jnp.zeros_like(acc_ref)

        acc_ref[...] += lax.dot_general(a_ref[...], b_ref[...], TN, preferred_element_type=F32)

        @pl.when(k == nk - 1)
        def _():
            o_ref[...] = acc_ref[...].astype(BF16)

    return pl.pallas_call(
        body, name=name, grid=(Ka // tm, Nb // tn, nk),
        in_specs=[pl.BlockSpec((tk, tm), lambda i, j, k: (k, i)), pl.BlockSpec((tk, tn), lambda i, j, k: (k, j))],
        out_specs=pl.BlockSpec((tm, tn), lambda i, j, k: (i, j)),
        out_shape=jax.ShapeDtypeStruct((Ka, Nb), BF16),
        scratch_shapes=[pltpu.VMEM((tm, tn), F32)],
        compiler_params=_params("parallel", "parallel", "arbitrary"),
    )(a, b)


def _na_index(bd):
    rq = jnp.arange(bd.bq // GRID_W)
    rk = jnp.arange(bd.bk // GRID_W)
    col = jnp.arange(GRID_W)
    ri = jnp.clip(rk[None, :] - rq[:, None] - bd.halo // GRID_W + NA_WIN_ROWS - 1, 0, 2 * NA_WIN_ROWS - 2)
    ci = jnp.clip(col[None, :] - col[:, None] + NA_WIN_COLS - 1, 0, 2 * NA_WIN_COLS - 2)
    return ri, ci


def _na_one_hots(bd):
    ri, ci = _na_index(bd)
    oh_r = jax.nn.one_hot(ri, 2 * NA_WIN_ROWS - 1, dtype=F32)
    oh_c = jax.nn.one_hot(ci, 2 * NA_WIN_COLS - 1, dtype=F32)
    return oh_r, oh_c


def _na_bias(bd, rpb):
    oh_r, oh_c = _na_one_hots(bd)
    t = jnp.einsum("hab,qra->hqrb", rpb, oh_r, precision=lax.Precision.HIGHEST)
    b = jnp.einsum("hqrb,ckb->hqcrk", t, oh_c, precision=lax.Precision.HIGHEST)
    return b.reshape(NA_HEADS, bd.bq, bd.bk)


def _na_bias_t(bd, dbias):
    oh_r, oh_c = _na_one_hots(bd)
    d5 = dbias.reshape(NA_HEADS, bd.bq // GRID_W, GRID_W, bd.bk // GRID_W, GRID_W)
    t = jnp.einsum("hqcrk,ckb->hqrb", d5, oh_c, precision=lax.Precision.HIGHEST)
    return jnp.einsum("hqrb,qra->hab", t, oh_r, precision=lax.Precision.HIGHEST)


def _t5_bucket(rel):
    nb = T5_BUCKETS // 2
    ret = (rel > 0).astype(I32) * nb
    n = jnp.abs(rel)
    max_exact = nb // 2
    nf = jnp.maximum(n, max_exact).astype(F32)
    large = max_exact + (jnp.log(nf / max_exact) / math.log(T5_MAX_DIST / max_exact)
                         * (nb - max_exact)).astype(I32)
    large = jnp.minimum(large, nb - 1)
    return ret + jnp.where(n < max_exact, n, large)


def _sw_bucket(bd):
    rel = (jnp.arange(bd.bk) - bd.halo)[None, :] - jnp.arange(bd.bq)[:, None]
    return _t5_bucket(rel)


def _sw_bias(bd, t5):
    oh = jax.nn.one_hot(_sw_bucket(bd), T5_BUCKETS, dtype=F32)
    return jnp.einsum("bh,qkb->hqk", t5, oh, precision=lax.Precision.HIGHEST)


def _sw_bias_t(bd, dbias):
    oh = jax.nn.one_hot(_sw_bucket(bd), T5_BUCKETS, dtype=F32)
    return jnp.einsum("hqk,qkb->bh", dbias, oh, precision=lax.Precision.HIGHEST)


def _local_step(x, target, mod, w_in, w_o, w_gu, w_down, g_attn, rpb_na, sink_sw, t5_table, gq_ax, gk_ax,
                g_group, g_ffn, g_final, late_shards=None):
    S, D = x.shape
    riding = late_shards is not None
    w_in = [w_in[l] for l in range(w_in.shape[0])]
    rope = _rope_tables(S)
    na, sw = _Band("na", S), _Band("sw", S)
    two = lambda v: jnp.concatenate([v, v])[None, :]
    sw_bias = _sw_bias(sw, t5_table)
    saved = []
    for l in range(DEPTH):
        sh_a, sc_a, gt_a, sh_f, sc_f, gt_f = [mod[l, k * D:(k + 1) * D][None, :] for k in range(6)]
        gq128, gk128 = two(gq_ax[l]), two(gk_ax[l])
        gg = g_group[l][None, :]
        sink = jnp.pad(sink_sw[l], (0, LANES - SW_HEADS))[None, :]
        na_bias = _na_bias(na, rpb_na[l])
        h, proj, qc, kc = _inproj_fwd(x, g_attn[l][None, :], sc_a, sh_a, w_in[l], gq128, gk128, rope)
        ya, yna = _band_fwd(na, proj, na_bias, None, gg[:, :NA_W])
        yb, ynb = _band_fwd(sw, proj, sw_bias, sink, gg[:, NA_W:NA_W + SW_W])
        rider = None
        if riding and l == 0:
            rider = _gather_exchange([late_shards[k] for k in BIG], [BIG_AXIS[k] for k in BIG],
                                     [BIG_ORDER[k] for k in BIG])
        yc, ync, lse, got = _ax_fwd(qc, kc, proj, gg[:, NA_W + SW_W:], rider)
        if rider:
            w_in_late, w_o, w_gu, w_down = got
            w_in += [w_in_late[k] for k in range(w_in_late.shape[0])]
        x1, ao, yn = _oproj_fwd(x, yna, ynb, ync, w_o[l], gt_a)
        hf, gu, act = _gu_fwd(x1, g_ffn[l][None, :], sc_f, sh_f, w_gu[l])
        x2, fo = _down_fwd(x1, act, w_down[l], gt_f)
        saved.append(dict(x=x, x1=x1, h=h, proj=proj, qc=qc, kc=kc, ya=ya, yb=yb, yc=yc, lse=lse, ao=ao, yn=yn,
                          hf=hf, gu=gu, act=act, fo=fo, na_bias=na_bias, sink=sink, gq128=gq128, gk128=gk128,
                          gg=gg, mods=(sh_a, sc_a, gt_a, sh_f, sc_f, gt_f)))
        x = x2

    dx, loss_row, dg_final = _final_loss(x, g_final[None, :], target)
    gw = {k: [None] * DEPTH for k in ("w_in", "w_o", "w_gu", "w_down")}
    gs = {k: [None] * DEPTH for k in ("b_mod", "g_attn", "rpb_na", "sink_sw", "gq_ax", "gk_ax", "g_group", "g_ffn")}
    d_t5 = jnp.zeros((T5_BUCKETS, SW_HEADS), F32)
    for l in reversed(range(DEPTH)):
        s = saved[l]
        sh_a, sc_a, gt_a, sh_f, sc_f, gt_f = s["mods"]
        dfo, dgu, dgt_f = _ffn_bwd1(dx, s["fo"], gt_f, w_down[l], s["gu"])
        gw["w_down"][l] = _tn_matmul(s["act"], dfo, "dw_down")
        gw["w_gu"][l] = _tn_matmul(s["hf"], dgu, "dw_gu")
        dx1, dsh_f, dsc_f, gs["g_ffn"][l] = _nt_ln_bwd(dgu, w_gu[l], s["x1"], g_ffn[l][None, :], sc_f, dx, "ffn_bwd2")
        dao, dya, dyb, dyc, dgt_a, gs["g_group"][l] = _oproj_bwd(dx1, s["ao"], gt_a, w_o[l], s["ya"], s["yb"],
                                                                 s["yc"], s["gg"])
        gw["w_o"][l] = _tn_matmul(s["yn"], dao, "dw_o")
        dqa, dka, dva, dkap, dvap, dkan, dvan, dbias_na = _band_bwd(na, s["proj"], jnp.swapaxes(s["na_bias"], 1, 2), None, dya)
        dqb, dkb, dvb, dkbp, dvbp, dkbn, dvbn, dbias_sw, dsink = _band_bwd(sw, s["proj"], jnp.swapaxes(sw_bias, 1, 2), s["sink"], dyb)
        rider = None
        if riding and l == 0:
            ready = [("w_in", k) for k in range(1, DEPTH)] + [(n, k) for n in BIG[1:] for k in range(DEPTH)]
            rider = _scatter_exchange(
                [gw[n][k] for n, k in ready], [BIG_AXIS[n] - 1 for n, _ in ready], [BIG_ORDER[n] for n, _ in ready],
                [None if n == "w_in" else (BIG.index(n), k, DEPTH) for n, k in ready])
        dqc, dkc, dvc, sent = _ax_bwd(s["qc"], s["kc"], s["proj"], dyc, s["lse"], _ax_delta(dyc, s["yc"]), rider)
        dproj, dgq, dgk = _dproj_assemble(
            s["proj"], (dqa, dka, _halo_to_rows(dkap, dkan), dva, _halo_to_rows(dvap, dvan)),
            (dqb, dkb, _halo_to_rows(dkbp, dkbn), dvb, _halo_to_rows(dvbp, dvbn)), (dqc, dkc, dvc),
            s["gq128"], s["gk128"], rope)
        gw["w_in"][l] = _tn_matmul(s["h"], dproj, "dw_in")
        dx, dsh_a, dsc_a, gs["g_attn"][l] = _nt_ln_bwd(dproj, w_in[l], s["x"], g_attn[l][None, :], sc_a, dx1,
                                                       "inproj_bwd")
        gs["b_mod"][l] = jnp.concatenate([dsh_a, dsc_a, dgt_a, dsh_f, dsc_f, dgt_f], axis=1)[0]
        gs["rpb_na"][l] = _na_bias_t(na, jnp.swapaxes(dbias_na, 1, 2))
        gs["sink_sw"][l] = dsink[0, :SW_HEADS]
        d_t5 = d_t5 + _sw_bias_t(sw, jnp.swapaxes(dbias_sw, 1, 2))
        gs["gq_ax"][l] = dgq[0, :HEAD_DIM] + dgq[0, HEAD_DIM:]
        gs["gk_ax"][l] = dgk[0, :HEAD_DIM] + dgk[0, HEAD_DIM:]
        gs["g_attn"][l] = gs["g_attn"][l][0]
        gs["g_ffn"][l] = gs["g_ffn"][l][0]
        gs["g_group"][l] = gs["g_group"][l][0]

    if riding:
        (first,) = _chip_scatter([gw["w_in"][0]], [BIG_AXIS["w_in"] - 1], [BIG_ORDER["w_in"]], "scatter_w_in0")
        gw = dict(zip(BIG[1:], sent[DEPTH - 1:]), w_in=jnp.stack([first] + sent[:DEPTH - 1], axis=1))
    else:
        gw = {k: jnp.stack(v) for k, v in gw.items()}
    small = {k: jnp.stack(v) for k, v in gs.items()}
    small["t5_table"] = d_t5
    small["g_final"] = dg_final[0]
    return loss_row[0, 0], dx, gw, small


MOD_ROWS = 16


def _mod_fwd(cond16, w):
    L, D, C = w.shape
    tn = _tile(C, 512, LANES)

    def body(c_ref, w_ref, o_ref):
        o_ref[0] = jnp.dot(c_ref[...], w_ref[0].astype(BF16), preferred_element_type=F32)

    return pl.pallas_call(
        body, name="mod_fwd", grid=(L, C // tn),
        in_specs=[pl.BlockSpec((MOD_ROWS, D), lambda l, j: (0, 0)), pl.BlockSpec((1, D, tn), lambda l, j: (l, 0, j))],
        out_specs=pl.BlockSpec((1, MOD_ROWS, tn), lambda l, j: (l, 0, j)),
        out_shape=jax.ShapeDtypeStruct((L, MOD_ROWS, C), F32),
        compiler_params=_params("parallel", "parallel"),
    )(cond16, w)


def _adamw_math(w, g, m, v):
    m = ADAM_B1 * m + (1.0 - ADAM_B1) * g
    v = ADAM_B2 * v + (1.0 - ADAM_B2) * (g * g)
    m_hat = m / (1.0 - ADAM_B1 ** ADAM_STEP)
    v_hat = v / (1.0 - ADAM_B2 ** ADAM_STEP)
    delta = -ADAM_LR * (m_hat / (jnp.sqrt(v_hat) + ADAM_EPS) + ADAM_WD * w)
    return delta, m, v


def _adamw(w, m, v, parts, name):
    R, C = w.shape
    tr = _tile(R, 256)
    n = len(parts)

    def body(*refs):
        w_ref, m_ref, v_ref = refs[:3]
        g = refs[3][...]
        for p in refs[4:3 + n]:
            g = g + p[...]
        g_ref, d_ref, m2_ref, v2_ref = refs[3 + n:]
        g_ref[...] = g
        d_ref[...], m2_ref[...], v2_ref[...] = _adamw_math(w_ref[...], g, m_ref[...], v_ref[...])

    spec = _row_spec(tr, C)
    shape = jax.ShapeDtypeStruct((R, C), F32)
    return pl.pallas_call(
        body, name=name, grid=(R // tr,), in_specs=[spec] * (3 + n), out_specs=[spec] * 4, out_shape=[shape] * 4,
        compiler_params=_params("parallel"),
    )(w, m, v, *parts)


def _wmod_adamw(cond_t, dmod16, w, m, v):
    L, D, C = w.shape
    tr = _tile(D, 256)

    def body(c_ref, d_ref, w_ref, m_ref, v_ref, g_ref, dl_ref, m2_ref, v2_ref):
        g = jnp.dot(c_ref[...], d_ref[0], preferred_element_type=F32)
        g_ref[0] = g
        dl_ref[0], m2_ref[0], v2_ref[0] = _adamw_math(w_ref[0], g, m_ref[0], v_ref[0])

    spec = pl.BlockSpec((1, tr, C), lambda l, i: (l, i, 0))
    shape = jax.ShapeDtypeStruct((L, D, C), F32)
    return pl.pallas_call(
        body, name="wmod_adamw", grid=(L, D // tr),
        in_specs=[pl.BlockSpec((tr, MOD_ROWS), lambda l, i: (i, 0)),
                  pl.BlockSpec((1, MOD_ROWS, C), lambda l, i: (l, 0, 0)), spec, spec, spec],
        out_specs=[spec] * 4, out_shape=[shape] * 4,
        compiler_params=_params("parallel", "parallel"),
    )(cond_t, dmod16, w, m, v)


def _sum_slots(a):
    P, R, C = a.shape
    tr = _tile(R, 256, 16)

    def body(a_ref, o_ref):
        s = a_ref[0].astype(F32)
        for k in range(1, P):
            s = s + a_ref[k].astype(F32)
        o_ref[...] = s

    return pl.pallas_call(
        body, name="sum_slots", grid=(R // tr,),
        in_specs=[pl.BlockSpec((P, tr, C), lambda i: (0, i, 0))], out_specs=_row_spec(tr, C),
        out_shape=jax.ShapeDtypeStruct((R, C), F32), compiler_params=_params("parallel"),
    )(a)


def _axes():
    return lax.axis_index("x"), lax.axis_index("y"), lax.axis_index("c")


def _allgather_devices(v):
    N = v.shape[1]

    def body(v_ref, out_ref, send_sems, recv_sems, local_sem):
        x, y, c = _axes()

        def row(px, py, pc):
            return out_ref.at[pl.ds(4 * px + 2 * py + pc, 1), :]

        mine = pltpu.make_async_copy(v_ref, row(x, y, c), local_sem)
        mine.start()
        sends, recvs = [], []
        for k in range(1, N_DEV):
            peer = (x ^ (k >> 2), y ^ ((k >> 1) & 1), c ^ (k & 1))
            sems = dict(send_sem=send_sems.at[k - 1], recv_sem=recv_sems.at[k - 1], device_id=peer, device_id_type=MESH)
            sends.append(pltpu.make_async_remote_copy(src_ref=v_ref, dst_ref=row(x, y, c), **sems))
            recvs.append(pltpu.make_async_remote_copy(src_ref=v_ref, dst_ref=row(*peer), **sems))
        for cp in sends:
            cp.start()
        for cp in recvs:
            cp.wait_recv()
        for cp in sends:
            cp.wait_send()
        mine.wait()

    vmem = pl.BlockSpec(memory_space=pltpu.VMEM)
    return pl.pallas_call(
        body, name="allgather_devices", in_specs=[vmem], out_specs=vmem,
        out_shape=jax.ShapeDtypeStruct((N_DEV, N), v.dtype),
        scratch_shapes=[pltpu.SemaphoreType.DMA((N_DEV - 1,)), pltpu.SemaphoreType.DMA((N_DEV - 1,)),
                        pltpu.SemaphoreType.DMA],
        compiler_params=pltpu.CompilerParams(vmem_limit_bytes=VMEM_LIMIT_V7X),
    )(v)


def _chip_pos(order, px, py):
    return 2 * px + py if order == "natural" else 2 * py + px


def _block(ref, axis, pos, width):
    idx = [slice(None)] * len(ref.shape)
    idx[axis] = pl.ds(pl.multiple_of(pos * width, width), width)
    return ref.at[tuple(idx)]


def _chip_allgather(shards, axes, orders, name):
    return _run_exchange(_gather_exchange(shards, axes, orders), name)


class _Exchange:
    def __init__(self, arrays, out_shapes, describe):
        self.arrays, self.out_shapes, self.describe = list(arrays), list(out_shapes), describe
        n_remote = len(self.arrays) * (N_CHIPS - 1)
        self.scratch = [pltpu.SemaphoreType.DMA((n_remote,)), pltpu.SemaphoreType.DMA((n_remote,)),
                        pltpu.SemaphoreType.DMA((len(self.arrays),))]

    def _copies(self, ins, outs, sems):
        send_sems, recv_sems, local_sems = sems
        x, y, c = _axes()
        local, sends, recvs = [], [], []
        for i in range(len(self.arrays)):
            src, dst = self.describe(i, ins, outs, x, y, x, y)
            local.append(pltpu.make_async_copy(src, dst, local_sems.at[i]))
            for k in range(1, N_CHIPS):
                px, py = x ^ (k >> 1), y ^ (k & 1)
                j = i * (N_CHIPS - 1) + k - 1
                sem = dict(send_sem=send_sems.at[j], recv_sem=recv_sems.at[j], device_id=(px, py, c),
                           device_id_type=MESH)
                src, dst = self.describe(i, ins, outs, x, y, px, py)
                sends.append(pltpu.make_async_remote_copy(src_ref=src, dst_ref=dst, **sem))
                src, dst = self.describe(i, ins, outs, px, py, x, y)
                recvs.append(pltpu.make_async_remote_copy(src_ref=src, dst_ref=dst, **sem))
        return local, sends, recvs

    def start(self, ins, outs, sems):
        local, sends, _ = self._copies(ins, outs, sems)
        for cp in local + sends:
            cp.start()

    def wait(self, ins, outs, sems):
        local, sends, recvs = self._copies(ins, outs, sems)
        for cp in recvs:
            cp.wait_recv()
        for cp in sends:
            cp.wait_send()
        for cp in local:
            cp.wait()


def _run_exchange(ex, name):
    n_in, n_out = len(ex.arrays), len(ex.out_shapes)

    def body(*refs):
        ins, outs, sems = refs[:n_in], refs[n_in:n_in + n_out], refs[n_in + n_out:]
        ex.start(ins, outs, sems)
        ex.wait(ins, outs, sems)

    hbm = pl.BlockSpec(memory_space=pl.ANY)
    return pl.pallas_call(body, name=name, in_specs=[hbm] * n_in, out_specs=[hbm] * n_out, out_shape=ex.out_shapes,
                          scratch_shapes=ex.scratch)(*ex.arrays)


def _gather_exchange(shards, axes, orders):
    out_shapes = []
    for s, ax, order in zip(shards, axes, orders):
        shp = list(s.shape)
        if order == "gate_up_tiles":
            shp = [shp[0], 2, shp[1], 2 * shp[2]]
        else:
            shp[ax] *= N_CHIPS
        out_shapes.append(jax.ShapeDtypeStruct(tuple(shp), s.dtype))

    def describe(i, ins, outs, fx, fy, tx, ty):
        pos, width = _chip_pos(orders[i], fx, fy), shards[i].shape[axes[i]]
        if orders[i] == "gate_up_tiles":
            return ins[i], outs[i].at[:, pos // 2, :, pl.ds(pl.multiple_of((pos % 2) * width, width), width)]
        return ins[i], _block(outs[i], axes[i], pos, width)

    return _Exchange(shards, out_shapes, describe)


def _chip_scatter(grads, axes, orders, name):
    return _run_exchange(_scatter_exchange(grads, axes, orders), name)


def _scatter_exchange(grads, axes, orders, layers=None):
    layers = layers or [None] * len(grads)
    widths, out_shapes = [], {}
    for i, (g, ax, lay) in enumerate(zip(grads, axes, layers)):
        shp = list(g.shape)
        shp[ax] //= N_CHIPS
        widths.append(shp[ax])
        key, lead = (("own", i), (N_CHIPS,)) if lay is None else (("shared", lay[0]), (N_CHIPS, lay[2]))
        out_shapes[key] = jax.ShapeDtypeStruct(lead + tuple(shp), g.dtype)
    keys = list(out_shapes)

    def describe(i, ins, outs, fx, fy, tx, ty):
        lay = layers[i]
        out = outs[keys.index(("own", i) if lay is None else ("shared", lay[0]))]
        slot = out.at[2 * fx + fy] if lay is None else out.at[2 * fx + fy, lay[1]]
        return _block(ins[i], axes[i], _chip_pos(orders[i], tx, ty), widths[i]), slot

    return _Exchange(grads, [out_shapes[k] for k in keys], describe)


def _core_swap(arrays, name):
    n = len(arrays)

    def body(*refs):
        ins, outs = refs[:n], refs[n:2 * n]
        send_sems, recv_sems = refs[2 * n:]
        x, y, c = _axes()
        copies = [pltpu.make_async_remote_copy(src_ref=ins[i], dst_ref=outs[i], send_sem=send_sems.at[i],
                                               recv_sem=recv_sems.at[i], device_id=(x, y, 1 - c), device_id_type=MESH)
                  for i in range(n)]
        for cp in copies:
            cp.start()
        for cp in copies:
            cp.wait_recv()
        for cp in copies:
            cp.wait_send()

    hbm = pl.BlockSpec(memory_space=pl.ANY)
    return pl.pallas_call(
        body, name=name, in_specs=[hbm] * n, out_specs=[hbm] * n,
        out_shape=[jax.ShapeDtypeStruct(a.shape, a.dtype) for a in arrays],
        scratch_shapes=[pltpu.SemaphoreType.DMA((n,)), pltpu.SemaphoreType.DMA((n,))],
    )(*arrays)


SMALL = ("b_mod", "g_attn", "rpb_na", "sink_sw", "t5_table", "gq_ax", "gk_ax", "g_group", "g_ffn", "g_final")
BIG = ("w_in", "w_o", "w_gu", "w_down")
BIG_AXIS = {"w_in": 2, "w_o": 1, "w_gu": 2, "w_down": 1}
BIG_ORDER = {"w_in": "natural", "w_o": "natural", "w_gu": "gate_up_tiles", "w_down": "natural"}
WEIGHTS = ("w_mod", "b_mod", "g_attn", "w_in", "rpb_na", "sink_sw", "t5_table", "gq_ax", "gk_ax", "g_group",
           "w_o", "g_ffn", "w_gu", "w_down", "g_final")


def _pack(arrs):
    flat = jnp.concatenate([a.reshape(-1) for a in arrs])
    n = flat.shape[0]
    padded = -(-n // (8 * LANES)) * (8 * LANES)
    return jnp.pad(flat, (0, padded - n))


def _unpack(flat, like):
    out, off = [], 0
    for a in like:
        out.append(flat[off:off + a.size].reshape(a.shape))
        off += a.size
    return out


def kernel(x, c, w_mod, b_mod, g_attn, w_in, rpb_na, sink_sw, t5_table, gq_ax, gk_ax, g_group, w_o, g_ffn, w_gu, w_down, g_final, loss_target, m_w_mod, m_b_mod, m_g_attn, m_w_in, m_rpb_na, m_sink_sw, m_t5_table, m_gq_ax, m_gk_ax, m_g_group, m_w_o, m_g_ffn, m_w_gu, m_w_down, m_g_final, v_w_mod, v_b_mod, v_g_attn, v_w_in, v_rpb_na, v_sink_sw, v_t5_table, v_gq_ax, v_gk_ax, v_g_group, v_w_o, v_g_ffn, v_w_gu, v_w_down, v_g_final):
    W = dict(w_mod=w_mod, b_mod=b_mod, g_attn=g_attn, w_in=w_in, rpb_na=rpb_na, sink_sw=sink_sw, t5_table=t5_table,
             gq_ax=gq_ax, gk_ax=gk_ax, g_group=g_group, w_o=w_o, g_ffn=g_ffn, w_gu=w_gu, w_down=w_down,
             g_final=g_final)
    M = dict(w_mod=m_w_mod, b_mod=m_b_mod, g_attn=m_g_attn, w_in=m_w_in, rpb_na=m_rpb_na, sink_sw=m_sink_sw,
             t5_table=m_t5_table, gq_ax=m_gq_ax, gk_ax=m_gk_ax, g_group=m_g_group, w_o=m_w_o, g_ffn=m_g_ffn,
             w_gu=m_w_gu, w_down=m_w_down, g_final=m_g_final)
    V = dict(w_mod=v_w_mod, b_mod=v_b_mod, g_attn=v_g_attn, w_in=v_w_in, rpb_na=v_rpb_na, sink_sw=v_sink_sw,
             t5_table=v_t5_table, gq_ax=v_gq_ax, gk_ax=v_gk_ax, g_group=v_g_group, w_o=v_w_o, g_ffn=v_g_ffn,
             w_gu=v_w_gu, w_down=v_w_down, g_final=v_g_final)
    xi, yi, ci = _axes()
    me = 4 * xi + 2 * yi + ci
    chip = 2 * xi + yi
    D = x.shape[-1]
    mod_w = w_mod.shape[2]

    c_all = _allgather_devices(c)
    cond = c_all * (1.0 / (1.0 + jnp.exp(-c_all)))
    cond16 = jnp.pad(cond, ((0, MOD_ROWS - N_DEV), (0, 0))).astype(BF16)
    mod_part = _mod_fwd(cond16, w_mod)
    (mod_all,) = _chip_allgather([mod_part], [2], ["natural"], "allgather_mod")
    mod = lax.dynamic_slice_in_dim(mod_all, me, 1, axis=1)[:, 0, :] + b_mod

    shards = {k: W[k].astype(BF16) for k in BIG}
    (w_in_first,) = _chip_allgather([shards["w_in"][:1]], [BIG_AXIS["w_in"]], [BIG_ORDER["w_in"]], "allgather_w_in0")
    shards["w_in"] = shards["w_in"][1:]
    loss_part, grad_x, slots, small = _local_step(x[0], loss_target[0], mod, w_in_first, None, None, None, g_attn,
                                                  rpb_na, sink_sw, t5_table, gq_ax, gk_ax, g_group, g_ffn, g_final,
                                                  late_shards=shards)

    small_all = _allgather_devices(_pack([small[k] for k in SMALL])[None, :])
    rows = small_all.shape[1] // LANES
    parts = [small_all[k].reshape(rows, LANES) for k in range(N_DEV)]
    pk = lambda d: _pack([d[k] for k in SMALL]).reshape(rows, LANES)
    small_out = [_unpack(o.reshape(-1), [W[k] for k in SMALL]) for o in _adamw(pk(W), pk(M), pk(V), parts, "adamw_small")]

    L = w_mod.shape[0]
    dmod_all = small_all[:, :L * 6 * D].reshape(N_DEV, L, 6 * D)
    dmod_mine = lax.dynamic_slice_in_dim(dmod_all, chip * mod_w, mod_w, axis=2)
    dmod16 = jnp.pad(jnp.transpose(dmod_mine, (1, 0, 2)), ((0, 0), (0, MOD_ROWS - N_DEV), (0, 0))).astype(BF16)
    wmod_out = _wmod_adamw(jnp.transpose(cond16), dmod16, w_mod, m_w_mod, v_w_mod)

    names = list(BIG)
    two_d = lambda a: a.reshape(-1, a.shape[-1])
    mine = [_sum_slots(slots[k].reshape(N_CHIPS, -1, slots[k].shape[-1])) for k in names]
    theirs = _core_swap(mine, "swap_grads")
    big_out = {}
    for k, a, b in zip(names, mine, theirs):
        outs = _adamw(two_d(W[k]), two_d(M[k]), two_d(V[k]), [a, b], "adamw_" + k)
        big_out[k] = [o.reshape(W[k].shape) for o in outs]

    loss = lax.psum(loss_part, ("x", "y", "c"))
    per_kind = []
    for kind in range(4):
        for k in WEIGHTS:
            if k == "w_mod":
                per_kind.append(wmod_out[kind])
            elif k in big_out:
                per_kind.append(big_out[k][kind])
            else:
                per_kind.append(small_out[kind][SMALL.index(k)])
    return (loss, grad_x[None], *per_kind)
```

```python
import functools
import math

import jax
import jax.numpy as jnp
from jax import lax
from jax.experimental import pallas as pl
from jax.experimental.pallas import tpu as pltpu

F32 = jnp.float32
BF16 = jnp.bfloat16
I32 = jnp.int32

DEPTH = 2
HEAD_DIM = 64
GRID_W = 64
NA_HEADS = 4
SW_HEADS = 6
SW_KV_HEADS = 2
AX_HEADS = 6
AX_KV_HEADS = 2
NA_WIN_ROWS = 8
NA_WIN_COLS = 16
SW_RADIUS = 128
T5_BUCKETS = 32
T5_MAX_DIST = 128
ROPE_THETA = 10000.0
EPS = 1e-6
NEG_INF = -1e30
QK_SCALE = HEAD_DIM ** -0.5

NA_W = NA_HEADS * HEAD_DIM
SW_W = SW_HEADS * HEAD_DIM
SW_KV_W = SW_KV_HEADS * HEAD_DIM
AX_W = AX_HEADS * HEAD_DIM
AX_KV_W = AX_KV_HEADS * HEAD_DIM
OFF_QA, OFF_KA, OFF_VA = 0, NA_W, 2 * NA_W
OFF_QB = 3 * NA_W
OFF_KB = OFF_QB + SW_W
OFF_VB = OFF_KB + SW_KV_W
OFF_QC = OFF_VB + SW_KV_W
OFF_KC = OFF_QC + AX_W
OFF_VC = OFF_KC + AX_KV_W
IN_WIDTH = OFF_VC + AX_KV_W
MIX_WIDTH = NA_W + SW_W + AX_W

ADAM_LR = 0.001
ADAM_B1 = 0.9
ADAM_B2 = 0.999
ADAM_EPS = 1e-08
ADAM_WD = 0.01
ADAM_STEP = 10

N_CHIPS = 4
N_DEV = 8
LANES = 128
VMEM_LIMIT_V7X = 56 * 1024 * 1024
ROW_TILE = 512
ROW_TILE_WIDE = 512
TOKEN_CHUNK = 2048
MESH = pl.DeviceIdType.MESH

NT = (((1,), (1,)), ((), ()))
TN = (((0,), (0,)), ((), ()))


def _params(*sem):
    return pltpu.CompilerParams(dimension_semantics=sem if sem else None,
                                vmem_limit_bytes=VMEM_LIMIT_V7X)


def _tile(n, pref, mult=8):
    t = (min(pref, n) // mult) * mult
    while t >= mult:
        if n % t == 0:
            return t
        t -= mult
    return n


def _row_spec(tm, width, col=0):
    return pl.BlockSpec((tm, width), lambda i, *_: (i, col))


def _const_spec(shape):
    nd = len(shape)
    return pl.BlockSpec(shape, lambda *_: (0,) * nd)


def _rsq(ms):
    return lax.rsqrt(ms + EPS)


def _rope_tables(S):
    rows = S // GRID_W
    axis_dim = HEAD_DIM // 2
    quarter = axis_dim // 2
    lane = jnp.arange(LANES)
    freq = (ROPE_THETA ** (-(2 * (lane % quarter)).astype(F32) / axis_dim))[None, :]
    by_row = ((lane % HEAD_DIM) < axis_dim)[None, None, :]
    first = ((lane % axis_dim) < quarter)[None, :]
    ang_r = jnp.arange(rows, dtype=F32)[:, None] * freq
    ang_c = jnp.arange(GRID_W, dtype=F32)[:, None] * freq

    def table(fr, fc):
        t = jnp.where(by_row, fr[:, None, :], fc[None, :, :])
        return t.reshape(S, LANES)

    sin_r, sin_c = jnp.sin(ang_r), jnp.sin(ang_c)
    return (table(jnp.cos(ang_r), jnp.cos(ang_c)),
            table(jnp.where(first, -sin_r, 0.0), jnp.where(first, -sin_c, 0.0)),
            table(jnp.where(first, 0.0, sin_r), jnp.where(first, 0.0, sin_c)))


def _pair_sum(v):
    lane = lax.broadcasted_iota(I32, v.shape, 1)
    lo = lane < HEAD_DIM
    s_lo = jnp.sum(jnp.where(lo, v, 0.0), axis=-1, keepdims=True)
    s_hi = jnp.sum(jnp.where(lo, 0.0, v), axis=-1, keepdims=True)
    return jnp.where(lo, s_lo, s_hi)


def _rope(t, cos, sa, sb):
    return t * cos + pltpu.roll(t, LANES - 16, 1) * sa + pltpu.roll(t, 16, 1) * sb


def _rope_t(t, cos, sa, sb):
    return t * cos + pltpu.roll(t * sa, 16, 1) + pltpu.roll(t * sb, LANES - 16, 1)


def _qk_prep_chunk(x, g128, cos, sa, sb):
    r = _rsq(_pair_sum(x * x) * (1.0 / HEAD_DIM))
    return _rope(x * r * g128, cos, sa, sb)


def _qk_prep_bwd_chunk(x, dy, g128, cos, sa, sb):
    dn = _rope_t(dy, cos, sa, sb)
    r = _rsq(_pair_sum(x * x) * (1.0 / HEAD_DIM))
    xhat = x * r
    dg = jnp.sum(dn * xhat, axis=0, keepdims=True)
    dxh = dn * g128
    dx = r * (dxh - xhat * (_pair_sum(dxh * xhat) * (1.0 / HEAD_DIM)))
    return dx, dg


def _ln_mod(xv, g, sc, sh):
    r = _rsq(jnp.mean(xv * xv, axis=-1, keepdims=True))
    return xv * r * g * (1.0 + sc) + sh


def _inproj_fwd(x, g, sc, sh, w, gq128, gk128, rope):
    S, D = x.shape
    tm = _tile(S, ROW_TILE)
    cos, sa, sb = rope

    def body(x_ref, g_ref, sc_ref, sh_ref, w_ref, gq_ref, gk_ref, cos_ref, sa_ref, sb_ref,
             h_ref, proj_ref, qc_ref, kc_ref):
        hb = _ln_mod(x_ref[...], g_ref[...], sc_ref[...], sh_ref[...]).astype(BF16)
        h_ref[...] = hb
        acc = jnp.dot(hb, w_ref[...], preferred_element_type=F32)
        proj_ref[...] = acc.astype(BF16)
        c, a, b = cos_ref[...], sa_ref[...], sb_ref[...]
        for j in range(AX_W // LANES):
            xq = acc[:, OFF_QC + j * LANES: OFF_QC + (j + 1) * LANES]
            qc_ref[:, j * LANES:(j + 1) * LANES] = (
                _qk_prep_chunk(xq, gq_ref[...], c, a, b) * QK_SCALE).astype(BF16)
        for j in range(AX_KV_W // LANES):
            xk = acc[:, OFF_KC + j * LANES: OFF_KC + (j + 1) * LANES]
            kc_ref[:, j * LANES:(j + 1) * LANES] = _qk_prep_chunk(xk, gk_ref[...], c, a, b).astype(BF16)

    vec = _const_spec((1, D))
    v128 = _const_spec((1, LANES))
    return pl.pallas_call(
        body, name="inproj_fwd", grid=(S // tm,),
        in_specs=[_row_spec(tm, D), vec, vec, vec, _const_spec(w.shape), v128, v128,
                  _row_spec(tm, LANES), _row_spec(tm, LANES), _row_spec(tm, LANES)],
        out_specs=[_row_spec(tm, D), _row_spec(tm, IN_WIDTH), _row_spec(tm, AX_W), _row_spec(tm, AX_KV_W)],
        out_shape=[jax.ShapeDtypeStruct((S, D), BF16), jax.ShapeDtypeStruct((S, IN_WIDTH), BF16),
                   jax.ShapeDtypeStruct((S, AX_W), BF16), jax.ShapeDtypeStruct((S, AX_KV_W), BF16)],
        compiler_params=_params("parallel"),
    )(x, g, sc, sh, w, gq128, gk128, cos, sa, sb)


class _Band:
    def __init__(self, kind, S):
        self.kind = kind
        self.S = S
        if kind == "na":
            self.hq, self.g, self.halo = NA_HEADS, NA_HEADS, (NA_WIN_ROWS // 2) * GRID_W
            self.q_off, self.k_off, self.v_off = OFF_QA, OFF_KA, OFF_VA
        else:
            self.hq, self.g, self.halo = SW_HEADS, SW_KV_HEADS, SW_RADIUS
            self.q_off, self.k_off, self.v_off = OFF_QB, OFF_KB, OFF_VB
        self.bq = 2 * self.halo
        self.bk = self.bq + 2 * self.halo
        self.nb = S // self.bq
        self.rep = self.hq // self.g
        self.qw = self.hq * HEAD_DIM
        self.kw = self.g * HEAD_DIM

    def kv_of(self, h):
        return (h // 2, h % 2) if self.kind == "na" else (0, h // self.rep)

    def mask(self, n, transposed=False):
        shape = (self.bk, self.bq) if transposed else (self.bq, self.bk)
        qi = lax.broadcasted_iota(I32, shape, 1 if transposed else 0) + n * self.bq
        kj = lax.broadcasted_iota(I32, shape, 0 if transposed else 1) + (n * self.bq - self.halo)
        if self.kind == "sw":
            return (jnp.abs(kj - qi) <= SW_RADIUS) & (kj >= 0) & (kj < self.S)
        rows = self.S // GRID_W
        r, col = qi >> 6, qi & (GRID_W - 1)
        kr, kc = kj >> 6, kj & (GRID_W - 1)
        rs = jnp.clip(r - NA_WIN_ROWS // 2, 0, rows - NA_WIN_ROWS)
        cs = jnp.clip(col - NA_WIN_COLS // 2, 0, GRID_W - NA_WIN_COLS)
        return (kr >= rs) & (kr < rs + NA_WIN_ROWS) & (kc >= cs) & (kc < cs + NA_WIN_COLS)

    def qkv_specs(self):
        ratio = self.bq // self.halo
        last = self.S // self.halo - 1
        q = pl.BlockSpec((self.bq, self.qw), lambda n, o=self.q_off // self.qw: (n, o))
        specs = [q]
        for off in (self.k_off, self.v_off):
            o = off // self.kw
            specs.append(pl.BlockSpec((self.halo, self.kw), lambda n, o=o: (jnp.maximum(n * ratio - 1, 0), o)))
            specs.append(pl.BlockSpec((self.bq, self.kw), lambda n, o=o: (n, o)))
            specs.append(pl.BlockSpec((self.halo, self.kw), lambda n, o=o: (jnp.minimum((n + 1) * ratio, last), o)))
        return specs


def _band_kv_variants(bd, refs, fill):
    out = []
    for blk in range(bd.kw // LANES):
        cols = slice(blk * LANES, (blk + 1) * LANES)
        out.append(_half_variants(jnp.concatenate([r[:, cols] for r in refs], axis=0), fill))
    return out


def _band_fwd(bd, proj, bias, sink, gg):
    S = bd.S
    has_sink = sink is not None

    def body(*refs):
        q_ref, kp, km, kn, vp, vm, vn, bias_ref = refs[:8]
        k = 8
        sink_ref = None
        if has_sink:
            sink_ref = refs[k]
            k += 1
        gg_ref, raw_ref, yn_ref, o_scr = refs[k:k + 4]
        mask = bd.mask(pl.program_id(0))
        lo = _left_half((bd.bq, LANES))
        kzs, vzs = _band_kv_variants(bd, (kp, km, kn), 0.0), _band_kv_variants(bd, (vp, vm, vn), 1.0)
        for pr in range(bd.hq // 2):
            cols = slice(pr * LANES, (pr + 1) * LANES)
            qp = q_ref[:, cols] * QK_SCALE
            acc = []
            for half in range(2):
                h = 2 * pr + half
                blk, src = bd.kv_of(h)
                s = lax.dot_general(qp, kzs[blk][src][half], NT, preferred_element_type=F32) + bias_ref[h]
                s = jnp.where(mask, s, NEG_INF)
                m = jnp.max(s, axis=-1, keepdims=True)
                if has_sink:
                    m = jnp.maximum(m, sink_ref[0:1, h:h + 1])
                a = jnp.dot(jnp.exp(s - m).astype(BF16), vzs[blk][src][half], preferred_element_type=F32)
                if has_sink:
                    e = jnp.exp(sink_ref[0:1, h:h + 1] - m)
                    a = a + (jnp.where(lo, 0.0, e) if half == 0 else jnp.where(lo, e, 0.0))
                acc.append(a)
            o_scr[:, cols] = jnp.where(lo, acc[0] / pltpu.roll(acc[0], HEAD_DIM, 1),
                                       acc[1] / pltpu.roll(acc[1], HEAD_DIM, 1))
        o = o_scr[...]
        raw_ref[...] = o.astype(BF16)
        r = _rsq(jnp.mean(o * o, axis=-1, keepdims=True))
        yn_ref[...] = (o * r * gg_ref[...]).astype(BF16)

    in_specs = bd.qkv_specs() + [_const_spec(bias.shape)]
    args = [proj] * 7 + [bias]
    if has_sink:
        in_specs.append(_const_spec(sink.shape))
        args.append(sink)
    in_specs.append(_const_spec(gg.shape))
    args.append(gg)
    out = jax.ShapeDtypeStruct((S, bd.qw), BF16)
    return pl.pallas_call(
        body, name=bd.kind + "_fwd", grid=(bd.nb,), in_specs=in_specs,
        out_specs=[_row_spec(bd.bq, bd.qw), _row_spec(bd.bq, bd.qw)], out_shape=[out, out],
        scratch_shapes=[pltpu.VMEM((bd.bq, bd.qw), F32)],
        compiler_params=_params("parallel"),
    )(*args)


def _band_bwd(bd, proj, bias, sink, dy):
    S = bd.S
    has_sink = sink is not None

    def body(*refs):
        q_ref, kp, km, kn, vp, vm, vn, bias_ref = refs[:8]
        k = 8
        sink_ref = None
        if has_sink:
            sink_ref = refs[k]
            k += 1
        do_ref = refs[k]
        dq_ref, dkm, dvm, dkp, dvp, dkn, dvn, dbias_ref = refs[k + 1:k + 9]
        k += 9
        dsink_ref = None
        if has_sink:
            dsink_ref = refs[k]
            k += 1
        dk_scr, dv_scr = refs[k:k + 2]
        n = pl.program_id(0)

        @pl.when(n == 0)
        def _():
            dbias_ref[...] = jnp.zeros_like(dbias_ref)
            if has_sink:
                dsink_ref[...] = jnp.zeros_like(dsink_ref)

        mask = bd.mask(n, transposed=True)
        lane = lax.broadcasted_iota(I32, (1, LANES), 1)
        kzs, vzs = _band_kv_variants(bd, (kp, km, kn), 0.0), _band_kv_variants(bd, (vp, vm, vn), 0.0)
        nblk = bd.kw // LANES
        dk, dv = [None] * nblk, [None] * nblk
        for pr in range(bd.hq // 2):
            cols = slice(pr * LANES, (pr + 1) * LANES)
            qp, dop = q_ref[:, cols] * QK_SCALE, do_ref[:, cols]
            qz, doz = _half_variants(qp), _half_variants(dop)
            dq = None
            for half in range(2):
                h = 2 * pr + half
                blk, dst = bd.kv_of(h)
                kz, vz = kzs[blk][dst][half], vzs[blk][dst][half]
                s = lax.dot_general(kz, qp, NT, preferred_element_type=F32) + bias_ref[h]
                s = jnp.where(mask, s, NEG_INF)
                m = jnp.max(s, axis=0, keepdims=True)
                if has_sink:
                    m = jnp.maximum(m, sink_ref[0:1, h:h + 1])
                p = jnp.exp(s - m)
                l = jnp.sum(p, axis=0, keepdims=True)
                if has_sink:
                    e = jnp.exp(sink_ref[0:1, h:h + 1] - m)
                    l = l + e
                inv = 1.0 / l
                pn = p * inv
                dp = lax.dot_general(vz, dop, NT, preferred_element_type=F32)
                delta = jnp.sum(pn * dp, axis=0, keepdims=True)
                ds = pn * (dp - delta)
                dbias_ref[h] += ds
                if has_sink:
                    dsink_ref[...] += jnp.where(lane == h, -jnp.sum(e * inv * delta, axis=1, keepdims=True), 0.0)
                dsb = ds.astype(BF16)
                a = jnp.dot(pn.astype(BF16), doz[half][dst], preferred_element_type=F32)
                b = jnp.dot(dsb, qz[half][dst], preferred_element_type=F32)
                d = lax.dot_general(dsb, kz, TN, preferred_element_type=F32)
                dv[blk] = a if dv[blk] is None else dv[blk] + a
                dk[blk] = b if dk[blk] is None else dk[blk] + b
                dq = d if dq is None else dq + d
            dq_ref[:, cols] = dq * QK_SCALE
        for blk in range(nblk):
            cols = slice(blk * LANES, (blk + 1) * LANES)
            dk_scr[:, cols] = dk[blk]
            dv_scr[:, cols] = dv[blk]
        h0, h1 = bd.halo, bd.halo + bd.bq
        dkp[0] = dk_scr[0:h0, :]
        dkm[...] = dk_scr[h0:h1, :]
        dkn[0] = dk_scr[h1:bd.bk, :]
        dvp[0] = dv_scr[0:h0, :]
        dvm[...] = dv_scr[h0:h1, :]
        dvn[0] = dv_scr[h1:bd.bk, :]

    in_specs = bd.qkv_specs() + [_const_spec(bias.shape)]
    args = [proj] * 7 + [bias]
    if has_sink:
        in_specs.append(_const_spec(sink.shape))
        args.append(sink)
    in_specs.append(_row_spec(bd.bq, bd.qw))
    args.append(dy)
    halo_spec = pl.BlockSpec((1, bd.halo, bd.kw), lambda n: (n, 0, 0))
    halo_shape = jax.ShapeDtypeStruct((bd.nb, bd.halo, bd.kw), F32)
    main_shape = jax.ShapeDtypeStruct((S, bd.kw), F32)
    out_specs = [_row_spec(bd.bq, bd.qw), _row_spec(bd.bq, bd.kw), _row_spec(bd.bq, bd.kw),
                 halo_spec, halo_spec, halo_spec, halo_spec, _const_spec(bias.shape)]
    out_shape = [jax.ShapeDtypeStruct((S, bd.qw), F32), main_shape, main_shape,
                 halo_shape, halo_shape, halo_shape, halo_shape, jax.ShapeDtypeStruct(bias.shape, F32)]
    if has_sink:
        out_specs.append(_const_spec((1, LANES)))
        out_shape.append(jax.ShapeDtypeStruct((1, LANES), F32))
    return pl.pallas_call(
        body, name=bd.kind + "_bwd", grid=(bd.nb,), in_specs=in_specs, out_specs=out_specs, out_shape=out_shape,
        scratch_shapes=[pltpu.VMEM((bd.bk, bd.kw), F32), pltpu.VMEM((bd.bk, bd.kw), F32)],
        compiler_params=_params("arbitrary"),
    )(*args)


def _halo_to_rows(prev, nxt):
    nb, halo, w = prev.shape
    z = jnp.zeros((1, halo, w), prev.dtype)
    first = jnp.concatenate([z, nxt[:-1]], axis=0)
    second = jnp.concatenate([prev[1:], z], axis=0)
    return jnp.concatenate([first, second], axis=1).reshape(nb * 2 * halo, w)


AX_PAIRS = AX_W // LANES


AX_FWD_BLOCKS = (1024, 2048)
AX_BWD_BLOCKS = (1024, 1024)


def _ax_blocks(S, blocks):
    return _tile(S, blocks[0]), _tile(S, blocks[1])


def _left_half(shape):
    return lax.broadcasted_iota(I32, shape, len(shape) - 1) < HEAD_DIM


def _as_row(a):
    return jnp.transpose(a)[0:1, :]


def _half_variants(a, fill=0.0):
    lo = _left_half(a.shape)
    other = jnp.full_like(a, fill)
    swapped = pltpu.roll(a, HEAD_DIM, 1)
    return ((jnp.where(lo, a, other), jnp.where(lo, other, swapped)),
            (jnp.where(lo, swapped, other), jnp.where(lo, other, a)))


def _split_rider_refs(refs, n_in, n_out, rider):
    r_in, r_out = (len(rider.arrays), len(rider.out_shapes)) if rider else (0, 0)
    a, b, c = n_in + r_in, n_in + r_in + n_out, n_in + r_in + n_out + r_out
    n_sems = 3 if rider else 0
    return refs[:n_in], refs[n_in:a], refs[a:b], refs[b:c], refs[c:len(refs) - n_sems], refs[len(refs) - n_sems:]


def _ax_fwd(qc, kc, proj, gg, rider=None):
    S = qc.shape[0]
    bq, bk = _ax_blocks(S, AX_FWD_BLOCKS)
    nq, nk = S // bq, S // bk
    rep = AX_HEADS // AX_KV_HEADS

    def body(*refs):
        (q_ref, k_ref, v_ref, gg_ref), r_ins, (raw_ref, yn_ref, lse_ref), r_outs, (m_scr, acc_scr), r_sems = (
            _split_rider_refs(refs, 4, 3, rider))
        qi, kv = pl.program_id(0), pl.program_id(1)

        if rider:
            @pl.when((qi == 0) & (kv == 0))
            def _():
                rider.start(r_ins, r_outs, r_sems)

        @pl.when(kv == 0)
        def _():
            m_scr[...] = jnp.full(m_scr.shape, NEG_INF, F32)
            acc_scr[...] = jnp.zeros_like(acc_scr)

        kz, vz = _half_variants(k_ref[...]), _half_variants(v_ref[...], 1.0)
        for pr in range(AX_PAIRS):
            qp = q_ref[:, pr * LANES:(pr + 1) * LANES]
            for half in range(2):
                h = 2 * pr + half
                g = h // rep
                s = lax.dot_general(qp, kz[g][half], NT, preferred_element_type=F32)
                m_prev = m_scr[h]
                m_new = jnp.maximum(m_prev, jnp.max(s, axis=-1, keepdims=True))
                p = jnp.exp(s - jnp.tile(m_new, (1, bk // LANES)))
                acc_scr[h] = jnp.exp(m_prev - m_new) * acc_scr[h] + jnp.dot(
                    p.astype(BF16), vz[g][half], preferred_element_type=F32)
                m_scr[h] = m_new

        @pl.when(kv == nk - 1)
        def _():
            lo = _left_half((bq, LANES))
            ssq = jnp.zeros((bq, 1), F32)
            for pr in range(AX_PAIRS):
                a0, a1 = acc_scr[2 * pr], acc_scr[2 * pr + 1]
                r0, r1 = pltpu.roll(a0, HEAD_DIM, 1), pltpu.roll(a1, HEAD_DIM, 1)
                lse_ref[2 * pr] = _as_row(m_scr[2 * pr] + jnp.log(jnp.where(lo, r0, a0)))
                lse_ref[2 * pr + 1] = _as_row(m_scr[2 * pr + 1] + jnp.log(jnp.where(lo, a1, r1)))
                o = jnp.where(lo, a0 / r0, a1 / r1)
                acc_scr[pr] = o
                ssq = ssq + jnp.sum(o * o, axis=-1, keepdims=True)
            r = _rsq(ssq * (1.0 / AX_W))
            for pr in range(AX_PAIRS):
                cols = slice(pr * LANES, (pr + 1) * LANES)
                o = acc_scr[pr]
                raw_ref[:, cols] = o.astype(BF16)
                yn_ref[:, cols] = (o * r * gg_ref[:, cols]).astype(BF16)

        if rider:
            @pl.when((qi == nq - 1) & (kv == nk - 1))
            def _():
                rider.wait(r_ins, r_outs, r_sems)

    out = jax.ShapeDtypeStruct((S, AX_W), BF16)
    hbm = pl.BlockSpec(memory_space=pl.ANY)
    r_arrays, r_shapes, r_scratch = (rider.arrays, rider.out_shapes, rider.scratch) if rider else ([], [], [])
    res = pl.pallas_call(
        body, name="ax_fwd_gather" if rider else "ax_fwd", grid=(nq, nk),
        in_specs=[pl.BlockSpec((bq, AX_W), lambda i, j: (i, 0)),
                  pl.BlockSpec((bk, AX_KV_W), lambda i, j: (j, 0)),
                  pl.BlockSpec((bk, AX_KV_W), lambda i, j: (j, OFF_VC // AX_KV_W)),
                  _const_spec(gg.shape)] + [hbm] * len(r_arrays),
        out_specs=[pl.BlockSpec((bq, AX_W), lambda i, j: (i, 0)),
                   pl.BlockSpec((bq, AX_W), lambda i, j: (i, 0)),
                   pl.BlockSpec((AX_HEADS, 1, bq), lambda i, j: (0, 0, i))] + [hbm] * len(r_shapes),
        out_shape=[out, out, jax.ShapeDtypeStruct((AX_HEADS, 1, S), F32)] + r_shapes,
        scratch_shapes=[pltpu.VMEM((AX_HEADS, bq, LANES), F32), pltpu.VMEM((AX_HEADS, bq, LANES), F32)] + r_scratch,
        compiler_params=_params("arbitrary", "arbitrary"),
    )(qc, kc, proj, gg, *r_arrays)
    return res[0], res[1], res[2], list(res[3:])


def _ax_delta(dy, raw):
    S = dy.shape[0]
    tm = _tile(S, ROW_TILE)

    def body(do_ref, o_ref, delta_ref):
        lo = _left_half((tm, LANES))
        for pr in range(AX_PAIRS):
            cols = slice(pr * LANES, (pr + 1) * LANES)
            prod = do_ref[:, cols].astype(F32) * o_ref[:, cols].astype(F32)
            left = jnp.sum(jnp.where(lo, prod, 0.0), axis=-1, keepdims=True)
            right = jnp.sum(jnp.where(lo, 0.0, prod), axis=-1, keepdims=True)
            delta_ref[2 * pr] = _as_row(jnp.broadcast_to(left, (tm, LANES)))
            delta_ref[2 * pr + 1] = _as_row(jnp.broadcast_to(right, (tm, LANES)))

    return pl.pallas_call(
        body, name="ax_delta", grid=(S // tm,), in_specs=[_row_spec(tm, AX_W), _row_spec(tm, AX_W)],
        out_specs=pl.BlockSpec((AX_HEADS, 1, tm), lambda i: (0, 0, i)),
        out_shape=jax.ShapeDtypeStruct((AX_HEADS, 1, S), F32), compiler_params=_params("parallel"),
    )(dy, raw)


def _ax_bwd(qc, kc, proj, dy, lse_row, delta_row, rider=None):
    S = qc.shape[0]
    bq, bk = _ax_blocks(S, AX_BWD_BLOCKS)
    nq, nk = S // bq, S // bk
    rep = AX_HEADS // AX_KV_HEADS

    def body(*refs):
        ((q_ref, k_ref, v_ref, do_ref, lse_ref, delta_ref), r_ins, (dk_ref, dv_ref, dq_hbm), r_outs, (dq_scr, sem),
         r_sems) = _split_rider_refs(refs, 6, 3, rider)
        j, i = pl.program_id(0), pl.program_id(1)

        if rider:
            @pl.when((j == 0) & (i == 0))
            def _():
                rider.start(r_ins, r_outs, r_sems)

        @pl.when(i == 0)
        def _():
            dk_ref[...] = jnp.zeros_like(dk_ref)
            dv_ref[...] = jnp.zeros_like(dv_ref)

        @pl.when(j == 0)
        def _():
            dq_scr[i] = jnp.zeros((bq, AX_W), F32)

        kz, vz = _half_variants(k_ref[...]), _half_variants(v_ref[...])
        dk, dv = None, None
        for pr in range(AX_PAIRS):
            cols = slice(pr * LANES, (pr + 1) * LANES)
            qp, dop = q_ref[:, cols], do_ref[:, cols]
            qz, doz = _half_variants(qp), _half_variants(dop)
            dq = None
            for half in range(2):
                h = 2 * pr + half
                g = h // rep
                s_t = lax.dot_general(kz[g][half], qp, NT, preferred_element_type=F32)
                p_t = jnp.exp(s_t - lse_ref[h])
                dp_t = lax.dot_general(vz[g][half], dop, NT, preferred_element_type=F32)
                ds_t = (p_t * (dp_t - delta_ref[h])).astype(BF16)
                a = jnp.dot(p_t.astype(BF16), doz[half][g], preferred_element_type=F32)
                b = jnp.dot(ds_t, qz[half][g], preferred_element_type=F32)
                d = lax.dot_general(ds_t, kz[g][half], TN, preferred_element_type=F32)
                dv = a if dv is None else dv + a
                dk = b if dk is None else dk + b
                dq = d if dq is None else dq + d
            dq_scr[i, :, cols] += dq
        dv_ref[...] += dv
        dk_ref[...] += dk

        @pl.when(j == nk - 1)
        def _():
            dq_scr[i] = dq_scr[i] * QK_SCALE
            out = pltpu.make_async_copy(dq_scr.at[i], dq_hbm.at[pl.ds(pl.multiple_of(i * bq, bq), bq), :], sem)
            out.start()
            out.wait()

        if rider:
            @pl.when((j == nk - 1) & (i == nq - 1))
            def _():
                rider.wait(r_ins, r_outs, r_sems)

    qspec = pl.BlockSpec((bq, AX_W), lambda j, i: (i, 0))
    kspec = pl.BlockSpec((bk, AX_KV_W), lambda j, i: (j, 0))
    stat = pl.BlockSpec((AX_HEADS, 1, bq), lambda j, i: (0, 0, i))
    out = jax.ShapeDtypeStruct((S, AX_KV_W), F32)
    hbm = pl.BlockSpec(memory_space=pl.ANY)
    r_arrays, r_shapes, r_scratch = (rider.arrays, rider.out_shapes, rider.scratch) if rider else ([], [], [])
    res = pl.pallas_call(
        body, name="ax_bwd_scatter" if rider else "ax_bwd", grid=(nk, nq),
        in_specs=[qspec, kspec, pl.BlockSpec((bk, AX_KV_W), lambda j, i: (j, OFF_VC // AX_KV_W)),
                  qspec, stat, stat] + [hbm] * len(r_arrays),
        out_specs=[kspec, kspec, hbm] + [hbm] * len(r_shapes),
        out_shape=[out, out, jax.ShapeDtypeStruct((S, AX_W), F32)] + r_shapes,
        scratch_shapes=[pltpu.VMEM((nq, bq, AX_W), F32), pltpu.SemaphoreType.DMA] + r_scratch,
        compiler_params=_params("arbitrary", "arbitrary"),
    )(qc, kc, proj, dy, lse_row, delta_row, *r_arrays)
    return res[2], res[0], res[1], list(res[3:])


def _oproj_fwd(x, yna, ynb, ync, w, gt):
    S, D = x.shape
    tm = _tile(S, ROW_TILE)

    def body(x_ref, a_ref, b_ref, c_ref, w_ref, gt_ref, x1_ref, ao_ref, yn_ref):
        yn_ref[:, 0:NA_W] = a_ref[...]
        yn_ref[:, NA_W:NA_W + SW_W] = b_ref[...]
        yn_ref[:, NA_W + SW_W:MIX_WIDTH] = c_ref[...]
        acc = jnp.dot(yn_ref[...], w_ref[...], preferred_element_type=F32)
        ao_ref[...] = acc.astype(BF16)
        x1_ref[...] = x_ref[...] + gt_ref[...] * acc

    return pl.pallas_call(
        body, name="oproj_fwd", grid=(S // tm,),
        in_specs=[_row_spec(tm, D), _row_spec(tm, NA_W), _row_spec(tm, SW_W), _row_spec(tm, AX_W),
                  _const_spec(w.shape), _const_spec((1, D))],
        out_specs=[_row_spec(tm, D), _row_spec(tm, D), _row_spec(tm, MIX_WIDTH)],
        out_shape=[jax.ShapeDtypeStruct((S, D), F32), jax.ShapeDtypeStruct((S, D), BF16),
                   jax.ShapeDtypeStruct((S, MIX_WIDTH), BF16)],
        compiler_params=_params("parallel"),
    )(x, yna, ynb, ync, w, gt)


def _gu_fwd(x, g, sc, sh, w):
    S, D = x.shape
    tn = w.shape[2] // 2
    F2 = 4 * tn
    tm = _tile(S, ROW_TILE)

    def body(x_ref, g_ref, sc_ref, sh_ref, w_ref, h_ref, gu_ref, act_ref):
        @pl.when(pl.program_id(1) == 0)
        def _():
            h_ref[...] = _ln_mod(x_ref[...], g_ref[...], sc_ref[...], sh_ref[...]).astype(BF16)

        acc = jnp.dot(h_ref[...], w_ref[pl.program_id(1)], preferred_element_type=F32)
        gu_ref[...] = acc.astype(BF16)
        gate, up = acc[:, :tn], acc[:, tn:]
        act_ref[...] = (gate * (1.0 / (1.0 + jnp.exp(-gate))) * up).astype(BF16)

    vec = pl.BlockSpec((1, D), lambda i, j: (0, 0))
    return pl.pallas_call(
        body, name="gu_fwd", grid=(S // tm, 2),
        in_specs=[pl.BlockSpec((tm, D), lambda i, j: (i, 0)), vec, vec, vec,
                  pl.BlockSpec((2, D, 2 * tn), lambda i, j: (0, 0, 0))],
        out_specs=[pl.BlockSpec((tm, D), lambda i, j: (i, 0)), pl.BlockSpec((tm, 2 * tn), lambda i, j: (i, j)),
                   pl.BlockSpec((tm, tn), lambda i, j: (i, j))],
        out_shape=[jax.ShapeDtypeStruct((S, D), BF16), jax.ShapeDtypeStruct((S, F2), BF16),
                   jax.ShapeDtypeStruct((S, F2 // 2), BF16)],
        compiler_params=_params("parallel", "arbitrary"),
    )(x, g, sc, sh, w)


def _down_fwd(x, act, w, gt):
    S, D = x.shape
    F = act.shape[1]
    tm = _tile(S, ROW_TILE)

    def body(x_ref, a_ref, w_ref, gt_ref, x2_ref, fo_ref):
        acc = jnp.dot(a_ref[...], w_ref[...], preferred_element_type=F32)
        fo_ref[...] = acc.astype(BF16)
        x2_ref[...] = x_ref[...] + gt_ref[...] * acc

    return pl.pallas_call(
        body, name="down_fwd", grid=(S // tm,),
        in_specs=[_row_spec(tm, D), _row_spec(tm, F), _const_spec(w.shape), _const_spec((1, D))],
        out_specs=[_row_spec(tm, D), _row_spec(tm, D)],
        out_shape=[jax.ShapeDtypeStruct((S, D), F32), jax.ShapeDtypeStruct((S, D), BF16)],
        compiler_params=_params("parallel"),
    )(x, act, w, gt)


def _final_loss(x, g, target):
    S, D = x.shape
    tm = _tile(S, ROW_TILE)

    def body(x_ref, g_ref, t_ref, dx_ref, loss_ref, dg_ref):
        @pl.when(pl.program_id(0) == 0)
        def _():
            loss_ref[...] = jnp.zeros_like(loss_ref)
            dg_ref[...] = jnp.zeros_like(dg_ref)

        xv = x_ref[...]
        r = _rsq(jnp.mean(xv * xv, axis=-1, keepdims=True))
        xhat = xv * r
        err = xhat * g_ref[...] - t_ref[...]
        loss_ref[...] += 0.5 * jnp.sum(jnp.mean(err * err, axis=-1, keepdims=True), axis=0, keepdims=True)
        dy = err * (1.0 / D)
        dg_ref[...] += jnp.sum(dy * xhat, axis=0, keepdims=True)
        dxh = dy * g_ref[...]
        dx_ref[...] = r * (dxh - xhat * jnp.mean(dxh * xhat, axis=-1, keepdims=True))

    return pl.pallas_call(
        body, name="final_loss", grid=(S // tm,),
        in_specs=[_row_spec(tm, D), _const_spec((1, D)), _row_spec(tm, D)],
        out_specs=[_row_spec(tm, D), _const_spec((1, LANES)), _const_spec((1, D))],
        out_shape=[jax.ShapeDtypeStruct((S, D), F32), jax.ShapeDtypeStruct((1, LANES), F32),
                   jax.ShapeDtypeStruct((1, D), F32)],
        compiler_params=_params("arbitrary"),
    )(x, g, target)


def _ffn_bwd1(dx2, fo, gt, w_down, gu):
    S, D = dx2.shape
    F2 = gu.shape[1]
    tn = F2 // 4
    tm = _tile(S, ROW_TILE)

    def body(dx_ref, fo_ref, gt_ref, w_ref, gu_ref, dfo_ref, dgu_ref, dgt_ref):
        i, j = pl.program_id(0), pl.program_id(1)

        @pl.when((i == 0) & (j == 0))
        def _():
            dgt_ref[...] = jnp.zeros_like(dgt_ref)

        @pl.when(j == 0)
        def _():
            dxv = dx_ref[...]
            dfo_ref[...] = (dxv * gt_ref[...]).astype(BF16)
            dgt_ref[...] += jnp.sum(dxv * fo_ref[...].astype(F32), axis=0, keepdims=True)

        dact = lax.dot_general(dfo_ref[...], w_ref[j], NT, preferred_element_type=F32)
        gate = gu_ref[:, :tn].astype(F32)
        up = gu_ref[:, tn:].astype(F32)
        sig = 1.0 / (1.0 + jnp.exp(-gate))
        dgu_ref[:, :tn] = (dact * up * (sig * (1.0 + gate * (1.0 - sig)))).astype(BF16)
        dgu_ref[:, tn:] = (dact * (gate * sig)).astype(BF16)

    vec = pl.BlockSpec((1, D), lambda i, j: (0, 0))
    row = pl.BlockSpec((tm, D), lambda i, j: (i, 0))
    return pl.pallas_call(
        body, name="ffn_bwd1", grid=(S // tm, 2),
        in_specs=[row, row, vec, pl.BlockSpec((2, tn, D), lambda i, j: (0, 0, 0)),
                  pl.BlockSpec((tm, 2 * tn), lambda i, j: (i, j))],
        out_specs=[row, pl.BlockSpec((tm, 2 * tn), lambda i, j: (i, j)), vec],
        out_shape=[jax.ShapeDtypeStruct((S, D), BF16), jax.ShapeDtypeStruct((S, F2), BF16),
                   jax.ShapeDtypeStruct((1, D), F32)],
        compiler_params=_params("arbitrary", "arbitrary"),
    )(dx2, fo, gt, w_down.reshape(2, tn, D), gu)


def _nt_ln_bwd(a, w, x, g, sc, dres, name):
    S, D = x.shape
    K = a.shape[1]
    tm = _tile(S, ROW_TILE_WIDE)

    def body(a_ref, w_ref, x_ref, g_ref, sc_ref, dres_ref, dx_ref, dsh_ref, dsc_ref, dg_ref):
        @pl.when(pl.program_id(0) == 0)
        def _():
            dsh_ref[...] = jnp.zeros_like(dsh_ref)
            dsc_ref[...] = jnp.zeros_like(dsc_ref)
            dg_ref[...] = jnp.zeros_like(dg_ref)

        if len(w.shape) == 2:
            dh = lax.dot_general(a_ref[...], w_ref[...], NT, preferred_element_type=F32)
        else:
            kt = w.shape[2]
            dh = sum(lax.dot_general(a_ref[:, t * kt:(t + 1) * kt], w_ref[t], NT, preferred_element_type=F32)
                     for t in range(w.shape[0]))
        xv = x_ref[...]
        r = _rsq(jnp.mean(xv * xv, axis=-1, keepdims=True))
        xhat = xv * r
        gv = g_ref[...]
        dsh_ref[...] += jnp.sum(dh, axis=0, keepdims=True)
        dsc_ref[...] += jnp.sum(dh * (xhat * gv), axis=0, keepdims=True)
        dn = dh * (1.0 + sc_ref[...])
        dg_ref[...] += jnp.sum(dn * xhat, axis=0, keepdims=True)
        dxh = dn * gv
        dx_ref[...] = dres_ref[...] + r * (dxh - xhat * jnp.mean(dxh * xhat, axis=-1, keepdims=True))

    vec = _const_spec((1, D))
    vshape = jax.ShapeDtypeStruct((1, D), F32)
    return pl.pallas_call(
        body, name=name, grid=(S // tm,),
        in_specs=[_row_spec(tm, K), _const_spec(w.shape), _row_spec(tm, D), vec, vec, _row_spec(tm, D)],
        out_specs=[_row_spec(tm, D), vec, vec, vec],
        out_shape=[jax.ShapeDtypeStruct((S, D), F32), vshape, vshape, vshape],
        compiler_params=_params("arbitrary"),
    )(a, w, x, g, sc, dres)


def _oproj_bwd(dx1, ao, gt, w, ya, yb, yc, gg):
    S, D = dx1.shape
    tm = _tile(S, ROW_TILE)
    groups = ((0, NA_W), (NA_W, SW_W), (NA_W + SW_W, AX_W))

    def body(dx_ref, ao_ref, gt_ref, w_ref, ya_ref, yb_ref, yc_ref, gg_ref,
             dao_ref, dya_ref, dyb_ref, dyc_ref, dgt_ref, dgg_ref):
        @pl.when(pl.program_id(0) == 0)
        def _():
            dgt_ref[...] = jnp.zeros_like(dgt_ref)
            dgg_ref[...] = jnp.zeros_like(dgg_ref)

        dxv = dx_ref[...]
        dao = (dxv * gt_ref[...]).astype(BF16)
        dao_ref[...] = dao
        dgt_ref[...] += jnp.sum(dxv * ao_ref[...].astype(F32), axis=0, keepdims=True)
        dyn = lax.dot_general(dao, w_ref[...], NT, preferred_element_type=F32)
        for (off, wd), y_ref, dy_ref in zip(groups, (ya_ref, yb_ref, yc_ref), (dya_ref, dyb_ref, dyc_ref)):
            y = y_ref[...].astype(F32)
            d = dyn[:, off:off + wd]
            r = _rsq(jnp.mean(y * y, axis=-1, keepdims=True))
            yhat = y * r
            dgg_ref[:, off:off + wd] += jnp.sum(d * yhat, axis=0, keepdims=True)
            dyh = d * gg_ref[:, off:off + wd]
            dy_ref[...] = (r * (dyh - yhat * jnp.mean(dyh * yhat, axis=-1, keepdims=True))).astype(BF16)

    vec = _const_spec((1, D))
    mvec = _const_spec((1, MIX_WIDTH))
    return pl.pallas_call(
        body, name="oproj_bwd", grid=(S // tm,),
        in_specs=[_row_spec(tm, D), _row_spec(tm, D), vec, _const_spec(w.shape),
                  _row_spec(tm, NA_W), _row_spec(tm, SW_W), _row_spec(tm, AX_W), mvec],
        out_specs=[_row_spec(tm, D), _row_spec(tm, NA_W), _row_spec(tm, SW_W), _row_spec(tm, AX_W), vec, mvec],
        out_shape=[jax.ShapeDtypeStruct((S, D), BF16), jax.ShapeDtypeStruct((S, NA_W), BF16),
                   jax.ShapeDtypeStruct((S, SW_W), BF16), jax.ShapeDtypeStruct((S, AX_W), BF16),
                   jax.ShapeDtypeStruct((1, D), F32), jax.ShapeDtypeStruct((1, MIX_WIDTH), F32)],
        compiler_params=_params("arbitrary"),
    )(dx1, ao, gt, w, ya, yb, yc, gg)


def _dproj_assemble(proj, na, sw, ax, gq128, gk128, rope):
    S = proj.shape[0]
    tm = _tile(S, ROW_TILE)
    cos, sa, sb = rope

    def body(proj_ref, qa, ka, kah, va, vah, qb, kb, kbh, vb, vbh, qc, kc, vc,
             gq_ref, gk_ref, cos_ref, sa_ref, sb_ref, out_ref, dgq_ref, dgk_ref):
        @pl.when(pl.program_id(0) == 0)
        def _():
            dgq_ref[...] = jnp.zeros_like(dgq_ref)
            dgk_ref[...] = jnp.zeros_like(dgk_ref)

        out_ref[:, OFF_QA:OFF_KA] = qa[...].astype(BF16)
        out_ref[:, OFF_KA:OFF_VA] = (ka[...] + kah[...]).astype(BF16)
        out_ref[:, OFF_VA:OFF_QB] = (va[...] + vah[...]).astype(BF16)
        out_ref[:, OFF_QB:OFF_KB] = qb[...].astype(BF16)
        out_ref[:, OFF_KB:OFF_VB] = (kb[...] + kbh[...]).astype(BF16)
        out_ref[:, OFF_VB:OFF_QC] = (vb[...] + vbh[...]).astype(BF16)
        c, a, b = cos_ref[...], sa_ref[...], sb_ref[...]
        for j in range(AX_W // LANES):
            cols = slice(OFF_QC + j * LANES, OFF_QC + (j + 1) * LANES)
            dx, dg = _qk_prep_bwd_chunk(proj_ref[:, cols].astype(F32), qc[:, j * LANES:(j + 1) * LANES],
                                        gq_ref[...], c, a, b)
            out_ref[:, cols] = dx.astype(BF16)
            dgq_ref[...] += dg
        for j in range(AX_KV_W // LANES):
            cols = slice(OFF_KC + j * LANES, OFF_KC + (j + 1) * LANES)
            dx, dg = _qk_prep_bwd_chunk(proj_ref[:, cols].astype(F32), kc[:, j * LANES:(j + 1) * LANES],
                                        gk_ref[...], c, a, b)
            out_ref[:, cols] = dx.astype(BF16)
            dgk_ref[...] += dg
        out_ref[:, OFF_VC:IN_WIDTH] = vc[...].astype(BF16)

    v128 = _const_spec((1, LANES))
    r = lambda w: _row_spec(tm, w)
    return pl.pallas_call(
        body, name="dproj_assemble", grid=(S // tm,),
        in_specs=[r(IN_WIDTH), r(NA_W), r(NA_W), r(NA_W), r(NA_W), r(NA_W),
                  r(SW_W), r(SW_KV_W), r(SW_KV_W), r(SW_KV_W), r(SW_KV_W),
                  r(AX_W), r(AX_KV_W), r(AX_KV_W), v128, v128, r(LANES), r(LANES), r(LANES)],
        out_specs=[r(IN_WIDTH), v128, v128],
        out_shape=[jax.ShapeDtypeStruct((S, IN_WIDTH), BF16), jax.ShapeDtypeStruct((1, LANES), F32),
                   jax.ShapeDtypeStruct((1, LANES), F32)],
        compiler_params=_params("arbitrary"),
    )(proj, *na, *sw, *ax, gq128, gk128, cos, sa, sb)


def _tn_matmul(a, b, name):
    S, Ka = a.shape
    Nb = b.shape[1]
    tm = _tile(Ka, 1408, LANES)
    tn = _tile(Nb, 1408, LANES)
    tk = _tile(S, TOKEN_CHUNK)
    nk = S // tk

    def body(a_ref, b_ref, o_ref, acc_ref):
        k = pl.program_id(2)

        @pl.when(k == 0)
        def _():
            acc_ref[...] = jnp.zeros_like(acc_ref)

        acc_ref[...] += lax.dot_general(a_ref[...], b_ref[...], TN, preferred_element_type=F32)

        @pl.when(k == nk - 1)
        def _():
            o_ref[...] = acc_ref[...].astype(BF16)

    return pl.pallas_call(
        body, name=name, grid=(Ka // tm, Nb // tn, nk),
        in_specs=[pl.BlockSpec((tk, tm), lambda i, j, k: (k, i)), pl.BlockSpec((tk, tn), lambda i, j, k: (k, j))],
        out_specs=pl.BlockSpec((tm, tn), lambda i, j, k: (i, j)),
        out_shape=jax.ShapeDtypeStruct((Ka, Nb), BF16),
        scratch_shapes=[pltpu.VMEM((tm, tn), F32)],
        compiler_params=_params("parallel", "parallel", "arbitrary"),
    )(a, b)


def _na_index(bd):
    rq = jnp.arange(bd.bq // GRID_W)
    rk = jnp.arange(bd.bk // GRID_W)
    col = jnp.arange(GRID_W)
    ri = jnp.clip(rk[None, :] - rq[:, None] - bd.halo // GRID_W + NA_WIN_ROWS - 1, 0, 2 * NA_WIN_ROWS - 2)
    ci = jnp.clip(col[None, :] - col[:, None] + NA_WIN_COLS - 1, 0, 2 * NA_WIN_COLS - 2)
    return ri, ci


def _na_one_hots(bd):
    ri, ci = _na_index(bd)
    oh_r = jax.nn.one_hot(ri, 2 * NA_WIN_ROWS - 1, dtype=F32)
    oh_c = jax.nn.one_hot(ci, 2 * NA_WIN_COLS - 1, dtype=F32)
    return oh_r, oh_c


def _na_bias(bd, rpb):
    oh_r, oh_c = _na_one_hots(bd)
    t = jnp.einsum("hab,qra->hqrb", rpb, oh_r, precision=lax.Precision.HIGHEST)
    b = jnp.einsum("hqrb,ckb->hqcrk", t, oh_c, precision=lax.Precision.HIGHEST)
    return b.reshape(NA_HEADS, bd.bq, bd.bk)


def _na_bias_t(bd, dbias):
    oh_r, oh_c = _na_one_hots(bd)
    d5 = dbias.reshape(NA_HEADS, bd.bq // GRID_W, GRID_W, bd.bk // GRID_W, GRID_W)
    t = jnp.einsum("hqcrk,ckb->hqrb", d5, oh_c, precision=lax.Precision.HIGHEST)
    return jnp.einsum("hqrb,qra->hab", t, oh_r, precision=lax.Precision.HIGHEST)


def _t5_bucket(rel):
    nb = T5_BUCKETS // 2
    ret = (rel > 0).astype(I32) * nb
    n = jnp.abs(rel)
    max_exact = nb // 2
    nf = jnp.maximum(n, max_exact).astype(F32)
    large = max_exact + (jnp.log(nf / max_exact) / math.log(T5_MAX_DIST / max_exact)
                         * (nb - max_exact)).astype(I32)
    large = jnp.minimum(large, nb - 1)
    return ret + jnp.where(n < max_exact, n, large)


def _sw_bucket(bd):
    rel = (jnp.arange(bd.bk) - bd.halo)[None, :] - jnp.arange(bd.bq)[:, None]
    return _t5_bucket(rel)


def _sw_one_hot(bd):
    return (_sw_bucket(bd)[None] == jnp.arange(T5_BUCKETS)[:, None, None]).astype(F32)


def _sw_bias(bd, t5):
    return jnp.einsum("bh,bqk->hqk", t5, _sw_one_hot(bd), precision=lax.Precision.HIGHEST)


def _sw_bias_t(bd, dbias):
    return jnp.einsum("hqk,bqk->bh", dbias, _sw_one_hot(bd), precision=lax.Precision.HIGHEST)


def _local_step(x, target, mod, w_in, w_o, w_gu, w_down, g_attn, rpb_na, sink_sw, t5_table, gq_ax, gk_ax,
                g_group, g_ffn, g_final, late_shards=None):
    S, D = x.shape
    riding = late_shards is not None
    w_in = [w_in[l] for l in range(w_in.shape[0])]
    rope = _rope_tables(S)
    na, sw = _Band("na", S), _Band("sw", S)
    two = lambda v: jnp.concatenate([v, v])[None, :]
    sw_bias = _sw_bias(sw, t5_table)
    saved = []
    for l in range(DEPTH):
        sh_a, sc_a, gt_a, sh_f, sc_f, gt_f = [mod[l, k * D:(k + 1) * D][None, :] for k in range(6)]
        gq128, gk128 = two(gq_ax[l]), two(gk_ax[l])
        gg = g_group[l][None, :]
        sink = jnp.pad(sink_sw[l], (0, LANES - SW_HEADS))[None, :]
        na_bias = _na_bias(na, rpb_na[l])
        h, proj, qc, kc = _inproj_fwd(x, g_attn[l][None, :], sc_a, sh_a, w_in[l], gq128, gk128, rope)
        ya, yna = _band_fwd(na, proj, na_bias, None, gg[:, :NA_W])
        yb, ynb = _band_fwd(sw, proj, sw_bias, sink, gg[:, NA_W:NA_W + SW_W])
        rider = None
        if riding and l == 0:
            rider = _gather_exchange([late_shards[k] for k in BIG], [BIG_AXIS[k] for k in BIG],
                                     [BIG_ORDER[k] for k in BIG])
        yc, ync, lse, got = _ax_fwd(qc, kc, proj, gg[:, NA_W + SW_W:], rider)
        if rider:
            w_in_late, w_o, w_gu, w_down = got
            w_in += [w_in_late[k] for k in range(w_in_late.shape[0])]
        x1, ao, yn = _oproj_fwd(x, yna, ynb, ync, w_o[l], gt_a)
        hf, gu, act = _gu_fwd(x1, g_ffn[l][None, :], sc_f, sh_f, w_gu[l])
        x2, fo = _down_fwd(x1, act, w_down[l], gt_f)
        saved.append(dict(x=x, x1=x1, h=h, proj=proj, qc=qc, kc=kc, ya=ya, yb=yb, yc=yc, lse=lse, ao=ao, yn=yn,
                          hf=hf, gu=gu, act=act, fo=fo, na_bias=na_bias, sink=sink, gq128=gq128, gk128=gk128,
                          gg=gg, mods=(sh_a, sc_a, gt_a, sh_f, sc_f, gt_f)))
        x = x2

    dx, loss_row, dg_final = _final_loss(x, g_final[None, :], target)
    gw = {k: [None] * DEPTH for k in ("w_in", "w_o", "w_gu", "w_down")}
    gs = {k: [None] * DEPTH for k in ("b_mod", "g_attn", "rpb_na", "sink_sw", "gq_ax", "gk_ax", "g_group", "g_ffn")}
    d_t5 = jnp.zeros((T5_BUCKETS, SW_HEADS), F32)
    for l in reversed(range(DEPTH)):
        s = saved[l]
        sh_a, sc_a, gt_a, sh_f, sc_f, gt_f = s["mods"]
        dfo, dgu, dgt_f = _ffn_bwd1(dx, s["fo"], gt_f, w_down[l], s["gu"])
        gw["w_down"][l] = _tn_matmul(s["act"], dfo, "dw_down")
        gw["w_gu"][l] = _tn_matmul(s["hf"], dgu, "dw_gu")
        dx1, dsh_f, dsc_f, gs["g_ffn"][l] = _nt_ln_bwd(dgu, w_gu[l], s["x1"], g_ffn[l][None, :], sc_f, dx, "ffn_bwd2")
        dao, dya, dyb, dyc, dgt_a, gs["g_group"][l] = _oproj_bwd(dx1, s["ao"], gt_a, w_o[l], s["ya"], s["yb"],
                                                                 s["yc"], s["gg"])
        gw["w_o"][l] = _tn_matmul(s["yn"], dao, "dw_o")
        dqa, dka, dva, dkap, dvap, dkan, dvan, dbias_na = _band_bwd(na, s["proj"], jnp.swapaxes(s["na_bias"], 1, 2), None, dya)
        dqb, dkb, dvb, dkbp, dvbp, dkbn, dvbn, dbias_sw, dsink = _band_bwd(sw, s["proj"], jnp.swapaxes(sw_bias, 1, 2), s["sink"], dyb)
        rider = None
        if riding and l == 0:
            ready = [("w_in", k) for k in range(1, DEPTH)] + [(n, k) for n in BIG[1:] for k in range(DEPTH)]
            rider = _scatter_exchange(
                [gw[n][k] for n, k in ready], [BIG_AXIS[n] - 1 for n, _ in ready], [BIG_ORDER[n] for n, _ in ready],
                [None if n == "w_in" else (BIG.index(n), k, DEPTH) for n, k in ready])
        dqc, dkc, dvc, sent = _ax_bwd(s["qc"], s["kc"], s["proj"], dyc, s["lse"], _ax_delta(dyc, s["yc"]), rider)
        dproj, dgq, dgk = _dproj_assemble(
            s["proj"], (dqa, dka, _halo_to_rows(dkap, dkan), dva, _halo_to_rows(dvap, dvan)),
            (dqb, dkb, _halo_to_rows(dkbp, dkbn), dvb, _halo_to_rows(dvbp, dvbn)), (dqc, dkc, dvc),
            s["gq128"], s["gk128"], rope)
        gw["w_in"][l] = _tn_matmul(s["h"], dproj, "dw_in")
        dx, dsh_a, dsc_a, gs["g_attn"][l] = _nt_ln_bwd(dproj, w_in[l], s["x"], g_attn[l][None, :], sc_a, dx1,
                                                       "inproj_bwd")
        gs["b_mod"][l] = jnp.concatenate([dsh_a, dsc_a, dgt_a, dsh_f, dsc_f, dgt_f], axis=1)[0]
        gs["rpb_na"][l] = _na_bias_t(na, jnp.swapaxes(dbias_na, 1, 2))
        gs["sink_sw"][l] = dsink[0, :SW_HEADS]
        d_t5 = d_t5 + _sw_bias_t(sw, jnp.swapaxes(dbias_sw, 1, 2))
        gs["gq_ax"][l] = dgq[0, :HEAD_DIM] + dgq[0, HEAD_DIM:]
        gs["gk_ax"][l] = dgk[0, :HEAD_DIM] + dgk[0, HEAD_DIM:]
        gs["g_attn"][l] = gs["g_attn"][l][0]
        gs["g_ffn"][l] = gs["g_ffn"][l][0]
        gs["g_group"][l] = gs["g_group"][l][0]

    if riding:
        (first,) = _chip_scatter([gw["w_in"][0]], [BIG_AXIS["w_in"] - 1], [BIG_ORDER["w_in"]], "scatter_w_in0")
        gw = dict(zip(BIG[1:], sent[DEPTH - 1:]), w_in=jnp.stack([first] + sent[:DEPTH - 1], axis=1))
    else:
        gw = {k: jnp.stack(v) for k, v in gw.items()}
    small = {k: jnp.stack(v) for k, v in gs.items()}
    small["t5_table"] = d_t5
    small["g_final"] = dg_final[0]
    return loss_row[0, 0], dx, gw, small


MOD_ROWS = 16


def _mod_fwd(cond16, w):
    L, D, C = w.shape
    tn = _tile(C, 512, LANES)

    def body(c_ref, w_ref, o_ref):
        o_ref[0] = jnp.dot(c_ref[...], w_ref[0].astype(BF16), preferred_element_type=F32)

    return pl.pallas_call(
        body, name="mod_fwd", grid=(L, C // tn),
        in_specs=[pl.BlockSpec((MOD_ROWS, D), lambda l, j: (0, 0)), pl.BlockSpec((1, D, tn), lambda l, j: (l, 0, j))],
        out_specs=pl.BlockSpec((1, MOD_ROWS, tn), lambda l, j: (l, 0, j)),
        out_shape=jax.ShapeDtypeStruct((L, MOD_ROWS, C), F32),
        compiler_params=_params("parallel", "parallel"),
    )(cond16, w)


def _adamw_math(w, g, m, v):
    m = ADAM_B1 * m + (1.0 - ADAM_B1) * g
    v = ADAM_B2 * v + (1.0 - ADAM_B2) * (g * g)
    m_hat = m / (1.0 - ADAM_B1 ** ADAM_STEP)
    v_hat = v / (1.0 - ADAM_B2 ** ADAM_STEP)
    delta = -ADAM_LR * (m_hat / (jnp.sqrt(v_hat) + ADAM_EPS) + ADAM_WD * w)
    return delta, m, v


def _adamw(w, m, v, parts, name):
    R, C = w.shape
    tr = _tile(R, 256)
    n = len(parts)

    def body(*refs):
        w_ref, m_ref, v_ref = refs[:3]
        g = refs[3][...]
        for p in refs[4:3 + n]:
            g = g + p[...]
        g_ref, d_ref, m2_ref, v2_ref = refs[3 + n:]
        g_ref[...] = g
        d_ref[...], m2_ref[...], v2_ref[...] = _adamw_math(w_ref[...], g, m_ref[...], v_ref[...])

    spec = _row_spec(tr, C)
    shape = jax.ShapeDtypeStruct((R, C), F32)
    return pl.pallas_call(
        body, name=name, grid=(R // tr,), in_specs=[spec] * (3 + n), out_specs=[spec] * 4, out_shape=[shape] * 4,
        compiler_params=_params("parallel"),
    )(w, m, v, *parts)


def _wmod_adamw(cond_t, dmod16, w, m, v):
    L, D, C = w.shape
    tr = _tile(D, 256)

    def body(c_ref, d_ref, w_ref, m_ref, v_ref, g_ref, dl_ref, m2_ref, v2_ref):
        g = jnp.dot(c_ref[...], d_ref[0], preferred_element_type=F32)
        g_ref[0] = g
        dl_ref[0], m2_ref[0], v2_ref[0] = _adamw_math(w_ref[0], g, m_ref[0], v_ref[0])

    spec = pl.BlockSpec((1, tr, C), lambda l, i: (l, i, 0))
    shape = jax.ShapeDtypeStruct((L, D, C), F32)
    return pl.pallas_call(
        body, name="wmod_adamw", grid=(L, D // tr),
        in_specs=[pl.BlockSpec((tr, MOD_ROWS), lambda l, i: (i, 0)),
                  pl.BlockSpec((1, MOD_ROWS, C), lambda l, i: (l, 0, 0)), spec, spec, spec],
        out_specs=[spec] * 4, out_shape=[shape] * 4,
        compiler_params=_params("parallel", "parallel"),
    )(cond_t, dmod16, w, m, v)


def _sum_slots(a):
    P, R, C = a.shape
    tr = _tile(R, 256, 16)

    def body(a_ref, o_ref):
        s = a_ref[0].astype(F32)
        for k in range(1, P):
            s = s + a_ref[k].astype(F32)
        o_ref[...] = s

    return pl.pallas_call(
        body, name="sum_slots", grid=(R // tr,),
        in_specs=[pl.BlockSpec((P, tr, C), lambda i: (0, i, 0))], out_specs=_row_spec(tr, C),
        out_shape=jax.ShapeDtypeStruct((R, C), F32), compiler_params=_params("parallel"),
    )(a)


def _axes():
    return lax.axis_index("x"), lax.axis_index("y"), lax.axis_index("c")


def _allgather_devices(v):
    N = v.shape[1]

    def body(v_ref, out_ref, send_sems, recv_sems, local_sem):
        x, y, c = _axes()

        def row(px, py, pc):
            return out_ref.at[pl.ds(4 * px + 2 * py + pc, 1), :]

        mine = pltpu.make_async_copy(v_ref, row(x, y, c), local_sem)
        mine.start()
        sends, recvs = [], []
        for k in range(1, N_DEV):
            peer = (x ^ (k >> 2), y ^ ((k >> 1) & 1), c ^ (k & 1))
            sems = dict(send_sem=send_sems.at[k - 1], recv_sem=recv_sems.at[k - 1], device_id=peer, device_id_type=MESH)
            sends.append(pltpu.make_async_remote_copy(src_ref=v_ref, dst_ref=row(x, y, c), **sems))
            recvs.append(pltpu.make_async_remote_copy(src_ref=v_ref, dst_ref=row(*peer), **sems))
        for cp in sends:
            cp.start()
        for cp in recvs:
            cp.wait_recv()
        for cp in sends:
            cp.wait_send()
        mine.wait()

    vmem = pl.BlockSpec(memory_space=pltpu.VMEM)
    return pl.pallas_call(
        body, name="allgather_devices", in_specs=[vmem], out_specs=vmem,
        out_shape=jax.ShapeDtypeStruct((N_DEV, N), v.dtype),
        scratch_shapes=[pltpu.SemaphoreType.DMA((N_DEV - 1,)), pltpu.SemaphoreType.DMA((N_DEV - 1,)),
                        pltpu.SemaphoreType.DMA],
        compiler_params=pltpu.CompilerParams(vmem_limit_bytes=VMEM_LIMIT_V7X),
    )(v)


def _chip_pos(order, px, py):
    return 2 * px + py if order == "natural" else 2 * py + px


def _block(ref, axis, pos, width):
    idx = [slice(None)] * len(ref.shape)
    idx[axis] = pl.ds(pl.multiple_of(pos * width, width), width)
    return ref.at[tuple(idx)]


def _chip_allgather(shards, axes, orders, name):
    return _run_exchange(_gather_exchange(shards, axes, orders), name)


class _Exchange:
    def __init__(self, arrays, out_shapes, describe):
        self.arrays, self.out_shapes, self.describe = list(arrays), list(out_shapes), describe
        n_remote = len(self.arrays) * (N_CHIPS - 1)
        self.scratch = [pltpu.SemaphoreType.DMA((n_remote,)), pltpu.SemaphoreType.DMA((n_remote,)),
                        pltpu.SemaphoreType.DMA((len(self.arrays),))]

    def _copies(self, ins, outs, sems):
        send_sems, recv_sems, local_sems = sems
        x, y, c = _axes()
        local, sends, recvs = [], [], []
        for i in range(len(self.arrays)):
            src, dst = self.describe(i, ins, outs, x, y, x, y)
            local.append(pltpu.make_async_copy(src, dst, local_sems.at[i]))
            for k in range(1, N_CHIPS):
                px, py = x ^ (k >> 1), y ^ (k & 1)
                j = i * (N_CHIPS - 1) + k - 1
                sem = dict(send_sem=send_sems.at[j], recv_sem=recv_sems.at[j], device_id=(px, py, c),
                           device_id_type=MESH)
                src, dst = self.describe(i, ins, outs, x, y, px, py)
                sends.append(pltpu.make_async_remote_copy(src_ref=src, dst_ref=dst, **sem))
                src, dst = self.describe(i, ins, outs, px, py, x, y)
                recvs.append(pltpu.make_async_remote_copy(src_ref=src, dst_ref=dst, **sem))
        return local, sends, recvs

    def start(self, ins, outs, sems):
        local, sends, _ = self._copies(ins, outs, sems)
        for cp in local + sends:
            cp.start()

    def wait(self, ins, outs, sems):
        local, sends, recvs = self._copies(ins, outs, sems)
        for cp in recvs:
            cp.wait_recv()
        for cp in sends:
            cp.wait_send()
        for cp in local:
            cp.wait()


def _run_exchange(ex, name):
    n_in, n_out = len(ex.arrays), len(ex.out_shapes)

    def body(*refs):
        ins, outs, sems = refs[:n_in], refs[n_in:n_in + n_out], refs[n_in + n_out:]
        ex.start(ins, outs, sems)
        ex.wait(ins, outs, sems)

    hbm = pl.BlockSpec(memory_space=pl.ANY)
    return pl.pallas_call(body, name=name, in_specs=[hbm] * n_in, out_specs=[hbm] * n_out, out_shape=ex.out_shapes,
                          scratch_shapes=ex.scratch)(*ex.arrays)


def _gather_exchange(shards, axes, orders):
    out_shapes = []
    for s, ax, order in zip(shards, axes, orders):
        shp = list(s.shape)
        if order == "gate_up_tiles":
            shp = [shp[0], 2, shp[1], 2 * shp[2]]
        else:
            shp[ax] *= N_CHIPS
        out_shapes.append(jax.ShapeDtypeStruct(tuple(shp), s.dtype))

    def describe(i, ins, outs, fx, fy, tx, ty):
        pos, width = _chip_pos(orders[i], fx, fy), shards[i].shape[axes[i]]
        if orders[i] == "gate_up_tiles":
            return ins[i], outs[i].at[:, pos // 2, :, pl.ds(pl.multiple_of((pos % 2) * width, width), width)]
        return ins[i], _block(outs[i], axes[i], pos, width)

    return _Exchange(shards, out_shapes, describe)


def _chip_scatter(grads, axes, orders, name):
    return _run_exchange(_scatter_exchange(grads, axes, orders), name)


def _scatter_exchange(grads, axes, orders, layers=None):
    layers = layers or [None] * len(grads)
    widths, out_shapes = [], {}
    for i, (g, ax, lay) in enumerate(zip(grads, axes, layers)):
        shp = list(g.shape)
        shp[ax] //= N_CHIPS
        widths.append(shp[ax])
        key, lead = (("own", i), (N_CHIPS,)) if lay is None else (("shared", lay[0]), (N_CHIPS, lay[2]))
        out_shapes[key] = jax.ShapeDtypeStruct(lead + tuple(shp), g.dtype)
    keys = list(out_shapes)

    def describe(i, ins, outs, fx, fy, tx, ty):
        lay = layers[i]
        out = outs[keys.index(("own", i) if lay is None else ("shared", lay[0]))]
        slot = out.at[2 * fx + fy] if lay is None else out.at[2 * fx + fy, lay[1]]
        return _block(ins[i], axes[i], _chip_pos(orders[i], tx, ty), widths[i]), slot

    return _Exchange(grads, [out_shapes[k] for k in keys], describe)


def _core_swap(arrays, name):
    n = len(arrays)

    def body(*refs):
        ins, outs = refs[:n], refs[n:2 * n]
        send_sems, recv_sems = refs[2 * n:]
        x, y, c = _axes()
        copies = [pltpu.make_async_remote_copy(src_ref=ins[i], dst_ref=outs[i], send_sem=send_sems.at[i],
                                               recv_sem=recv_sems.at[i], device_id=(x, y, 1 - c), device_id_type=MESH)
                  for i in range(n)]
        for cp in copies:
            cp.start()
        for cp in copies:
            cp.wait_recv()
        for cp in copies:
            cp.wait_send()

    hbm = pl.BlockSpec(memory_space=pl.ANY)
    return pl.pallas_call(
        body, name=name, in_specs=[hbm] * n, out_specs=[hbm] * n,
        out_shape=[jax.ShapeDtypeStruct(a.shape, a.dtype) for a in arrays],
        scratch_shapes=[pltpu.SemaphoreType.DMA((n,)), pltpu.SemaphoreType.DMA((n,))],
    )(*arrays)


SMALL = ("b_mod", "g_attn", "rpb_na", "sink_sw", "t5_table", "gq_ax", "gk_ax", "g_group", "g_ffn", "g_final")
BIG = ("w_in", "w_o", "w_gu", "w_down")
BIG_AXIS = {"w_in": 2, "w_o": 1, "w_gu": 2, "w_down": 1}
BIG_ORDER = {"w_in": "natural", "w_o": "natural", "w_gu": "gate_up_tiles", "w_down": "natural"}
WEIGHTS = ("w_mod", "b_mod", "g_attn", "w_in", "rpb_na", "sink_sw", "t5_table", "gq_ax", "gk_ax", "g_group",
           "w_o", "g_ffn", "w_gu", "w_down", "g_final")


def _pack(arrs):
    flat = jnp.concatenate([a.reshape(-1) for a in arrs])
    n = flat.shape[0]
    padded = -(-n // (8 * LANES)) * (8 * LANES)
    return jnp.pad(flat, (0, padded - n))


def _unpack(flat, like):
    out, off = [], 0
    for a in like:
        out.append(flat[off:off + a.size].reshape(a.shape))
        off += a.size
    return out


def kernel(x, c, w_mod, b_mod, g_attn, w_in, rpb_na, sink_sw, t5_table, gq_ax, gk_ax, g_group, w_o, g_ffn, w_gu, w_down, g_final, loss_target, m_w_mod, m_b_mod, m_g_attn, m_w_in, m_rpb_na, m_sink_sw, m_t5_table, m_gq_ax, m_gk_ax, m_g_group, m_w_o, m_g_ffn, m_w_gu, m_w_down, m_g_final, v_w_mod, v_b_mod, v_g_attn, v_w_in, v_rpb_na, v_sink_sw, v_t5_table, v_gq_ax, v_gk_ax, v_g_group, v_w_o, v_g_ffn, v_w_gu, v_w_down, v_g_final):
    W = dict(w_mod=w_mod, b_mod=b_mod, g_attn=g_attn, w_in=w_in, rpb_na=rpb_na, sink_sw=sink_sw, t5_table=t5_table,
             gq_ax=gq_ax, gk_ax=gk_ax, g_group=g_group, w_o=w_o, g_ffn=g_ffn, w_gu=w_gu, w_down=w_down,
             g_final=g_final)
    M = dict(w_mod=m_w_mod, b_mod=m_b_mod, g_attn=m_g_attn, w_in=m_w_in, rpb_na=m_rpb_na, sink_sw=m_sink_sw,
             t5_table=m_t5_table, gq_ax=m_gq_ax, gk_ax=m_gk_ax, g_group=m_g_group, w_o=m_w_o, g_ffn=m_g_ffn,
             w_gu=m_w_gu, w_down=m_w_down, g_final=m_g_final)
    V = dict(w_mod=v_w_mod, b_mod=v_b_mod, g_attn=v_g_attn, w_in=v_w_in, rpb_na=v_rpb_na, sink_sw=v_sink_sw,
             t5_table=v_t5_table, gq_ax=v_gq_ax, gk_ax=v_gk_ax, g_group=v_g_group, w_o=v_w_o, g_ffn=v_g_ffn,
             w_gu=v_w_gu, w_down=v_w_down, g_final=v_g_final)
    xi, yi, ci = _axes()
    me = 4 * xi + 2 * yi + ci
    chip = 2 * xi + yi
    D = x.shape[-1]
    mod_w = w_mod.shape[2]

    c_all = _allgather_devices(c)
    cond = c_all * (1.0 / (1.0 + jnp.exp(-c_all)))
    cond16 = jnp.pad(cond, ((0, MOD_ROWS - N_DEV), (0, 0))).astype(BF16)
    mod_part = _mod_fwd(cond16, w_mod)
    (mod_all,) = _chip_allgather([mod_part], [2], ["natural"], "allgather_mod")
    mod = lax.dynamic_slice_in_dim(mod_all, me, 1, axis=1)[:, 0, :] + b_mod

    shards = {k: W[k].astype(BF16) for k in BIG}
    (w_in_first,) = _chip_allgather([shards["w_in"][:1]], [BIG_AXIS["w_in"]], [BIG_ORDER["w_in"]], "allgather_w_in0")
    shards["w_in"] = shards["w_in"][1:]
    loss_part, grad_x, slots, small = _local_step(x[0], loss_target[0], mod, w_in_first, None, None, None, g_attn,
                                                  rpb_na, sink_sw, t5_table, gq_ax, gk_ax, g_group, g_ffn, g_final,
                                                  late_shards=shards)

    small_all = _allgather_devices(_pack([small[k] for k in SMALL])[None, :])
    rows = small_all.shape[1] // LANES
    parts = [small_all[k].reshape(rows, LANES) for k in range(N_DEV)]
    pk = lambda d: _pack([d[k] for k in SMALL]).reshape(rows, LANES)
    small_out = [_unpack(o.reshape(-1), [W[k] for k in SMALL]) for o in _adamw(pk(W), pk(M), pk(V), parts, "adamw_small")]

    L = w_mod.shape[0]
    dmod_all = small_all[:, :L * 6 * D].reshape(N_DEV, L, 6 * D)
    dmod_mine = lax.dynamic_slice_in_dim(dmod_all, chip * mod_w, mod_w, axis=2)
    dmod16 = jnp.pad(jnp.transpose(dmod_mine, (1, 0, 2)), ((0, 0), (0, MOD_ROWS - N_DEV), (0, 0))).astype(BF16)
    wmod_out = _wmod_adamw(jnp.transpose(cond16), dmod16, w_mod, m_w_mod, v_w_mod)

    names = list(BIG)
    two_d = lambda a: a.reshape(-1, a.shape[-1])
    mine = [_sum_slots(slots[k].reshape(N_CHIPS, -1, slots[k].shape[-1])) for k in names]
    theirs = _core_swap(mine, "swap_grads")
    big_out = {}
    for k, a, b in zip(names, mine, theirs):
        outs = _adamw(two_d(W[k]), two_d(M[k]), two_d(V[k]), [a, b], "adamw_" + k)
        big_out[k] = [o.reshape(W[k].shape) for o in outs]

    loss = lax.psum(loss_part, ("x", "y", "c"))
    per_kind = []
    for kind in range(4):
        for k in WEIGHTS:
            if k == "w_mod":
                per_kind.append(wmod_out[kind])
            elif k in big_out:
                per_kind.append(big_out[k][kind])
            else:
                per_kind.append(small_out[kind][SMALL.index(k)])
    return (loss, grad_x[None], *per_kind)
```

```python
import functools
import math

import jax
import jax.numpy as jnp
from jax import lax
from jax.experimental import pallas as pl
from jax.experimental.pallas import tpu as pltpu

F32 = jnp.float32
BF16 = jnp.bfloat16
I32 = jnp.int32

DEPTH = 2
HEAD_DIM = 64
GRID_W = 64
NA_HEADS = 4
SW_HEADS = 6
SW_KV_HEADS = 2
AX_HEADS = 6
AX_KV_HEADS = 2
NA_WIN_ROWS = 8
NA_WIN_COLS = 16
SW_RADIUS = 128
T5_BUCKETS = 32
T5_MAX_DIST = 128
ROPE_THETA = 10000.0
EPS = 1e-6
NEG_INF = -1e30
QK_SCALE = HEAD_DIM ** -0.5

NA_W = NA_HEADS * HEAD_DIM
SW_W = SW_HEADS * HEAD_DIM
SW_KV_W = SW_KV_HEADS * HEAD_DIM
AX_W = AX_HEADS * HEAD_DIM
AX_KV_W = AX_KV_HEADS * HEAD_DIM
OFF_QA, OFF_KA, OFF_VA = 0, NA_W, 2 * NA_W
OFF_QB = 3 * NA_W
OFF_KB = OFF_QB + SW_W
OFF_VB = OFF_KB + SW_KV_W
OFF_QC = OFF_VB + SW_KV_W
OFF_KC = OFF_QC + AX_W
OFF_VC = OFF_KC + AX_KV_W
IN_WIDTH = OFF_VC + AX_KV_W
MIX_WIDTH = NA_W + SW_W + AX_W

ADAM_LR = 0.001
ADAM_B1 = 0.9
ADAM_B2 = 0.999
ADAM_EPS = 1e-08
ADAM_WD = 0.01
ADAM_STEP = 10

N_CHIPS = 4
N_DEV = 8
LANES = 128
VMEM_LIMIT_V7X = 56 * 1024 * 1024
ROW_TILE = 512
ROW_TILE_WIDE = 512
TOKEN_CHUNK = 2048
MESH = pl.DeviceIdType.MESH

NT = (((1,), (1,)), ((), ()))
TN = (((0,), (0,)), ((), ()))


def _params(*sem):
    return pltpu.CompilerParams(dimension_semantics=sem if sem else None,
                                vmem_limit_bytes=VMEM_LIMIT_V7X)


def _tile(n, pref, mult=8):
    t = (min(pref, n) // mult) * mult
    while t >= mult:
        if n % t == 0:
            return t
        t -= mult
    return n


def _row_spec(tm, width, col=0):
    return pl.BlockSpec((tm, width), lambda i, *_: (i, col))


def _const_spec(shape):
    nd = len(shape)
    return pl.BlockSpec(shape, lambda *_: (0,) * nd)


def _rsq(ms):
    return lax.rsqrt(ms + EPS)


def _rope_tables(S):
    rows = S // GRID_W
    axis_dim = HEAD_DIM // 2
    quarter = axis_dim // 2
    lane = jnp.arange(LANES)
    freq = (ROPE_THETA ** (-(2 * (lane % quarter)).astype(F32) / axis_dim))[None, :]
    by_row = ((lane % HEAD_DIM) < axis_dim)[None, None, :]
    first = ((lane % axis_dim) < quarter)[None, :]
    ang_r = jnp.arange(rows, dtype=F32)[:, None] * freq
    ang_c = jnp.arange(GRID_W, dtype=F32)[:, None] * freq

    def table(fr, fc):
        t = jnp.where(by_row, fr[:, None, :], fc[None, :, :])
        return t.reshape(S, LANES)

    sin_r, sin_c = jnp.sin(ang_r), jnp.sin(ang_c)
    return (table(jnp.cos(ang_r), jnp.cos(ang_c)),
            table(jnp.where(first, -sin_r, 0.0), jnp.where(first, -sin_c, 0.0)),
            table(jnp.where(first, 0.0, sin_r), jnp.where(first, 0.0, sin_c)))


def _pair_sum(v):
    lane = lax.broadcasted_iota(I32, v.shape, 1)
    lo = lane < HEAD_DIM
    s_lo = jnp.sum(jnp.where(lo, v, 0.0), axis=-1, keepdims=True)
    s_hi = jnp.sum(jnp.where(lo, 0.0, v), axis=-1, keepdims=True)
    return jnp.where(lo, s_lo, s_hi)


def _rope(t, cos, sa, sb):
    return t * cos + pltpu.roll(t, LANES - 16, 1) * sa + pltpu.roll(t, 16, 1) * sb


def _rope_t(t, cos, sa, sb):
    return t * cos + pltpu.roll(t * sa, 16, 1) + pltpu.roll(t * sb, LANES - 16, 1)


def _qk_prep_chunk(x, g128, cos, sa, sb):
    r = _rsq(_pair_sum(x * x) * (1.0 / HEAD_DIM))
    return _rope(x * r * g128, cos, sa, sb)


def _qk_prep_bwd_chunk(x, dy, g128, cos, sa, sb):
    dn = _rope_t(dy, cos, sa, sb)
    r = _rsq(_pair_sum(x * x) * (1.0 / HEAD_DIM))
    xhat = x * r
    dg = jnp.sum(dn * xhat, axis=0, keepdims=True)
    dxh = dn * g128
    dx = r * (dxh - xhat * (_pair_sum(dxh * xhat) * (1.0 / HEAD_DIM)))
    return dx, dg


def _ln_mod(xv, g, sc, sh):
    r = _rsq(jnp.mean(xv * xv, axis=-1, keepdims=True))
    return xv * r * g * (1.0 + sc) + sh


def _inproj_fwd(x, g, sc, sh, w, gq128, gk128, rope):
    S, D = x.shape
    tm = _tile(S, ROW_TILE)
    cos, sa, sb = rope

    def body(x_ref, g_ref, sc_ref, sh_ref, w_ref, gq_ref, gk_ref, cos_ref, sa_ref, sb_ref,
             h_ref, proj_ref, qc_ref, kc_ref):
        hb = _ln_mod(x_ref[...], g_ref[...], sc_ref[...], sh_ref[...]).astype(BF16)
        h_ref[...] = hb
        acc = jnp.dot(hb, w_ref[...], preferred_element_type=F32)
        proj_ref[...] = acc.astype(BF16)
        c, a, b = cos_ref[...], sa_ref[...], sb_ref[...]
        for j in range(AX_W // LANES):
            xq = acc[:, OFF_QC + j * LANES: OFF_QC + (j + 1) * LANES]
            qc_ref[:, j * LANES:(j + 1) * LANES] = (
                _qk_prep_chunk(xq, gq_ref[...], c, a, b) * QK_SCALE).astype(BF16)
        for j in range(AX_KV_W // LANES):
            xk = acc[:, OFF_KC + j * LANES: OFF_KC + (j + 1) * LANES]
            kc_ref[:, j * LANES:(j + 1) * LANES] = _qk_prep_chunk(xk, gk_ref[...], c, a, b).astype(BF16)

    vec = _const_spec((1, D))
    v128 = _const_spec((1, LANES))
    return pl.pallas_call(
        body, name="inproj_fwd", grid=(S // tm,),
        in_specs=[_row_spec(tm, D), vec, vec, vec, _const_spec(w.shape), v128, v128,
                  _row_spec(tm, LANES), _row_spec(tm, LANES), _row_spec(tm, LANES)],
        out_specs=[_row_spec(tm, D), _row_spec(tm, IN_WIDTH), _row_spec(tm, AX_W), _row_spec(tm, AX_KV_W)],
        out_shape=[jax.ShapeDtypeStruct((S, D), BF16), jax.ShapeDtypeStruct((S, IN_WIDTH), BF16),
                   jax.ShapeDtypeStruct((S, AX_W), BF16), jax.ShapeDtypeStruct((S, AX_KV_W), BF16)],
        compiler_params=_params("parallel"),
    )(x, g, sc, sh, w, gq128, gk128, cos, sa, sb)


class _Band:
    def __init__(self, kind, S):
        self.kind = kind
        self.S = S
        if kind == "na":
            self.hq, self.g, self.halo = NA_HEADS, NA_HEADS, (NA_WIN_ROWS // 2) * GRID_W
            self.q_off, self.k_off, self.v_off = OFF_QA, OFF_KA, OFF_VA
        else:
            self.hq, self.g, self.halo = SW_HEADS, SW_KV_HEADS, SW_RADIUS
            self.q_off, self.k_off, self.v_off = OFF_QB, OFF_KB, OFF_VB
        self.bq = 2 * self.halo
        self.bk = self.bq + 2 * self.halo
        self.nb = S // self.bq
        self.rep = self.hq // self.g
        self.qw = self.hq * HEAD_DIM
        self.kw = self.g * HEAD_DIM

    def kv_of(self, h):
        return (h // 2, h % 2) if self.kind == "na" else (0, h // self.rep)

    def mask(self, n, transposed=False):
        shape = (self.bk, self.bq) if transposed else (self.bq, self.bk)
        qi = lax.broadcasted_iota(I32, shape, 1 if transposed else 0) + n * self.bq
        kj = lax.broadcasted_iota(I32, shape, 0 if transposed else 1) + (n * self.bq - self.halo)
        if self.kind == "sw":
            return (jnp.abs(kj - qi) <= SW_RADIUS) & (kj >= 0) & (kj < self.S)
        rows = self.S // GRID_W
        r, col = qi >> 6, qi & (GRID_W - 1)
        kr, kc = kj >> 6, kj & (GRID_W - 1)
        rs = jnp.clip(r - NA_WIN_ROWS // 2, 0, rows - NA_WIN_ROWS)
        cs = jnp.clip(col - NA_WIN_COLS // 2, 0, GRID_W - NA_WIN_COLS)
        return (kr >= rs) & (kr < rs + NA_WIN_ROWS) & (kc >= cs) & (kc < cs + NA_WIN_COLS)

    def qkv_specs(self):
        ratio = self.bq // self.halo
        last = self.S // self.halo - 1
        q = pl.BlockSpec((self.bq, self.qw), lambda n, o=self.q_off // self.qw: (n, o))
        specs = [q]
        for off in (self.k_off, self.v_off):
            o = off // self.kw
            specs.append(pl.BlockSpec((self.halo, self.kw), lambda n, o=o: (jnp.maximum(n * ratio - 1, 0), o)))
            specs.append(pl.BlockSpec((self.bq, self.kw), lambda n, o=o: (n, o)))
            specs.append(pl.BlockSpec((self.halo, self.kw), lambda n, o=o: (jnp.minimum((n + 1) * ratio, last), o)))
        return specs


def _band_kv_variants(bd, refs, fill):
    out = []
    for blk in range(bd.kw // LANES):
        cols = slice(blk * LANES, (blk + 1) * LANES)
        out.append(_half_variants(jnp.concatenate([r[:, cols] for r in refs], axis=0), fill))
    return out


def _band_fwd(bd, proj, bias, sink, gg):
    S = bd.S
    has_sink = sink is not None

    def body(*refs):
        q_ref, kp, km, kn, vp, vm, vn, bias_ref = refs[:8]
        k = 8
        sink_ref = None
        if has_sink:
            sink_ref = refs[k]
            k += 1
        gg_ref, raw_ref, yn_ref, o_scr = refs[k:k + 4]
        mask = bd.mask(pl.program_id(0))
        lo = _left_half((bd.bq, LANES))
        kzs, vzs = _band_kv_variants(bd, (kp, km, kn), 0.0), _band_kv_variants(bd, (vp, vm, vn), 1.0)
        for pr in range(bd.hq // 2):
            cols = slice(pr * LANES, (pr + 1) * LANES)
            qp = q_ref[:, cols] * QK_SCALE
            acc = []
            for half in range(2):
                h = 2 * pr + half
                blk, src = bd.kv_of(h)
                s = lax.dot_general(qp, kzs[blk][src][half], NT, preferred_element_type=F32) + bias_ref[h]
                s = jnp.where(mask, s, NEG_INF)
                m = jnp.max(s, axis=-1, keepdims=True)
                if has_sink:
                    m = jnp.maximum(m, sink_ref[0:1, h:h + 1])
                a = jnp.dot(jnp.exp(s - m).astype(BF16), vzs[blk][src][half], preferred_element_type=F32)
                if has_sink:
                    e = jnp.exp(sink_ref[0:1, h:h + 1] - m)
                    a = a + (jnp.where(lo, 0.0, e) if half == 0 else jnp.where(lo, e, 0.0))
                acc.append(a)
            o_scr[:, cols] = jnp.where(lo, acc[0] / pltpu.roll(acc[0], HEAD_DIM, 1),
                                       acc[1] / pltpu.roll(acc[1], HEAD_DIM, 1))
        o = o_scr[...]
        raw_ref[...] = o.astype(BF16)
        r = _rsq(jnp.mean(o * o, axis=-1, keepdims=True))
        yn_ref[...] = (o * r * gg_ref[...]).astype(BF16)

    in_specs = bd.qkv_specs() + [_const_spec(bias.shape)]
    args = [proj] * 7 + [bias]
    if has_sink:
        in_specs.append(_const_spec(sink.shape))
        args.append(sink)
    in_specs.append(_const_spec(gg.shape))
    args.append(gg)
    out = jax.ShapeDtypeStruct((S, bd.qw), BF16)
    return pl.pallas_call(
        body, name=bd.kind + "_fwd", grid=(bd.nb,), in_specs=in_specs,
        out_specs=[_row_spec(bd.bq, bd.qw), _row_spec(bd.bq, bd.qw)], out_shape=[out, out],
        scratch_shapes=[pltpu.VMEM((bd.bq, bd.qw), F32)],
        compiler_params=_params("parallel"),
    )(*args)


def _band_bwd(bd, proj, bias, sink, dy):
    S = bd.S
    has_sink = sink is not None

    def body(*refs):
        q_ref, kp, km, kn, vp, vm, vn, bias_ref = refs[:8]
        k = 8
        sink_ref = None
        if has_sink:
            sink_ref = refs[k]
            k += 1
        do_ref = refs[k]
        dq_ref, dkm, dvm, dkp, dvp, dkn, dvn, dbias_ref = refs[k + 1:k + 9]
        k += 9
        dsink_ref = None
        if has_sink:
            dsink_ref = refs[k]
            k += 1
        dk_scr, dv_scr = refs[k:k + 2]
        n = pl.program_id(0)

        @pl.when(n == 0)
        def _():
            dbias_ref[...] = jnp.zeros_like(dbias_ref)
            if has_sink:
                dsink_ref[...] = jnp.zeros_like(dsink_ref)

        mask = bd.mask(n, transposed=True)
        lane = lax.broadcasted_iota(I32, (1, LANES), 1)
        kzs, vzs = _band_kv_variants(bd, (kp, km, kn), 0.0), _band_kv_variants(bd, (vp, vm, vn), 0.0)
        nblk = bd.kw // LANES
        dk, dv = [None] * nblk, [None] * nblk
        for pr in range(bd.hq // 2):
            cols = slice(pr * LANES, (pr + 1) * LANES)
            qp, dop = q_ref[:, cols] * QK_SCALE, do_ref[:, cols]
            qz, doz = _half_variants(qp), _half_variants(dop)
            dq = None
            for half in range(2):
                h = 2 * pr + half
                blk, dst = bd.kv_of(h)
                kz, vz = kzs[blk][dst][half], vzs[blk][dst][half]
                s = lax.dot_general(kz, qp, NT, preferred_element_type=F32) + bias_ref[h]
                s = jnp.where(mask, s, NEG_INF)
                m = jnp.max(s, axis=0, keepdims=True)
                if has_sink:
                    m = jnp.maximum(m, sink_ref[0:1, h:h + 1])
                p = jnp.exp(s - m)
                l = jnp.sum(p, axis=0, keepdims=True)
                if has_sink:
                    e = jnp.exp(sink_ref[0:1, h:h + 1] - m)
                    l = l + e
                inv = 1.0 / l
                pn = p * inv
                dp = lax.dot_general(vz, dop, NT, preferred_element_type=F32)
                delta = jnp.sum(pn * dp, axis=0, keepdims=True)
                ds = pn * (dp - delta)
                dbias_ref[h] += ds
                if has_sink:
                    dsink_ref[...] += jnp.where(lane == h, -jnp.sum(e * inv * delta, axis=1, keepdims=True), 0.0)
                dsb = ds.astype(BF16)
                a = jnp.dot(pn.astype(BF16), doz[half][dst], preferred_element_type=F32)
                b = jnp.dot(dsb, qz[half][dst], preferred_element_type=F32)
                d = lax.dot_general(dsb, kz, TN, preferred_element_type=F32)
                dv[blk] = a if dv[blk] is None else dv[blk] + a
                dk[blk] = b if dk[blk] is None else dk[blk] + b
                dq = d if dq is None else dq + d
            dq_ref[:, cols] = dq * QK_SCALE
        for blk in range(nblk):
            cols = slice(blk * LANES, (blk + 1) * LANES)
            dk_scr[:, cols] = dk[blk]
            dv_scr[:, cols] = dv[blk]
        h0, h1 = bd.halo, bd.halo + bd.bq
        dkp[0] = dk_scr[0:h0, :]
        dkm[...] = dk_scr[h0:h1, :]
        dkn[0] = dk_scr[h1:bd.bk, :]
        dvp[0] = dv_scr[0:h0, :]
        dvm[...] = dv_scr[h0:h1, :]
        dvn[0] = dv_scr[h1:bd.bk, :]

    in_specs = bd.qkv_specs() + [_const_spec(bias.shape)]
    args = [proj] * 7 + [bias]
    if has_sink:
        in_specs.append(_const_spec(sink.shape))
        args.append(sink)
    in_specs.append(_row_spec(bd.bq, bd.qw))
    args.append(dy)
    halo_spec = pl.BlockSpec((1, bd.halo, bd.kw), lambda n: (n, 0, 0))
    halo_shape = jax.ShapeDtypeStruct((bd.nb, bd.halo, bd.kw), F32)
    main_shape = jax.ShapeDtypeStruct((S, bd.kw), F32)
    out_specs = [_row_spec(bd.bq, bd.qw), _row_spec(bd.bq, bd.kw), _row_spec(bd.bq, bd.kw),
                 halo_spec, halo_spec, halo_spec, halo_spec, _const_spec(bias.shape)]
    out_shape = [jax.ShapeDtypeStruct((S, bd.qw), F32), main_shape, main_shape,
                 halo_shape, halo_shape, halo_shape, halo_shape, jax.ShapeDtypeStruct(bias.shape, F32)]
    if has_sink:
        out_specs.append(_const_spec((1, LANES)))
        out_shape.append(jax.ShapeDtypeStruct((1, LANES), F32))
    return pl.pallas_call(
        body, name=bd.kind + "_bwd", grid=(bd.nb,), in_specs=in_specs, out_specs=out_specs, out_shape=out_shape,
        scratch_shapes=[pltpu.VMEM((bd.bk, bd.kw), F32), pltpu.VMEM((bd.bk, bd.kw), F32)],
        compiler_params=_params("arbitrary"),
    )(*args)


def _halo_to_rows(prev, nxt):
    nb, halo, w = prev.shape
    z = jnp.zeros((1, halo, w), prev.dtype)
    first = jnp.concatenate([z, nxt[:-1]], axis=0)
    second = jnp.concatenate([prev[1:], z], axis=0)
    return jnp.concatenate([first, second], axis=1).reshape(nb * 2 * halo, w)


AX_PAIRS = AX_W // LANES


AX_FWD_BLOCKS = (1024, 2048)
AX_BWD_BLOCKS = (1024, 1024)


def _ax_blocks(S, blocks):
    return _tile(S, blocks[0]), _tile(S, blocks[1])


def _left_half(shape):
    return lax.broadcasted_iota(I32, shape, len(shape) - 1) < HEAD_DIM


def _as_row(a):
    return jnp.transpose(a)[0:1, :]


def _half_variants(a, fill=0.0):
    lo = _left_half(a.shape)
    other = jnp.full_like(a, fill)
    swapped = pltpu.roll(a, HEAD_DIM, 1)
    return ((jnp.where(lo, a, other), jnp.where(lo, other, swapped)),
            (jnp.where(lo, swapped, other), jnp.where(lo, other, a)))


def _split_rider_refs(refs, n_in, n_out, rider):
    r_in, r_out = (len(rider.arrays), len(rider.out_shapes)) if rider else (0, 0)
    a, b, c = n_in + r_in, n_in + r_in + n_out, n_in + r_in + n_out + r_out
    n_sems = 3 if rider else 0
    return refs[:n_in], refs[n_in:a], refs[a:b], refs[b:c], refs[c:len(refs) - n_sems], refs[len(refs) - n_sems:]


def _ax_fwd(qc, kc, proj, gg, rider=None):
    S = qc.shape[0]
    bq, bk = _ax_blocks(S, AX_FWD_BLOCKS)
    nq, nk = S // bq, S // bk
    rep = AX_HEADS // AX_KV_HEADS

    def body(*refs):
        (q_ref, k_ref, v_ref, gg_ref), r_ins, (raw_ref, yn_ref, lse_ref), r_outs, (m_scr, acc_scr), r_sems = (
            _split_rider_refs(refs, 4, 3, rider))
        qi, kv = pl.program_id(0), pl.program_id(1)

        if rider:
            @pl.when((qi == 0) & (kv == 0))
            def _():
                rider.start(r_ins, r_outs, r_sems)

        @pl.when(kv == 0)
        def _():
            m_scr[...] = jnp.full(m_scr.shape, NEG_INF, F32)
            acc_scr[...] = jnp.zeros_like(acc_scr)

        kz, vz = _half_variants(k_ref[...]), _half_variants(v_ref[...], 1.0)
        for pr in range(AX_PAIRS):
            qp = q_ref[:, pr * LANES:(pr + 1) * LANES]
            for half in range(2):
                h = 2 * pr + half
                g = h // rep
                s = lax.dot_general(qp, kz[g][half], NT, preferred_element_type=F32)
                m_prev = m_scr[h]
                m_new = jnp.maximum(m_prev, jnp.max(s, axis=-1, keepdims=True))
                p = jnp.exp(s - jnp.tile(m_new, (1, bk // LANES)))
                acc_scr[h] = jnp.exp(m_prev - m_new) * acc_scr[h] + jnp.dot(
                    p.astype(BF16), vz[g][half], preferred_element_type=F32)
                m_scr[h] = m_new

        @pl.when(kv == nk - 1)
        def _():
            lo = _left_half((bq, LANES))
            ssq = jnp.zeros((bq, 1), F32)
            for pr in range(AX_PAIRS):
                a0, a1 = acc_scr[2 * pr], acc_scr[2 * pr + 1]
                r0, r1 = pltpu.roll(a0, HEAD_DIM, 1), pltpu.roll(a1, HEAD_DIM, 1)
                lse_ref[2 * pr] = _as_row(m_scr[2 * pr] + jnp.log(jnp.where(lo, r0, a0)))
                lse_ref[2 * pr + 1] = _as_row(m_scr[2 * pr + 1] + jnp.log(jnp.where(lo, a1, r1)))
                o = jnp.where(lo, a0 / r0, a1 / r1)
                acc_scr[pr] = o
                ssq = ssq + jnp.sum(o * o, axis=-1, keepdims=True)
            r = _rsq(ssq * (1.0 / AX_W))
            for pr in range(AX_PAIRS):
                cols = slice(pr * LANES, (pr + 1) * LANES)
                o = acc_scr[pr]
                raw_ref[:, cols] = o.astype(BF16)
                yn_ref[:, cols] = (o * r * gg_ref[:, cols]).astype(BF16)

        if rider:
            @pl.when((qi == nq - 1) & (kv == nk - 1))
            def _():
                rider.wait(r_ins, r_outs, r_sems)

    out = jax.ShapeDtypeStruct((S, AX_W), BF16)
    hbm = pl.BlockSpec(memory_space=pl.ANY)
    r_arrays, r_shapes, r_scratch = (rider.arrays, rider.out_shapes, rider.scratch) if rider else ([], [], [])
    res = pl.pallas_call(
        body, name="ax_fwd_gather" if rider else "ax_fwd", grid=(nq, nk),
        in_specs=[pl.BlockSpec((bq, AX_W), lambda i, j: (i, 0)),
                  pl.BlockSpec((bk, AX_KV_W), lambda i, j: (j, 0)),
                  pl.BlockSpec((bk, AX_KV_W), lambda i, j: (j, OFF_VC // AX_KV_W)),
                  _const_spec(gg.shape)] + [hbm] * len(r_arrays),
        out_specs=[pl.BlockSpec((bq, AX_W), lambda i, j: (i, 0)),
                   pl.BlockSpec((bq, AX_W), lambda i, j: (i, 0)),
                   pl.BlockSpec((AX_HEADS, 1, bq), lambda i, j: (0, 0, i))] + [hbm] * len(r_shapes),
        out_shape=[out, out, jax.ShapeDtypeStruct((AX_HEADS, 1, S), F32)] + r_shapes,
        scratch_shapes=[pltpu.VMEM((AX_HEADS, bq, LANES), F32), pltpu.VMEM((AX_HEADS, bq, LANES), F32)] + r_scratch,
        compiler_params=_params("arbitrary", "arbitrary"),
    )(qc, kc, proj, gg, *r_arrays)
    return res[0], res[1], res[2], list(res[3:])


def _ax_delta(dy, raw):
    S = dy.shape[0]
    tm = _tile(S, ROW_TILE)

    def body(do_ref, o_ref, delta_ref):
        lo = _left_half((tm, LANES))
        for pr in range(AX_PAIRS):
            cols = slice(pr * LANES, (pr + 1) * LANES)
            prod = do_ref[:, cols].astype(F32) * o_ref[:, cols].astype(F32)
            left = jnp.sum(jnp.where(lo, prod, 0.0), axis=-1, keepdims=True)
            right = jnp.sum(jnp.where(lo, 0.0, prod), axis=-1, keepdims=True)
            delta_ref[2 * pr] = _as_row(jnp.broadcast_to(left, (tm, LANES)))
            delta_ref[2 * pr + 1] = _as_row(jnp.broadcast_to(right, (tm, LANES)))

    return pl.pallas_call(
        body, name="ax_delta", grid=(S // tm,), in_specs=[_row_spec(tm, AX_W), _row_spec(tm, AX_W)],
        out_specs=pl.BlockSpec((AX_HEADS, 1, tm), lambda i: (0, 0, i)),
        out_shape=jax.ShapeDtypeStruct((AX_HEADS, 1, S), F32), compiler_params=_params("parallel"),
    )(dy, raw)


def _ax_bwd(qc, kc, proj, dy, lse_row, delta_row, rider=None):
    S = qc.shape[0]
    bq, bk = _ax_blocks(S, AX_BWD_BLOCKS)
    nq, nk = S // bq, S // bk
    rep = AX_HEADS // AX_KV_HEADS

    def body(*refs):
        ((q_ref, k_ref, v_ref, do_ref, lse_ref, delta_ref), r_ins, (dk_ref, dv_ref, dq_hbm), r_outs, (dq_scr, sem),
         r_sems) = _split_rider_refs(refs, 6, 3, rider)
        j, i = pl.program_id(0), pl.program_id(1)

        if rider:
            @pl.when((j == 0) & (i == 0))
            def _():
                rider.start(r_ins, r_outs, r_sems)

        @pl.when(i == 0)
        def _():
            dk_ref[...] = jnp.zeros_like(dk_ref)
            dv_ref[...] = jnp.zeros_like(dv_ref)

        @pl.when(j == 0)
        def _():
            dq_scr[i] = jnp.zeros((bq, AX_W), F32)

        kz, vz = _half_variants(k_ref[...]), _half_variants(v_ref[...])
        dk, dv = None, None
        for pr in range(AX_PAIRS):
            cols = slice(pr * LANES, (pr + 1) * LANES)
            qp, dop = q_ref[:, cols], do_ref[:, cols]
            qz, doz = _half_variants(qp), _half_variants(dop)
            dq = None
            for half in range(2):
                h = 2 * pr + half
                g = h // rep
                s_t = lax.dot_general(kz[g][half], qp, NT, preferred_element_type=F32)
                p_t = jnp.exp(s_t - lse_ref[h])
                dp_t = lax.dot_general(vz[g][half], dop, NT, preferred_element_type=F32)
                ds_t = (p_t * (dp_t - delta_ref[h])).astype(BF16)
                a = jnp.dot(p_t.astype(BF16), doz[half][g], preferred_element_type=F32)
                b = jnp.dot(ds_t, qz[half][g], preferred_element_type=F32)
                d = lax.dot_general(ds_t, kz[g][half], TN, preferred_element_type=F32)
                dv = a if dv is None else dv + a
                dk = b if dk is None else dk + b
                dq = d if dq is None else dq + d
            dq_scr[i, :, cols] += dq
        dv_ref[...] += dv
        dk_ref[...] += dk

        @pl.when(j == nk - 1)
        def _():
            dq_scr[i] = dq_scr[i] * QK_SCALE
            out = pltpu.make_async_copy(dq_scr.at[i], dq_hbm.at[pl.ds(pl.multiple_of(i * bq, bq), bq), :], sem)
            out.start()
            out.wait()

        if rider:
            @pl.when((j == nk - 1) & (i == nq - 1))
            def _():
                rider.wait(r_ins, r_outs, r_sems)

    qspec = pl.BlockSpec((bq, AX_W), lambda j, i: (i, 0))
    kspec = pl.BlockSpec((bk, AX_KV_W), lambda j, i: (j, 0))
    stat = pl.BlockSpec((AX_HEADS, 1, bq), lambda j, i: (0, 0, i))
    out = jax.ShapeDtypeStruct((S, AX_KV_W), F32)
    hbm = pl.BlockSpec(memory_space=pl.ANY)
    r_arrays, r_shapes, r_scratch = (rider.arrays, rider.out_shapes, rider.scratch) if rider else ([], [], [])
    res = pl.pallas_call(
        body, name="ax_bwd_scatter" if rider else "ax_bwd", grid=(nk, nq),
        in_specs=[qspec, kspec, pl.BlockSpec((bk, AX_KV_W), lambda j, i: (j, OFF_VC // AX_KV_W)),
                  qspec, stat, stat] + [hbm] * len(r_arrays),
        out_specs=[kspec, kspec, hbm] + [hbm] * len(r_shapes),
        out_shape=[out, out, jax.ShapeDtypeStruct((S, AX_W), F32)] + r_shapes,
        scratch_shapes=[pltpu.VMEM((nq, bq, AX_W), F32), pltpu.SemaphoreType.DMA] + r_scratch,
        compiler_params=_params("arbitrary", "arbitrary"),
    )(qc, kc, proj, dy, lse_row, delta_row, *r_arrays)
    return res[2], res[0], res[1], list(res[3:])


def _oproj_fwd(x, yna, ynb, ync, w, gt):
    S, D = x.shape
    tm = _tile(S, ROW_TILE)

    def body(x_ref, a_ref, b_ref, c_ref, w_ref, gt_ref, x1_ref, ao_ref, yn_ref):
        yn_ref[:, 0:NA_W] = a_ref[...]
        yn_ref[:, NA_W:NA_W + SW_W] = b_ref[...]
        yn_ref[:, NA_W + SW_W:MIX_WIDTH] = c_ref[...]
        acc = jnp.dot(yn_ref[...], w_ref[...], preferred_element_type=F32)
        ao_ref[...] = acc.astype(BF16)
        x1_ref[...] = x_ref[...] + gt_ref[...] * acc

    return pl.pallas_call(
        body, name="oproj_fwd", grid=(S // tm,),
        in_specs=[_row_spec(tm, D), _row_spec(tm, NA_W), _row_spec(tm, SW_W), _row_spec(tm, AX_W),
                  _const_spec(w.shape), _const_spec((1, D))],
        out_specs=[_row_spec(tm, D), _row_spec(tm, D), _row_spec(tm, MIX_WIDTH)],
        out_shape=[jax.ShapeDtypeStruct((S, D), F32), jax.ShapeDtypeStruct((S, D), BF16),
                   jax.ShapeDtypeStruct((S, MIX_WIDTH), BF16)],
        compiler_params=_params("parallel"),
    )(x, yna, ynb, ync, w, gt)


def _gu_fwd(x, g, sc, sh, w):
    S, D = x.shape
    tn = w.shape[2] // 2
    F2 = 4 * tn
    tm = _tile(S, ROW_TILE)

    def body(x_ref, g_ref, sc_ref, sh_ref, w_ref, h_ref, gu_ref, act_ref):
        @pl.when(pl.program_id(1) == 0)
        def _():
            h_ref[...] = _ln_mod(x_ref[...], g_ref[...], sc_ref[...], sh_ref[...]).astype(BF16)

        acc = jnp.dot(h_ref[...], w_ref[pl.program_id(1)], preferred_element_type=F32)
        gu_ref[...] = acc.astype(BF16)
        gate, up = acc[:, :tn], acc[:, tn:]
        act_ref[...] = (gate * (1.0 / (1.0 + jnp.exp(-gate))) * up).astype(BF16)

    vec = pl.BlockSpec((1, D), lambda i, j: (0, 0))
    return pl.pallas_call(
        body, name="gu_fwd", grid=(S // tm, 2),
        in_specs=[pl.BlockSpec((tm, D), lambda i, j: (i, 0)), vec, vec, vec,
                  pl.BlockSpec((2, D, 2 * tn), lambda i, j: (0, 0, 0))],
        out_specs=[pl.BlockSpec((tm, D), lambda i, j: (i, 0)), pl.BlockSpec((tm, 2 * tn), lambda i, j: (i, j)),
                   pl.BlockSpec((tm, tn), lambda i, j: (i, j))],
        out_shape=[jax.ShapeDtypeStruct((S, D), BF16), jax.ShapeDtypeStruct((S, F2), BF16),
                   jax.ShapeDtypeStruct((S, F2 // 2), BF16)],
        compiler_params=_params("parallel", "arbitrary"),
    )(x, g, sc, sh, w)


def _down_fwd(x, act, w, gt):
    S, D = x.shape
    F = act.shape[1]
    tm = _tile(S, ROW_TILE)

    def body(x_ref, a_ref, w_ref, gt_ref, x2_ref, fo_ref):
        acc = jnp.dot(a_ref[...], w_ref[...], preferred_element_type=F32)
        fo_ref[...] = acc.astype(BF16)
        x2_ref[...] = x_ref[...] + gt_ref[...] * acc

    return pl.pallas_call(
        body, name="down_fwd", grid=(S // tm,),
        in_specs=[_row_spec(tm, D), _row_spec(tm, F), _const_spec(w.shape), _const_spec((1, D))],
        out_specs=[_row_spec(tm, D), _row_spec(tm, D)],
        out_shape=[jax.ShapeDtypeStruct((S, D), F32), jax.ShapeDtypeStruct((S, D), BF16)],
        compiler_params=_params("parallel"),
    )(x, act, w, gt)


def _final_loss(x, g, target):
    S, D = x.shape
    tm = _tile(S, ROW_TILE)

    def body(x_ref, g_ref, t_ref, dx_ref, loss_ref, dg_ref):
        @pl.when(pl.program_id(0) == 0)
        def _():
            loss_ref[...] = jnp.zeros_like(loss_ref)
            dg_ref[...] = jnp.zeros_like(dg_ref)

        xv = x_ref[...]
        r = _rsq(jnp.mean(xv * xv, axis=-1, keepdims=True))
        xhat = xv * r
        err = xhat * g_ref[...] - t_ref[...]
        loss_ref[...] += 0.5 * jnp.sum(jnp.mean(err * err, axis=-1, keepdims=True), axis=0, keepdims=True)
        dy = err * (1.0 / D)
        dg_ref[...] += jnp.sum(dy * xhat, axis=0, keepdims=True)
        dxh = dy * g_ref[...]
        dx_ref[...] = r * (dxh - xhat * jnp.mean(dxh * xhat, axis=-1, keepdims=True))

    return pl.pallas_call(
        body, name="final_loss", grid=(S // tm,),
        in_specs=[_row_spec(tm, D), _const_spec((1, D)), _row_spec(tm, D)],
        out_specs=[_row_spec(tm, D), _const_spec((1, LANES)), _const_spec((1, D))],
        out_shape=[jax.ShapeDtypeStruct((S, D), F32), jax.ShapeDtypeStruct((1, LANES), F32),
                   jax.ShapeDtypeStruct((1, D), F32)],
        compiler_params=_params("arbitrary"),
    )(x, g, target)


def _ffn_bwd1(dx2, fo, gt, w_down, gu):
    S, D = dx2.shape
    F2 = gu.shape[1]
    tn = F2 // 4
    tm = _tile(S, ROW_TILE)

    def body(dx_ref, fo_ref, gt_ref, w_ref, gu_ref, dfo_ref, dgu_ref, dgt_ref):
        i, j = pl.program_id(0), pl.program_id(1)

        @pl.when((i == 0) & (j == 0))
        def _():
            dgt_ref[...] = jnp.zeros_like(dgt_ref)

        @pl.when(j == 0)
        def _():
            dxv = dx_ref[...]
            dfo_ref[...] = (dxv * gt_ref[...]).astype(BF16)
            dgt_ref[...] += jnp.sum(dxv * fo_ref[...].astype(F32), axis=0, keepdims=True)

        dact = lax.dot_general(dfo_ref[...], w_ref[j], NT, preferred_element_type=F32)
        gate = gu_ref[:, :tn].astype(F32)
        up = gu_ref[:, tn:].astype(F32)
        sig = 1.0 / (1.0 + jnp.exp(-gate))
        dgu_ref[:, :tn] = (dact * up * (sig * (1.0 + gate * (1.0 - sig)))).astype(BF16)
        dgu_ref[:, tn:] = (dact * (gate * sig)).astype(BF16)

    vec = pl.BlockSpec((1, D), lambda i, j: (0, 0))
    row = pl.BlockSpec((tm, D), lambda i, j: (i, 0))
    return pl.pallas_call(
        body, name="ffn_bwd1", grid=(S // tm, 2),
        in_specs=[row, row, vec, pl.BlockSpec((2, tn, D), lambda i, j: (0, 0, 0)),
                  pl.BlockSpec((tm, 2 * tn), lambda i, j: (i, j))],
        out_specs=[row, pl.BlockSpec((tm, 2 * tn), lambda i, j: (i, j)), vec],
        out_shape=[jax.ShapeDtypeStruct((S, D), BF16), jax.ShapeDtypeStruct((S, F2), BF16),
                   jax.ShapeDtypeStruct((1, D), F32)],
        compiler_params=_params("arbitrary", "arbitrary"),
    )(dx2, fo, gt, w_down.reshape(2, tn, D), gu)


def _nt_ln_bwd(a, w, x, g, sc, dres, name):
    S, D = x.shape
    K = a.shape[1]
    tm = _tile(S, ROW_TILE_WIDE)

    def body(a_ref, w_ref, x_ref, g_ref, sc_ref, dres_ref, dx_ref, dsh_ref, dsc_ref, dg_ref):
        @pl.when(pl.program_id(0) == 0)
        def _():
            dsh_ref[...] = jnp.zeros_like(dsh_ref)
            dsc_ref[...] = jnp.zeros_like(dsc_ref)
            dg_ref[...] = jnp.zeros_like(dg_ref)

        if len(w.shape) == 2:
            dh = lax.dot_general(a_ref[...], w_ref[...], NT, preferred_element_type=F32)
        else:
            kt = w.shape[2]
            dh = sum(lax.dot_general(a_ref[:, t * kt:(t + 1) * kt], w_ref[t], NT, preferred_element_type=F32)
                     for t in range(w.shape[0]))
        xv = x_ref[...]
        r = _rsq(jnp.mean(xv * xv, axis=-1, keepdims=True))
        xhat = xv * r
        gv = g_ref[...]
        dsh_ref[...] += jnp.sum(dh, axis=0, keepdims=True)
        dsc_ref[...] += jnp.sum(dh * (xhat * gv), axis=0, keepdims=True)
        dn = dh * (1.0 + sc_ref[...])
        dg_ref[...] += jnp.sum(dn * xhat, axis=0, keepdims=True)
        dxh = dn * gv
        dx_ref[...] = dres_ref[...] + r * (dxh - xhat * jnp.mean(dxh * xhat, axis=-1, keepdims=True))

    vec = _const_spec((1, D))
    vshape = jax.ShapeDtypeStruct((1, D), F32)
    return pl.pallas_call(
        body, name=name, grid=(S // tm,),
        in_specs=[_row_spec(tm, K), _const_spec(w.shape), _row_spec(tm, D), vec, vec, _row_spec(tm, D)],
        out_specs=[_row_spec(tm, D), vec, vec, vec],
        out_shape=[jax.ShapeDtypeStruct((S, D), F32), vshape, vshape, vshape],
        compiler_params=_params("arbitrary"),
    )(a, w, x, g, sc, dres)


def _oproj_bwd(dx1, ao, gt, w, ya, yb, yc, gg):
    S, D = dx1.shape
    tm = _tile(S, ROW_TILE)
    groups = ((0, NA_W), (NA_W, SW_W), (NA_W + SW_W, AX_W))

    def body(dx_ref, ao_ref, gt_ref, w_ref, ya_ref, yb_ref, yc_ref, gg_ref,
             dao_ref, dya_ref, dyb_ref, dyc_ref, dgt_ref, dgg_ref):
        @pl.when(pl.program_id(0) == 0)
        def _():
            dgt_ref[...] = jnp.zeros_like(dgt_ref)
            dgg_ref[...] = jnp.zeros_like(dgg_ref)

        dxv = dx_ref[...]
        dao = (dxv * gt_ref[...]).astype(BF16)
        dao_ref[...] = dao
        dgt_ref[...] += jnp.sum(dxv * ao_ref[...].astype(F32), axis=0, keepdims=True)
        dyn = lax.dot_general(dao, w_ref[...], NT, preferred_element_type=F32)
        for (off, wd), y_ref, dy_ref in zip(groups, (ya_ref, yb_ref, yc_ref), (dya_ref, dyb_ref, dyc_ref)):
            y = y_ref[...].astype(F32)
            d = dyn[:, off:off + wd]
            r = _rsq(jnp.mean(y * y, axis=-1, keepdims=True))
            yhat = y * r
            dgg_ref[:, off:off + wd] += jnp.sum(d * yhat, axis=0, keepdims=True)
            dyh = d * gg_ref[:, off:off + wd]
            dy_ref[...] = (r * (dyh - yhat * jnp.mean(dyh * yhat, axis=-1, keepdims=True))).astype(BF16)

    vec = _const_spec((1, D))
    mvec = _const_spec((1, MIX_WIDTH))
    return pl.pallas_call(
        body, name="oproj_bwd", grid=(S // tm,),
        in_specs=[_row_spec(tm, D), _row_spec(tm, D), vec, _const_spec(w.shape),
                  _row_spec(tm, NA_W), _row_spec(tm, SW_W), _row_spec(tm, AX_W), mvec],
        out_specs=[_row_spec(tm, D), _row_spec(tm, NA_W), _row_spec(tm, SW_W), _row_spec(tm, AX_W), vec, mvec],
        out_shape=[jax.ShapeDtypeStruct((S, D), BF16), jax.ShapeDtypeStruct((S, NA_W), BF16),
                   jax.ShapeDtypeStruct((S, SW_W), BF16), jax.ShapeDtypeStruct((S, AX_W), BF16),
                   jax.ShapeDtypeStruct((1, D), F32), jax.ShapeDtypeStruct((1, MIX_WIDTH), F32)],
        compiler_params=_params("arbitrary"),
    )(dx1, ao, gt, w, ya, yb, yc, gg)


def _dproj_assemble(proj, na, sw, ax, gq128, gk128, rope):
    S = proj.shape[0]
    tm = _tile(S, ROW_TILE)
    cos, sa, sb = rope

    def body(proj_ref, qa, ka, kah, va, vah, qb, kb, kbh, vb, vbh, qc, kc, vc,
             gq_ref, gk_ref, cos_ref, sa_ref, sb_ref, out_ref, dgq_ref, dgk_ref):
        @pl.when(pl.program_id(0) == 0)
        def _():
            dgq_ref[...] = jnp.zeros_like(dgq_ref)
            dgk_ref[...] = jnp.zeros_like(dgk_ref)

        out_ref[:, OFF_QA:OFF_KA] = qa[...].astype(BF16)
        out_ref[:, OFF_KA:OFF_VA] = (ka[...] + kah[...]).astype(BF16)
        out_ref[:, OFF_VA:OFF_QB] = (va[...] + vah[...]).astype(BF16)
        out_ref[:, OFF_QB:OFF_KB] = qb[...].astype(BF16)
        out_ref[:, OFF_KB:OFF_VB] = (kb[...] + kbh[...]).astype(BF16)
        out_ref[:, OFF_VB:OFF_QC] = (vb[...] + vbh[...]).astype(BF16)
        c, a, b = cos_ref[...], sa_ref[...], sb_ref[...]
        for j in range(AX_W // LANES):
            cols = slice(OFF_QC + j * LANES, OFF_QC + (j + 1) * LANES)
            dx, dg = _qk_prep_bwd_chunk(proj_ref[:, cols].astype(F32), qc[:, j * LANES:(j + 1) * LANES],
                                        gq_ref[...], c, a, b)
            out_ref[:, cols] = dx.astype(BF16)
            dgq_ref[...] += dg
        for j in range(AX_KV_W // LANES):
            cols = slice(OFF_KC + j * LANES, OFF_KC + (j + 1) * LANES)
            dx, dg = _qk_prep_bwd_chunk(proj_ref[:, cols].astype(F32), kc[:, j * LANES:(j + 1) * LANES],
                                        gk_ref[...], c, a, b)
            out_ref[:, cols] = dx.astype(BF16)
            dgk_ref[...] += dg
        out_ref[:, OFF_VC:IN_WIDTH] = vc[...].astype(BF16)

    v128 = _const_spec((1, LANES))
    r = lambda w: _row_spec(tm, w)
    return pl.pallas_call(
        body, name="dproj_assemble", grid=(S // tm,),
        in_specs=[r(IN_WIDTH), r(NA_W), r(NA_W), r(NA_W), r(NA_W), r(NA_W),
                  r(SW_W), r(SW_KV_W), r(SW_KV_W), r(SW_KV_W), r(SW_KV_W),
                  r(AX_W), r(AX_KV_W), r(AX_KV_W), v128, v128, r(LANES), r(LANES), r(LANES)],
        out_specs=[r(IN_WIDTH), v128, v128],
        out_shape=[jax.ShapeDtypeStruct((S, IN_WIDTH), BF16), jax.ShapeDtypeStruct((1, LANES), F32),
                   jax.ShapeDtypeStruct((1, LANES), F32)],
        compiler_params=_params("arbitrary"),
    )(proj, *na, *sw, *ax, gq128, gk128, cos, sa, sb)


def _tn_matmul(a, b, name):
    S, Ka = a.shape
    Nb = b.shape[1]
    tm = _tile(Ka, 1408, LANES)
    tn = _tile(Nb, 1408, LANES)
    tk = _tile(S, TOKEN_CHUNK)
    nk = S // tk

    def body(a_ref, b_ref, o_ref, acc_ref):
        k = pl.program_id(2)

        @pl.when(k == 0)
        def _():
            acc_ref[...] = jnp.zeros_like(acc_ref)

        acc_ref[...] += lax.dot_general(a_ref[...], b_ref[...], TN, preferred_element_type=F32)

        @pl.when(k == nk - 1)
        def _():
            o_ref[...] = acc_ref[...].astype(BF16)

    return pl.pallas_call(
        body, name=name, grid=(Ka // tm, Nb // tn, nk),
        in_specs=[pl.BlockSpec((tk, tm), lambda i, j, k: (k, i)), pl.BlockSpec((tk, tn), lambda i, j, k: (k, j))],
        out_specs=pl.BlockSpec((tm, tn), lambda i, j, k: (i, j)),
        out_shape=jax.ShapeDtypeStruct((Ka, Nb), BF16),
        scratch_shapes=[pltpu.VMEM((tm, tn), F32)],
        compiler_params=_params("parallel", "parallel", "arbitrary"),
    )(a, b)


def _na_index(bd):
    rq = jnp.arange(bd.bq // GRID_W)
    rk = jnp.arange(bd.bk // GRID_W)
    col = jnp.arange(GRID_W)
    ri = jnp.clip(rk[None, :] - rq[:, None] - bd.halo // GRID_W + NA_WIN_ROWS - 1, 0, 2 * NA_WIN_ROWS - 2)
    ci = jnp.clip(col[None, :] - col[:, None] + NA_WIN_COLS - 1, 0, 2 * NA_WIN_COLS - 2)
    return ri, ci


def _na_one_hots(bd):
    ri, ci = _na_index(bd)
    oh_r = jax.nn.one_hot(ri, 2 * NA_WIN_ROWS - 1, dtype=F32)
    oh_c = jax.nn.one_hot(ci, 2 * NA_WIN_COLS - 1, dtype=F32)
    return oh_r, oh_c


def _na_bias(bd, rpb):
    oh_r, oh_c = _na_one_hots(bd)
    t = jnp.einsum("hab,qra->hqrb", rpb, oh_r, precision=lax.Precision.HIGHEST)
    b = jnp.einsum("hqrb,ckb->hqcrk", t, oh_c, precision=lax.Precision.HIGHEST)
    return b.reshape(NA_HEADS, bd.bq, bd.bk)


def _na_bias_t(bd, dbias):
    oh_r, oh_c = _na_one_hots(bd)
    d5 = dbias.reshape(NA_HEADS, bd.bq // GRID_W, GRID_W, bd.bk // GRID_W, GRID_W)
    t = jnp.einsum("hqcrk,ckb->hqrb", d5, oh_c, precision=lax.Precision.HIGHEST)
    return jnp.einsum("hqrb,qra->hab", t, oh_r, precision=lax.Precision.HIGHEST)


def _t5_bucket(rel):
    nb = T5_BUCKETS // 2
    ret = (rel > 0).astype(I32) * nb
    n = jnp.abs(rel)
    max_exact = nb // 2
    nf = jnp.maximum(n, max_exact).astype(F32)
    large = max_exact + (jnp.log(nf / max_exact) / math.log(T5_MAX_DIST / max_exact)
                         * (nb - max_exact)).astype(I32)
    large = jnp.minimum(large, nb - 1)
    return ret + jnp.where(n < max_exact, n, large)


def _sw_bucket(bd):
    rel = (jnp.arange(bd.bk) - bd.halo)[None, :] - jnp.arange(bd.bq)[:, None]
    return _t5_bucket(rel)


def _sw_bias(bd, t5):
    def body(t5_ref, bucket_ref, out_ref):
        bucket = bucket_ref[...]
        for h in range(SW_HEADS):
            acc = jnp.zeros((bd.bq, bd.bk), F32)
            for b in range(T5_BUCKETS):
                acc = jnp.where(bucket == b, t5_ref[b, h], acc)
            out_ref[h] = acc

    vmem = pl.BlockSpec(memory_space=pltpu.VMEM)
    return pl.pallas_call(
        body, name="sw_bias", in_specs=[pl.BlockSpec(memory_space=pltpu.SMEM), vmem], out_specs=vmem,
        out_shape=jax.ShapeDtypeStruct((SW_HEADS, bd.bq, bd.bk), F32),
    )(t5, _sw_bucket(bd))


def _sw_bias_t(bd, dbias):
    def body(bucket_ref, d_ref, out_ref):
        bucket = bucket_ref[...]
        lane = lax.broadcasted_iota(I32, (1, LANES), 1)
        for b in range(T5_BUCKETS):
            hit = bucket == b
            row = jnp.zeros((1, LANES), F32)
            for h in range(SW_HEADS):
                row = jnp.where(lane == h, jnp.sum(jnp.where(hit, d_ref[h], 0.0)), row)
            out_ref[b:b + 1, :] = row

    vmem = pl.BlockSpec(memory_space=pltpu.VMEM)
    out = pl.pallas_call(
        body, name="sw_bias_t", in_specs=[vmem, vmem], out_specs=vmem,
        out_shape=jax.ShapeDtypeStruct((T5_BUCKETS, LANES), F32),
    )(_sw_bucket(bd), dbias)
    return out[:, :SW_HEADS]


def _local_step(x, target, mod, w_in, w_o, w_gu, w_down, g_attn, rpb_na, sink_sw, t5_table, gq_ax, gk_ax,
                g_group, g_ffn, g_final, late_shards=None):
    S, D = x.shape
    riding = late_shards is not None
    w_in = [w_in[l] for l in range(w_in.shape[0])]
    rope = _rope_tables(S)
    na, sw = _Band("na", S), _Band("sw", S)
    two = lambda v: jnp.concatenate([v, v])[None, :]
    sw_bias = _sw_bias(sw, t5_table)
    saved = []
    for l in range(DEPTH):
        sh_a, sc_a, gt_a, sh_f, sc_f, gt_f = [mod[l, k * D:(k + 1) * D][None, :] for k in range(6)]
        gq128, gk128 = two(gq_ax[l]), two(gk_ax[l])
        gg = g_group[l][None, :]
        sink = jnp.pad(sink_sw[l], (0, LANES - SW_HEADS))[None, :]
        na_bias = _na_bias(na, rpb_na[l])
        h, proj, qc, kc = _inproj_fwd(x, g_attn[l][None, :], sc_a, sh_a, w_in[l], gq128, gk128, rope)
        ya, yna = _band_fwd(na, proj, na_bias, None, gg[:, :NA_W])
        yb, ynb = _band_fwd(sw, proj, sw_bias, sink, gg[:, NA_W:NA_W + SW_W])
        rider = None
        if riding and l == 0:
            rider = _gather_exchange([late_shards[k] for k in BIG], [BIG_AXIS[k] for k in BIG],
                                     [BIG_ORDER[k] for k in BIG])
        yc, ync, lse, got = _ax_fwd(qc, kc, proj, gg[:, NA_W + SW_W:], rider)
        if rider:
            w_in_late, w_o, w_gu, w_down = got
            w_in += [w_in_late[k] for k in range(w_in_late.shape[0])]
        x1, ao, yn = _oproj_fwd(x, yna, ynb, ync, w_o[l], gt_a)
        hf, gu, act = _gu_fwd(x1, g_ffn[l][None, :], sc_f, sh_f, w_gu[l])
        x2, fo = _down_fwd(x1, act, w_down[l], gt_f)
        saved.append(dict(x=x, x1=x1, h=h, proj=proj, qc=qc, kc=kc, ya=ya, yb=yb, yc=yc, lse=lse, ao=ao, yn=yn,
                          hf=hf, gu=gu, act=act, fo=fo, na_bias=na_bias, sink=sink, gq128=gq128, gk128=gk128,
                          gg=gg, mods=(sh_a, sc_a, gt_a, sh_f, sc_f, gt_f)))
        x = x2

    dx, loss_row, dg_final = _final_loss(x, g_final[None, :], target)
    gw = {k: [None] * DEPTH for k in ("w_in", "w_o", "w_gu", "w_down")}
    gs = {k: [None] * DEPTH for k in ("b_mod", "g_attn", "rpb_na", "sink_sw", "gq_ax", "gk_ax", "g_group", "g_ffn")}
    d_t5 = jnp.zeros((T5_BUCKETS, SW_HEADS), F32)
    for l in reversed(range(DEPTH)):
        s = saved[l]
        sh_a, sc_a, gt_a, sh_f, sc_f, gt_f = s["mods"]
        dfo, dgu, dgt_f = _ffn_bwd1(dx, s["fo"], gt_f, w_down[l], s["gu"])
        gw["w_down"][l] = _tn_matmul(s["act"], dfo, "dw_down")
        gw["w_gu"][l] = _tn_matmul(s["hf"], dgu, "dw_gu")
        dx1, dsh_f, dsc_f, gs["g_ffn"][l] = _nt_ln_bwd(dgu, w_gu[l], s["x1"], g_ffn[l][None, :], sc_f, dx, "ffn_bwd2")
        dao, dya, dyb, dyc, dgt_a, gs["g_group"][l] = _oproj_bwd(dx1, s["ao"], gt_a, w_o[l], s["ya"], s["yb"],
                                                                 s["yc"], s["gg"])
        gw["w_o"][l] = _tn_matmul(s["yn"], dao, "dw_o")
        dqa, dka, dva, dkap, dvap, dkan, dvan, dbias_na = _band_bwd(na, s["proj"], jnp.swapaxes(s["na_bias"], 1, 2), None, dya)
        dqb, dkb, dvb, dkbp, dvbp, dkbn, dvbn, dbias_sw, dsink = _band_bwd(sw, s["proj"], jnp.swapaxes(sw_bias, 1, 2), s["sink"], dyb)
        rider = None
        if riding and l == 0:
            ready = [("w_in", k) for k in range(1, DEPTH)] + [(n, k) for n in BIG[1:] for k in range(DEPTH)]
            rider = _scatter_exchange(
                [gw[n][k] for n, k in ready], [BIG_AXIS[n] - 1 for n, _ in ready], [BIG_ORDER[n] for n, _ in ready],
                [None if n == "w_in" else (BIG.index(n), k, DEPTH) for n, k in ready])
        dqc, dkc, dvc, sent = _ax_bwd(s["qc"], s["kc"], s["proj"], dyc, s["lse"], _ax_delta(dyc, s["yc"]), rider)
        dproj, dgq, dgk = _dproj_assemble(
            s["proj"], (dqa, dka, _halo_to_rows(dkap, dkan), dva, _halo_to_rows(dvap, dvan)),
            (dqb, dkb, _halo_to_rows(dkbp, dkbn), dvb, _halo_to_rows(dvbp, dvbn)), (dqc, dkc, dvc),
            s["gq128"], s["gk128"], rope)
        gw["w_in"][l] = _tn_matmul(s["h"], dproj, "dw_in")
        dx, dsh_a, dsc_a, gs["g_attn"][l] = _nt_ln_bwd(dproj, w_in[l], s["x"], g_attn[l][None, :], sc_a, dx1,
                                                       "inproj_bwd")
        gs["b_mod"][l] = jnp.concatenate([dsh_a, dsc_a, dgt_a, dsh_f, dsc_f, dgt_f], axis=1)[0]
        gs["rpb_na"][l] = _na_bias_t(na, jnp.swapaxes(dbias_na, 1, 2))
        gs["sink_sw"][l] = dsink[0, :SW_HEADS]
        d_t5 = d_t5 + _sw_bias_t(sw, jnp.swapaxes(dbias_sw, 1, 2))
        gs["gq_ax"][l] = dgq[0, :HEAD_DIM] + dgq[0, HEAD_DIM:]
        gs["gk_ax"][l] = dgk[0, :HEAD_DIM] + dgk[0, HEAD_DIM:]
        gs["g_attn"][l] = gs["g_attn"][l][0]
        gs["g_ffn"][l] = gs["g_ffn"][l][0]
        gs["g_group"][l] = gs["g_group"][l][0]

    if riding:
        (first,) = _chip_scatter([gw["w_in"][0]], [BIG_AXIS["w_in"] - 1], [BIG_ORDER["w_in"]], "scatter_w_in0")
        gw = dict(zip(BIG[1:], sent[DEPTH - 1:]), w_in=jnp.stack([first] + sent[:DEPTH - 1], axis=1))
    else:
        gw = {k: jnp.stack(v) for k, v in gw.items()}
    small = {k: jnp.stack(v) for k, v in gs.items()}
    small["t5_table"] = d_t5
    small["g_final"] = dg_final[0]
    return loss_row[0, 0], dx, gw, small


MOD_ROWS = 16


def _mod_fwd(cond16, w):
    L, D, C = w.shape
    tn = _tile(C, 512, LANES)

    def body(c_ref, w_ref, o_ref):
        o_ref[0] = jnp.dot(c_ref[...], w_ref[0].astype(BF16), preferred_element_type=F32)

    return pl.pallas_call(
        body, name="mod_fwd", grid=(L, C // tn),
        in_specs=[pl.BlockSpec((MOD_ROWS, D), lambda l, j: (0, 0)), pl.BlockSpec((1, D, tn), lambda l, j: (l, 0, j))],
        out_specs=pl.BlockSpec((1, MOD_ROWS, tn), lambda l, j: (l, 0, j)),
        out_shape=jax.ShapeDtypeStruct((L, MOD_ROWS, C), F32),
        compiler_params=_params("parallel", "parallel"),
    )(cond16, w)


def _adamw_math(w, g, m, v):
    m = ADAM_B1 * m + (1.0 - ADAM_B1) * g
    v = ADAM_B2 * v + (1.0 - ADAM_B2) * (g * g)
    m_hat = m / (1.0 - ADAM_B1 ** ADAM_STEP)
    v_hat = v / (1.0 - ADAM_B2 ** ADAM_STEP)
    delta = -ADAM_LR * (m_hat / (jnp.sqrt(v_hat) + ADAM_EPS) + ADAM_WD * w)
    return delta, m, v


def _adamw(w, m, v, parts, name):
    R, C = w.shape
    tr = _tile(R, 256)
    n = len(parts)

    def body(*refs):
        w_ref, m_ref, v_ref = refs[:3]
        g = refs[3][...]
        for p in refs[4:3 + n]:
            g = g + p[...]
        g_ref, d_ref, m2_ref, v2_ref = refs[3 + n:]
        g_ref[...] = g
        d_ref[...], m2_ref[...], v2_ref[...] = _adamw_math(w_ref[...], g, m_ref[...], v_ref[...])

    spec = _row_spec(tr, C)
    shape = jax.ShapeDtypeStruct((R, C), F32)
    return pl.pallas_call(
        body, name=name, grid=(R // tr,), in_specs=[spec] * (3 + n), out_specs=[spec] * 4, out_shape=[shape] * 4,
        compiler_params=_params("parallel"),
    )(w, m, v, *parts)


def _wmod_adamw(cond_t, dmod16, w, m, v):
    L, D, C = w.shape
    tr = _tile(D, 256)

    def body(c_ref, d_ref, w_ref, m_ref, v_ref, g_ref, dl_ref, m2_ref, v2_ref):
        g = jnp.dot(c_ref[...], d_ref[0], preferred_element_type=F32)
        g_ref[0] = g
        dl_ref[0], m2_ref[0], v2_ref[0] = _adamw_math(w_ref[0], g, m_ref[0], v_ref[0])

    spec = pl.BlockSpec((1, tr, C), lambda l, i: (l, i, 0))
    shape = jax.ShapeDtypeStruct((L, D, C), F32)
    return pl.pallas_call(
        body, name="wmod_adamw", grid=(L, D // tr),
        in_specs=[pl.BlockSpec((tr, MOD_ROWS), lambda l, i: (i, 0)),
                  pl.BlockSpec((1, MOD_ROWS, C), lambda l, i: (l, 0, 0)), spec, spec, spec],
        out_specs=[spec] * 4, out_shape=[shape] * 4,
        compiler_params=_params("parallel", "parallel"),
    )(cond_t, dmod16, w, m, v)


def _sum_slots(a):
    P, R, C = a.shape
    tr = _tile(R, 256, 16)

    def body(a_ref, o_ref):
        s = a_ref[0].astype(F32)
        for k in range(1, P):
            s = s + a_ref[k].astype(F32)
        o_ref[...] = s

    return pl.pallas_call(
        body, name="sum_slots", grid=(R // tr,),
        in_specs=[pl.BlockSpec((P, tr, C), lambda i: (0, i, 0))], out_specs=_row_spec(tr, C),
        out_shape=jax.ShapeDtypeStruct((R, C), F32), compiler_params=_params("parallel"),
    )(a)


def _axes():
    return lax.axis_index("x"), lax.axis_index("y"), lax.axis_index("c")


def _allgather_devices(v):
    N = v.shape[1]

    def body(v_ref, out_ref, send_sems, recv_sems, local_sem):
        x, y, c = _axes()

        def row(px, py, pc):
            return out_ref.at[pl.ds(4 * px + 2 * py + pc, 1), :]

        mine = pltpu.make_async_copy(v_ref, row(x, y, c), local_sem)
        mine.start()
        sends, recvs = [], []
        for k in range(1, N_DEV):
            peer = (x ^ (k >> 2), y ^ ((k >> 1) & 1), c ^ (k & 1))
            sems = dict(send_sem=send_sems.at[k - 1], recv_sem=recv_sems.at[k - 1], device_id=peer, device_id_type=MESH)
            sends.append(pltpu.make_async_remote_copy(src_ref=v_ref, dst_ref=row(x, y, c), **sems))
            recvs.append(pltpu.make_async_remote_copy(src_ref=v_ref, dst_ref=row(*peer), **sems))
        for cp in sends:
            cp.start()
        for cp in recvs:
            cp.wait_recv()
        for cp in sends:
            cp.wait_send()
        mine.wait()

    vmem = pl.BlockSpec(memory_space=pltpu.VMEM)
    return pl.pallas_call(
        body, name="allgather_devices", in_specs=[vmem], out_specs=vmem,
        out_shape=jax.ShapeDtypeStruct((N_DEV, N), v.dtype),
        scratch_shapes=[pltpu.SemaphoreType.DMA((N_DEV - 1,)), pltpu.SemaphoreType.DMA((N_DEV - 1,)),
                        pltpu.SemaphoreType.DMA],
        compiler_params=pltpu.CompilerParams(vmem_limit_bytes=VMEM_LIMIT_V7X),
    )(v)


def _chip_pos(order, px, py):
    return 2 * px + py if order == "natural" else 2 * py + px


def _block(ref, axis, pos, width):
    idx = [slice(None)] * len(ref.shape)
    idx[axis] = pl.ds(pl.multiple_of(pos * width, width), width)
    return ref.at[tuple(idx)]


def _chip_allgather(shards, axes, orders, name):
    return _run_exchange(_gather_exchange(shards, axes, orders), name)


class _Exchange:
    def __init__(self, arrays, out_shapes, describe):
        self.arrays, self.out_shapes, self.describe = list(arrays), list(out_shapes), describe
        n_remote = len(self.arrays) * (N_CHIPS - 1)
        self.scratch = [pltpu.SemaphoreType.DMA((n_remote,)), pltpu.SemaphoreType.DMA((n_remote,)),
                        pltpu.SemaphoreType.DMA((len(self.arrays),))]

    def _copies(self, ins, outs, sems):
        send_sems, recv_sems, local_sems = sems
        x, y, c = _axes()
        local, sends, recvs = [], [], []
        for i in range(len(self.arrays)):
            src, dst = self.describe(i, ins, outs, x, y, x, y)
            local.append(pltpu.make_async_copy(src, dst, local_sems.at[i]))
            for k in range(1, N_CHIPS):
                px, py = x ^ (k >> 1), y ^ (k & 1)
                j = i * (N_CHIPS - 1) + k - 1
                sem = dict(send_sem=send_sems.at[j], recv_sem=recv_sems.at[j], device_id=(px, py, c),
                           device_id_type=MESH)
                src, dst = self.describe(i, ins, outs, x, y, px, py)
                sends.append(pltpu.make_async_remote_copy(src_ref=src, dst_ref=dst, **sem))
                src, dst = self.describe(i, ins, outs, px, py, x, y)
                recvs.append(pltpu.make_async_remote_copy(src_ref=src, dst_ref=dst, **sem))
        return local, sends, recvs

    def start(self, ins, outs, sems):
        local, sends, _ = self._copies(ins, outs, sems)
        for cp in local + sends:
            cp.start()

    def wait(self, ins, outs, sems):
        local, sends, recvs = self._copies(ins, outs, sems)
        for cp in recvs:
            cp.wait_recv()
        for cp in sends:
            cp.wait_send()
        for cp in local:
            cp.wait()


def _run_exchange(ex, name):
    n_in, n_out = len(ex.arrays), len(ex.out_shapes)

    def body(*refs):
        ins, outs, sems = refs[:n_in], refs[n_in:n_in + n_out], refs[n_in + n_out:]
        ex.start(ins, outs, sems)
        ex.wait(ins, outs, sems)

    hbm = pl.BlockSpec(memory_space=pl.ANY)
    return pl.pallas_call(body, name=name, in_specs=[hbm] * n_in, out_specs=[hbm] * n_out, out_shape=ex.out_shapes,
                          scratch_shapes=ex.scratch)(*ex.arrays)


def _gather_exchange(shards, axes, orders):
    out_shapes = []
    for s, ax, order in zip(shards, axes, orders):
        shp = list(s.shape)
        if order == "gate_up_tiles":
            shp = [shp[0], 2, shp[1], 2 * shp[2]]
        else:
            shp[ax] *= N_CHIPS
        out_shapes.append(jax.ShapeDtypeStruct(tuple(shp), s.dtype))

    def describe(i, ins, outs, fx, fy, tx, ty):
        pos, width = _chip_pos(orders[i], fx, fy), shards[i].shape[axes[i]]
        if orders[i] == "gate_up_tiles":
            return ins[i], outs[i].at[:, pos // 2, :, pl.ds(pl.multiple_of((pos % 2) * width, width), width)]
        return ins[i], _block(outs[i], axes[i], pos, width)

    return _Exchange(shards, out_shapes, describe)


def _chip_scatter(grads, axes, orders, name):
    return _run_exchange(_scatter_exchange(grads, axes, orders), name)


def _scatter_exchange(grads, axes, orders, layers=None):
    layers = layers or [None] * len(grads)
    widths, out_shapes = [], {}
    for i, (g, ax, lay) in enumerate(zip(grads, axes, layers)):
        shp = list(g.shape)
        shp[ax] //= N_CHIPS
        widths.append(shp[ax])
        key, lead = (("own", i), (N_CHIPS,)) if lay is None else (("shared", lay[0]), (N_CHIPS, lay[2]))
        out_shapes[key] = jax.ShapeDtypeStruct(lead + tuple(shp), g.dtype)
    keys = list(out_shapes)

    def describe(i, ins, outs, fx, fy, tx, ty):
        lay = layers[i]
        out = outs[keys.index(("own", i) if lay is None else ("shared", lay[0]))]
        slot = out.at[2 * fx + fy] if lay is None else out.at[2 * fx + fy, lay[1]]
        return _block(ins[i], axes[i], _chip_pos(orders[i], tx, ty), widths[i]), slot

    return _Exchange(grads, [out_shapes[k] for k in keys], describe)


def _core_swap(arrays, name):
    n = len(arrays)

    def body(*refs):
        ins, outs = refs[:n], refs[n:2 * n]
        send_sems, recv_sems = refs[2 * n:]
        x, y, c = _axes()
        copies = [pltpu.make_async_remote_copy(src_ref=ins[i], dst_ref=outs[i], send_sem=send_sems.at[i],
                                               recv_sem=recv_sems.at[i], device_id=(x, y, 1 - c), device_id_type=MESH)
                  for i in range(n)]
        for cp in copies:
            cp.start()
        for cp in copies:
            cp.wait_recv()
        for cp in copies:
            cp.wait_send()

    hbm = pl.BlockSpec(memory_space=pl.ANY)
    return pl.pallas_call(
        body, name=name, in_specs=[hbm] * n, out_specs=[hbm] * n,
        out_shape=[jax.ShapeDtypeStruct(a.shape, a.dtype) for a in arrays],
        scratch_shapes=[pltpu.SemaphoreType.DMA((n,)), pltpu.SemaphoreType.DMA((n,))],
    )(*arrays)


SMALL = ("b_mod", "g_attn", "rpb_na", "sink_sw", "t5_table", "gq_ax", "gk_ax", "g_group", "g_ffn", "g_final")
BIG = ("w_in", "w_o", "w_gu", "w_down")
BIG_AXIS = {"w_in": 2, "w_o": 1, "w_gu": 2, "w_down": 1}
BIG_ORDER = {"w_in": "natural", "w_o": "natural", "w_gu": "gate_up_tiles", "w_down": "natural"}
WEIGHTS = ("w_mod", "b_mod", "g_attn", "w_in", "rpb_na", "sink_sw", "t5_table", "gq_ax", "gk_ax", "g_group",
           "w_o", "g_ffn", "w_gu", "w_down", "g_final")


def _pack(arrs):
    flat = jnp.concatenate([a.reshape(-1) for a in arrs])
    n = flat.shape[0]
    padded = -(-n // (8 * LANES)) * (8 * LANES)
    return jnp.pad(flat, (0, padded - n))


def _unpack(flat, like):
    out, off = [], 0
    for a in like:
        out.append(flat[off:off + a.size].reshape(a.shape))
        off += a.size
    return out


def kernel(x, c, w_mod, b_mod, g_attn, w_in, rpb_na, sink_sw, t5_table, gq_ax, gk_ax, g_group, w_o, g_ffn, w_gu, w_down, g_final, loss_target, m_w_mod, m_b_mod, m_g_attn, m_w_in, m_rpb_na, m_sink_sw, m_t5_table, m_gq_ax, m_gk_ax, m_g_group, m_w_o, m_g_ffn, m_w_gu, m_w_down, m_g_final, v_w_mod, v_b_mod, v_g_attn, v_w_in, v_rpb_na, v_sink_sw, v_t5_table, v_gq_ax, v_gk_ax, v_g_group, v_w_o, v_g_ffn, v_w_gu, v_w_down, v_g_final):
    W = dict(w_mod=w_mod, b_mod=b_mod, g_attn=g_attn, w_in=w_in, rpb_na=rpb_na, sink_sw=sink_sw, t5_table=t5_table,
             gq_ax=gq_ax, gk_ax=gk_ax, g_group=g_group, w_o=w_o, g_ffn=g_ffn, w_gu=w_gu, w_down=w_down,
             g_final=g_final)
    M = dict(w_mod=m_w_mod, b_mod=m_b_mod, g_attn=m_g_attn, w_in=m_w_in, rpb_na=m_rpb_na, sink_sw=m_sink_sw,
             t5_table=m_t5_table, gq_ax=m_gq_ax, gk_ax=m_gk_ax, g_group=m_g_group, w_o=m_w_o, g_ffn=m_g_ffn,
             w_gu=m_w_gu, w_down=m_w_down, g_final=m_g_final)
    V = dict(w_mod=v_w_mod, b_mod=v_b_mod, g_attn=v_g_attn, w_in=v_w_in, rpb_na=v_rpb_na, sink_sw=v_sink_sw,
             t5_table=v_t5_table, gq_ax=v_gq_ax, gk_ax=v_gk_ax, g_group=v_g_group, w_o=v_w_o, g_ffn=v_g_ffn,
             w_gu=v_w_gu, w_down=v_w_down, g_final=v_g_final)
    xi, yi, ci = _axes()
    me = 4 * xi + 2 * yi + ci
    chip = 2 * xi + yi
    D = x.shape[-1]
    mod_w = w_mod.shape[2]

    c_all = _allgather_devices(c)
    cond = c_all * (1.0 / (1.0 + jnp.exp(-c_all)))
    cond16 = jnp.pad(cond, ((0, MOD_ROWS - N_DEV), (0, 0))).astype(BF16)
    mod_part = _mod_fwd(cond16, w_mod)
    (mod_all,) = _chip_allgather([mod_part], [2], ["natural"], "allgather_mod")
    mod = lax.dynamic_slice_in_dim(mod_all, me, 1, axis=1)[:, 0, :] + b_mod

    shards = {k: W[k].astype(BF16) for k in BIG}
    (w_in_first,) = _chip_allgather([shards["w_in"][:1]], [BIG_AXIS["w_in"]], [BIG_ORDER["w_in"]], "allgather_w_in0")
    shards["w_in"] = shards["w_in"][1:]
    loss_part, grad_x, slots, small = _local_step(x[0], loss_target[0], mod, w_in_first, None, None, None, g_attn,
                                                  rpb_na, sink_sw, t5_table, gq_ax, gk_ax, g_group, g_ffn, g_final,
                                                  late_shards=shards)

    small_all = _allgather_devices(_pack([small[k] for k in SMALL])[None, :])
    rows = small_all.shape[1] // LANES
    parts = [small_all[k].reshape(rows, LANES) for k in range(N_DEV)]
    pk = lambda d: _pack([d[k] for k in SMALL]).reshape(rows, LANES)
    small_out = [_unpack(o.reshape(-1), [W[k] for k in SMALL]) for o in _adamw(pk(W), pk(M), pk(V), parts, "adamw_small")]

    L = w_mod.shape[0]
    dmod_all = small_all[:, :L * 6 * D].reshape(N_DEV, L, 6 * D)
    dmod_mine = lax.dynamic_slice_in_dim(dmod_all, chip * mod_w, mod_w, axis=2)
    dmod16 = jnp.pad(jnp.transpose(dmod_mine, (1, 0, 2)), ((0, 0), (0, MOD_ROWS - N_DEV), (0, 0))).astype(BF16)
    wmod_out = _wmod_adamw(jnp.transpose(cond16), dmod16, w_mod, m_w_mod, v_w_mod)

    names = list(BIG)
    two_d = lambda a: a.reshape(-1, a.shape[-1])
    mine = [_sum_slots(slots[k].reshape(N_CHIPS, -1, slots[k].shape[-1])) for k in names]
    theirs = _core_swap(mine, "swap_grads")
    big_out = {}
    for k, a, b in zip(names, mine, theirs):
        outs = _adamw(two_d(W[k]), two_d(M[k]), two_d(V[k]), [a, b], "adamw_" + k)
        big_out[k] = [o.reshape(W[k].shape) for o in outs]

    loss = lax.psum(loss_part, ("x", "y", "c"))
    per_kind = []
    for kind in range(4):
        for k in WEIGHTS:
            if k == "w_mod":
                per_kind.append(wmod_out[kind])
            elif k in big_out:
                per_kind.append(big_out[k][kind])
            else:
                per_kind.append(small_out[kind][SMALL.index(k)])
    return (loss, grad_x[None], *per_kind)
```

```python
import math

import jax
import jax.numpy as jnp
from jax import lax
from jax.experimental import pallas as pl
from jax.experimental.pallas import tpu as pltpu

F32 = jnp.float32
BF16 = jnp.bfloat16
I32 = jnp.int32

DEPTH = 2
HEAD_DIM = 64
GRID_W = 64
NA_HEADS = 4
SW_HEADS = 6
SW_KV_HEADS = 2
AX_HEADS = 6
AX_KV_HEADS = 2
NA_WIN_ROWS = 8
NA_WIN_COLS = 16
SW_RADIUS = 128
T5_BUCKETS = 32
T5_MAX_DIST = 128
ROPE_THETA = 10000.0
EPS = 1e-6
NEG_INF = -1e30
QK_SCALE = HEAD_DIM ** -0.5

NA_W = NA_HEADS * HEAD_DIM
SW_W = SW_HEADS * HEAD_DIM
SW_KV_W = SW_KV_HEADS * HEAD_DIM
AX_W = AX_HEADS * HEAD_DIM
AX_KV_W = AX_KV_HEADS * HEAD_DIM
OFF_QA, OFF_KA, OFF_VA = 0, NA_W, 2 * NA_W
OFF_QB = 3 * NA_W
OFF_KB = OFF_QB + SW_W
OFF_VB = OFF_KB + SW_KV_W
OFF_QC = OFF_VB + SW_KV_W
OFF_KC = OFF_QC + AX_W
OFF_VC = OFF_KC + AX_KV_W
IN_WIDTH = OFF_VC + AX_KV_W
MIX_WIDTH = NA_W + SW_W + AX_W

ADAM_LR = 0.001
ADAM_B1 = 0.9
ADAM_B2 = 0.999
ADAM_EPS = 1e-08
ADAM_WD = 0.01
ADAM_STEP = 10

N_CHIPS = 4
N_DEV = 8
LANES = 128
VMEM_LIMIT_V7X = 56 * 1024 * 1024
ROW_TILE = 512
ROW_TILE_WIDE = 512
TOKEN_CHUNK = 2048
MESH = pl.DeviceIdType.MESH

NT = (((1,), (1,)), ((), ()))
TN = (((0,), (0,)), ((), ()))


def _params(*sem):
    return pltpu.CompilerParams(dimension_semantics=sem if sem else None,
                                vmem_limit_bytes=VMEM_LIMIT_V7X)


def _tile(n, pref, mult=8):
    t = (min(pref, n) // mult) * mult
    while t >= mult:
        if n % t == 0:
            return t
        t -= mult
    return n


def _row_spec(tm, width, col=0):
    return pl.BlockSpec((tm, width), lambda i, *_: (i, col))


def _const_spec(shape):
    nd = len(shape)
    return pl.BlockSpec(shape, lambda *_: (0,) * nd)


def _rsq(ms):
    return lax.rsqrt(ms + EPS)


def _rope_tables(S):
    rows = S // GRID_W
    axis_dim = HEAD_DIM // 2
    quarter = axis_dim // 2
    lane = jnp.arange(LANES)
    freq = (ROPE_THETA ** (-(2 * (lane % quarter)).astype(F32) / axis_dim))[None, :]
    by_row = ((lane % HEAD_DIM) < axis_dim)[None, None, :]
    first = ((lane % axis_dim) < quarter)[None, :]
    ang_r = jnp.arange(rows, dtype=F32)[:, None] * freq
    ang_c = jnp.arange(GRID_W, dtype=F32)[:, None] * freq

    def table(fr, fc):
        t = jnp.where(by_row, fr[:, None, :], fc[None, :, :])
        return t.reshape(S, LANES)

    sin_r, sin_c = jnp.sin(ang_r), jnp.sin(ang_c)
    return (table(jnp.cos(ang_r), jnp.cos(ang_c)),
            table(jnp.where(first, -sin_r, 0.0), jnp.where(first, -sin_c, 0.0)),
            table(jnp.where(first, 0.0, sin_r), jnp.where(first, 0.0, sin_c)))


def _pair_sum(v):
    lane = lax.broadcasted_iota(I32, v.shape, 1)
    lo = lane < HEAD_DIM
    s_lo = jnp.sum(jnp.where(lo, v, 0.0), axis=-1, keepdims=True)
    s_hi = jnp.sum(jnp.where(lo, 0.0, v), axis=-1, keepdims=True)
    return jnp.where(lo, s_lo, s_hi)


def _rope(t, cos, sa, sb):
    return t * cos + pltpu.roll(t, LANES - 16, 1) * sa + pltpu.roll(t, 16, 1) * sb


def _rope_t(t, cos, sa, sb):
    return t * cos + pltpu.roll(t * sa, 16, 1) + pltpu.roll(t * sb, LANES - 16, 1)


def _qk_prep_chunk(x, g128, cos, sa, sb):
    r = _rsq(_pair_sum(x * x) * (1.0 / HEAD_DIM))
    return _rope(x * r * g128, cos, sa, sb)


def _qk_prep_bwd_chunk(x, dy, g128, cos, sa, sb):
    dn = _rope_t(dy, cos, sa, sb)
    r = _rsq(_pair_sum(x * x) * (1.0 / HEAD_DIM))
    xhat = x * r
    dg = jnp.sum(dn * xhat, axis=0, keepdims=True)
    dxh = dn * g128
    dx = r * (dxh - xhat * (_pair_sum(dxh * xhat) * (1.0 / HEAD_DIM)))
    return dx, dg


def _ln_mod(xv, g, sc, sh):
    r = _rsq(jnp.mean(xv * xv, axis=-1, keepdims=True))
    return xv * r * g * (1.0 + sc) + sh


def _inproj_fwd(x, g, sc, sh, w, gq128, gk128, rope):
    S, D = x.shape
    tm = _tile(S, ROW_TILE)
    cos, sa, sb = rope

    def body(x_ref, g_ref, sc_ref, sh_ref, w_ref, gq_ref, gk_ref, cos_ref, sa_ref, sb_ref,
             h_ref, proj_ref, qc_ref, kc_ref):
        hb = _ln_mod(x_ref[...], g_ref[...], sc_ref[...], sh_ref[...]).astype(BF16)
        h_ref[...] = hb
        acc = jnp.dot(hb, w_ref[...], preferred_element_type=F32)
        proj_ref[...] = acc.astype(BF16)
        c, a, b = cos_ref[...], sa_ref[...], sb_ref[...]
        for j in range(AX_W // LANES):
            xq = acc[:, OFF_QC + j * LANES: OFF_QC + (j + 1) * LANES]
            qc_ref[:, j * LANES:(j + 1) * LANES] = (
                _qk_prep_chunk(xq, gq_ref[...], c, a, b) * QK_SCALE).astype(BF16)
        for j in range(AX_KV_W // LANES):
            xk = acc[:, OFF_KC + j * LANES: OFF_KC + (j + 1) * LANES]
            kc_ref[:, j * LANES:(j + 1) * LANES] = _qk_prep_chunk(xk, gk_ref[...], c, a, b).astype(BF16)

    vec = _const_spec((1, D))
    v128 = _const_spec((1, LANES))
    return pl.pallas_call(
        body, name="inproj_fwd", grid=(S // tm,),
        in_specs=[_row_spec(tm, D), vec, vec, vec, _const_spec(w.shape), v128, v128,
                  _row_spec(tm, LANES), _row_spec(tm, LANES), _row_spec(tm, LANES)],
        out_specs=[_row_spec(tm, D), _row_spec(tm, IN_WIDTH), _row_spec(tm, AX_W), _row_spec(tm, AX_KV_W)],
        out_shape=[jax.ShapeDtypeStruct((S, D), BF16), jax.ShapeDtypeStruct((S, IN_WIDTH), BF16),
                   jax.ShapeDtypeStruct((S, AX_W), BF16), jax.ShapeDtypeStruct((S, AX_KV_W), BF16)],
        compiler_params=_params("parallel"),
    )(x, g, sc, sh, w, gq128, gk128, cos, sa, sb)


class _Band:
    def __init__(self, kind, S):
        self.kind = kind
        self.S = S
        if kind == "na":
            self.hq, self.g, self.halo = NA_HEADS, NA_HEADS, (NA_WIN_ROWS // 2) * GRID_W
            self.q_off, self.k_off, self.v_off = OFF_QA, OFF_KA, OFF_VA
        else:
            self.hq, self.g, self.halo = SW_HEADS, SW_KV_HEADS, SW_RADIUS
            self.q_off, self.k_off, self.v_off = OFF_QB, OFF_KB, OFF_VB
        self.bq = 2 * self.halo
        self.bk = self.bq + 2 * self.halo
        self.nb = S // self.bq
        self.rep = self.hq // self.g
        self.qw = self.hq * HEAD_DIM
        self.kw = self.g * HEAD_DIM

    def kv_of(self, h):
        return (h // 2, h % 2) if self.kind == "na" else (0, h // self.rep)

    def mask(self, n, transposed=False):
        shape = (self.bk, self.bq) if transposed else (self.bq, self.bk)
        qi = lax.broadcasted_iota(I32, shape, 1 if transposed else 0) + n * self.bq
        kj = lax.broadcasted_iota(I32, shape, 0 if transposed else 1) + (n * self.bq - self.halo)
        if self.kind == "sw":
            return (jnp.abs(kj - qi) <= SW_RADIUS) & (kj >= 0) & (kj < self.S)
        rows = self.S // GRID_W
        r, col = qi >> 6, qi & (GRID_W - 1)
        kr, kc = kj >> 6, kj & (GRID_W - 1)
        rs = jnp.clip(r - NA_WIN_ROWS // 2, 0, rows - NA_WIN_ROWS)
        cs = jnp.clip(col - NA_WIN_COLS // 2, 0, GRID_W - NA_WIN_COLS)
        return (kr >= rs) & (kr < rs + NA_WIN_ROWS) & (kc >= cs) & (kc < cs + NA_WIN_COLS)

    def qkv_specs(self):
        ratio = self.bq // self.halo
        last = self.S // self.halo - 1
        q = pl.BlockSpec((self.bq, self.qw), lambda n, o=self.q_off // self.qw: (n, o))
        specs = [q]
        for off in (self.k_off, self.v_off):
            o = off // self.kw
            specs.append(pl.BlockSpec((self.halo, self.kw), lambda n, o=o: (jnp.maximum(n * ratio - 1, 0), o)))
            specs.append(pl.BlockSpec((self.bq, self.kw), lambda n, o=o: (n, o)))
            specs.append(pl.BlockSpec((self.halo, self.kw), lambda n, o=o: (jnp.minimum((n + 1) * ratio, last), o)))
        return specs


def _band_kv_variants(bd, refs, fill):
    out = []
    for blk in range(bd.kw // LANES):
        cols = slice(blk * LANES, (blk + 1) * LANES)
        out.append(_half_variants(jnp.concatenate([r[:, cols] for r in refs], axis=0), fill))
    return out


def _band_fwd(bd, proj, bias, sink, gg):
    S = bd.S
    has_sink = sink is not None

    def body(*refs):
        q_ref, kp, km, kn, vp, vm, vn, bias_ref = refs[:8]
        k = 8
        sink_ref = None
        if has_sink:
            sink_ref = refs[k]
            k += 1
        gg_ref, raw_ref, yn_ref, o_scr = refs[k:k + 4]
        mask = bd.mask(pl.program_id(0))
        lo = _left_half((bd.bq, LANES))
        kzs, vzs = _band_kv_variants(bd, (kp, km, kn), 0.0), _band_kv_variants(bd, (vp, vm, vn), 1.0)
        for pr in range(bd.hq // 2):
            cols = slice(pr * LANES, (pr + 1) * LANES)
            qp = q_ref[:, cols] * QK_SCALE
            acc = []
            for half in range(2):
                h = 2 * pr + half
                blk, src = bd.kv_of(h)
                s = lax.dot_general(qp, kzs[blk][src][half], NT, preferred_element_type=F32) + bias_ref[h]
                s = jnp.where(mask, s, NEG_INF)
                m = jnp.max(s, axis=-1, keepdims=True)
                if has_sink:
                    m = jnp.maximum(m, sink_ref[0:1, h:h + 1])
                a = jnp.dot(jnp.exp(s - m).astype(BF16), vzs[blk][src][half], preferred_element_type=F32)
                if has_sink:
                    e = jnp.exp(sink_ref[0:1, h:h + 1] - m)
                    a = a + (jnp.where(lo, 0.0, e) if half == 0 else jnp.where(lo, e, 0.0))
                acc.append(a)
            o_scr[:, cols] = jnp.where(lo, acc[0] / pltpu.roll(acc[0], HEAD_DIM, 1),
                                       acc[1] / pltpu.roll(acc[1], HEAD_DIM, 1))
        o = o_scr[...]
        raw_ref[...] = o.astype(BF16)
        r = _rsq(jnp.mean(o * o, axis=-1, keepdims=True))
        yn_ref[...] = (o * r * gg_ref[...]).astype(BF16)

    in_specs = bd.qkv_specs() + [_const_spec(bias.shape)]
    args = [proj] * 7 + [bias]
    if has_sink:
        in_specs.append(_const_spec(sink.shape))
        args.append(sink)
    in_specs.append(_const_spec(gg.shape))
    args.append(gg)
    out = jax.ShapeDtypeStruct((S, bd.qw), BF16)
    return pl.pallas_call(
        body, name=bd.kind + "_fwd", grid=(bd.nb,), in_specs=in_specs,
        out_specs=[_row_spec(bd.bq, bd.qw), _row_spec(bd.bq, bd.qw)], out_shape=[out, out],
        scratch_shapes=[pltpu.VMEM((bd.bq, bd.qw), F32)],
        compiler_params=_params("parallel"),
    )(*args)


def _band_bwd(bd, proj, bias, sink, dy):
    S = bd.S
    has_sink = sink is not None

    def body(*refs):
        q_ref, kp, km, kn, vp, vm, vn, bias_ref = refs[:8]
        k = 8
        sink_ref = None
        if has_sink:
            sink_ref = refs[k]
            k += 1
        do_ref = refs[k]
        dq_ref, dkm, dvm, dkp, dvp, dkn, dvn, dbias_ref = refs[k + 1:k + 9]
        k += 9
        dsink_ref = None
        if has_sink:
            dsink_ref = refs[k]
            k += 1
        dk_scr, dv_scr = refs[k:k + 2]
        n = pl.program_id(0)

        @pl.when(n == 0)
        def _():
            dbias_ref[...] = jnp.zeros_like(dbias_ref)
            if has_sink:
                dsink_ref[...] = jnp.zeros_like(dsink_ref)

        mask = bd.mask(n, transposed=True)
        lane = lax.broadcasted_iota(I32, (1, LANES), 1)
        kzs, vzs = _band_kv_variants(bd, (kp, km, kn), 0.0), _band_kv_variants(bd, (vp, vm, vn), 0.0)
        nblk = bd.kw // LANES
        dk, dv = [None] * nblk, [None] * nblk
        for pr in range(bd.hq // 2):
            cols = slice(pr * LANES, (pr + 1) * LANES)
            qp, dop = q_ref[:, cols] * QK_SCALE, do_ref[:, cols]
            qz, doz = _half_variants(qp), _half_variants(dop)
            dq = None
            for half in range(2):
                h = 2 * pr + half
                blk, dst = bd.kv_of(h)
                kz, vz = kzs[blk][dst][half], vzs[blk][dst][half]
                s = lax.dot_general(kz, qp, NT, preferred_element_type=F32) + bias_ref[h]
                s = jnp.where(mask, s, NEG_INF)
                m = jnp.max(s, axis=0, keepdims=True)
                if has_sink:
                    m = jnp.maximum(m, sink_ref[0:1, h:h + 1])
                p = jnp.exp(s - m)
                l = jnp.sum(p, axis=0, keepdims=True)
                if has_sink:
                    e = jnp.exp(sink_ref[0:1, h:h + 1] - m)
                    l = l + e
                inv = 1.0 / l
                pn = p * inv
                dp = lax.dot_general(vz, dop, NT, preferred_element_type=F32)
                delta = jnp.sum(pn * dp, axis=0, keepdims=True)
                ds = pn * (dp - delta)
                dbias_ref[h] += ds
                if has_sink:
                    dsink_ref[...] += jnp.where(lane == h, -jnp.sum(e * inv * delta, axis=1, keepdims=True), 0.0)
                dsb = ds.astype(BF16)
                a = jnp.dot(pn.astype(BF16), doz[half][dst], preferred_element_type=F32)
                b = jnp.dot(dsb, qz[half][dst], preferred_element_type=F32)
                d = lax.dot_general(dsb, kz, TN, preferred_element_type=F32)
                dv[blk] = a if dv[blk] is None else dv[blk] + a
                dk[blk] = b if dk[blk] is None else dk[blk] + b
                dq = d if dq is None else dq + d
            dq_ref[:, cols] = dq * QK_SCALE
        for blk in range(nblk):
            cols = slice(blk * LANES, (blk + 1) * LANES)
            dk_scr[:, cols] = dk[blk]
            dv_scr[:, cols] = dv[blk]
        h0, h1 = bd.halo, bd.halo + bd.bq
        dkp[0] = dk_scr[0:h0, :]
        dkm[...] = dk_scr[h0:h1, :]
        dkn[0] = dk_scr[h1:bd.bk, :]
        dvp[0] = dv_scr[0:h0, :]
        dvm[...] = dv_scr[h0:h1, :]
        dvn[0] = dv_scr[h1:bd.bk, :]

    in_specs = bd.qkv_specs() + [_const_spec(bias.shape)]
    args = [proj] * 7 + [bias]
    if has_sink:
        in_specs.append(_const_spec(sink.shape))
        args.append(sink)
    in_specs.append(_row_spec(bd.bq, bd.qw))
    args.append(dy)
    halo_spec = pl.BlockSpec((1, bd.halo, bd.kw), lambda n: (n, 0, 0))
    halo_shape = jax.ShapeDtypeStruct((bd.nb, bd.halo, bd.kw), F32)
    main_shape = jax.ShapeDtypeStruct((S, bd.kw), F32)
    out_specs = [_row_spec(bd.bq, bd.qw), _row_spec(bd.bq, bd.kw), _row_spec(bd.bq, bd.kw),
                 halo_spec, halo_spec, halo_spec, halo_spec, _const_spec(bias.shape)]
    out_shape = [jax.ShapeDtypeStruct((S, bd.qw), F32), main_shape, main_shape,
                 halo_shape, halo_shape, halo_shape, halo_shape, jax.ShapeDtypeStruct(bias.shape, F32)]
    if has_sink:
        out_specs.append(_const_spec((1, LANES)))
        out_shape.append(jax.ShapeDtypeStruct((1, LANES), F32))
    return pl.pallas_call(
        body, name=bd.kind + "_bwd", grid=(bd.nb,), in_specs=in_specs, out_specs=out_specs, out_shape=out_shape,
        scratch_shapes=[pltpu.VMEM((bd.bk, bd.kw), F32), pltpu.VMEM((bd.bk, bd.kw), F32)],
        compiler_params=_params("arbitrary"),
    )(*args)


def _halo_to_rows(prev, nxt):
    nb, halo, w = prev.shape
    z = jnp.zeros((1, halo, w), prev.dtype)
    first = jnp.concatenate([z, nxt[:-1]], axis=0)
    second = jnp.concatenate([prev[1:], z], axis=0)
    return jnp.concatenate([first, second], axis=1).reshape(nb * 2 * halo, w)


AX_PAIRS = AX_W // LANES


AX_FWD_BLOCKS = (1024, 2048)
AX_BWD_BLOCKS = (1024, 1024)


def _ax_blocks(S, blocks):
    return _tile(S, blocks[0]), _tile(S, blocks[1])


def _left_half(shape):
    return lax.broadcasted_iota(I32, shape, len(shape) - 1) < HEAD_DIM


def _as_row(a):
    return jnp.transpose(a)[0:1, :]


def _half_variants(a, fill=0.0):
    lo = _left_half(a.shape)
    other = jnp.full_like(a, fill)
    swapped = pltpu.roll(a, HEAD_DIM, 1)
    return ((jnp.where(lo, a, other), jnp.where(lo, other, swapped)),
            (jnp.where(lo, swapped, other), jnp.where(lo, other, a)))


def _split_rider_refs(refs, n_in, n_out, rider):
    r_in, r_out = (len(rider.arrays), len(rider.out_shapes)) if rider else (0, 0)
    a, b, c = n_in + r_in, n_in + r_in + n_out, n_in + r_in + n_out + r_out
    n_sems = 3 if rider else 0
    return refs[:n_in], refs[n_in:a], refs[a:b], refs[b:c], refs[c:len(refs) - n_sems], refs[len(refs) - n_sems:]


def _ax_fwd(qc, kc, proj, gg, rider=None):
    S = qc.shape[0]
    bq, bk = _ax_blocks(S, AX_FWD_BLOCKS)
    nq, nk = S // bq, S // bk
    rep = AX_HEADS // AX_KV_HEADS

    def body(*refs):
        (q_ref, k_ref, v_ref, gg_ref), r_ins, (raw_ref, yn_ref, lse_ref), r_outs, (m_scr, acc_scr), r_sems = (
            _split_rider_refs(refs, 4, 3, rider))
        qi, kv = pl.program_id(0), pl.program_id(1)

        if rider:
            @pl.when((qi == 0) & (kv == 0))
            def _():
                rider.start(r_ins, r_outs, r_sems)

        @pl.when(kv == 0)
        def _():
            m_scr[...] = jnp.full(m_scr.shape, NEG_INF, F32)
            acc_scr[...] = jnp.zeros_like(acc_scr)

        kz, vz = _half_variants(k_ref[...]), _half_variants(v_ref[...], 1.0)
        for pr in range(AX_PAIRS):
            qp = q_ref[:, pr * LANES:(pr + 1) * LANES]
            for half in range(2):
                h = 2 * pr + half
                g = h // rep
                s = lax.dot_general(qp, kz[g][half], NT, preferred_element_type=F32)
                m_prev = m_scr[h]
                m_new = jnp.maximum(m_prev, jnp.max(s, axis=-1, keepdims=True))
                p = jnp.exp(s - jnp.tile(m_new, (1, bk // LANES)))
                acc_scr[h] = jnp.exp(m_prev - m_new) * acc_scr[h] + jnp.dot(
                    p.astype(BF16), vz[g][half], preferred_element_type=F32)
                m_scr[h] = m_new

        @pl.when(kv == nk - 1)
        def _():
            lo = _left_half((bq, LANES))
            ssq = jnp.zeros((bq, 1), F32)
            for pr in range(AX_PAIRS):
                a0, a1 = acc_scr[2 * pr], acc_scr[2 * pr + 1]
                r0, r1 = pltpu.roll(a0, HEAD_DIM, 1), pltpu.roll(a1, HEAD_DIM, 1)
                lse_ref[2 * pr] = _as_row(m_scr[2 * pr] + jnp.log(jnp.where(lo, r0, a0)))
                lse_ref[2 * pr + 1] = _as_row(m_scr[2 * pr + 1] + jnp.log(jnp.where(lo, a1, r1)))
                o = jnp.where(lo, a0 / r0, a1 / r1)
                acc_scr[pr] = o
                ssq = ssq + jnp.sum(o * o, axis=-1, keepdims=True)
            r = _rsq(ssq * (1.0 / AX_W))
            for pr in range(AX_PAIRS):
                cols = slice(pr * LANES, (pr + 1) * LANES)
                o = acc_scr[pr]
                raw_ref[:, cols] = o.astype(BF16)
                yn_ref[:, cols] = (o * r * gg_ref[:, cols]).astype(BF16)

        if rider:
            @pl.when((qi == nq - 1) & (kv == nk - 1))
            def _():
                rider.wait(r_ins, r_outs, r_sems)

    out = jax.ShapeDtypeStruct((S, AX_W), BF16)
    hbm = pl.BlockSpec(memory_space=pl.ANY)
    r_arrays, r_shapes, r_scratch = (rider.arrays, rider.out_shapes, rider.scratch) if rider else ([], [], [])
    res = pl.pallas_call(
        body, name="ax_fwd_gather" if rider else "ax_fwd", grid=(nq, nk),
        in_specs=[pl.BlockSpec((bq, AX_W), lambda i, j: (i, 0)),
                  pl.BlockSpec((bk, AX_KV_W), lambda i, j: (j, 0)),
                  pl.BlockSpec((bk, AX_KV_W), lambda i, j: (j, OFF_VC // AX_KV_W)),
                  _const_spec(gg.shape)] + [hbm] * len(r_arrays),
        out_specs=[pl.BlockSpec((bq, AX_W), lambda i, j: (i, 0)),
                   pl.BlockSpec((bq, AX_W), lambda i, j: (i, 0)),
                   pl.BlockSpec((AX_HEADS, 1, bq), lambda i, j: (0, 0, i))] + [hbm] * len(r_shapes),
        out_shape=[out, out, jax.ShapeDtypeStruct((AX_HEADS, 1, S), F32)] + r_shapes,
        scratch_shapes=[pltpu.VMEM((AX_HEADS, bq, LANES), F32), pltpu.VMEM((AX_HEADS, bq, LANES), F32)] + r_scratch,
        compiler_params=_params("arbitrary", "arbitrary"),
    )(qc, kc, proj, gg, *r_arrays)
    return res[0], res[1], res[2], list(res[3:])


def _ax_delta(dy, raw):
    S = dy.shape[0]
    tm = _tile(S, ROW_TILE)

    def body(do_ref, o_ref, delta_ref):
        lo = _left_half((tm, LANES))
        for pr in range(AX_PAIRS):
            cols = slice(pr * LANES, (pr + 1) * LANES)
            prod = do_ref[:, cols].astype(F32) * o_ref[:, cols].astype(F32)
            left = jnp.sum(jnp.where(lo, prod, 0.0), axis=-1, keepdims=True)
            right = jnp.sum(jnp.where(lo, 0.0, prod), axis=-1, keepdims=True)
            delta_ref[2 * pr] = _as_row(jnp.broadcast_to(left, (tm, LANES)))
            delta_ref[2 * pr + 1] = _as_row(jnp.broadcast_to(right, (tm, LANES)))

    return pl.pallas_call(
        body, name="ax_delta", grid=(S // tm,), in_specs=[_row_spec(tm, AX_W), _row_spec(tm, AX_W)],
        out_specs=pl.BlockSpec((AX_HEADS, 1, tm), lambda i: (0, 0, i)),
        out_shape=jax.ShapeDtypeStruct((AX_HEADS, 1, S), F32), compiler_params=_params("parallel"),
    )(dy, raw)


def _ax_bwd(qc, kc, proj, dy, lse_row, delta_row, rider=None):
    S = qc.shape[0]
    bq, bk = _ax_blocks(S, AX_BWD_BLOCKS)
    nq, nk = S // bq, S // bk
    rep = AX_HEADS // AX_KV_HEADS

    def body(*refs):
        ((q_ref, k_ref, v_ref, do_ref, lse_ref, delta_ref), r_ins, (dk_ref, dv_ref, dq_hbm), r_outs, (dq_scr, sem),
         r_sems) = _split_rider_refs(refs, 6, 3, rider)
        j, i = pl.program_id(0), pl.program_id(1)

        if rider:
            @pl.when((j == 0) & (i == 0))
            def _():
                rider.start(r_ins, r_outs, r_sems)

        @pl.when(i == 0)
        def _():
            dk_ref[...] = jnp.zeros_like(dk_ref)
            dv_ref[...] = jnp.zeros_like(dv_ref)

        @pl.when(j == 0)
        def _():
            dq_scr[i] = jnp.zeros((bq, AX_W), F32)

        kz, vz = _half_variants(k_ref[...]), _half_variants(v_ref[...])
        dk, dv = None, None
        for pr in range(AX_PAIRS):
            cols = slice(pr * LANES, (pr + 1) * LANES)
            qp, dop = q_ref[:, cols], do_ref[:, cols]
            qz, doz = _half_variants(qp), _half_variants(dop)
            dq = None
            for half in range(2):
                h = 2 * pr + half
                g = h // rep
                s_t = lax.dot_general(kz[g][half], qp, NT, preferred_element_type=F32)
                p_t = jnp.exp(s_t - lse_ref[h])
                dp_t = lax.dot_general(vz[g][half], dop, NT, preferred_element_type=F32)
                ds_t = (p_t * (dp_t - delta_ref[h])).astype(BF16)
                a = jnp.dot(p_t.astype(BF16), doz[half][g], preferred_element_type=F32)
                b = jnp.dot(ds_t, qz[half][g], preferred_element_type=F32)
                d = lax.dot_general(ds_t, kz[g][half], TN, preferred_element_type=F32)
                dv = a if dv is None else dv + a
                dk = b if dk is None else dk + b
                dq = d if dq is None else dq + d
            dq_scr[i, :, cols] += dq
        dv_ref[...] += dv
        dk_ref[...] += dk

        @pl.when(j == nk - 1)
        def _():
            dq_scr[i] = dq_scr[i] * QK_SCALE
            out = pltpu.make_async_copy(dq_scr.at[i], dq_hbm.at[pl.ds(pl.multiple_of(i * bq, bq), bq), :], sem)
            out.start()
            out.wait()

        if rider:
            @pl.when((j == nk - 1) & (i == nq - 1))
            def _():
                rider.wait(r_ins, r_outs, r_sems)

    qspec = pl.BlockSpec((bq, AX_W), lambda j, i: (i, 0))
    kspec = pl.BlockSpec((bk, AX_KV_W), lambda j, i: (j, 0))
    stat = pl.BlockSpec((AX_HEADS, 1, bq), lambda j, i: (0, 0, i))
    out = jax.ShapeDtypeStruct((S, AX_KV_W), F32)
    hbm = pl.BlockSpec(memory_space=pl.ANY)
    r_arrays, r_shapes, r_scratch = (rider.arrays, rider.out_shapes, rider.scratch) if rider else ([], [], [])
    res = pl.pallas_call(
        body, name="ax_bwd_scatter" if rider else "ax_bwd", grid=(nk, nq),
        in_specs=[qspec, kspec, pl.BlockSpec((bk, AX_KV_W), lambda j, i: (j, OFF_VC // AX_KV_W)),
                  qspec, stat, stat] + [hbm] * len(r_arrays),
        out_specs=[kspec, kspec, hbm] + [hbm] * len(r_shapes),
        out_shape=[out, out, jax.ShapeDtypeStruct((S, AX_W), F32)] + r_shapes,
        scratch_shapes=[pltpu.VMEM((nq, bq, AX_W), F32), pltpu.SemaphoreType.DMA] + r_scratch,
        compiler_params=_params("arbitrary", "arbitrary"),
    )(qc, kc, proj, dy, lse_row, delta_row, *r_arrays)
    return res[2], res[0], res[1], list(res[3:])


def _oproj_fwd(x, yna, ynb, ync, w, gt):
    S, D = x.shape
    tm = _tile(S, ROW_TILE)

    def body(x_ref, a_ref, b_ref, c_ref, w_ref, gt_ref, x1_ref, ao_ref, yn_ref):
        yn_ref[:, 0:NA_W] = a_ref[...]
        yn_ref[:, NA_W:NA_W + SW_W] = b_ref[...]
        yn_ref[:, NA_W + SW_W:MIX_WIDTH] = c_ref[...]
        acc = jnp.dot(yn_ref[...], w_ref[...], preferred_element_type=F32)
        ao_ref[...] = acc.astype(BF16)
        x1_ref[...] = x_ref[...] + gt_ref[...] * acc

    return pl.pallas_call(
        body, name="oproj_fwd", grid=(S // tm,),
        in_specs=[_row_spec(tm, D), _row_spec(tm, NA_W), _row_spec(tm, SW_W), _row_spec(tm, AX_W),
                  _const_spec(w.shape), _const_spec((1, D))],
        out_specs=[_row_spec(tm, D), _row_spec(tm, D), _row_spec(tm, MIX_WIDTH)],
        out_shape=[jax.ShapeDtypeStruct((S, D), F32), jax.ShapeDtypeStruct((S, D), BF16),
                   jax.ShapeDtypeStruct((S, MIX_WIDTH), BF16)],
        compiler_params=_params("parallel"),
    )(x, yna, ynb, ync, w, gt)


def _gu_fwd(x, g, sc, sh, w):
    S, D = x.shape
    tn = w.shape[2] // 2
    F2 = 4 * tn
    tm = _tile(S, ROW_TILE)

    def body(x_ref, g_ref, sc_ref, sh_ref, w_ref, h_ref, gu_ref, act_ref):
        @pl.when(pl.program_id(1) == 0)
        def _():
            h_ref[...] = _ln_mod(x_ref[...], g_ref[...], sc_ref[...], sh_ref[...]).astype(BF16)

        acc = jnp.dot(h_ref[...], w_ref[pl.program_id(1)], preferred_element_type=F32)
        gu_ref[...] = acc.astype(BF16)
        gate, up = acc[:, :tn], acc[:, tn:]
        act_ref[...] = (gate * (1.0 / (1.0 + jnp.exp(-gate))) * up).astype(BF16)

    vec = pl.BlockSpec((1, D), lambda i, j: (0, 0))
    return pl.pallas_call(
        body, name="gu_fwd", grid=(S // tm, 2),
        in_specs=[pl.BlockSpec((tm, D), lambda i, j: (i, 0)), vec, vec, vec,
                  pl.BlockSpec((2, D, 2 * tn), lambda i, j: (0, 0, 0))],
        out_specs=[pl.BlockSpec((tm, D), lambda i, j: (i, 0)), pl.BlockSpec((tm, 2 * tn), lambda i, j: (i, j)),
                   pl.BlockSpec((tm, tn), lambda i, j: (i, j))],
        out_shape=[jax.ShapeDtypeStruct((S, D), BF16), jax.ShapeDtypeStruct((S, F2), BF16),
                   jax.ShapeDtypeStruct((S, F2 // 2), BF16)],
        compiler_params=_params("parallel", "arbitrary"),
    )(x, g, sc, sh, w)


def _down_fwd(x, act, w, gt):
    S, D = x.shape
    F = act.shape[1]
    tm = _tile(S, ROW_TILE)

    def body(x_ref, a_ref, w_ref, gt_ref, x2_ref, fo_ref):
        acc = jnp.dot(a_ref[...], w_ref[...], preferred_element_type=F32)
        fo_ref[...] = acc.astype(BF16)
        x2_ref[...] = x_ref[...] + gt_ref[...] * acc

    return pl.pallas_call(
        body, name="down_fwd", grid=(S // tm,),
        in_specs=[_row_spec(tm, D), _row_spec(tm, F), _const_spec(w.shape), _const_spec((1, D))],
        out_specs=[_row_spec(tm, D), _row_spec(tm, D)],
        out_shape=[jax.ShapeDtypeStruct((S, D), F32), jax.ShapeDtypeStruct((S, D), BF16)],
        compiler_params=_params("parallel"),
    )(x, act, w, gt)


def _final_loss(x, g, target):
    S, D = x.shape
    tm = _tile(S, ROW_TILE)

    def body(x_ref, g_ref, t_ref, dx_ref, loss_ref, dg_ref):
        @pl.when(pl.program_id(0) == 0)
        def _():
            loss_ref[...] = jnp.zeros_like(loss_ref)
            dg_ref[...] = jnp.zeros_like(dg_ref)

        xv = x_ref[...]
        r = _rsq(jnp.mean(xv * xv, axis=-1, keepdims=True))
        xhat = xv * r
        err = xhat * g_ref[...] - t_ref[...]
        loss_ref[...] += 0.5 * jnp.sum(jnp.mean(err * err, axis=-1, keepdims=True), axis=0, keepdims=True)
        dy = err * (1.0 / D)
        dg_ref[...] += jnp.sum(dy * xhat, axis=0, keepdims=True)
        dxh = dy * g_ref[...]
        dx_ref[...] = r * (dxh - xhat * jnp.mean(dxh * xhat, axis=-1, keepdims=True))

    return pl.pallas_call(
        body, name="final_loss", grid=(S // tm,),
        in_specs=[_row_spec(tm, D), _const_spec((1, D)), _row_spec(tm, D)],
        out_specs=[_row_spec(tm, D), _const_spec((1, LANES)), _const_spec((1, D))],
        out_shape=[jax.ShapeDtypeStruct((S, D), F32), jax.ShapeDtypeStruct((1, LANES), F32),
                   jax.ShapeDtypeStruct((1, D), F32)],
        compiler_params=_params("arbitrary"),
    )(x, g, target)


def _ffn_bwd1(dx2, fo, gt, w_down, gu):
    S, D = dx2.shape
    F2 = gu.shape[1]
    tn = F2 // 4
    tm = _tile(S, ROW_TILE)

    def body(dx_ref, fo_ref, gt_ref, w_ref, gu_ref, dfo_ref, dgu_ref, dgt_ref):
        i, j = pl.program_id(0), pl.program_id(1)

        @pl.when((i == 0) & (j == 0))
        def _():
            dgt_ref[...] = jnp.zeros_like(dgt_ref)

        @pl.when(j == 0)
        def _():
            dxv = dx_ref[...]
            dfo_ref[...] = (dxv * gt_ref[...]).astype(BF16)
            dgt_ref[...] += jnp.sum(dxv * fo_ref[...].astype(F32), axis=0, keepdims=True)

        dact = lax.dot_general(dfo_ref[...], w_ref[j], NT, preferred_element_type=F32)
        gate = gu_ref[:, :tn].astype(F32)
        up = gu_ref[:, tn:].astype(F32)
        sig = 1.0 / (1.0 + jnp.exp(-gate))
        dgu_ref[:, :tn] = (dact * up * (sig * (1.0 + gate * (1.0 - sig)))).astype(BF16)
        dgu_ref[:, tn:] = (dact * (gate * sig)).astype(BF16)

    vec = pl.BlockSpec((1, D), lambda i, j: (0, 0))
    row = pl.BlockSpec((tm, D), lambda i, j: (i, 0))
    return pl.pallas_call(
        body, name="ffn_bwd1", grid=(S // tm, 2),
        in_specs=[row, row, vec, pl.BlockSpec((2, tn, D), lambda i, j: (0, 0, 0)),
                  pl.BlockSpec((tm, 2 * tn), lambda i, j: (i, j))],
        out_specs=[row, pl.BlockSpec((tm, 2 * tn), lambda i, j: (i, j)), vec],
        out_shape=[jax.ShapeDtypeStruct((S, D), BF16), jax.ShapeDtypeStruct((S, F2), BF16),
                   jax.ShapeDtypeStruct((1, D), F32)],
        compiler_params=_params("arbitrary", "arbitrary"),
    )(dx2, fo, gt, w_down.reshape(2, tn, D), gu)


def _nt_ln_bwd(a, w, x, g, sc, dres, name):
    S, D = x.shape
    K = a.shape[1]
    tm = _tile(S, ROW_TILE_WIDE)

    def body(a_ref, w_ref, x_ref, g_ref, sc_ref, dres_ref, dx_ref, dsh_ref, dsc_ref, dg_ref):
        @pl.when(pl.program_id(0) == 0)
        def _():
            dsh_ref[...] = jnp.zeros_like(dsh_ref)
            dsc_ref[...] = jnp.zeros_like(dsc_ref)
            dg_ref[...] = jnp.zeros_like(dg_ref)

        if len(w.shape) == 2:
            dh = lax.dot_general(a_ref[...], w_ref[...], NT, preferred_element_type=F32)
        else:
            kt = w.shape[2]
            dh = sum(lax.dot_general(a_ref[:, t * kt:(t + 1) * kt], w_ref[t], NT, preferred_element_type=F32)
                     for t in range(w.shape[0]))
        xv = x_ref[...]
        r = _rsq(jnp.mean(xv * xv, axis=-1, keepdims=True))
        xhat = xv * r
        gv = g_ref[...]
        dsh_ref[...] += jnp.sum(dh, axis=0, keepdims=True)
        dsc_ref[...] += jnp.sum(dh * (xhat * gv), axis=0, keepdims=True)
        dn = dh * (1.0 + sc_ref[...])
        dg_ref[...] += jnp.sum(dn * xhat, axis=0, keepdims=True)
        dxh = dn * gv
        dx_ref[...] = dres_ref[...] + r * (dxh - xhat * jnp.mean(dxh * xhat, axis=-1, keepdims=True))

    vec = _const_spec((1, D))
    vshape = jax.ShapeDtypeStruct((1, D), F32)
    return pl.pallas_call(
        body, name=name, grid=(S // tm,),
        in_specs=[_row_spec(tm, K), _const_spec(w.shape), _row_spec(tm, D), vec, vec, _row_spec(tm, D)],
        out_specs=[_row_spec(tm, D), vec, vec, vec],
        out_shape=[jax.ShapeDtypeStruct((S, D), F32), vshape, vshape, vshape],
        compiler_params=_params("arbitrary"),
    )(a, w, x, g, sc, dres)


def _oproj_bwd(dx1, ao, gt, w, ya, yb, yc, gg):
    S, D = dx1.shape
    tm = _tile(S, ROW_TILE)
    groups = ((0, NA_W), (NA_W, SW_W), (NA_W + SW_W, AX_W))

    def body(dx_ref, ao_ref, gt_ref, w_ref, ya_ref, yb_ref, yc_ref, gg_ref,
             dao_ref, dya_ref, dyb_ref, dyc_ref, dgt_ref, dgg_ref):
        @pl.when(pl.program_id(0) == 0)
        def _():
            dgt_ref[...] = jnp.zeros_like(dgt_ref)
            dgg_ref[...] = jnp.zeros_like(dgg_ref)

        dxv = dx_ref[...]
        dao = (dxv * gt_ref[...]).astype(BF16)
        dao_ref[...] = dao
        dgt_ref[...] += jnp.sum(dxv * ao_ref[...].astype(F32), axis=0, keepdims=True)
        dyn = lax.dot_general(dao, w_ref[...], NT, preferred_element_type=F32)
        for (off, wd), y_ref, dy_ref in zip(groups, (ya_ref, yb_ref, yc_ref), (dya_ref, dyb_ref, dyc_ref)):
            y = y_ref[...].astype(F32)
            d = dyn[:, off:off + wd]
            r = _rsq(jnp.mean(y * y, axis=-1, keepdims=True))
            yhat = y * r
            dgg_ref[:, off:off + wd] += jnp.sum(d * yhat, axis=0, keepdims=True)
            dyh = d * gg_ref[:, off:off + wd]
            dy_ref[...] = (r * (dyh - yhat * jnp.mean(dyh * yhat, axis=-1, keepdims=True))).astype(BF16)

    vec = _const_spec((1, D))
    mvec = _const_spec((1, MIX_WIDTH))
    return pl.pallas_call(
        body, name="oproj_bwd", grid=(S // tm,),
        in_specs=[_row_spec(tm, D), _row_spec(tm, D), vec, _const_spec(w.shape),
                  _row_spec(tm, NA_W), _row_spec(tm, SW_W), _row_spec(tm, AX_W), mvec],
        out_specs=[_row_spec(tm, D), _row_spec(tm, NA_W), _row_spec(tm, SW_W), _row_spec(tm, AX_W), vec, mvec],
        out_shape=[jax.ShapeDtypeStruct((S, D), BF16), jax.ShapeDtypeStruct((S, NA_W), BF16),
                   jax.ShapeDtypeStruct((S, SW_W), BF16), jax.ShapeDtypeStruct((S, AX_W), BF16),
                   jax.ShapeDtypeStruct((1, D), F32), jax.ShapeDtypeStruct((1, MIX_WIDTH), F32)],
        compiler_params=_params("arbitrary"),
    )(dx1, ao, gt, w, ya, yb, yc, gg)


def _dproj_assemble(proj, na, sw, ax, gq128, gk128, rope):
    S = proj.shape[0]
    tm = _tile(S, ROW_TILE)
    cos, sa, sb = rope

    def body(proj_ref, qa, ka, kah, va, vah, qb, kb, kbh, vb, vbh, qc, kc, vc,
             gq_ref, gk_ref, cos_ref, sa_ref, sb_ref, out_ref, dgq_ref, dgk_ref):
        @pl.when(pl.program_id(0) == 0)
        def _():
            dgq_ref[...] = jnp.zeros_like(dgq_ref)
            dgk_ref[...] = jnp.zeros_like(dgk_ref)

        out_ref[:, OFF_QA:OFF_KA] = qa[...].astype(BF16)
        out_ref[:, OFF_KA:OFF_VA] = (ka[...] + kah[...]).astype(BF16)
        out_ref[:, OFF_VA:OFF_QB] = (va[...] + vah[...]).astype(BF16)
        out_ref[:, OFF_QB:OFF_KB] = qb[...].astype(BF16)
        out_ref[:, OFF_KB:OFF_VB] = (kb[...] + kbh[...]).astype(BF16)
        out_ref[:, OFF_VB:OFF_QC] = (vb[...] + vbh[...]).astype(BF16)
        c, a, b = cos_ref[...], sa_ref[...], sb_ref[...]
        for j in range(AX_W // LANES):
            cols = slice(OFF_QC + j * LANES, OFF_QC + (j + 1) * LANES)
            dx, dg = _qk_prep_bwd_chunk(proj_ref[:, cols].astype(F32), qc[:, j * LANES:(j + 1) * LANES],
                                        gq_ref[...], c, a, b)
            out_ref[:, cols] = dx.astype(BF16)
            dgq_ref[...] += dg
        for j in range(AX_KV_W // LANES):
            cols = slice(OFF_KC + j * LANES, OFF_KC + (j + 1) * LANES)
            dx, dg = _qk_prep_bwd_chunk(proj_ref[:, cols].astype(F32), kc[:, j * LANES:(j + 1) * LANES],
                                        gk_ref[...], c, a, b)
            out_ref[:, cols] = dx.astype(BF16)
            dgk_ref[...] += dg
        out_ref[:, OFF_VC:IN_WIDTH] = vc[...].astype(BF16)

    v128 = _const_spec((1, LANES))
    r = lambda w: _row_spec(tm, w)
    return pl.pallas_call(
        body, name="dproj_assemble", grid=(S // tm,),
        in_specs=[r(IN_WIDTH), r(NA_W), r(NA_W), r(NA_W), r(NA_W), r(NA_W),
                  r(SW_W), r(SW_KV_W), r(SW_KV_W), r(SW_KV_W), r(SW_KV_W),
                  r(AX_W), r(AX_KV_W), r(AX_KV_W), v128, v128, r(LANES), r(LANES), r(LANES)],
        out_specs=[r(IN_WIDTH), v128, v128],
        out_shape=[jax.ShapeDtypeStruct((S, IN_WIDTH), BF16), jax.ShapeDtypeStruct((1, LANES), F32),
                   jax.ShapeDtypeStruct((1, LANES), F32)],
        compiler_params=_params("arbitrary"),
    )(proj, *na, *sw, *ax, gq128, gk128, cos, sa, sb)


def _tn_matmul(a, b, name):
    S, Ka = a.shape
    Nb = b.shape[1]
    tm = _tile(Ka, 1408, LANES)
    tn = _tile(Nb, 1408, LANES)
    tk = _tile(S, TOKEN_CHUNK)
    nk = S // tk

    def body(a_ref, b_ref, o_ref, acc_ref):
        k = pl.program_id(2)

        @pl.when(k == 0)
        def _():
            acc_ref[...] = jnp.zeros_like(acc_ref)

        acc_ref[...] += lax.dot_general(a_ref[...], b_ref[...], TN, preferred_element_type=F32)

        @pl.when(k == nk - 1)
        def _():
            o_ref[...] = acc_ref[...].astype(BF16)

    return pl.pallas_call(
        body, name=name, grid=(Ka // tm, Nb // tn, nk),
        in_specs=[pl.BlockSpec((tk, tm), lambda i, j, k: (k, i)), pl.BlockSpec((tk, tn), lambda i, j, k: (k, j))],
        out_specs=pl.BlockSpec((tm, tn), lambda i, j, k: (i, j)),
        out_shape=jax.ShapeDtypeStruct((Ka, Nb), BF16),
        scratch_shapes=[pltpu.VMEM((tm, tn), F32)],
        compiler_params=_params("parallel", "parallel", "arbitrary"),
    )(a, b)


def _na_index(bd):
    rq = jnp.arange(bd.bq // GRID_W)
    rk = jnp.arange(bd.bk // GRID_W)
    col = jnp.arange(GRID_W)
    ri = jnp.clip(rk[None, :] - rq[:, None] - bd.halo // GRID_W + NA_WIN_ROWS - 1, 0, 2 * NA_WIN_ROWS - 2)
    ci = jnp.clip(col[None, :] - col[:, None] + NA_WIN_COLS - 1, 0, 2 * NA_WIN_COLS - 2)
    return ri, ci


def _na_one_hots(bd):
    ri, ci = _na_index(bd)
    oh_r = jax.nn.one_hot(ri, 2 * NA_WIN_ROWS - 1, dtype=F32)
    oh_c = jax.nn.one_hot(ci, 2 * NA_WIN_COLS - 1, dtype=F32)
    return oh_r, oh_c


def _na_bias(bd, rpb):
    oh_r, oh_c = _na_one_hots(bd)
    t = jnp.einsum("hab,qra->hqrb", rpb, oh_r, precision=lax.Precision.HIGHEST)
    b = jnp.einsum("hqrb,ckb->hqcrk", t, oh_c, precision=lax.Precision.HIGHEST)
    return b.reshape(NA_HEADS, bd.bq, bd.bk)


def _na_bias_t(bd, dbias):
    oh_r, oh_c = _na_one_hots(bd)
    d5 = dbias.reshape(NA_HEADS, bd.bq // GRID_W, GRID_W, bd.bk // GRID_W, GRID_W)
    t = jnp.einsum("hqcrk,ckb->hqrb", d5, oh_c, precision=lax.Precision.HIGHEST)
    return jnp.einsum("hqrb,qra->hab", t, oh_r, precision=lax.Precision.HIGHEST)


def _t5_bucket(rel):
    nb = T5_BUCKETS // 2
    ret = (rel > 0).astype(I32) * nb
    n = jnp.abs(rel)
    max_exact = nb // 2
    nf = jnp.maximum(n, max_exact).astype(F32)
    large = max_exact + (jnp.log(nf / max_exact) / math.log(T5_MAX_DIST / max_exact)
                         * (nb - max_exact)).astype(I32)
    large = jnp.minimum(large, nb - 1)
    return ret + jnp.where(n < max_exact, n, large)


def _sw_bucket(bd):
    rel = (jnp.arange(bd.bk) - bd.halo)[None, :] - jnp.arange(bd.bq)[:, None]
    return _t5_bucket(rel)


def _sw_bias(bd, t5):
    def body(t5_ref, bucket_ref, out_ref):
        bucket = bucket_ref[...]
        for h in range(SW_HEADS):
            acc = jnp.zeros((bd.bq, bd.bk), F32)
            for b in range(T5_BUCKETS):
                acc = jnp.where(bucket == b, t5_ref[b, h], acc)
            out_ref[h] = acc

    vmem = pl.BlockSpec(memory_space=pltpu.VMEM)
    return pl.pallas_call(
        body, name="sw_bias", in_specs=[pl.BlockSpec(memory_space=pltpu.SMEM), vmem], out_specs=vmem,
        out_shape=jax.ShapeDtypeStruct((SW_HEADS, bd.bq, bd.bk), F32),
    )(t5, _sw_bucket(bd))


def _sw_bias_t(bd, dbias):
    def body(bucket_ref, d_ref, out_ref):
        bucket = bucket_ref[...]
        lane = lax.broadcasted_iota(I32, (1, LANES), 1)
        for b in range(T5_BUCKETS):
            hit = bucket == b
            row = jnp.zeros((1, LANES), F32)
            for h in range(SW_HEADS):
                row = jnp.where(lane == h, jnp.sum(jnp.where(hit, d_ref[h], 0.0)), row)
            out_ref[b:b + 1, :] = row

    vmem = pl.BlockSpec(memory_space=pltpu.VMEM)
    out = pl.pallas_call(
        body, name="sw_bias_t", in_specs=[vmem, vmem], out_specs=vmem,
        out_shape=jax.ShapeDtypeStruct((T5_BUCKETS, LANES), F32),
    )(_sw_bucket(bd), dbias)
    return out[:, :SW_HEADS]


def _local_step(x, target, mod, w_in, w_o, w_gu, w_down, g_attn, rpb_na, sink_sw, t5_table, gq_ax, gk_ax,
                g_group, g_ffn, g_final, late_shards=None):
    S, D = x.shape
    riding = late_shards is not None
    w_in = [w_in[l] for l in range(w_in.shape[0])]
    rope = _rope_tables(S)
    na, sw = _Band("na", S), _Band("sw", S)
    two = lambda v: jnp.concatenate([v, v])[None, :]
    sw_bias = _sw_bias(sw, t5_table)
    saved = []
    for l in range(DEPTH):
        sh_a, sc_a, gt_a, sh_f, sc_f, gt_f = [mod[l, k * D:(k + 1) * D][None, :] for k in range(6)]
        gq128, gk128 = two(gq_ax[l]), two(gk_ax[l])
        gg = g_group[l][None, :]
        sink = jnp.pad(sink_sw[l], (0, LANES - SW_HEADS))[None, :]
        na_bias = _na_bias(na, rpb_na[l])
        h, proj, qc, kc = _inproj_fwd(x, g_attn[l][None, :], sc_a, sh_a, w_in[l], gq128, gk128, rope)
        ya, yna = _band_fwd(na, proj, na_bias, None, gg[:, :NA_W])
        yb, ynb = _band_fwd(sw, proj, sw_bias, sink, gg[:, NA_W:NA_W + SW_W])
        rider = None
        if riding and l == 0:
            rider = _gather_exchange([late_shards[k] for k in BIG], [BIG_AXIS[k] for k in BIG],
                                     [BIG_ORDER[k] for k in BIG])
        yc, ync, lse, got = _ax_fwd(qc, kc, proj, gg[:, NA_W + SW_W:], rider)
        if rider:
            w_in_late, w_o, w_gu, w_down = got
            w_in += [w_in_late[k] for k in range(w_in_late.shape[0])]
        x1, ao, yn = _oproj_fwd(x, yna, ynb, ync, w_o[l], gt_a)
        hf, gu, act = _gu_fwd(x1, g_ffn[l][None, :], sc_f, sh_f, w_gu[l])
        x2, fo = _down_fwd(x1, act, w_down[l], gt_f)
        saved.append(dict(x=x, x1=x1, h=h, proj=proj, qc=qc, kc=kc, ya=ya, yb=yb, yc=yc, lse=lse, ao=ao, yn=yn,
                          hf=hf, gu=gu, act=act, fo=fo, na_bias=na_bias, sink=sink, gq128=gq128, gk128=gk128,
                          gg=gg, mods=(sh_a, sc_a, gt_a, sh_f, sc_f, gt_f)))
        x = x2

    dx, loss_row, dg_final = _final_loss(x, g_final[None, :], target)
    gw = {k: [None] * DEPTH for k in ("w_in", "w_o", "w_gu", "w_down")}
    gs = {k: [None] * DEPTH for k in ("b_mod", "g_attn", "rpb_na", "sink_sw", "gq_ax", "gk_ax", "g_group", "g_ffn")}
    d_t5 = jnp.zeros((T5_BUCKETS, SW_HEADS), F32)
    for l in reversed(range(DEPTH)):
        s = saved[l]
        sh_a, sc_a, gt_a, sh_f, sc_f, gt_f = s["mods"]
        dfo, dgu, dgt_f = _ffn_bwd1(dx, s["fo"], gt_f, w_down[l], s["gu"])
        gw["w_down"][l] = _tn_matmul(s["act"], dfo, "dw_down")
        gw["w_gu"][l] = _tn_matmul(s["hf"], dgu, "dw_gu")
        dx1, dsh_f, dsc_f, gs["g_ffn"][l] = _nt_ln_bwd(dgu, w_gu[l], s["x1"], g_ffn[l][None, :], sc_f, dx, "ffn_bwd2")
        dao, dya, dyb, dyc, dgt_a, gs["g_group"][l] = _oproj_bwd(dx1, s["ao"], gt_a, w_o[l], s["ya"], s["yb"],
                                                                 s["yc"], s["gg"])
        gw["w_o"][l] = _tn_matmul(s["yn"], dao, "dw_o")
        dqa, dka, dva, dkap, dvap, dkan, dvan, dbias_na = _band_bwd(na, s["proj"], jnp.swapaxes(s["na_bias"], 1, 2), None, dya)
        dqb, dkb, dvb, dkbp, dvbp, dkbn, dvbn, dbias_sw, dsink = _band_bwd(sw, s["proj"], jnp.swapaxes(sw_bias, 1, 2), s["sink"], dyb)
        rider = None
        if riding and l == 0:
            ready = [("w_in", k) for k in range(1, DEPTH)] + [(n, k) for n in BIG[1:] for k in range(DEPTH)]
            rider = _scatter_exchange(
                [gw[n][k] for n, k in ready], [BIG_AXIS[n] - 1 for n, _ in ready], [BIG_ORDER[n] for n, _ in ready],
                [None if n == "w_in" else (BIG.index(n), k, DEPTH) for n, k in ready])
        dqc, dkc, dvc, sent = _ax_bwd(s["qc"], s["kc"], s["proj"], dyc, s["lse"], _ax_delta(dyc, s["yc"]), rider)
        dproj, dgq, dgk = _dproj_assemble(
            s["proj"], (dqa, dka, _halo_to_rows(dkap, dkan), dva, _halo_to_rows(dvap, dvan)),
            (dqb, dkb, _halo_to_rows(dkbp, dkbn), dvb, _halo_to_rows(dvbp, dvbn)), (dqc, dkc, dvc),
            s["gq128"], s["gk128"], rope)
        gw["w_in"][l] = _tn_matmul(s["h"], dproj, "dw_in")
        dx, dsh_a, dsc_a, gs["g_attn"][l] = _nt_ln_bwd(dproj, w_in[l], s["x"], g_attn[l][None, :], sc_a, dx1,
                                                       "inproj_bwd")
        gs["b_mod"][l] = jnp.concatenate([dsh_a, dsc_a, dgt_a, dsh_f, dsc_f, dgt_f], axis=1)[0]
        gs["rpb_na"][l] = _na_bias_t(na, jnp.swapaxes(dbias_na, 1, 2))
        gs["sink_sw"][l] = dsink[0, :SW_HEADS]
        d_t5 = d_t5 + _sw_bias_t(sw, jnp.swapaxes(dbias_sw, 1, 2))
        gs["gq_ax"][l] = dgq[0, :HEAD_DIM] + dgq[0, HEAD_DIM:]
        gs["gk_ax"][l] = dgk[0, :HEAD_DIM] + dgk[0, HEAD_DIM:]
        gs["g_attn"][l] = gs["g_attn"][l][0]
        gs["g_ffn"][l] = gs["g_ffn"][l][0]
        gs["g_group"][l] = gs["g_group"][l][0]

    if riding:
        (first,) = _chip_scatter([gw["w_in"][0]], [BIG_AXIS["w_in"] - 1], [BIG_ORDER["w_in"]], "scatter_w_in0")
        gw = dict(zip(BIG[1:], sent[DEPTH - 1:]), w_in=jnp.stack([first] + sent[:DEPTH - 1], axis=1))
    else:
        gw = {k: jnp.stack(v) for k, v in gw.items()}
    small = {k: jnp.stack(v) for k, v in gs.items()}
    small["t5_table"] = d_t5
    small["g_final"] = dg_final[0]
    return loss_row[0, 0], dx, gw, small


MOD_ROWS = 16


def _mod_fwd(cond16, w):
    L, D, C = w.shape
    tn = _tile(C, 512, LANES)

    def body(c_ref, w_ref, o_ref):
        o_ref[0] = jnp.dot(c_ref[...], w_ref[0].astype(BF16), preferred_element_type=F32)

    return pl.pallas_call(
        body, name="mod_fwd", grid=(L, C // tn),
        in_specs=[pl.BlockSpec((MOD_ROWS, D), lambda l, j: (0, 0)), pl.BlockSpec((1, D, tn), lambda l, j: (l, 0, j))],
        out_specs=pl.BlockSpec((1, MOD_ROWS, tn), lambda l, j: (l, 0, j)),
        out_shape=jax.ShapeDtypeStruct((L, MOD_ROWS, C), F32),
        compiler_params=_params("parallel", "parallel"),
    )(cond16, w)


def _adamw_math(w, g, m, v):
    m = ADAM_B1 * m + (1.0 - ADAM_B1) * g
    v = ADAM_B2 * v + (1.0 - ADAM_B2) * (g * g)
    m_hat = m / (1.0 - ADAM_B1 ** ADAM_STEP)
    v_hat = v / (1.0 - ADAM_B2 ** ADAM_STEP)
    delta = -ADAM_LR * (m_hat / (jnp.sqrt(v_hat) + ADAM_EPS) + ADAM_WD * w)
    return delta, m, v


def _adamw(w, m, v, parts, name):
    R, C = w.shape
    tr = _tile(R, 256)
    n = len(parts)

    def body(*refs):
        w_ref, m_ref, v_ref = refs[:3]
        g = refs[3][...]
        for p in refs[4:3 + n]:
            g = g + p[...]
        g_ref, d_ref, m2_ref, v2_ref = refs[3 + n:]
        g_ref[...] = g
        d_ref[...], m2_ref[...], v2_ref[...] = _adamw_math(w_ref[...], g, m_ref[...], v_ref[...])

    spec = _row_spec(tr, C)
    shape = jax.ShapeDtypeStruct((R, C), F32)
    return pl.pallas_call(
        body, name=name, grid=(R // tr,), in_specs=[spec] * (3 + n), out_specs=[spec] * 4, out_shape=[shape] * 4,
        compiler_params=_params("parallel"),
    )(w, m, v, *parts)


def _wmod_adamw(cond_t, dmod16, w, m, v):
    L, D, C = w.shape
    tr = _tile(D, 256)

    def body(c_ref, d_ref, w_ref, m_ref, v_ref, g_ref, dl_ref, m2_ref, v2_ref):
        g = jnp.dot(c_ref[...], d_ref[0], preferred_element_type=F32)
        g_ref[0] = g
        dl_ref[0], m2_ref[0], v2_ref[0] = _adamw_math(w_ref[0], g, m_ref[0], v_ref[0])

    spec = pl.BlockSpec((1, tr, C), lambda l, i: (l, i, 0))
    shape = jax.ShapeDtypeStruct((L, D, C), F32)
    return pl.pallas_call(
        body, name="wmod_adamw", grid=(L, D // tr),
        in_specs=[pl.BlockSpec((tr, MOD_ROWS), lambda l, i: (i, 0)),
                  pl.BlockSpec((1, MOD_ROWS, C), lambda l, i: (l, 0, 0)), spec, spec, spec],
        out_specs=[spec] * 4, out_shape=[shape] * 4,
        compiler_params=_params("parallel", "parallel"),
    )(cond_t, dmod16, w, m, v)


def _sum_slots(a):
    P, R, C = a.shape
    tr = _tile(R, 256, 16)

    def body(a_ref, o_ref):
        s = a_ref[0].astype(F32)
        for k in range(1, P):
            s = s + a_ref[k].astype(F32)
        o_ref[...] = s

    return pl.pallas_call(
        body, name="sum_slots", grid=(R // tr,),
        in_specs=[pl.BlockSpec((P, tr, C), lambda i: (0, i, 0))], out_specs=_row_spec(tr, C),
        out_shape=jax.ShapeDtypeStruct((R, C), F32), compiler_params=_params("parallel"),
    )(a)


def _axes():
    return lax.axis_index("x"), lax.axis_index("y"), lax.axis_index("c")


def _allgather_devices(v):
    N = v.shape[1]

    def body(v_ref, out_ref, send_sems, recv_sems, local_sem):
        x, y, c = _axes()

        def row(px, py, pc):
            return out_ref.at[pl.ds(4 * px + 2 * py + pc, 1), :]

        mine = pltpu.make_async_copy(v_ref, row(x, y, c), local_sem)
        mine.start()
        sends, recvs = [], []
        for k in range(1, N_DEV):
            peer = (x ^ (k >> 2), y ^ ((k >> 1) & 1), c ^ (k & 1))
            sems = dict(send_sem=send_sems.at[k - 1], recv_sem=recv_sems.at[k - 1], device_id=peer, device_id_type=MESH)
            sends.append(pltpu.make_async_remote_copy(src_ref=v_ref, dst_ref=row(x, y, c), **sems))
            recvs.append(pltpu.make_async_remote_copy(src_ref=v_ref, dst_ref=row(*peer), **sems))
        for cp in sends:
            cp.start()
        for cp in recvs:
            cp.wait_recv()
        for cp in sends:
            cp.wait_send()
        mine.wait()

    vmem = pl.BlockSpec(memory_space=pltpu.VMEM)
    return pl.pallas_call(
        body, name="allgather_devices", in_specs=[vmem], out_specs=vmem,
        out_shape=jax.ShapeDtypeStruct((N_DEV, N), v.dtype),
        scratch_shapes=[pltpu.SemaphoreType.DMA((N_DEV - 1,)), pltpu.SemaphoreType.DMA((N_DEV - 1,)),
                        pltpu.SemaphoreType.DMA],
        compiler_params=pltpu.CompilerParams(vmem_limit_bytes=VMEM_LIMIT_V7X),
    )(v)


def _chip_pos(order, px, py):
    return 2 * px + py if order == "natural" else 2 * py + px


def _block(ref, axis, pos, width):
    idx = [slice(None)] * len(ref.shape)
    idx[axis] = pl.ds(pl.multiple_of(pos * width, width), width)
    return ref.at[tuple(idx)]


def _chip_allgather(shards, axes, orders, name):
    return _run_exchange(_gather_exchange(shards, axes, orders), name)


class _Exchange:
    def __init__(self, arrays, out_shapes, describe):
        self.arrays, self.out_shapes, self.describe = list(arrays), list(out_shapes), describe
        n_remote = len(self.arrays) * (N_CHIPS - 1)
        self.scratch = [pltpu.SemaphoreType.DMA((n_remote,)), pltpu.SemaphoreType.DMA((n_remote,)),
                        pltpu.SemaphoreType.DMA((len(self.arrays),))]

    def _copies(self, ins, outs, sems):
        send_sems, recv_sems, local_sems = sems
        x, y, c = _axes()
        local, sends, recvs = [], [], []
        for i in range(len(self.arrays)):
            src, dst = self.describe(i, ins, outs, x, y, x, y)
            local.append(pltpu.make_async_copy(src, dst, local_sems.at[i]))
            for k in range(1, N_CHIPS):
                px, py = x ^ (k >> 1), y ^ (k & 1)
                j = i * (N_CHIPS - 1) + k - 1
                sem = dict(send_sem=send_sems.at[j], recv_sem=recv_sems.at[j], device_id=(px, py, c),
                           device_id_type=MESH)
                src, dst = self.describe(i, ins, outs, x, y, px, py)
                sends.append(pltpu.make_async_remote_copy(src_ref=src, dst_ref=dst, **sem))
                src, dst = self.describe(i, ins, outs, px, py, x, y)
                recvs.append(pltpu.make_async_remote_copy(src_ref=src, dst_ref=dst, **sem))
        return local, sends, recvs

    def start(self, ins, outs, sems):
        local, sends, _ = self._copies(ins, outs, sems)
        for cp in local + sends:
            cp.start()

    def wait(self, ins, outs, sems):
        local, sends, recvs = self._copies(ins, outs, sems)
        for cp in recvs:
            cp.wait_recv()
        for cp in sends:
            cp.wait_send()
        for cp in local:
            cp.wait()


def _run_exchange(ex, name):
    n_in, n_out = len(ex.arrays), len(ex.out_shapes)

    def body(*refs):
        ins, outs, sems = refs[:n_in], refs[n_in:n_in + n_out], refs[n_in + n_out:]
        ex.start(ins, outs, sems)
        ex.wait(ins, outs, sems)

    hbm = pl.BlockSpec(memory_space=pl.ANY)
    return pl.pallas_call(body, name=name, in_specs=[hbm] * n_in, out_specs=[hbm] * n_out, out_shape=ex.out_shapes,
                          scratch_shapes=ex.scratch)(*ex.arrays)


def _gather_exchange(shards, axes, orders):
    out_shapes = []
    for s, ax, order in zip(shards, axes, orders):
        shp = list(s.shape)
        if order == "gate_up_tiles":
            shp = [shp[0], 2, shp[1], 2 * shp[2]]
        else:
            shp[ax] *= N_CHIPS
        out_shapes.append(jax.ShapeDtypeStruct(tuple(shp), s.dtype))

    def describe(i, ins, outs, fx, fy, tx, ty):
        pos, width = _chip_pos(orders[i], fx, fy), shards[i].shape[axes[i]]
        if orders[i] == "gate_up_tiles":
            return ins[i], outs[i].at[:, pos // 2, :, pl.ds(pl.multiple_of((pos % 2) * width, width), width)]
        return ins[i], _block(outs[i], axes[i], pos, width)

    return _Exchange(shards, out_shapes, describe)


def _chip_scatter(grads, axes, orders, name):
    return _run_exchange(_scatter_exchange(grads, axes, orders), name)


def _scatter_exchange(grads, axes, orders, layers=None):
    layers = layers or [None] * len(grads)
    widths, out_shapes = [], {}
    for i, (g, ax, lay) in enumerate(zip(grads, axes, layers)):
        shp = list(g.shape)
        shp[ax] //= N_CHIPS
        widths.append(shp[ax])
        key, lead = (("own", i), (N_CHIPS,)) if lay is None else (("shared", lay[0]), (N_CHIPS, lay[2]))
        out_shapes[key] = jax.ShapeDtypeStruct(lead + tuple(shp), g.dtype)
    keys = list(out_shapes)

    def describe(i, ins, outs, fx, fy, tx, ty):
        lay = layers[i]
        out = outs[keys.index(("own", i) if lay is None else ("shared", lay[0]))]
        slot = out.at[2 * fx + fy] if lay is None else out.at[2 * fx + fy, lay[1]]
        return _block(ins[i], axes[i], _chip_pos(orders[i], tx, ty), widths[i]), slot

    return _Exchange(grads, [out_shapes[k] for k in keys], describe)


def _core_swap(arrays, name):
    n = len(arrays)

    def body(*refs):
        ins, outs = refs[:n], refs[n:2 * n]
        send_sems, recv_sems = refs[2 * n:]
        x, y, c = _axes()
        copies = [pltpu.make_async_remote_copy(src_ref=ins[i], dst_ref=outs[i], send_sem=send_sems.at[i],
                                               recv_sem=recv_sems.at[i], device_id=(x, y, 1 - c), device_id_type=MESH)
                  for i in range(n)]
        for cp in copies:
            cp.start()
        for cp in copies:
            cp.wait_recv()
        for cp in copies:
            cp.wait_send()

    hbm = pl.BlockSpec(memory_space=pl.ANY)
    return pl.pallas_call(
        body, name=name, in_specs=[hbm] * n, out_specs=[hbm] * n,
        out_shape=[jax.ShapeDtypeStruct(a.shape, a.dtype) for a in arrays],
        scratch_shapes=[pltpu.SemaphoreType.DMA((n,)), pltpu.SemaphoreType.DMA((n,))],
    )(*arrays)


SMALL = ("b_mod", "g_attn", "rpb_na", "sink_sw", "t5_table", "gq_ax", "gk_ax", "g_group", "g_ffn", "g_final")
BIG = ("w_in", "w_o", "w_gu", "w_down")
BIG_AXIS = {"w_in": 2, "w_o": 1, "w_gu": 2, "w_down": 1}
BIG_ORDER = {"w_in": "natural", "w_o": "natural", "w_gu": "gate_up_tiles", "w_down": "natural"}
WEIGHTS = ("w_mod", "b_mod", "g_attn", "w_in", "rpb_na", "sink_sw", "t5_table", "gq_ax", "gk_ax", "g_group",
           "w_o", "g_ffn", "w_gu", "w_down", "g_final")


def _pack(arrs):
    flat = jnp.concatenate([a.reshape(-1) for a in arrs])
    n = flat.shape[0]
    padded = -(-n // (8 * LANES)) * (8 * LANES)
    return jnp.pad(flat, (0, padded - n))


def _unpack(flat, like):
    out, off = [], 0
    for a in like:
        out.append(flat[off:off + a.size].reshape(a.shape))
        off += a.size
    return out


def kernel(x, c, w_mod, b_mod, g_attn, w_in, rpb_na, sink_sw, t5_table, gq_ax, gk_ax, g_group, w_o, g_ffn, w_gu, w_down, g_final, loss_target, m_w_mod, m_b_mod, m_g_attn, m_w_in, m_rpb_na, m_sink_sw, m_t5_table, m_gq_ax, m_gk_ax, m_g_group, m_w_o, m_g_ffn, m_w_gu, m_w_down, m_g_final, v_w_mod, v_b_mod, v_g_attn, v_w_in, v_rpb_na, v_sink_sw, v_t5_table, v_gq_ax, v_gk_ax, v_g_group, v_w_o, v_g_ffn, v_w_gu, v_w_down, v_g_final):
    W = dict(w_mod=w_mod, b_mod=b_mod, g_attn=g_attn, w_in=w_in, rpb_na=rpb_na, sink_sw=sink_sw, t5_table=t5_table,
             gq_ax=gq_ax, gk_ax=gk_ax, g_group=g_group, w_o=w_o, g_ffn=g_ffn, w_gu=w_gu, w_down=w_down,
             g_final=g_final)
    M = dict(w_mod=m_w_mod, b_mod=m_b_mod, g_attn=m_g_attn, w_in=m_w_in, rpb_na=m_rpb_na, sink_sw=m_sink_sw,
             t5_table=m_t5_table, gq_ax=m_gq_ax, gk_ax=m_gk_ax, g_group=m_g_group, w_o=m_w_o, g_ffn=m_g_ffn,
             w_gu=m_w_gu, w_down=m_w_down, g_final=m_g_final)
    V = dict(w_mod=v_w_mod, b_mod=v_b_mod, g_attn=v_g_attn, w_in=v_w_in, rpb_na=v_rpb_na, sink_sw=v_sink_sw,
             t5_table=v_t5_table, gq_ax=v_gq_ax, gk_ax=v_gk_ax, g_group=v_g_group, w_o=v_w_o, g_ffn=v_g_ffn,
             w_gu=v_w_gu, w_down=v_w_down, g_final=v_g_final)
    xi, yi, ci = _axes()
    me = 4 * xi + 2 * yi + ci
    chip = 2 * xi + yi
    D = x.shape[-1]
    mod_w = w_mod.shape[2]

    c_all = _allgather_devices(c)
    cond = c_all * (1.0 / (1.0 + jnp.exp(-c_all)))
    cond16 = jnp.pad(cond, ((0, MOD_ROWS - N_DEV), (0, 0))).astype(BF16)
    mod_part = _mod_fwd(cond16, w_mod)
    shards = {k: W[k].astype(BF16) for k in BIG}
    mod_all, w_in_first = _chip_allgather([mod_part, shards["w_in"][:1]], [2, BIG_AXIS["w_in"]],
                                          ["natural", BIG_ORDER["w_in"]], "allgather_mod_w_in0")
    mod = lax.dynamic_slice_in_dim(mod_all, me, 1, axis=1)[:, 0, :] + b_mod
    shards["w_in"] = shards["w_in"][1:]
    loss_part, grad_x, slots, small = _local_step(x[0], loss_target[0], mod, w_in_first, None, None, None, g_attn,
                                                  rpb_na, sink_sw, t5_table, gq_ax, gk_ax, g_group, g_ffn, g_final,
                                                  late_shards=shards)

    small_all = _allgather_devices(_pack([small[k] for k in SMALL])[None, :])
    rows = small_all.shape[1] // LANES
    parts = [small_all[k].reshape(rows, LANES) for k in range(N_DEV)]
    pk = lambda d: _pack([d[k] for k in SMALL]).reshape(rows, LANES)
    small_out = [_unpack(o.reshape(-1), [W[k] for k in SMALL]) for o in _adamw(pk(W), pk(M), pk(V), parts, "adamw_small")]

    L = w_mod.shape[0]
    dmod_all = small_all[:, :L * 6 * D].reshape(N_DEV, L, 6 * D)
    dmod_mine = lax.dynamic_slice_in_dim(dmod_all, chip * mod_w, mod_w, axis=2)
    dmod16 = jnp.pad(jnp.transpose(dmod_mine, (1, 0, 2)), ((0, 0), (0, MOD_ROWS - N_DEV), (0, 0))).astype(BF16)
    wmod_out = _wmod_adamw(jnp.transpose(cond16), dmod16, w_mod, m_w_mod, v_w_mod)

    names = list(BIG)
    two_d = lambda a: a.reshape(-1, a.shape[-1])
    mine = [_sum_slots(slots[k].reshape(N_CHIPS, -1, slots[k].shape[-1])) for k in names]
    theirs = _core_swap(mine, "swap_grads")
    big_out = {}
    for k, a, b in zip(names, mine, theirs):
        outs = _adamw(two_d(W[k]), two_d(M[k]), two_d(V[k]), [a, b], "adamw_" + k)
        big_out[k] = [o.reshape(W[k].shape) for o in outs]

    loss = lax.psum(loss_part, ("x", "y", "c"))
    per_kind = []
    for kind in range(4):
        for k in WEIGHTS:
            if k == "w_mod":
                per_kind.append(wmod_out[kind])
            elif k in big_out:
                per_kind.append(big_out[k][kind])
            else:
                per_kind.append(small_out[kind][SMALL.index(k)])
    return (loss, grad_x[None], *per_kind)
```

```python
import math

import jax
import jax.numpy as jnp
from jax import lax
from jax.experimental import pallas as pl
from jax.experimental.pallas import tpu as pltpu

F32 = jnp.float32
BF16 = jnp.bfloat16
I32 = jnp.int32

DEPTH = 2
HEAD_DIM = 64
GRID_W = 64
NA_HEADS = 4
SW_HEADS = 6
SW_KV_HEADS = 2
AX_HEADS = 6
AX_KV_HEADS = 2
NA_WIN_ROWS = 8
NA_WIN_COLS = 16
SW_RADIUS = 128
T5_BUCKETS = 32
T5_MAX_DIST = 128
ROPE_THETA = 10000.0
EPS = 1e-6
NEG_INF = -1e30
QK_SCALE = HEAD_DIM ** -0.5

NA_W = NA_HEADS * HEAD_DIM
SW_W = SW_HEADS * HEAD_DIM
SW_KV_W = SW_KV_HEADS * HEAD_DIM
AX_W = AX_HEADS * HEAD_DIM
AX_KV_W = AX_KV_HEADS * HEAD_DIM
OFF_QA, OFF_KA, OFF_VA = 0, NA_W, 2 * NA_W
OFF_QB = 3 * NA_W
OFF_KB = OFF_QB + SW_W
OFF_VB = OFF_KB + SW_KV_W
OFF_QC = OFF_VB + SW_KV_W
OFF_KC = OFF_QC + AX_W
OFF_VC = OFF_KC + AX_KV_W
IN_WIDTH = OFF_VC + AX_KV_W
MIX_WIDTH = NA_W + SW_W + AX_W

ADAM_LR = 0.001
ADAM_B1 = 0.9
ADAM_B2 = 0.999
ADAM_EPS = 1e-08
ADAM_WD = 0.01
ADAM_STEP = 10

N_CHIPS = 4
N_DEV = 8
LANES = 128
VMEM_LIMIT_V7X = 56 * 1024 * 1024
ROW_TILE = 512
ROW_TILE_WIDE = 512
TOKEN_CHUNK = 2048
MESH = pl.DeviceIdType.MESH

NT = (((1,), (1,)), ((), ()))
TN = (((0,), (0,)), ((), ()))


def _params(*sem):
    return pltpu.CompilerParams(dimension_semantics=sem if sem else None,
                                vmem_limit_bytes=VMEM_LIMIT_V7X)


def _tile(n, pref, mult=8):
    t = (min(pref, n) // mult) * mult
    while t >= mult:
        if n % t == 0:
            return t
        t -= mult
    return n


def _row_spec(tm, width, col=0):
    return pl.BlockSpec((tm, width), lambda i, *_: (i, col))


def _const_spec(shape):
    nd = len(shape)
    return pl.BlockSpec(shape, lambda *_: (0,) * nd)


def _rsq(ms):
    return lax.rsqrt(ms + EPS)


def _rope_tables(S):
    rows = S // GRID_W
    axis_dim = HEAD_DIM // 2
    quarter = axis_dim // 2
    lane = jnp.arange(LANES)
    freq = (ROPE_THETA ** (-(2 * (lane % quarter)).astype(F32) / axis_dim))[None, :]
    by_row = ((lane % HEAD_DIM) < axis_dim)[None, None, :]
    first = ((lane % axis_dim) < quarter)[None, :]
    ang_r = jnp.arange(rows, dtype=F32)[:, None] * freq
    ang_c = jnp.arange(GRID_W, dtype=F32)[:, None] * freq

    def table(fr, fc):
        t = jnp.where(by_row, fr[:, None, :], fc[None, :, :])
        return t.reshape(S, LANES)

    sin_r, sin_c = jnp.sin(ang_r), jnp.sin(ang_c)
    return (table(jnp.cos(ang_r), jnp.cos(ang_c)),
            table(jnp.where(first, -sin_r, 0.0), jnp.where(first, -sin_c, 0.0)),
            table(jnp.where(first, 0.0, sin_r), jnp.where(first, 0.0, sin_c)))


def _pair_sum(v):
    lane = lax.broadcasted_iota(I32, v.shape, 1)
    lo = lane < HEAD_DIM
    s_lo = jnp.sum(jnp.where(lo, v, 0.0), axis=-1, keepdims=True)
    s_hi = jnp.sum(jnp.where(lo, 0.0, v), axis=-1, keepdims=True)
    return jnp.where(lo, s_lo, s_hi)


def _rope(t, cos, sa, sb):
    return t * cos + pltpu.roll(t, LANES - 16, 1) * sa + pltpu.roll(t, 16, 1) * sb


def _rope_t(t, cos, sa, sb):
    return t * cos + pltpu.roll(t * sa, 16, 1) + pltpu.roll(t * sb, LANES - 16, 1)


def _qk_prep_chunk(x, g128, cos, sa, sb):
    r = _rsq(_pair_sum(x * x) * (1.0 / HEAD_DIM))
    return _rope(x * r * g128, cos, sa, sb)


def _qk_prep_bwd_chunk(x, dy, g128, cos, sa, sb):
    dn = _rope_t(dy, cos, sa, sb)
    r = _rsq(_pair_sum(x * x) * (1.0 / HEAD_DIM))
    xhat = x * r
    dg = jnp.sum(dn * xhat, axis=0, keepdims=True)
    dxh = dn * g128
    dx = r * (dxh - xhat * (_pair_sum(dxh * xhat) * (1.0 / HEAD_DIM)))
    return dx, dg


def _ln_mod(xv, g, sc, sh):
    r = _rsq(jnp.mean(xv * xv, axis=-1, keepdims=True))
    return xv * r * g * (1.0 + sc) + sh


def _inproj_fwd(x, g, sc, sh, w, gq128, gk128, rope):
    S, D = x.shape
    tm = _tile(S, ROW_TILE)
    cos, sa, sb = rope

    def body(x_ref, g_ref, sc_ref, sh_ref, w_ref, gq_ref, gk_ref, cos_ref, sa_ref, sb_ref,
             h_ref, proj_ref, qc_ref, kc_ref):
        hb = _ln_mod(x_ref[...], g_ref[...], sc_ref[...], sh_ref[...]).astype(BF16)
        h_ref[...] = hb
        acc = jnp.dot(hb, w_ref[...], preferred_element_type=F32)
        proj_ref[...] = acc.astype(BF16)
        c, a, b = cos_ref[...], sa_ref[...], sb_ref[...]
        for j in range(AX_W // LANES):
            xq = acc[:, OFF_QC + j * LANES: OFF_QC + (j + 1) * LANES]
            qc_ref[:, j * LANES:(j + 1) * LANES] = (
                _qk_prep_chunk(xq, gq_ref[...], c, a, b) * QK_SCALE).astype(BF16)
        for j in range(AX_KV_W // LANES):
            xk = acc[:, OFF_KC + j * LANES: OFF_KC + (j + 1) * LANES]
            kc_ref[:, j * LANES:(j + 1) * LANES] = _qk_prep_chunk(xk, gk_ref[...], c, a, b).astype(BF16)

    vec = _const_spec((1, D))
    v128 = _const_spec((1, LANES))
    return pl.pallas_call(
        body, name="inproj_fwd", grid=(S // tm,),
        in_specs=[_row_spec(tm, D), vec, vec, vec, _const_spec(w.shape), v128, v128,
                  _row_spec(tm, LANES), _row_spec(tm, LANES), _row_spec(tm, LANES)],
        out_specs=[_row_spec(tm, D), _row_spec(tm, IN_WIDTH), _row_spec(tm, AX_W), _row_spec(tm, AX_KV_W)],
        out_shape=[jax.ShapeDtypeStruct((S, D), BF16), jax.ShapeDtypeStruct((S, IN_WIDTH), BF16),
                   jax.ShapeDtypeStruct((S, AX_W), BF16), jax.ShapeDtypeStruct((S, AX_KV_W), BF16)],
        compiler_params=_params("parallel"),
    )(x, g, sc, sh, w, gq128, gk128, cos, sa, sb)


class _Band:
    def __init__(self, kind, S):
        self.kind = kind
        self.S = S
        if kind == "na":
            self.hq, self.g, self.halo = NA_HEADS, NA_HEADS, (NA_WIN_ROWS // 2) * GRID_W
            self.q_off, self.k_off, self.v_off = OFF_QA, OFF_KA, OFF_VA
        else:
            self.hq, self.g, self.halo = SW_HEADS, SW_KV_HEADS, SW_RADIUS
            self.q_off, self.k_off, self.v_off = OFF_QB, OFF_KB, OFF_VB
        self.bq = 2 * self.halo
        self.bk = self.bq + 2 * self.halo
        self.nb = S // self.bq
        self.rep = self.hq // self.g
        self.qw = self.hq * HEAD_DIM
        self.kw = self.g * HEAD_DIM

    def kv_of(self, h):
        return (h // 2, h % 2) if self.kind == "na" else (0, h // self.rep)

    def mask(self, n, transposed=False):
        shape = (self.bk, self.bq) if transposed else (self.bq, self.bk)
        qi = lax.broadcasted_iota(I32, shape, 1 if transposed else 0) + n * self.bq
        kj = lax.broadcasted_iota(I32, shape, 0 if transposed else 1) + (n * self.bq - self.halo)
        if self.kind == "sw":
            return (jnp.abs(kj - qi) <= SW_RADIUS) & (kj >= 0) & (kj < self.S)
        rows = self.S // GRID_W
        r, col = qi >> 6, qi & (GRID_W - 1)
        kr, kc = kj >> 6, kj & (GRID_W - 1)
        rs = jnp.clip(r - NA_WIN_ROWS // 2, 0, rows - NA_WIN_ROWS)
        cs = jnp.clip(col - NA_WIN_COLS // 2, 0, GRID_W - NA_WIN_COLS)
        return (kr >= rs) & (kr < rs + NA_WIN_ROWS) & (kc >= cs) & (kc < cs + NA_WIN_COLS)

    def qkv_specs(self):
        ratio = self.bq // self.halo
        last = self.S // self.halo - 1
        q = pl.BlockSpec((self.bq, self.qw), lambda n, o=self.q_off // self.qw: (n, o))
        specs = [q]
        for off in (self.k_off, self.v_off):
            o = off // self.kw
            specs.append(pl.BlockSpec((self.halo, self.kw), lambda n, o=o: (jnp.maximum(n * ratio - 1, 0), o)))
            specs.append(pl.BlockSpec((self.bq, self.kw), lambda n, o=o: (n, o)))
            specs.append(pl.BlockSpec((self.halo, self.kw), lambda n, o=o: (jnp.minimum((n + 1) * ratio, last), o)))
        return specs


def _band_kv_variants(bd, refs, fill):
    out = []
    for blk in range(bd.kw // LANES):
        cols = slice(blk * LANES, (blk + 1) * LANES)
        out.append(_half_variants(jnp.concatenate([r[:, cols] for r in refs], axis=0), fill))
    return out


def _band_fwd(bd, proj, bias, sink, gg):
    S = bd.S
    has_sink = sink is not None

    def body(*refs):
        q_ref, kp, km, kn, vp, vm, vn, bias_ref = refs[:8]
        k = 8
        sink_ref = None
        if has_sink:
            sink_ref = refs[k]
            k += 1
        gg_ref, raw_ref, yn_ref, o_scr = refs[k:k + 4]
        mask = bd.mask(pl.program_id(0))
        lo = _left_half((bd.bq, LANES))
        kzs, vzs = _band_kv_variants(bd, (kp, km, kn), 0.0), _band_kv_variants(bd, (vp, vm, vn), 1.0)
        for pr in range(bd.hq // 2):
            cols = slice(pr * LANES, (pr + 1) * LANES)
            qp = q_ref[:, cols] * QK_SCALE
            acc = []
            for half in range(2):
                h = 2 * pr + half
                blk, src = bd.kv_of(h)
                s = lax.dot_general(qp, kzs[blk][src][half], NT, preferred_element_type=F32) + bias_ref[h]
                s = jnp.where(mask, s, NEG_INF)
                m = jnp.max(s, axis=-1, keepdims=True)
                if has_sink:
                    m = jnp.maximum(m, sink_ref[0:1, h:h + 1])
                a = jnp.dot(jnp.exp(s - m).astype(BF16), vzs[blk][src][half], preferred_element_type=F32)
                if has_sink:
                    e = jnp.exp(sink_ref[0:1, h:h + 1] - m)
                    a = a + (jnp.where(lo, 0.0, e) if half == 0 else jnp.where(lo, e, 0.0))
                acc.append(a)
            o_scr[:, cols] = jnp.where(lo, acc[0] / pltpu.roll(acc[0], HEAD_DIM, 1),
                                       acc[1] / pltpu.roll(acc[1], HEAD_DIM, 1))
        o = o_scr[...]
        raw_ref[...] = o.astype(BF16)
        r = _rsq(jnp.mean(o * o, axis=-1, keepdims=True))
        yn_ref[...] = (o * r * gg_ref[...]).astype(BF16)

    in_specs = bd.qkv_specs() + [_const_spec(bias.shape)]
    args = [proj] * 7 + [bias]
    if has_sink:
        in_specs.append(_const_spec(sink.shape))
        args.append(sink)
    in_specs.append(_const_spec(gg.shape))
    args.append(gg)
    out = jax.ShapeDtypeStruct((S, bd.qw), BF16)
    return pl.pallas_call(
        body, name=bd.kind + "_fwd", grid=(bd.nb,), in_specs=in_specs,
        out_specs=[_row_spec(bd.bq, bd.qw), _row_spec(bd.bq, bd.qw)], out_shape=[out, out],
        scratch_shapes=[pltpu.VMEM((bd.bq, bd.qw), F32)],
        compiler_params=_params("parallel"),
    )(*args)


def _band_bwd(bd, proj, bias, sink, dy):
    S = bd.S
    has_sink = sink is not None

    def body(*refs):
        q_ref, kp, km, kn, vp, vm, vn, bias_ref = refs[:8]
        k = 8
        sink_ref = None
        if has_sink:
            sink_ref = refs[k]
            k += 1
        do_ref = refs[k]
        dq_ref, dkm, dvm, dkp, dvp, dkn, dvn, dbias_ref = refs[k + 1:k + 9]
        k += 9
        dsink_ref = None
        if has_sink:
            dsink_ref = refs[k]
            k += 1
        dk_scr, dv_scr = refs[k:k + 2]
        n = pl.program_id(0)

        @pl.when(n == 0)
        def _():
            dbias_ref[...] = jnp.zeros_like(dbias_ref)
            if has_sink:
                dsink_ref[...] = jnp.zeros_like(dsink_ref)

        mask = bd.mask(n, transposed=True)
        lane = lax.broadcasted_iota(I32, (1, LANES), 1)
        kzs, vzs = _band_kv_variants(bd, (kp, km, kn), 0.0), _band_kv_variants(bd, (vp, vm, vn), 0.0)
        nblk = bd.kw // LANES
        dk, dv = [None] * nblk, [None] * nblk
        for pr in range(bd.hq // 2):
            cols = slice(pr * LANES, (pr + 1) * LANES)
            qp, dop = q_ref[:, cols] * QK_SCALE, do_ref[:, cols]
            qz, doz = _half_variants(qp), _half_variants(dop)
            dq = None
            for half in range(2):
                h = 2 * pr + half
                blk, dst = bd.kv_of(h)
                kz, vz = kzs[blk][dst][half], vzs[blk][dst][half]
                s = lax.dot_general(kz, qp, NT, preferred_element_type=F32) + bias_ref[h]
                s = jnp.where(mask, s, NEG_INF)
                m = jnp.max(s, axis=0, keepdims=True)
                if has_sink:
                    m = jnp.maximum(m, sink_ref[0:1, h:h + 1])
                p = jnp.exp(s - m)
                l = jnp.sum(p, axis=0, keepdims=True)
                if has_sink:
                    e = jnp.exp(sink_ref[0:1, h:h + 1] - m)
                    l = l + e
                inv = 1.0 / l
                pn = p * inv
                dp = lax.dot_general(vz, dop, NT, preferred_element_type=F32)
                delta = jnp.sum(pn * dp, axis=0, keepdims=True)
                ds = pn * (dp - delta)
                dbias_ref[h] += ds
                if has_sink:
                    dsink_ref[...] += jnp.where(lane == h, -jnp.sum(e * inv * delta, axis=1, keepdims=True), 0.0)
                dsb = ds.astype(BF16)
                a = jnp.dot(pn.astype(BF16), doz[half][dst], preferred_element_type=F32)
                b = jnp.dot(dsb, qz[half][dst], preferred_element_type=F32)
                d = lax.dot_general(dsb, kz, TN, preferred_element_type=F32)
                dv[blk] = a if dv[blk] is None else dv[blk] + a
                dk[blk] = b if dk[blk] is None else dk[blk] + b
                dq = d if dq is None else dq + d
            dq_ref[:, cols] = dq * QK_SCALE
        for blk in range(nblk):
            cols = slice(blk * LANES, (blk + 1) * LANES)
            dk_scr[:, cols] = dk[blk]
            dv_scr[:, cols] = dv[blk]
        h0, h1 = bd.halo, bd.halo + bd.bq
        dkp[0] = dk_scr[0:h0, :]
        dkm[...] = dk_scr[h0:h1, :]
        dkn[0] = dk_scr[h1:bd.bk, :]
        dvp[0] = dv_scr[0:h0, :]
        dvm[...] = dv_scr[h0:h1, :]
        dvn[0] = dv_scr[h1:bd.bk, :]

    in_specs = bd.qkv_specs() + [_const_spec(bias.shape)]
    args = [proj] * 7 + [bias]
    if has_sink:
        in_specs.append(_const_spec(sink.shape))
        args.append(sink)
    in_specs.append(_row_spec(bd.bq, bd.qw))
    args.append(dy)
    halo_spec = pl.BlockSpec((1, bd.halo, bd.kw), lambda n: (n, 0, 0))
    halo_shape = jax.ShapeDtypeStruct((bd.nb, bd.halo, bd.kw), F32)
    main_shape = jax.ShapeDtypeStruct((S, bd.kw), F32)
    out_specs = [_row_spec(bd.bq, bd.qw), _row_spec(bd.bq, bd.kw), _row_spec(bd.bq, bd.kw),
                 halo_spec, halo_spec, halo_spec, halo_spec, _const_spec(bias.shape)]
    out_shape = [jax.ShapeDtypeStruct((S, bd.qw), F32), main_shape, main_shape,
                 halo_shape, halo_shape, halo_shape, halo_shape, jax.ShapeDtypeStruct(bias.shape, F32)]
    if has_sink:
        out_specs.append(_const_spec((1, LANES)))
        out_shape.append(jax.ShapeDtypeStruct((1, LANES), F32))
    return pl.pallas_call(
        body, name=bd.kind + "_bwd", grid=(bd.nb,), in_specs=in_specs, out_specs=out_specs, out_shape=out_shape,
        scratch_shapes=[pltpu.VMEM((bd.bk, bd.kw), F32), pltpu.VMEM((bd.bk, bd.kw), F32)],
        compiler_params=_params("arbitrary"),
    )(*args)


def _halo_to_rows(prev, nxt):
    nb, halo, w = prev.shape
    z = jnp.zeros((1, halo, w), prev.dtype)
    first = jnp.concatenate([z, nxt[:-1]], axis=0)
    second = jnp.concatenate([prev[1:], z], axis=0)
    return jnp.concatenate([first, second], axis=1).reshape(nb * 2 * halo, w)


AX_PAIRS = AX_W // LANES


AX_FWD_BLOCKS = (1024, 2048)
AX_BWD_BLOCKS = (512, 2048)


def _ax_blocks(S, blocks):
    return _tile(S, blocks[0]), _tile(S, blocks[1])


def _left_half(shape):
    return lax.broadcasted_iota(I32, shape, len(shape) - 1) < HEAD_DIM


def _as_row(a):
    return jnp.transpose(a)[0:1, :]


def _half_variants(a, fill=0.0):
    lo = _left_half(a.shape)
    other = jnp.full_like(a, fill)
    swapped = pltpu.roll(a, HEAD_DIM, 1)
    return ((jnp.where(lo, a, other), jnp.where(lo, other, swapped)),
            (jnp.where(lo, swapped, other), jnp.where(lo, other, a)))


def _split_rider_refs(refs, n_in, n_out, rider):
    r_in, r_out = (len(rider.arrays), len(rider.out_shapes)) if rider else (0, 0)
    a, b, c = n_in + r_in, n_in + r_in + n_out, n_in + r_in + n_out + r_out
    n_sems = 3 if rider else 0
    return refs[:n_in], refs[n_in:a], refs[a:b], refs[b:c], refs[c:len(refs) - n_sems], refs[len(refs) - n_sems:]


def _ax_fwd(qc, kc, proj, gg, rider=None):
    S = qc.shape[0]
    bq, bk = _ax_blocks(S, AX_FWD_BLOCKS)
    nq, nk = S // bq, S // bk
    rep = AX_HEADS // AX_KV_HEADS

    def body(*refs):
        (q_ref, k_ref, v_ref, gg_ref), r_ins, (raw_ref, yn_ref, lse_ref), r_outs, (m_scr, acc_scr), r_sems = (
            _split_rider_refs(refs, 4, 3, rider))
        qi, kv = pl.program_id(0), pl.program_id(1)

        if rider:
            @pl.when((qi == 0) & (kv == 0))
            def _():
                rider.start(r_ins, r_outs, r_sems)

        @pl.when(kv == 0)
        def _():
            m_scr[...] = jnp.full(m_scr.shape, NEG_INF, F32)
            acc_scr[...] = jnp.zeros_like(acc_scr)

        kz, vz = _half_variants(k_ref[...]), _half_variants(v_ref[...], 1.0)
        for pr in range(AX_PAIRS):
            qp = q_ref[:, pr * LANES:(pr + 1) * LANES]
            for half in range(2):
                h = 2 * pr + half
                g = h // rep
                s = lax.dot_general(qp, kz[g][half], NT, preferred_element_type=F32)
                m_prev = m_scr[h]
                m_new = jnp.maximum(m_prev, jnp.max(s, axis=-1, keepdims=True))
                p = jnp.exp(s - jnp.tile(m_new, (1, bk // LANES)))
                acc_scr[h] = jnp.exp(m_prev - m_new) * acc_scr[h] + jnp.dot(
                    p.astype(BF16), vz[g][half], preferred_element_type=F32)
                m_scr[h] = m_new

        @pl.when(kv == nk - 1)
        def _():
            lo = _left_half((bq, LANES))
            ssq = jnp.zeros((bq, 1), F32)
            for pr in range(AX_PAIRS):
                a0, a1 = acc_scr[2 * pr], acc_scr[2 * pr + 1]
                r0, r1 = pltpu.roll(a0, HEAD_DIM, 1), pltpu.roll(a1, HEAD_DIM, 1)
                lse_ref[2 * pr] = _as_row(m_scr[2 * pr] + jnp.log(jnp.where(lo, r0, a0)))
                lse_ref[2 * pr + 1] = _as_row(m_scr[2 * pr + 1] + jnp.log(jnp.where(lo, a1, r1)))
                o = jnp.where(lo, a0 / r0, a1 / r1)
                acc_scr[pr] = o
                ssq = ssq + jnp.sum(o * o, axis=-1, keepdims=True)
            r = _rsq(ssq * (1.0 / AX_W))
            for pr in range(AX_PAIRS):
                cols = slice(pr * LANES, (pr + 1) * LANES)
                o = acc_scr[pr]
                raw_ref[:, cols] = o.astype(BF16)
                yn_ref[:, cols] = (o * r * gg_ref[:, cols]).astype(BF16)

        if rider:
            @pl.when((qi == nq - 1) & (kv == nk - 1))
            def _():
                rider.wait(r_ins, r_outs, r_sems)

    out = jax.ShapeDtypeStruct((S, AX_W), BF16)
    hbm = pl.BlockSpec(memory_space=pl.ANY)
    r_arrays, r_shapes, r_scratch = (rider.arrays, rider.out_shapes, rider.scratch) if rider else ([], [], [])
    res = pl.pallas_call(
        body, name="ax_fwd_gather" if rider else "ax_fwd", grid=(nq, nk),
        in_specs=[pl.BlockSpec((bq, AX_W), lambda i, j: (i, 0)),
                  pl.BlockSpec((bk, AX_KV_W), lambda i, j: (j, 0)),
                  pl.BlockSpec((bk, AX_KV_W), lambda i, j: (j, OFF_VC // AX_KV_W)),
                  _const_spec(gg.shape)] + [hbm] * len(r_arrays),
        out_specs=[pl.BlockSpec((bq, AX_W), lambda i, j: (i, 0)),
                   pl.BlockSpec((bq, AX_W), lambda i, j: (i, 0)),
                   pl.BlockSpec((AX_HEADS, 1, bq), lambda i, j: (0, 0, i))] + [hbm] * len(r_shapes),
        out_shape=[out, out, jax.ShapeDtypeStruct((AX_HEADS, 1, S), F32)] + r_shapes,
        scratch_shapes=[pltpu.VMEM((AX_HEADS, bq, LANES), F32), pltpu.VMEM((AX_HEADS, bq, LANES), F32)] + r_scratch,
        compiler_params=_params("arbitrary", "arbitrary"),
    )(qc, kc, proj, gg, *r_arrays)
    return res[0], res[1], res[2], list(res[3:])


def _ax_delta(dy, raw):
    S = dy.shape[0]
    tm = _tile(S, ROW_TILE)

    def body(do_ref, o_ref, delta_ref):
        lo = _left_half((tm, LANES))
        for pr in range(AX_PAIRS):
            cols = slice(pr * LANES, (pr + 1) * LANES)
            prod = do_ref[:, cols].astype(F32) * o_ref[:, cols].astype(F32)
            left = jnp.sum(jnp.where(lo, prod, 0.0), axis=-1, keepdims=True)
            right = jnp.sum(jnp.where(lo, 0.0, prod), axis=-1, keepdims=True)
            delta_ref[2 * pr] = _as_row(jnp.broadcast_to(left, (tm, LANES)))
            delta_ref[2 * pr + 1] = _as_row(jnp.broadcast_to(right, (tm, LANES)))

    return pl.pallas_call(
        body, name="ax_delta", grid=(S // tm,), in_specs=[_row_spec(tm, AX_W), _row_spec(tm, AX_W)],
        out_specs=pl.BlockSpec((AX_HEADS, 1, tm), lambda i: (0, 0, i)),
        out_shape=jax.ShapeDtypeStruct((AX_HEADS, 1, S), F32), compiler_params=_params("parallel"),
    )(dy, raw)


def _ax_bwd(qc, kc, proj, dy, lse_row, delta_row, rider=None):
    S = qc.shape[0]
    bq, bk = _ax_blocks(S, AX_BWD_BLOCKS)
    nq, nk = S // bq, S // bk
    rep = AX_HEADS // AX_KV_HEADS

    def body(*refs):
        ((q_ref, k_ref, v_ref, do_ref, lse_ref, delta_ref), r_ins, (dk_ref, dv_ref, dq_hbm), r_outs, (dq_scr, sem),
         r_sems) = _split_rider_refs(refs, 6, 3, rider)
        j, i = pl.program_id(0), pl.program_id(1)

        if rider:
            @pl.when((j == 0) & (i == 0))
            def _():
                rider.start(r_ins, r_outs, r_sems)

        @pl.when(i == 0)
        def _():
            dk_ref[...] = jnp.zeros_like(dk_ref)
            dv_ref[...] = jnp.zeros_like(dv_ref)

        @pl.when(j == 0)
        def _():
            dq_scr[i] = jnp.zeros((bq, AX_W), F32)

        kz, vz = _half_variants(k_ref[...]), _half_variants(v_ref[...])
        dk, dv = None, None
        for pr in range(AX_PAIRS):
            cols = slice(pr * LANES, (pr + 1) * LANES)
            qp, dop = q_ref[:, cols], do_ref[:, cols]
            qz, doz = _half_variants(qp), _half_variants(dop)
            dq = None
            for half in range(2):
                h = 2 * pr + half
                g = h // rep
                s_t = lax.dot_general(kz[g][half], qp, NT, preferred_element_type=F32)
                p_t = jnp.exp(s_t - lse_ref[h])
                dp_t = lax.dot_general(vz[g][half], dop, NT, preferred_element_type=F32)
                ds_t = (p_t * (dp_t - delta_ref[h])).astype(BF16)
                a = jnp.dot(p_t.astype(BF16), doz[half][g], preferred_element_type=F32)
                b = jnp.dot(ds_t, qz[half][g], preferred_element_type=F32)
                d = lax.dot_general(ds_t, kz[g][half], TN, preferred_element_type=F32)
                dv = a if dv is None else dv + a
                dk = b if dk is None else dk + b
                dq = d if dq is None else dq + d
            dq_scr[i, :, cols] += dq
        dv_ref[...] += dv
        dk_ref[...] += dk

        @pl.when(j == nk - 1)
        def _():
            dq_scr[i] = dq_scr[i] * QK_SCALE
            out = pltpu.make_async_copy(dq_scr.at[i], dq_hbm.at[pl.ds(pl.multiple_of(i * bq, bq), bq), :], sem)
            out.start()
            out.wait()

        if rider:
            @pl.when((j == nk - 1) & (i == nq - 1))
            def _():
                rider.wait(r_ins, r_outs, r_sems)

    qspec = pl.BlockSpec((bq, AX_W), lambda j, i: (i, 0))
    kspec = pl.BlockSpec((bk, AX_KV_W), lambda j, i: (j, 0))
    stat = pl.BlockSpec((AX_HEADS, 1, bq), lambda j, i: (0, 0, i))
    out = jax.ShapeDtypeStruct((S, AX_KV_W), F32)
    hbm = pl.BlockSpec(memory_space=pl.ANY)
    r_arrays, r_shapes, r_scratch = (rider.arrays, rider.out_shapes, rider.scratch) if rider else ([], [], [])
    res = pl.pallas_call(
        body, name="ax_bwd_scatter" if rider else "ax_bwd", grid=(nk, nq),
        in_specs=[qspec, kspec, pl.BlockSpec((bk, AX_KV_W), lambda j, i: (j, OFF_VC // AX_KV_W)),
                  qspec, stat, stat] + [hbm] * len(r_arrays),
        out_specs=[kspec, kspec, hbm] + [hbm] * len(r_shapes),
        out_shape=[out, out, jax.ShapeDtypeStruct((S, AX_W), F32)] + r_shapes,
        scratch_shapes=[pltpu.VMEM((nq, bq, AX_W), F32), pltpu.SemaphoreType.DMA] + r_scratch,
        compiler_params=_params("arbitrary", "arbitrary"),
    )(qc, kc, proj, dy, lse_row, delta_row, *r_arrays)
    return res[2], res[0], res[1], list(res[3:])


def _oproj_fwd(x, yna, ynb, ync, w, gt):
    S, D = x.shape
    tm = _tile(S, ROW_TILE)

    def body(x_ref, a_ref, b_ref, c_ref, w_ref, gt_ref, x1_ref, ao_ref, yn_ref):
        yn_ref[:, 0:NA_W] = a_ref[...]
        yn_ref[:, NA_W:NA_W + SW_W] = b_ref[...]
        yn_ref[:, NA_W + SW_W:MIX_WIDTH] = c_ref[...]
        acc = jnp.dot(yn_ref[...], w_ref[...], preferred_element_type=F32)
        ao_ref[...] = acc.astype(BF16)
        x1_ref[...] = x_ref[...] + gt_ref[...] * acc

    return pl.pallas_call(
        body, name="oproj_fwd", grid=(S // tm,),
        in_specs=[_row_spec(tm, D), _row_spec(tm, NA_W), _row_spec(tm, SW_W), _row_spec(tm, AX_W),
                  _const_spec(w.shape), _const_spec((1, D))],
        out_specs=[_row_spec(tm, D), _row_spec(tm, D), _row_spec(tm, MIX_WIDTH)],
        out_shape=[jax.ShapeDtypeStruct((S, D), F32), jax.ShapeDtypeStruct((S, D), BF16),
                   jax.ShapeDtypeStruct((S, MIX_WIDTH), BF16)],
        compiler_params=_params("parallel"),
    )(x, yna, ynb, ync, w, gt)


def _gu_fwd(x, g, sc, sh, w):
    S, D = x.shape
    tn = w.shape[2] // 2
    F2 = 4 * tn
    tm = _tile(S, ROW_TILE)

    def body(x_ref, g_ref, sc_ref, sh_ref, w_ref, h_ref, gu_ref, act_ref):
        @pl.when(pl.program_id(1) == 0)
        def _():
            h_ref[...] = _ln_mod(x_ref[...], g_ref[...], sc_ref[...], sh_ref[...]).astype(BF16)

        acc = jnp.dot(h_ref[...], w_ref[pl.program_id(1)], preferred_element_type=F32)
        gu_ref[...] = acc.astype(BF16)
        gate, up = acc[:, :tn], acc[:, tn:]
        act_ref[...] = (gate * (1.0 / (1.0 + jnp.exp(-gate))) * up).astype(BF16)

    vec = pl.BlockSpec((1, D), lambda i, j: (0, 0))
    return pl.pallas_call(
        body, name="gu_fwd", grid=(S // tm, 2),
        in_specs=[pl.BlockSpec((tm, D), lambda i, j: (i, 0)), vec, vec, vec,
                  pl.BlockSpec((2, D, 2 * tn), lambda i, j: (0, 0, 0))],
        out_specs=[pl.BlockSpec((tm, D), lambda i, j: (i, 0)), pl.BlockSpec((tm, 2 * tn), lambda i, j: (i, j)),
                   pl.BlockSpec((tm, tn), lambda i, j: (i, j))],
        out_shape=[jax.ShapeDtypeStruct((S, D), BF16), jax.ShapeDtypeStruct((S, F2), BF16),
                   jax.ShapeDtypeStruct((S, F2 // 2), BF16)],
        compiler_params=_params("parallel", "arbitrary"),
    )(x, g, sc, sh, w)


def _down_fwd(x, act, w, gt):
    S, D = x.shape
    F = act.shape[1]
    tm = _tile(S, ROW_TILE)

    def body(x_ref, a_ref, w_ref, gt_ref, x2_ref, fo_ref):
        acc = jnp.dot(a_ref[...], w_ref[...], preferred_element_type=F32)
        fo_ref[...] = acc.astype(BF16)
        x2_ref[...] = x_ref[...] + gt_ref[...] * acc

    return pl.pallas_call(
        body, name="down_fwd", grid=(S // tm,),
        in_specs=[_row_spec(tm, D), _row_spec(tm, F), _const_spec(w.shape), _const_spec((1, D))],
        out_specs=[_row_spec(tm, D), _row_spec(tm, D)],
        out_shape=[jax.ShapeDtypeStruct((S, D), F32), jax.ShapeDtypeStruct((S, D), BF16)],
        compiler_params=_params("parallel"),
    )(x, act, w, gt)


def _final_loss(x, g, target):
    S, D = x.shape
    tm = _tile(S, ROW_TILE)

    def body(x_ref, g_ref, t_ref, dx_ref, loss_ref, dg_ref):
        @pl.when(pl.program_id(0) == 0)
        def _():
            loss_ref[...] = jnp.zeros_like(loss_ref)
            dg_ref[...] = jnp.zeros_like(dg_ref)

        xv = x_ref[...]
        r = _rsq(jnp.mean(xv * xv, axis=-1, keepdims=True))
        xhat = xv * r
        err = xhat * g_ref[...] - t_ref[...]
        loss_ref[...] += 0.5 * jnp.sum(jnp.mean(err * err, axis=-1, keepdims=True), axis=0, keepdims=True)
        dy = err * (1.0 / D)
        dg_ref[...] += jnp.sum(dy * xhat, axis=0, keepdims=True)
        dxh = dy * g_ref[...]
        dx_ref[...] = r * (dxh - xhat * jnp.mean(dxh * xhat, axis=-1, keepdims=True))

    return pl.pallas_call(
        body, name="final_loss", grid=(S // tm,),
        in_specs=[_row_spec(tm, D), _const_spec((1, D)), _row_spec(tm, D)],
        out_specs=[_row_spec(tm, D), _const_spec((1, LANES)), _const_spec((1, D))],
        out_shape=[jax.ShapeDtypeStruct((S, D), F32), jax.ShapeDtypeStruct((1, LANES), F32),
                   jax.ShapeDtypeStruct((1, D), F32)],
        compiler_params=_params("arbitrary"),
    )(x, g, target)


def _ffn_bwd1(dx2, fo, gt, w_down, gu):
    S, D = dx2.shape
    F2 = gu.shape[1]
    tn = F2 // 4
    tm = _tile(S, ROW_TILE)

    def body(dx_ref, fo_ref, gt_ref, w_ref, gu_ref, dfo_ref, dgu_ref, dgt_ref):
        i, j = pl.program_id(0), pl.program_id(1)

        @pl.when((i == 0) & (j == 0))
        def _():
            dgt_ref[...] = jnp.zeros_like(dgt_ref)

        @pl.when(j == 0)
        def _():
            dxv = dx_ref[...]
            dfo_ref[...] = (dxv * gt_ref[...]).astype(BF16)
            dgt_ref[...] += jnp.sum(dxv * fo_ref[...].astype(F32), axis=0, keepdims=True)

        dact = lax.dot_general(dfo_ref[...], w_ref[j], NT, preferred_element_type=F32)
        gate = gu_ref[:, :tn].astype(F32)
        up = gu_ref[:, tn:].astype(F32)
        sig = 1.0 / (1.0 + jnp.exp(-gate))
        dgu_ref[:, :tn] = (dact * up * (sig * (1.0 + gate * (1.0 - sig)))).astype(BF16)
        dgu_ref[:, tn:] = (dact * (gate * sig)).astype(BF16)

    vec = pl.BlockSpec((1, D), lambda i, j: (0, 0))
    row = pl.BlockSpec((tm, D), lambda i, j: (i, 0))
    return pl.pallas_call(
        body, name="ffn_bwd1", grid=(S // tm, 2),
        in_specs=[row, row, vec, pl.BlockSpec((2, tn, D), lambda i, j: (0, 0, 0)),
                  pl.BlockSpec((tm, 2 * tn), lambda i, j: (i, j))],
        out_specs=[row, pl.BlockSpec((tm, 2 * tn), lambda i, j: (i, j)), vec],
        out_shape=[jax.ShapeDtypeStruct((S, D), BF16), jax.ShapeDtypeStruct((S, F2), BF16),
                   jax.ShapeDtypeStruct((1, D), F32)],
        compiler_params=_params("arbitrary", "arbitrary"),
    )(dx2, fo, gt, w_down.reshape(2, tn, D), gu)


def _nt_ln_bwd(a, w, x, g, sc, dres, name):
    S, D = x.shape
    K = a.shape[1]
    tm = _tile(S, ROW_TILE_WIDE)

    def body(a_ref, w_ref, x_ref, g_ref, sc_ref, dres_ref, dx_ref, dsh_ref, dsc_ref, dg_ref):
        @pl.when(pl.program_id(0) == 0)
        def _():
            dsh_ref[...] = jnp.zeros_like(dsh_ref)
            dsc_ref[...] = jnp.zeros_like(dsc_ref)
            dg_ref[...] = jnp.zeros_like(dg_ref)

        if len(w.shape) == 2:
            dh = lax.dot_general(a_ref[...], w_ref[...], NT, preferred_element_type=F32)
        else:
            kt = w.shape[2]
            dh = sum(lax.dot_general(a_ref[:, t * kt:(t + 1) * kt], w_ref[t], NT, preferred_element_type=F32)
                     for t in range(w.shape[0]))
        xv = x_ref[...]
        r = _rsq(jnp.mean(xv * xv, axis=-1, keepdims=True))
        xhat = xv * r
        gv = g_ref[...]
        dsh_ref[...] += jnp.sum(dh, axis=0, keepdims=True)
        dsc_ref[...] += jnp.sum(dh * (xhat * gv), axis=0, keepdims=True)
        dn = dh * (1.0 + sc_ref[...])
        dg_ref[...] += jnp.sum(dn * xhat, axis=0, keepdims=True)
        dxh = dn * gv
        dx_ref[...] = dres_ref[...] + r * (dxh - xhat * jnp.mean(dxh * xhat, axis=-1, keepdims=True))

    vec = _const_spec((1, D))
    vshape = jax.ShapeDtypeStruct((1, D), F32)
    return pl.pallas_call(
        body, name=name, grid=(S // tm,),
        in_specs=[_row_spec(tm, K), _const_spec(w.shape), _row_spec(tm, D), vec, vec, _row_spec(tm, D)],
        out_specs=[_row_spec(tm, D), vec, vec, vec],
        out_shape=[jax.ShapeDtypeStruct((S, D), F32), vshape, vshape, vshape],
        compiler_params=_params("arbitrary"),
    )(a, w, x, g, sc, dres)


def _oproj_bwd(dx1, ao, gt, w, ya, yb, yc, gg):
    S, D = dx1.shape
    tm = _tile(S, ROW_TILE)
    groups = ((0, NA_W), (NA_W, SW_W), (NA_W + SW_W, AX_W))

    def body(dx_ref, ao_ref, gt_ref, w_ref, ya_ref, yb_ref, yc_ref, gg_ref,
             dao_ref, dya_ref, dyb_ref, dyc_ref, dgt_ref, dgg_ref):
        @pl.when(pl.program_id(0) == 0)
        def _():
            dgt_ref[...] = jnp.zeros_like(dgt_ref)
            dgg_ref[...] = jnp.zeros_like(dgg_ref)

        dxv = dx_ref[...]
        dao = (dxv * gt_ref[...]).astype(BF16)
        dao_ref[...] = dao
        dgt_ref[...] += jnp.sum(dxv * ao_ref[...].astype(F32), axis=0, keepdims=True)
        dyn = lax.dot_general(dao, w_ref[...], NT, preferred_element_type=F32)
        for (off, wd), y_ref, dy_ref in zip(groups, (ya_ref, yb_ref, yc_ref), (dya_ref, dyb_ref, dyc_ref)):
            y = y_ref[...].astype(F32)
            d = dyn[:, off:off + wd]
            r = _rsq(jnp.mean(y * y, axis=-1, keepdims=True))
            yhat = y * r
            dgg_ref[:, off:off + wd] += jnp.sum(d * yhat, axis=0, keepdims=True)
            dyh = d * gg_ref[:, off:off + wd]
            dy_ref[...] = (r * (dyh - yhat * jnp.mean(dyh * yhat, axis=-1, keepdims=True))).astype(BF16)

    vec = _const_spec((1, D))
    mvec = _const_spec((1, MIX_WIDTH))
    return pl.pallas_call(
        body, name="oproj_bwd", grid=(S // tm,),
        in_specs=[_row_spec(tm, D), _row_spec(tm, D), vec, _const_spec(w.shape),
                  _row_spec(tm, NA_W), _row_spec(tm, SW_W), _row_spec(tm, AX_W), mvec],
        out_specs=[_row_spec(tm, D), _row_spec(tm, NA_W), _row_spec(tm, SW_W), _row_spec(tm, AX_W), vec, mvec],
        out_shape=[jax.ShapeDtypeStruct((S, D), BF16), jax.ShapeDtypeStruct((S, NA_W), BF16),
                   jax.ShapeDtypeStruct((S, SW_W), BF16), jax.ShapeDtypeStruct((S, AX_W), BF16),
                   jax.ShapeDtypeStruct((1, D), F32), jax.ShapeDtypeStruct((1, MIX_WIDTH), F32)],
        compiler_params=_params("arbitrary"),
    )(dx1, ao, gt, w, ya, yb, yc, gg)


def _dproj_assemble(proj, na, sw, ax, gq128, gk128, rope):
    S = proj.shape[0]
    tm = _tile(S, ROW_TILE)
    cos, sa, sb = rope

    def body(proj_ref, qa, ka, kah, va, vah, qb, kb, kbh, vb, vbh, qc, kc, vc,
             gq_ref, gk_ref, cos_ref, sa_ref, sb_ref, out_ref, dgq_ref, dgk_ref):
        @pl.when(pl.program_id(0) == 0)
        def _():
            dgq_ref[...] = jnp.zeros_like(dgq_ref)
            dgk_ref[...] = jnp.zeros_like(dgk_ref)

        out_ref[:, OFF_QA:OFF_KA] = qa[...].astype(BF16)
        out_ref[:, OFF_KA:OFF_VA] = (ka[...] + kah[...]).astype(BF16)
        out_ref[:, OFF_VA:OFF_QB] = (va[...] + vah[...]).astype(BF16)
        out_ref[:, OFF_QB:OFF_KB] = qb[...].astype(BF16)
        out_ref[:, OFF_KB:OFF_VB] = (kb[...] + kbh[...]).astype(BF16)
        out_ref[:, OFF_VB:OFF_QC] = (vb[...] + vbh[...]).astype(BF16)
        c, a, b = cos_ref[...], sa_ref[...], sb_ref[...]
        for j in range(AX_W // LANES):
            cols = slice(OFF_QC + j * LANES, OFF_QC + (j + 1) * LANES)
            dx, dg = _qk_prep_bwd_chunk(proj_ref[:, cols].astype(F32), qc[:, j * LANES:(j + 1) * LANES],
                                        gq_ref[...], c, a, b)
            out_ref[:, cols] = dx.astype(BF16)
            dgq_ref[...] += dg
        for j in range(AX_KV_W // LANES):
            cols = slice(OFF_KC + j * LANES, OFF_KC + (j + 1) * LANES)
            dx, dg = _qk_prep_bwd_chunk(proj_ref[:, cols].astype(F32), kc[:, j * LANES:(j + 1) * LANES],
                                        gk_ref[...], c, a, b)
            out_ref[:, cols] = dx.astype(BF16)
            dgk_ref[...] += dg
        out_ref[:, OFF_VC:IN_WIDTH] = vc[...].astype(BF16)

    v128 = _const_spec((1, LANES))
    r = lambda w: _row_spec(tm, w)
    return pl.pallas_call(
        body, name="dproj_assemble", grid=(S // tm,),
        in_specs=[r(IN_WIDTH), r(NA_W), r(NA_W), r(NA_W), r(NA_W), r(NA_W),
                  r(SW_W), r(SW_KV_W), r(SW_KV_W), r(SW_KV_W), r(SW_KV_W),
                  r(AX_W), r(AX_KV_W), r(AX_KV_W), v128, v128, r(LANES), r(LANES), r(LANES)],
        out_specs=[r(IN_WIDTH), v128, v128],
        out_shape=[jax.ShapeDtypeStruct((S, IN_WIDTH), BF16), jax.ShapeDtypeStruct((1, LANES), F32),
                   jax.ShapeDtypeStruct((1, LANES), F32)],
        compiler_params=_params("arbitrary"),
    )(proj, *na, *sw, *ax, gq128, gk128, cos, sa, sb)


def _tn_matmul(a, b, name):
    S, Ka = a.shape
    Nb = b.shape[1]
    tm = _tile(Ka, 1408, LANES)
    tn = _tile(Nb, 1408, LANES)
    tk = _tile(S, TOKEN_CHUNK)
    nk = S // tk

    def body(a_ref, b_ref, o_ref, acc_ref):
        k = pl.program_id(2)

        @pl.when(k == 0)
        def _():
            acc_ref[...] = jnp.zeros_like(acc_ref)

        acc_ref[...] += lax.dot_general(a_ref[...], b_ref[...], TN, preferred_element_type=F32)

        @pl.when(k == nk - 1)
        def _():
            o_ref[...] = acc_ref[...].astype(BF16)

    return pl.pallas_call(
        body, name=name, grid=(Ka // tm, Nb // tn, nk),
        in_specs=[pl.BlockSpec((tk, tm), lambda i, j, k: (k, i)), pl.BlockSpec((tk, tn), lambda i, j, k: (k, j))],
        out_specs=pl.BlockSpec((tm, tn), lambda i, j, k: (i, j)),
        out_shape=jax.ShapeDtypeStruct((Ka, Nb), BF16),
        scratch_shapes=[pltpu.VMEM((tm, tn), F32)],
        compiler_params=_params("parallel", "parallel", "arbitrary"),
    )(a, b)


def _na_index(bd):
    rq = jnp.arange(bd.bq // GRID_W)
    rk = jnp.arange(bd.bk // GRID_W)
    col = jnp.arange(GRID_W)
    ri = jnp.clip(rk[None, :] - rq[:, None] - bd.halo // GRID_W + NA_WIN_ROWS - 1, 0, 2 * NA_WIN_ROWS - 2)
    ci = jnp.clip(col[None, :] - col[:, None] + NA_WIN_COLS - 1, 0, 2 * NA_WIN_COLS - 2)
    return ri, ci


def _na_one_hots(bd):
    ri, ci = _na_index(bd)
    oh_r = jax.nn.one_hot(ri, 2 * NA_WIN_ROWS - 1, dtype=F32)
    oh_c = jax.nn.one_hot(ci, 2 * NA_WIN_COLS - 1, dtype=F32)
    return oh_r, oh_c


def _na_bias(bd, rpb):
    oh_r, oh_c = _na_one_hots(bd)
    t = jnp.einsum("hab,qra->hqrb", rpb, oh_r, precision=lax.Precision.HIGHEST)
    b = jnp.einsum("hqrb,ckb->hqcrk", t, oh_c, precision=lax.Precision.HIGHEST)
    return b.reshape(NA_HEADS, bd.bq, bd.bk)


def _na_bias_t(bd, dbias):
    oh_r, oh_c = _na_one_hots(bd)
    d5 = dbias.reshape(NA_HEADS, bd.bq // GRID_W, GRID_W, bd.bk // GRID_W, GRID_W)
    t = jnp.einsum("hqcrk,ckb->hqrb", d5, oh_c, precision=lax.Precision.HIGHEST)
    return jnp.einsum("hqrb,qra->hab", t, oh_r, precision=lax.Precision.HIGHEST)


def _t5_bucket(rel):
    nb = T5_BUCKETS // 2
    ret = (rel > 0).astype(I32) * nb
    n = jnp.abs(rel)
    max_exact = nb // 2
    nf = jnp.maximum(n, max_exact).astype(F32)
    large = max_exact + (jnp.log(nf / max_exact) / math.log(T5_MAX_DIST / max_exact)
                         * (nb - max_exact)).astype(I32)
    large = jnp.minimum(large, nb - 1)
    return ret + jnp.where(n < max_exact, n, large)


def _sw_bucket(bd):
    rel = (jnp.arange(bd.bk) - bd.halo)[None, :] - jnp.arange(bd.bq)[:, None]
    return _t5_bucket(rel)


def _sw_bias(bd, t5):
    def body(t5_ref, bucket_ref, out_ref):
        bucket = bucket_ref[...]
        for h in range(SW_HEADS):
            acc = jnp.zeros((bd.bq, bd.bk), F32)
            for b in range(T5_BUCKETS):
                acc = jnp.where(bucket == b, t5_ref[b, h], acc)
            out_ref[h] = acc

    vmem = pl.BlockSpec(memory_space=pltpu.VMEM)
    return pl.pallas_call(
        body, name="sw_bias", in_specs=[pl.BlockSpec(memory_space=pltpu.SMEM), vmem], out_specs=vmem,
        out_shape=jax.ShapeDtypeStruct((SW_HEADS, bd.bq, bd.bk), F32),
    )(t5, _sw_bucket(bd))


def _sw_bias_t(bd, dbias):
    def body(bucket_ref, d_ref, out_ref):
        bucket = bucket_ref[...]
        lane = lax.broadcasted_iota(I32, (1, LANES), 1)
        for b in range(T5_BUCKETS):
            hit = bucket == b
            row = jnp.zeros((1, LANES), F32)
            for h in range(SW_HEADS):
                row = jnp.where(lane == h, jnp.sum(jnp.where(hit, d_ref[h], 0.0)), row)
            out_ref[b:b + 1, :] = row

    vmem = pl.BlockSpec(memory_space=pltpu.VMEM)
    out = pl.pallas_call(
        body, name="sw_bias_t", in_specs=[vmem, vmem], out_specs=vmem,
        out_shape=jax.ShapeDtypeStruct((T5_BUCKETS, LANES), F32),
    )(_sw_bucket(bd), dbias)
    return out[:, :SW_HEADS]


def _local_step(x, target, mod, w_in, w_o, w_gu, w_down, g_attn, rpb_na, sink_sw, t5_table, gq_ax, gk_ax,
                g_group, g_ffn, g_final, late_shards=None):
    S, D = x.shape
    riding = late_shards is not None
    w_in = [w_in[l] for l in range(w_in.shape[0])]
    rope = _rope_tables(S)
    na, sw = _Band("na", S), _Band("sw", S)
    two = lambda v: jnp.concatenate([v, v])[None, :]
    sw_bias = _sw_bias(sw, t5_table)
    saved = []
    for l in range(DEPTH):
        sh_a, sc_a, gt_a, sh_f, sc_f, gt_f = [mod[l, k * D:(k + 1) * D][None, :] for k in range(6)]
        gq128, gk128 = two(gq_ax[l]), two(gk_ax[l])
        gg = g_group[l][None, :]
        sink = jnp.pad(sink_sw[l], (0, LANES - SW_HEADS))[None, :]
        na_bias = _na_bias(na, rpb_na[l])
        h, proj, qc, kc = _inproj_fwd(x, g_attn[l][None, :], sc_a, sh_a, w_in[l], gq128, gk128, rope)
        ya, yna = _band_fwd(na, proj, na_bias, None, gg[:, :NA_W])
        yb, ynb = _band_fwd(sw, proj, sw_bias, sink, gg[:, NA_W:NA_W + SW_W])
        rider = None
        if riding and l == 0:
            rider = _gather_exchange([late_shards[k] for k in BIG], [BIG_AXIS[k] for k in BIG],
                                     [BIG_ORDER[k] for k in BIG])
        yc, ync, lse, got = _ax_fwd(qc, kc, proj, gg[:, NA_W + SW_W:], rider)
        if rider:
            w_in_late, w_o, w_gu, w_down = got
            w_in += [w_in_late[k] for k in range(w_in_late.shape[0])]
        x1, ao, yn = _oproj_fwd(x, yna, ynb, ync, w_o[l], gt_a)
        hf, gu, act = _gu_fwd(x1, g_ffn[l][None, :], sc_f, sh_f, w_gu[l])
        x2, fo = _down_fwd(x1, act, w_down[l], gt_f)
        saved.append(dict(x=x, x1=x1, h=h, proj=proj, qc=qc, kc=kc, ya=ya, yb=yb, yc=yc, lse=lse, ao=ao, yn=yn,
                          hf=hf, gu=gu, act=act, fo=fo, na_bias=na_bias, sink=sink, gq128=gq128, gk128=gk128,
                          gg=gg, mods=(sh_a, sc_a, gt_a, sh_f, sc_f, gt_f)))
        x = x2

    dx, loss_row, dg_final = _final_loss(x, g_final[None, :], target)
    gw = {k: [None] * DEPTH for k in ("w_in", "w_o", "w_gu", "w_down")}
    gs = {k: [None] * DEPTH for k in ("b_mod", "g_attn", "rpb_na", "sink_sw", "gq_ax", "gk_ax", "g_group", "g_ffn")}
    d_t5 = jnp.zeros((T5_BUCKETS, SW_HEADS), F32)
    for l in reversed(range(DEPTH)):
        s = saved[l]
        sh_a, sc_a, gt_a, sh_f, sc_f, gt_f = s["mods"]
        dfo, dgu, dgt_f = _ffn_bwd1(dx, s["fo"], gt_f, w_down[l], s["gu"])
        gw["w_down"][l] = _tn_matmul(s["act"], dfo, "dw_down")
        gw["w_gu"][l] = _tn_matmul(s["hf"], dgu, "dw_gu")
        dx1, dsh_f, dsc_f, gs["g_ffn"][l] = _nt_ln_bwd(dgu, w_gu[l], s["x1"], g_ffn[l][None, :], sc_f, dx, "ffn_bwd2")
        dao, dya, dyb, dyc, dgt_a, gs["g_group"][l] = _oproj_bwd(dx1, s["ao"], gt_a, w_o[l], s["ya"], s["yb"],
                                                                 s["yc"], s["gg"])
        gw["w_o"][l] = _tn_matmul(s["yn"], dao, "dw_o")
        dqa, dka, dva, dkap, dvap, dkan, dvan, dbias_na = _band_bwd(na, s["proj"], jnp.swapaxes(s["na_bias"], 1, 2), None, dya)
        dqb, dkb, dvb, dkbp, dvbp, dkbn, dvbn, dbias_sw, dsink = _band_bwd(sw, s["proj"], jnp.swapaxes(sw_bias, 1, 2), s["sink"], dyb)
        rider = None
        if riding and l == 0:
            ready = [("w_in", k) for k in range(1, DEPTH)] + [(n, k) for n in BIG[1:] for k in range(DEPTH)]
            rider = _scatter_exchange(
                [gw[n][k] for n, k in ready], [BIG_AXIS[n] - 1 for n, _ in ready], [BIG_ORDER[n] for n, _ in ready],
                [None if n == "w_in" else (BIG.index(n), k, DEPTH) for n, k in ready])
        dqc, dkc, dvc, sent = _ax_bwd(s["qc"], s["kc"], s["proj"], dyc, s["lse"], _ax_delta(dyc, s["yc"]), rider)
        dproj, dgq, dgk = _dproj_assemble(
            s["proj"], (dqa, dka, _halo_to_rows(dkap, dkan), dva, _halo_to_rows(dvap, dvan)),
            (dqb, dkb, _halo_to_rows(dkbp, dkbn), dvb, _halo_to_rows(dvbp, dvbn)), (dqc, dkc, dvc),
            s["gq128"], s["gk128"], rope)
        gw["w_in"][l] = _tn_matmul(s["h"], dproj, "dw_in")
        dx, dsh_a, dsc_a, gs["g_attn"][l] = _nt_ln_bwd(dproj, w_in[l], s["x"], g_attn[l][None, :], sc_a, dx1,
                                                       "inproj_bwd")
        gs["b_mod"][l] = jnp.concatenate([dsh_a, dsc_a, dgt_a, dsh_f, dsc_f, dgt_f], axis=1)[0]
        gs["rpb_na"][l] = _na_bias_t(na, jnp.swapaxes(dbias_na, 1, 2))
        gs["sink_sw"][l] = dsink[0, :SW_HEADS]
        d_t5 = d_t5 + _sw_bias_t(sw, jnp.swapaxes(dbias_sw, 1, 2))
        gs["gq_ax"][l] = dgq[0, :HEAD_DIM] + dgq[0, HEAD_DIM:]
        gs["gk_ax"][l] = dgk[0, :HEAD_DIM] + dgk[0, HEAD_DIM:]
        gs["g_attn"][l] = gs["g_attn"][l][0]
        gs["g_ffn"][l] = gs["g_ffn"][l][0]
        gs["g_group"][l] = gs["g_group"][l][0]

    if riding:
        (first,) = _chip_scatter([gw["w_in"][0]], [BIG_AXIS["w_in"] - 1], [BIG_ORDER["w_in"]], "scatter_w_in0")
        gw = dict(zip(BIG[1:], sent[DEPTH - 1:]), w_in=jnp.stack([first] + sent[:DEPTH - 1], axis=1))
    else:
        gw = {k: jnp.stack(v) for k, v in gw.items()}
    small = {k: jnp.stack(v) for k, v in gs.items()}
    small["t5_table"] = d_t5
    small["g_final"] = dg_final[0]
    return loss_row[0, 0], dx, gw, small


MOD_ROWS = 16


def _mod_fwd(cond16, w):
    L, D, C = w.shape
    tn = _tile(C, 512, LANES)

    def body(c_ref, w_ref, o_ref):
        o_ref[0] = jnp.dot(c_ref[...], w_ref[0].astype(BF16), preferred_element_type=F32)

    return pl.pallas_call(
        body, name="mod_fwd", grid=(L, C // tn),
        in_specs=[pl.BlockSpec((MOD_ROWS, D), lambda l, j: (0, 0)), pl.BlockSpec((1, D, tn), lambda l, j: (l, 0, j))],
        out_specs=pl.BlockSpec((1, MOD_ROWS, tn), lambda l, j: (l, 0, j)),
        out_shape=jax.ShapeDtypeStruct((L, MOD_ROWS, C), F32),
        compiler_params=_params("parallel", "parallel"),
    )(cond16, w)


def _adamw_math(w, g, m, v):
    m = ADAM_B1 * m + (1.0 - ADAM_B1) * g
    v = ADAM_B2 * v + (1.0 - ADAM_B2) * (g * g)
    m_hat = m / (1.0 - ADAM_B1 ** ADAM_STEP)
    v_hat = v / (1.0 - ADAM_B2 ** ADAM_STEP)
    delta = -ADAM_LR * (m_hat / (jnp.sqrt(v_hat) + ADAM_EPS) + ADAM_WD * w)
    return delta, m, v


def _adamw(w, m, v, parts, name):
    R, C = w.shape
    tr = _tile(R, 256)
    n = len(parts)

    def body(*refs):
        w_ref, m_ref, v_ref = refs[:3]
        g = refs[3][...]
        for p in refs[4:3 + n]:
            g = g + p[...]
        g_ref, d_ref, m2_ref, v2_ref = refs[3 + n:]
        g_ref[...] = g
        d_ref[...], m2_ref[...], v2_ref[...] = _adamw_math(w_ref[...], g, m_ref[...], v_ref[...])

    spec = _row_spec(tr, C)
    shape = jax.ShapeDtypeStruct((R, C), F32)
    return pl.pallas_call(
        body, name=name, grid=(R // tr,), in_specs=[spec] * (3 + n), out_specs=[spec] * 4, out_shape=[shape] * 4,
        compiler_params=_params("parallel"),
    )(w, m, v, *parts)


def _wmod_adamw(cond_t, dmod16, w, m, v):
    L, D, C = w.shape
    tr = _tile(D, 256)

    def body(c_ref, d_ref, w_ref, m_ref, v_ref, g_ref, dl_ref, m2_ref, v2_ref):
        g = jnp.dot(c_ref[...], d_ref[0], preferred_element_type=F32)
        g_ref[0] = g
        dl_ref[0], m2_ref[0], v2_ref[0] = _adamw_math(w_ref[0], g, m_ref[0], v_ref[0])

    spec = pl.BlockSpec((1, tr, C), lambda l, i: (l, i, 0))
    shape = jax.ShapeDtypeStruct((L, D, C), F32)
    return pl.pallas_call(
        body, name="wmod_adamw", grid=(L, D // tr),
        in_specs=[pl.BlockSpec((tr, MOD_ROWS), lambda l, i: (i, 0)),
                  pl.BlockSpec((1, MOD_ROWS, C), lambda l, i: (l, 0, 0)), spec, spec, spec],
        out_specs=[spec] * 4, out_shape=[shape] * 4,
        compiler_params=_params("parallel", "parallel"),
    )(cond_t, dmod16, w, m, v)


def _sum_slots(a):
    P, R, C = a.shape
    tr = _tile(R, 256, 16)

    def body(a_ref, o_ref):
        s = a_ref[0].astype(F32)
        for k in range(1, P):
            s = s + a_ref[k].astype(F32)
        o_ref[...] = s

    return pl.pallas_call(
        body, name="sum_slots", grid=(R // tr,),
        in_specs=[pl.BlockSpec((P, tr, C), lambda i: (0, i, 0))], out_specs=_row_spec(tr, C),
        out_shape=jax.ShapeDtypeStruct((R, C), F32), compiler_params=_params("parallel"),
    )(a)


def _axes():
    return lax.axis_index("x"), lax.axis_index("y"), lax.axis_index("c")


def _allgather_devices(v):
    N = v.shape[1]

    def body(v_ref, out_ref, send_sems, recv_sems, local_sem):
        x, y, c = _axes()

        def row(px, py, pc):
            return out_ref.at[pl.ds(4 * px + 2 * py + pc, 1), :]

        mine = pltpu.make_async_copy(v_ref, row(x, y, c), local_sem)
        mine.start()
        sends, recvs = [], []
        for k in range(1, N_DEV):
            peer = (x ^ (k >> 2), y ^ ((k >> 1) & 1), c ^ (k & 1))
            sems = dict(send_sem=send_sems.at[k - 1], recv_sem=recv_sems.at[k - 1], device_id=peer, device_id_type=MESH)
            sends.append(pltpu.make_async_remote_copy(src_ref=v_ref, dst_ref=row(x, y, c), **sems))
            recvs.append(pltpu.make_async_remote_copy(src_ref=v_ref, dst_ref=row(*peer), **sems))
        for cp in sends:
            cp.start()
        for cp in recvs:
            cp.wait_recv()
        for cp in sends:
            cp.wait_send()
        mine.wait()

    vmem = pl.BlockSpec(memory_space=pltpu.VMEM)
    return pl.pallas_call(
        body, name="allgather_devices", in_specs=[vmem], out_specs=vmem,
        out_shape=jax.ShapeDtypeStruct((N_DEV, N), v.dtype),
        scratch_shapes=[pltpu.SemaphoreType.DMA((N_DEV - 1,)), pltpu.SemaphoreType.DMA((N_DEV - 1,)),
                        pltpu.SemaphoreType.DMA],
        compiler_params=pltpu.CompilerParams(vmem_limit_bytes=VMEM_LIMIT_V7X),
    )(v)


def _chip_pos(order, px, py):
    return 2 * px + py if order == "natural" else 2 * py + px


def _block(ref, axis, pos, width):
    idx = [slice(None)] * len(ref.shape)
    idx[axis] = pl.ds(pl.multiple_of(pos * width, width), width)
    return ref.at[tuple(idx)]


def _chip_allgather(shards, axes, orders, name):
    return _run_exchange(_gather_exchange(shards, axes, orders), name)


class _Exchange:
    def __init__(self, arrays, out_shapes, describe):
        self.arrays, self.out_shapes, self.describe = list(arrays), list(out_shapes), describe
        n_remote = len(self.arrays) * (N_CHIPS - 1)
        self.scratch = [pltpu.SemaphoreType.DMA((n_remote,)), pltpu.SemaphoreType.DMA((n_remote,)),
                        pltpu.SemaphoreType.DMA((len(self.arrays),))]

    def _copies(self, ins, outs, sems):
        send_sems, recv_sems, local_sems = sems
        x, y, c = _axes()
        local, sends, recvs = [], [], []
        for i in range(len(self.arrays)):
            src, dst = self.describe(i, ins, outs, x, y, x, y)
            local.append(pltpu.make_async_copy(src, dst, local_sems.at[i]))
            for k in range(1, N_CHIPS):
                px, py = x ^ (k >> 1), y ^ (k & 1)
                j = i * (N_CHIPS - 1) + k - 1
                sem = dict(send_sem=send_sems.at[j], recv_sem=recv_sems.at[j], device_id=(px, py, c),
                           device_id_type=MESH)
                src, dst = self.describe(i, ins, outs, x, y, px, py)
                sends.append(pltpu.make_async_remote_copy(src_ref=src, dst_ref=dst, **sem))
                src, dst = self.describe(i, ins, outs, px, py, x, y)
                recvs.append(pltpu.make_async_remote_copy(src_ref=src, dst_ref=dst, **sem))
        return local, sends, recvs

    def start(self, ins, outs, sems):
        local, sends, _ = self._copies(ins, outs, sems)
        for cp in local + sends:
            cp.start()

    def wait(self, ins, outs, sems):
        local, sends, recvs = self._copies(ins, outs, sems)
        for cp in recvs:
            cp.wait_recv()
        for cp in sends:
            cp.wait_send()
        for cp in local:
            cp.wait()


def _run_exchange(ex, name):
    n_in, n_out = len(ex.arrays), len(ex.out_shapes)

    def body(*refs):
        ins, outs, sems = refs[:n_in], refs[n_in:n_in + n_out], refs[n_in + n_out:]
        ex.start(ins, outs, sems)
        ex.wait(ins, outs, sems)

    hbm = pl.BlockSpec(memory_space=pl.ANY)
    return pl.pallas_call(body, name=name, in_specs=[hbm] * n_in, out_specs=[hbm] * n_out, out_shape=ex.out_shapes,
                          scratch_shapes=ex.scratch)(*ex.arrays)


def _gather_exchange(shards, axes, orders):
    out_shapes = []
    for s, ax, order in zip(shards, axes, orders):
        shp = list(s.shape)
        if order == "gate_up_tiles":
            shp = [shp[0], 2, shp[1], 2 * shp[2]]
        else:
            shp[ax] *= N_CHIPS
        out_shapes.append(jax.ShapeDtypeStruct(tuple(shp), s.dtype))

    def describe(i, ins, outs, fx, fy, tx, ty):
        pos, width = _chip_pos(orders[i], fx, fy), shards[i].shape[axes[i]]
        if orders[i] == "gate_up_tiles":
            return ins[i], outs[i].at[:, pos // 2, :, pl.ds(pl.multiple_of((pos % 2) * width, width), width)]
        return ins[i], _block(outs[i], axes[i], pos, width)

    return _Exchange(shards, out_shapes, describe)


def _chip_scatter(grads, axes, orders, name):
    return _run_exchange(_scatter_exchange(grads, axes, orders), name)


def _scatter_exchange(grads, axes, orders, layers=None):
    layers = layers or [None] * len(grads)
    widths, out_shapes = [], {}
    for i, (g, ax, lay) in enumerate(zip(grads, axes, layers)):
        shp = list(g.shape)
        shp[ax] //= N_CHIPS
        widths.append(shp[ax])
        key, lead = (("own", i), (N_CHIPS,)) if lay is None else (("shared", lay[0]), (N_CHIPS, lay[2]))
        out_shapes[key] = jax.ShapeDtypeStruct(lead + tuple(shp), g.dtype)
    keys = list(out_shapes)

    def describe(i, ins, outs, fx, fy, tx, ty):
        lay = layers[i]
        out = outs[keys.index(("own", i) if lay is None else ("shared", lay[0]))]
        slot = out.at[2 * fx + fy] if lay is None else out.at[2 * fx + fy, lay[1]]
        return _block(ins[i], axes[i], _chip_pos(orders[i], tx, ty), widths[i]), slot

    return _Exchange(grads, [out_shapes[k] for k in keys], describe)


def _core_swap(arrays, name):
    n = len(arrays)

    def body(*refs):
        ins, outs = refs[:n], refs[n:2 * n]
        send_sems, recv_sems = refs[2 * n:]
        x, y, c = _axes()
        copies = [pltpu.make_async_remote_copy(src_ref=ins[i], dst_ref=outs[i], send_sem=send_sems.at[i],
                                               recv_sem=recv_sems.at[i], device_id=(x, y, 1 - c), device_id_type=MESH)
                  for i in range(n)]
        for cp in copies:
            cp.start()
        for cp in copies:
            cp.wait_recv()
        for cp in copies:
            cp.wait_send()

    hbm = pl.BlockSpec(memory_space=pl.ANY)
    return pl.pallas_call(
        body, name=name, in_specs=[hbm] * n, out_specs=[hbm] * n,
        out_shape=[jax.ShapeDtypeStruct(a.shape, a.dtype) for a in arrays],
        scratch_shapes=[pltpu.SemaphoreType.DMA((n,)), pltpu.SemaphoreType.DMA((n,))],
    )(*arrays)


SMALL = ("b_mod", "g_attn", "rpb_na", "sink_sw", "t5_table", "gq_ax", "gk_ax", "g_group", "g_ffn", "g_final")
BIG = ("w_in", "w_o", "w_gu", "w_down")
BIG_AXIS = {"w_in": 2, "w_o": 1, "w_gu": 2, "w_down": 1}
BIG_ORDER = {"w_in": "natural", "w_o": "natural", "w_gu": "gate_up_tiles", "w_down": "natural"}
WEIGHTS = ("w_mod", "b_mod", "g_attn", "w_in", "rpb_na", "sink_sw", "t5_table", "gq_ax", "gk_ax", "g_group",
           "w_o", "g_ffn", "w_gu", "w_down", "g_final")


def _pack(arrs):
    flat = jnp.concatenate([a.reshape(-1) for a in arrs])
    n = flat.shape[0]
    padded = -(-n // (8 * LANES)) * (8 * LANES)
    return jnp.pad(flat, (0, padded - n))


def _unpack(flat, like):
    out, off = [], 0
    for a in like:
        out.append(flat[off:off + a.size].reshape(a.shape))
        off += a.size
    return out


def kernel(x, c, w_mod, b_mod, g_attn, w_in, rpb_na, sink_sw, t5_table, gq_ax, gk_ax, g_group, w_o, g_ffn, w_gu, w_down, g_final, loss_target, m_w_mod, m_b_mod, m_g_attn, m_w_in, m_rpb_na, m_sink_sw, m_t5_table, m_gq_ax, m_gk_ax, m_g_group, m_w_o, m_g_ffn, m_w_gu, m_w_down, m_g_final, v_w_mod, v_b_mod, v_g_attn, v_w_in, v_rpb_na, v_sink_sw, v_t5_table, v_gq_ax, v_gk_ax, v_g_group, v_w_o, v_g_ffn, v_w_gu, v_w_down, v_g_final):
    W = dict(w_mod=w_mod, b_mod=b_mod, g_attn=g_attn, w_in=w_in, rpb_na=rpb_na, sink_sw=sink_sw, t5_table=t5_table,
             gq_ax=gq_ax, gk_ax=gk_ax, g_group=g_group, w_o=w_o, g_ffn=g_ffn, w_gu=w_gu, w_down=w_down,
             g_final=g_final)
    M = dict(w_mod=m_w_mod, b_mod=m_b_mod, g_attn=m_g_attn, w_in=m_w_in, rpb_na=m_rpb_na, sink_sw=m_sink_sw,
             t5_table=m_t5_table, gq_ax=m_gq_ax, gk_ax=m_gk_ax, g_group=m_g_group, w_o=m_w_o, g_ffn=m_g_ffn,
             w_gu=m_w_gu, w_down=m_w_down, g_final=m_g_final)
    V = dict(w_mod=v_w_mod, b_mod=v_b_mod, g_attn=v_g_attn, w_in=v_w_in, rpb_na=v_rpb_na, sink_sw=v_sink_sw,
             t5_table=v_t5_table, gq_ax=v_gq_ax, gk_ax=v_gk_ax, g_group=v_g_group, w_o=v_w_o, g_ffn=v_g_ffn,
             w_gu=v_w_gu, w_down=v_w_down, g_final=v_g_final)
    xi, yi, ci = _axes()
    me = 4 * xi + 2 * yi + ci
    chip = 2 * xi + yi
    D = x.shape[-1]
    mod_w = w_mod.shape[2]

    c_all = _allgather_devices(c)
    cond = c_all * (1.0 / (1.0 + jnp.exp(-c_all)))
    cond16 = jnp.pad(cond, ((0, MOD_ROWS - N_DEV), (0, 0))).astype(BF16)
    mod_part = _mod_fwd(cond16, w_mod)
    shards = {k: W[k].astype(BF16) for k in BIG}
    mod_all, w_in_first = _chip_allgather([mod_part, shards["w_in"][:1]], [2, BIG_AXIS["w_in"]],
                                          ["natural", BIG_ORDER["w_in"]], "allgather_mod_w_in0")
    mod = lax.dynamic_slice_in_dim(mod_all, me, 1, axis=1)[:, 0, :] + b_mod
    shards["w_in"] = shards["w_in"][1:]
    loss_part, grad_x, slots, small = _local_step(x[0], loss_target[0], mod, w_in_first, None, None, None, g_attn,
                                                  rpb_na, sink_sw, t5_table, gq_ax, gk_ax, g_group, g_ffn, g_final,
                                                  late_shards=shards)

    small_all = _allgather_devices(_pack([small[k] for k in SMALL])[None, :])
    rows = small_all.shape[1] // LANES
    parts = [small_all[k].reshape(rows, LANES) for k in range(N_DEV)]
    pk = lambda d: _pack([d[k] for k in SMALL]).reshape(rows, LANES)
    small_out = [_unpack(o.reshape(-1), [W[k] for k in SMALL]) for o in _adamw(pk(W), pk(M), pk(V), parts, "adamw_small")]

    L = w_mod.shape[0]
    dmod_all = small_all[:, :L * 6 * D].reshape(N_DEV, L, 6 * D)
    dmod_mine = lax.dynamic_slice_in_dim(dmod_all, chip * mod_w, mod_w, axis=2)
    dmod16 = jnp.pad(jnp.transpose(dmod_mine, (1, 0, 2)), ((0, 0), (0, MOD_ROWS - N_DEV), (0, 0))).astype(BF16)
    wmod_out = _wmod_adamw(jnp.transpose(cond16), dmod16, w_mod, m_w_mod, v_w_mod)

    names = list(BIG)
    two_d = lambda a: a.reshape(-1, a.shape[-1])
    mine = [_sum_slots(slots[k].reshape(N_CHIPS, -1, slots[k].shape[-1])) for k in names]
    theirs = _core_swap(mine, "swap_grads")
    big_out = {}
    for k, a, b in zip(names, mine, theirs):
        outs = _adamw(two_d(W[k]), two_d(M[k]), two_d(V[k]), [a, b], "adamw_" + k)
        big_out[k] = [o.reshape(W[k].shape) for o in outs]

    loss = lax.psum(loss_part, ("x", "y", "c"))
    per_kind = []
    for kind in range(4):
        for k in WEIGHTS:
            if k == "w_mod":
                per_kind.append(wmod_out[kind])
            elif k in big_out:
                per_kind.append(big_out[k][kind])
            else:
                per_kind.append(small_out[kind][SMALL.index(k)])
    return (loss, grad_x[None], *per_kind)
```
